```python
import math
import jax, jax.numpy as jnp
from jax import lax
import numpy as np


D_MODEL = 1024
BATCH = 8
SEQ = 2048
DEPTH = 2

GRID_W = 64
CTX_LEN = 256
N_MOD = 6
EPS = 1e-6
CONV_WIDTH = 4
BRANCH_WIDTH = 512
N_BRANCHES = 3
GDN_HEADS = 4
GDN_HEAD_DIM = 128
GDN_CHUNK = 64
DIFF_HEADS = 4
DIFF_HEAD_DIM = 64
DIFF_V_DIM = 2 * DIFF_HEAD_DIM
Q_BLOCK = 128
ROPE_BASE = 10000.0
ROPE_PAIRS = DIFF_HEAD_DIM // 4
LRU_BLOCKS = 8
LRU_BLOCK_DIM = BRANCH_WIDTH // LRU_BLOCKS
LRU_C = 8.0
N_EXPERTS = 16
N_GROUPS = 4
TOP_K = 2
EXPERT_FF = 512
PROJ_SIZES = (3 * BRANCH_WIDTH,
              BRANCH_WIDTH,
              2 * GDN_HEADS,
              2 * GDN_HEADS,
              BRANCH_WIDTH,
              BRANCH_WIDTH,
              BRANCH_WIDTH,
              BRANCH_WIDTH,
              BRANCH_WIDTH,
              N_BRANCHES * D_MODEL)
PROJ_WIDTH = sum(PROJ_SIZES)

kernel_name = 'hybrid_gdn_diffattn_rglru_grouped_moe_dit'


def rms_norm(x, gain):
    xf = x.astype(jnp.float32)
    y = xf * lax.rsqrt(jnp.mean(xf * xf, axis=-1, keepdims=True) + EPS)
    return (y * gain.astype(jnp.float32)).astype(x.dtype)


def l2_norm(x):
    return x * lax.rsqrt(jnp.sum(x * x, axis=-1, keepdims=True) + EPS)


def modulate(h, shift, scale):
    return h * (1.0 + scale) + shift


def split_proj(p):
    return jnp.split(p, np.cumsum(PROJ_SIZES)[:-1].tolist(), axis=-1)


def centred_dwconv(x, w, bias=None):
    t = x.shape[1]
    left = CONV_WIDTH // 2
    xp = jnp.pad(x, ((0, 0), (left, CONV_WIDTH - 1 - left), (0, 0)))
    y = sum(xp[:, j:j + t] * w[j] for j in range(CONV_WIDTH))
    return y if bias is None else y + bias


def rope_2d(x, row_ang, col_ang):
    half = x.shape[-1] // 2

    def rot(xs, ang):
        cos = jnp.cos(ang).astype(x.dtype)[:, None, None, :]
        sin = jnp.sin(ang).astype(x.dtype)[:, None, None, :]
        x1, x2 = jnp.split(xs, 2, axis=-1)
        return jnp.concatenate([x1 * cos - x2 * sin, x2 * cos + x1 * sin], axis=-1)

    return jnp.concatenate([rot(x[..., :half], row_ang), rot(x[..., half:], col_ang)], axis=-1)


def gated_delta_chunked(q, k, v, g, beta, s0):
    b, h, t, _ = q.shape
    n = t // GDN_CHUNK
    rs = lambda a: a.reshape(b, h, n, GDN_CHUNK, *a.shape[3:])
    q, k, v, g, beta = rs(q), rs(k), rs(v), rs(g), rs(beta)
    g = jnp.cumsum(g, axis=-1)
    idx = jnp.arange(GDN_CHUNK)
    lower = idx[:, None] >= idx[None, :]
    strict = idx[:, None] > idx[None, :]
    decay = jnp.exp(jnp.where(lower, g[..., :, None] - g[..., None, :], -jnp.inf))
    k_beta = k * beta[..., None]
    a_mat = jnp.where(strict, jnp.einsum('bhnid,bhnjd->bhnij', k_beta, k) * decay, 0.0)
    eye = jnp.eye(GDN_CHUNK, dtype=q.dtype)
    t_mat = lax.linalg.triangular_solve(eye + a_mat, jnp.broadcast_to(eye, a_mat.shape),
                                        left_side=True, lower=True, unit_diagonal=True)
    u = jnp.einsum('bhnij,bhnje->bhnie', t_mat, v * beta[..., None])
    w = jnp.einsum('bhnij,bhnjd->bhnid', t_mat, k_beta * jnp.exp(g)[..., None])
    qk = jnp.einsum('bhnid,bhnjd->bhnij', q, k) * decay
    q_dec = q * jnp.exp(g)[..., None]
    k_dec = k * jnp.exp(g[..., -1:] - g)[..., None]
    g_last = jnp.exp(g[..., -1])

    def step(s, inp):
        u_i, w_i, qk_i, q_i, k_i, gl_i = inp
        v_new = u_i - jnp.einsum('bhid,bhde->bhie', w_i, s)
        o = jnp.einsum('bhid,bhde->bhie', q_i, s) + jnp.einsum('bhij,bhje->bhie', qk_i, v_new)
        s = s * gl_i[..., None, None] + jnp.einsum('bhid,bhie->bhde', k_i, v_new)
        return s, o

    mv = lambda a: jnp.moveaxis(a, 2, 0)
    s_fin, o = lax.scan(step, s0, (mv(u), mv(w), mv(qk), mv(q_dec), mv(k_dec), mv(g_last)))
    return jnp.moveaxis(o, 0, 2).reshape(b, h, t, -1), s_fin


def gdn_inputs(qkv, beta_raw, alpha_raw, conv_w, a_log, dt_bias):
    b, t, _ = qkv.shape
    qkv = jax.nn.silu(centred_dwconv(qkv.astype(jnp.float32), conv_w.astype(jnp.float32)))
    q, k, v = qkv.reshape(b, t, 3, GDN_HEADS, GDN_HEAD_DIM).transpose(2, 0, 3, 1, 4)
    q = l2_norm(q) * GDN_HEAD_DIM ** -0.5
    k = l2_norm(k)
    beta = jax.nn.sigmoid(beta_raw.astype(jnp.float32)).reshape(b, t, 2, GDN_HEADS).transpose(2, 0, 3, 1)
    alpha = alpha_raw.astype(jnp.float32).reshape(b, t, 2, GDN_HEADS).transpose(2, 0, 3, 1)
    g = -jnp.exp(a_log.astype(jnp.float32))[:, None, :, None] * jax.nn.softplus(
        alpha + dt_bias.astype(jnp.float32)[:, None, :, None])
    return q, k, v, beta, g


def gdn_branch(qkv_l, z_l, beta_l, alpha_l, qkv_c, z_c, beta_c, alpha_c,
               conv_w, a_log, dt_bias, out_gain, need_ctx_out):
    ql, kl, vl, bl, gl = gdn_inputs(qkv_l, beta_l, alpha_l, conv_w, a_log, dt_bias)
    qc, kc, vc, bc, gc = gdn_inputs(qkv_c, beta_c, alpha_c, conv_w, a_log, dt_bias)
    s0 = jnp.zeros(qc.shape[:2] + (GDN_HEAD_DIM, GDN_HEAD_DIM), jnp.float32)
    rev = lambda a: jnp.flip(a, axis=2)
    oc_f, sc_f = gated_delta_chunked(qc, kc, vc, gc[0], bc[0], s0)
    ol_f, _ = gated_delta_chunked(ql, kl, vl, gl[0], bl[0], sc_f)
    oc_b, sc_b = gated_delta_chunked(rev(qc), rev(kc), rev(vc), rev(gc[1]), rev(bc[1]), s0)
    ol_b, _ = gated_delta_chunked(rev(ql), rev(kl), rev(vl), rev(gl[1]), rev(bl[1]), sc_b)

    def post(o, z):
        o = o.transpose(0, 2, 1, 3)
        zz = z.astype(jnp.float32).reshape(o.shape)
        y = rms_norm(o, out_gain) * jax.nn.silu(zz)
        return y.reshape(o.shape[0], o.shape[1], BRANCH_WIDTH).astype(z.dtype)

    out_l = post(ol_f + rev(ol_b), z_l)
    out_c = post(oc_f + rev(oc_b), z_c) if need_ctx_out else None
    return out_l, out_c


def diff_softmax_attend(q, k, v, lam):
    s = jnp.einsum('bqhcd,bkhcd->bhcqk', q, k).astype(jnp.float32) * DIFF_HEAD_DIM ** -0.5
    p = jax.nn.softmax(s, axis=-1)
    p = p[:, :, 0] - lam * p[:, :, 1]
    return jnp.einsum('bhqk,bkhe->bqhe', p.astype(v.dtype), v)


def diff_branch(q_l, k_l, v_l, q_c, k_c, v_c, q_gain, k_gain, lam_vecs, out_gain, lam_init,
                row_ang, col_ang, need_ctx_out):
    b, s = q_l.shape[:2]
    heads = lambda p: p.reshape(p.shape[0], p.shape[1], DIFF_HEADS, 2, DIFF_HEAD_DIM)
    vals = lambda p: p.reshape(p.shape[0], p.shape[1], DIFF_HEADS, DIFF_V_DIM)
    ql = rope_2d(rms_norm(heads(q_l), q_gain), row_ang, col_ang)
    kl = rope_2d(rms_norm(heads(k_l), k_gain), row_ang, col_ang)
    kc = rms_norm(heads(k_c), k_gain)
    vc = vals(v_c)
    k_all = jnp.concatenate([kl, kc], axis=1)
    v_all = jnp.concatenate([vals(v_l), vc], axis=1)
    lv = lam_vecs.astype(jnp.float32)
    lam = jnp.exp(jnp.sum(lv[0] * lv[1])) - jnp.exp(jnp.sum(lv[2] * lv[3])) + lam_init
    n_blk = s // Q_BLOCK
    q_blocks = jnp.moveaxis(ql.reshape(b, n_blk, Q_BLOCK, DIFF_HEADS, 2, DIFF_HEAD_DIM), 1, 0)
    ol = lax.map(lambda qb: diff_softmax_attend(qb, k_all, v_all, lam), q_blocks)
    ol = jnp.moveaxis(ol, 0, 1).reshape(b, s, DIFF_HEADS, DIFF_V_DIM)

    def post(o):
        return (rms_norm(o, out_gain) * (1.0 - lam_init)).reshape(o.shape[0], o.shape[1], BRANCH_WIDTH)

    out_l = post(ol)
    out_c = post(diff_softmax_attend(rms_norm(heads(q_c), q_gain), kc, vc, lam)) if need_ctx_out else None
    return out_l, out_c


def lru_gates(xc, w_gate, b_gate, lam):
    b, t, _ = xc.shape
    pre = jnp.einsum('btnj,dgnjk->dgbtnk', xc.reshape(b, t, LRU_BLOCKS, LRU_BLOCK_DIM), w_gate)
    pre = pre.reshape(2, 2, b, t, BRANCH_WIDTH) + b_gate[:, :, None, None, :]
    r = jax.nn.sigmoid(pre[:, 0])
    i = jax.nn.sigmoid(pre[:, 1])
    log_a = -LRU_C * r * jax.nn.softplus(-lam)[:, None, None, :]
    return jnp.exp(log_a), jnp.sqrt(-jnp.expm1(2.0 * log_a)) * i * xc[None]


def linear_scan(a, b, h0):
    a_cum, b_cum = lax.associative_scan(lambda l, r: (l[0] * r[0], r[0] * l[1] + r[1]), (a, b), axis=1)
    h = a_cum * h0[:, None, :] + b_cum
    return h, h[:, -1]


def lru_branch(x_l, y_l, x_c, y_c, conv_w, conv_b, w_gate, b_gate, lam, need_ctx_out):
    f32 = jnp.float32
    cw, cb, wg, bg, lm = conv_w.astype(f32), conv_b.astype(f32), w_gate.astype(f32), b_gate.astype(f32), lam.astype(f32)
    xl = centred_dwconv(x_l.astype(f32), cw, cb)
    xc = centred_dwconv(x_c.astype(f32), cw, cb)
    al, bl = lru_gates(xl, wg, bg, lm)
    ac, bc = lru_gates(xc, wg, bg, lm)
    h0 = jnp.zeros((xc.shape[0], BRANCH_WIDTH), f32)
    rev = lambda a: jnp.flip(a, axis=1)
    hc_f, sc_f = linear_scan(ac[0], bc[0], h0)
    hl_f, _ = linear_scan(al[0], bl[0], sc_f)
    hc_b, sc_b = linear_scan(rev(ac[1]), rev(bc[1]), h0)
    hl_b, _ = linear_scan(rev(al[1]), rev(bl[1]), sc_b)
    out_l = ((hl_f + rev(hl_b)) * jax.nn.gelu(y_l.astype(f32))).astype(x_l.dtype)
    out_c = ((hc_f + rev(hc_b)) * jax.nn.gelu(y_c.astype(f32))).astype(x_c.dtype) if need_ctx_out else None
    return out_l, out_c


def merge_branches(ya, yb, yc, gates_raw, w_branch, w_out):
    y = jnp.einsum('btne,ned->btnd', jnp.stack([ya, yb, yc], axis=2), w_branch)
    g = jax.nn.sigmoid(gates_raw.reshape(y.shape))
    return jnp.sum(g * y, axis=2) @ w_out


def token_mixer(hl, hc, lam_init, row_ang, col_ang, need_ctx_out, w_in, gdn_conv_w, gdn_a_log, gdn_dt_bias,
                gdn_out_gain, diff_q_gain, diff_k_gain, diff_lambda, diff_out_gain, lru_conv_w, lru_conv_b,
                lru_w_gate, lru_b_gate, lru_lambda, w_branch, w_out):
    (a_qkv_l, a_z_l, a_b_l, a_a_l, b_q_l, b_k_l, b_v_l, c_x_l, c_y_l, gate_l) = split_proj(hl @ w_in)
    (a_qkv_c, a_z_c, a_b_c, a_a_c, b_q_c, b_k_c, b_v_c, c_x_c, c_y_c, gate_c) = split_proj(hc @ w_in)
    ya_l, ya_c = gdn_branch(a_qkv_l, a_z_l, a_b_l, a_a_l, a_qkv_c, a_z_c, a_b_c, a_a_c,
                            gdn_conv_w, gdn_a_log, gdn_dt_bias, gdn_out_gain, need_ctx_out)
    yb_l, yb_c = diff_branch(b_q_l, b_k_l, b_v_l, b_q_c, b_k_c, b_v_c, diff_q_gain, diff_k_gain,
                             diff_lambda, diff_out_gain, lam_init, row_ang, col_ang, need_ctx_out)
    yc_l, yc_c = lru_branch(c_x_l, c_y_l, c_x_c, c_y_c, lru_conv_w, lru_conv_b, lru_w_gate, lru_b_gate,
                            lru_lambda, need_ctx_out)
    out_l = merge_branches(ya_l, yb_l, yc_l, gate_l, w_branch, w_out)
    out_c = merge_branches(ya_c, yb_c, yc_c, gate_c, w_branch, w_out) if need_ctx_out else None
    return out_l, out_c


def moe_ffn(h, w_router, b_router, w_gate_up, w_down):
    probs = jax.nn.softmax((h @ w_router).astype(jnp.float32) + b_router.astype(jnp.float32), axis=-1)
    grouped = probs.reshape(*probs.shape[:-1], N_GROUPS, N_EXPERTS // N_GROUPS)
    group_score = jnp.sum(lax.top_k(grouped, TOP_K)[0], axis=-1)
    best = jnp.argmax(group_score, axis=-1)
    in_group = (best[..., None] == jnp.arange(N_GROUPS))[..., None]
    masked = jnp.where(in_group, grouped, -1.0).reshape(probs.shape)
    top_w, top_i = lax.top_k(masked, TOP_K)
    top_w = top_w / jnp.sum(top_w, axis=-1, keepdims=True)
    combine = jnp.einsum('btk,btke->bte', top_w, jax.nn.one_hot(top_i, N_EXPERTS, dtype=jnp.float32))
    out = jnp.zeros_like(h)
    for e in range(N_EXPERTS):
        gate, up = jnp.split(h @ w_gate_up[e], 2, axis=-1)
        out = out + combine[..., e:e + 1].astype(h.dtype) * ((jax.nn.silu(gate) * up) @ w_down[e])
    return out


def setup_inputs(seed: int = 0) -> dict:
    key = jax.random.key(seed)
    keys = jax.random.split(key, 32)
    f32 = jnp.float32
    L, D, W = DEPTH, D_MODEL, BRANCH_WIDTH

    def nrm(i, shape, scale):
        return jax.random.normal(keys[i], shape, f32) * scale

    def gain(i, shape):
        return 1.0 + nrm(i, shape, 0.02)

    dt = jnp.exp(jax.random.uniform(keys[11], (L, 2, GDN_HEADS), f32, math.log(1e-3), math.log(1e-1)))
    a_pow = jax.random.uniform(keys[21], (L, 2, W), f32, 0.9, 0.999)
    a_base = jnp.exp(jnp.log(a_pow) / LRU_C)
    return {
        'x': nrm(0, (BATCH, SEQ, D), 1.0),
        'c': nrm(1, (BATCH, D), 1.0),
        'ctx': nrm(2, (BATCH, CTX_LEN, D), 1.0),
        'c_ctx': nrm(3, (D,), 1.0),
        'w_mod': nrm(4, (L, D, N_MOD * D), D ** -0.5),
        'b_mod': nrm(5, (L, N_MOD * D), 0.01),
        'norm1_gain': gain(6, (L, D)),
        'norm2_gain': gain(7, (L, D)),
        'w_in': nrm(8, (L, D, PROJ_WIDTH), D ** -0.5),
        'gdn_conv_w': nrm(9, (L, CONV_WIDTH, 3 * W), CONV_WIDTH ** -0.5),
        'gdn_a_log': jnp.log(jax.random.uniform(keys[10], (L, 2, GDN_HEADS), f32, 1.0, 16.0)),
        'gdn_dt_bias': dt + jnp.log(-jnp.expm1(-dt)),
        'gdn_out_gain': gain(12, (L, GDN_HEAD_DIM)),
        'diff_q_gain': gain(13, (L, DIFF_HEAD_DIM)),
        'diff_k_gain': gain(14, (L, DIFF_HEAD_DIM)),
        'diff_lambda': nrm(15, (L, 4, DIFF_HEAD_DIM), 0.1),
        'diff_out_gain': gain(16, (L, DIFF_V_DIM)),
        'lru_conv_w': nrm(17, (L, CONV_WIDTH, W), CONV_WIDTH ** -0.5),
        'lru_conv_b': nrm(18, (L, W), 0.01),
        'lru_w_gate': nrm(19, (L, 2, 2, LRU_BLOCKS, LRU_BLOCK_DIM, LRU_BLOCK_DIM), LRU_BLOCK_DIM ** -0.5),
        'lru_b_gate': nrm(20, (L, 2, 2, W), 0.01),
        'lru_lambda': jnp.log(a_base) - jnp.log1p(-a_base),
        'w_branch': nrm(22, (L, N_BRANCHES, W, D), W ** -0.5),
        'w_out': nrm(23, (L, D, D), D ** -0.5),
        'w_router': nrm(24, (D, N_EXPERTS), D ** -0.5),
        'b_router': nrm(25, (N_EXPERTS,), 0.01),
        'w_gate_up': nrm(26, (L, N_EXPERTS, D, 2 * EXPERT_FF), D ** -0.5),
        'w_down': nrm(27, (L, N_EXPERTS, EXPERT_FF, D), EXPERT_FF ** -0.5),
    }


def reference(x, c, ctx, c_ctx, w_mod, b_mod, norm1_gain, norm2_gain, w_in, gdn_conv_w, gdn_a_log,
              gdn_dt_bias, gdn_out_gain, diff_q_gain, diff_k_gain, diff_lambda, diff_out_gain, lru_conv_w,
              lru_conv_b, lru_w_gate, lru_b_gate, lru_lambda, w_branch, w_out, w_router, b_router,
              w_gate_up, w_down):
    b, s, d = x.shape
    n_rows = s // GRID_W
    row_id = jnp.broadcast_to(jnp.arange(n_rows, dtype=jnp.float32)[:, None], (n_rows, GRID_W)).reshape(-1)
    col_id = jnp.broadcast_to(jnp.arange(GRID_W, dtype=jnp.float32)[None, :], (n_rows, GRID_W)).reshape(-1)
    inv_freq = jnp.power(ROPE_BASE, -jnp.arange(ROPE_PAIRS, dtype=jnp.float32) / ROPE_PAIRS)
    row_ang = row_id[:, None] * inv_freq
    col_ang = col_id[:, None] * inv_freq

    for layer in range(DEPTH):
        last = layer == DEPTH - 1
        lam_init = 0.8 - 0.6 * math.exp(-0.3 * layer)
        m_l = (jax.nn.silu(c) @ w_mod[layer] + b_mod[layer]).reshape(b, N_MOD, 1, d)
        m_c = (jax.nn.silu(c_ctx) @ w_mod[layer] + b_mod[layer]).reshape(N_MOD, d)
        hl = modulate(rms_norm(x, norm1_gain[layer]), m_l[:, 0], m_l[:, 1])
        hc = modulate(rms_norm(ctx, norm1_gain[layer]), m_c[0], m_c[1])
        out_l, out_c = token_mixer(
            hl, hc, lam_init, row_ang, col_ang, not last, w_in[layer], gdn_conv_w[layer], gdn_a_log[layer],
            gdn_dt_bias[layer], gdn_out_gain[layer], diff_q_gain[layer], diff_k_gain[layer], diff_lambda[layer],
            diff_out_gain[layer], lru_conv_w[layer], lru_conv_b[layer], lru_w_gate[layer], lru_b_gate[layer],
            lru_lambda[layer], w_branch[layer], w_out[layer])
        x = x + m_l[:, 2] * out_l
        h2l = modulate(rms_norm(x, norm2_gain[layer]), m_l[:, 3], m_l[:, 4])
        if last:
            x = x + m_l[:, 5] * moe_ffn(h2l, w_router, b_router, w_gate_up[layer], w_down[layer])
        else:
            ctx = ctx + m_c[2] * out_c
            h2c = modulate(rms_norm(ctx, norm2_gain[layer]), m_c[3], m_c[4])
            f = moe_ffn(jnp.concatenate([h2l, h2c], axis=1), w_router, b_router, w_gate_up[layer], w_down[layer])
            x = x + m_l[:, 5] * f[:, :s]
            ctx = ctx + m_c[5] * f[:, s:]
    return x
```

```python
import functools
import math

import jax
import jax.numpy as jnp
from jax import lax
from jax.experimental import pallas as pl
from jax.experimental.pallas import tpu as pltpu

F32 = jnp.float32
BF16 = jnp.bfloat16

D_MODEL = 1024
BATCH = 8
SEQ = 2048
DEPTH = 2
GRID_W = 64
CTX_LEN = 256
TOK = SEQ + CTX_LEN
N_MOD = 6
EPS = 1e-6
CONV_WIDTH = 4
BRANCH_WIDTH = 512
GDN_HEADS = 4
GDN_HEAD_DIM = 128
GDN_CHUNK = 64
DIFF_HEADS = 4
DIFF_HEAD_DIM = 64
ROPE_BASE = 10000.0
ROPE_PAIRS = DIFF_HEAD_DIM // 4
LRU_BLOCKS = 8
LRU_BLOCK_DIM = BRANCH_WIDTH // LRU_BLOCKS
LRU_C = 8.0
N_EXPERTS = 16
N_GROUPS = 4
EXPERT_FF = 512

LANES = 128
VMEM_LIMIT = 56 * 1024 * 1024

COL_GATES = 0
COL_QKV = 3 * D_MODEL
COL_Z = COL_QKV + 3 * BRANCH_WIDTH
COL_DQ = COL_Z + BRANCH_WIDTH
COL_DK = COL_DQ + BRANCH_WIDTH
COL_DV = COL_DK + BRANCH_WIDTH
COL_LX = COL_DV + BRANCH_WIDTH
COL_LY = COL_LX + BRANCH_WIDTH
PROJ_COLS = COL_LY + BRANCH_WIDTH


def _params(*sem):
    return pltpu.CompilerParams(dimension_semantics=sem, vmem_limit_bytes=VMEM_LIMIT)


def _dot(a, b, precision=None):
    return jnp.dot(a, b, preferred_element_type=F32, precision=precision)


def _dot_nt(a, b):
    return lax.dot_general(a, b, (((1,), (1,)), ((), ())), preferred_element_type=F32)


def _silu(x):
    return x * jax.nn.sigmoid(x)


def _softplus(x):
    return jnp.maximum(x, 0.0) + jnp.log(1.0 + jnp.exp(-jnp.abs(x)))


def _rms(x, gain):
    return x * lax.rsqrt(jnp.mean(x * x, axis=-1, keepdims=True) + EPS) * gain


def _mod_kernel(c_ref, w_ref, b_ref, o_ref):
    c = c_ref[...]
    o_ref[0] = _dot(_silu(c), w_ref[0], precision=lax.Precision.HIGHEST) + b_ref[0]


def _mods(c, c_ctx, w_mod, b_mod):
    depth = w_mod.shape[0]
    rows = 16
    cc = jnp.zeros((rows, D_MODEL), F32).at[:BATCH].set(c).at[BATCH].set(c_ctx)
    tn = 1536
    out = pl.pallas_call(
        _mod_kernel,
        grid=(depth, N_MOD * D_MODEL // tn),
        in_specs=[pl.BlockSpec((rows, D_MODEL), lambda l, j: (0, 0)),
                  pl.BlockSpec((1, D_MODEL, tn), lambda l, j: (l, 0, j)),
                  pl.BlockSpec((1, 1, tn), lambda l, j: (l, 0, j))],
        out_specs=pl.BlockSpec((1, rows, tn), lambda l, j: (l, 0, j)),
        out_shape=jax.ShapeDtypeStruct((depth, rows, N_MOD * D_MODEL), F32),
        compiler_params=_params("parallel", "parallel"),
        name="mods",
    )(cc, w_mod, b_mod.reshape(depth, 1, N_MOD * D_MODEL))
    return out.reshape(depth, rows, N_MOD, D_MODEL)


def _modulated_norm(x, gain, ml_ref, mc_ref, is_ctx, shift_idx):
    shift = jnp.where(is_ctx, mc_ref[0, shift_idx:shift_idx + 1, :], ml_ref[0, shift_idx:shift_idx + 1, :])
    scale = jnp.where(is_ctx, mc_ref[0, shift_idx + 1:shift_idx + 2, :], ml_ref[0, shift_idx + 1:shift_idx + 2, :])
    return _rms(x, gain) * (1.0 + scale) + shift


def _ctx_rows(tile, tm, width):
    row = tile * tm + lax.broadcasted_iota(jnp.int32, (tm, width), 0)
    return row >= SEQ


def _proj_kernel(x_ref, ml_ref, mc_ref, g_ref, w_ref, wba_ref, p_ref, ba_ref, h_scr, *, tm):
    i = pl.program_id(1)
    j = pl.program_id(2)

    @pl.when(j == 0)
    def _():
        is_ctx = _ctx_rows(i, tm, D_MODEL)
        h_scr[...] = _modulated_norm(x_ref[0], g_ref[...], ml_ref, mc_ref, is_ctx, 0).astype(BF16)

    h = h_scr[...]
    p_ref[0] = _dot(h, w_ref[...]).astype(BF16)
    ba_ref[0] = _dot(h, wba_ref[...])


def _project(xs, mods, gain, w_main, w_ba):
    tm, tn = 1152, 1280
    return pl.pallas_call(
        functools.partial(_proj_kernel, tm=tm),
        grid=(BATCH, TOK // tm, PROJ_COLS // tn),
        in_specs=[pl.BlockSpec((1, tm, D_MODEL), lambda b, i, j: (b, i, 0)),
                  pl.BlockSpec((1, N_MOD, D_MODEL), lambda b, i, j: (b, 0, 0)),
                  pl.BlockSpec((1, N_MOD, D_MODEL), lambda b, i, j: (BATCH, 0, 0)),
                  pl.BlockSpec((1, D_MODEL), lambda b, i, j: (0, 0)),
                  pl.BlockSpec((D_MODEL, tn), lambda b, i, j: (0, j)),
                  pl.BlockSpec((D_MODEL, LANES), lambda b, i, j: (0, 0))],
        out_specs=[pl.BlockSpec((1, tm, tn), lambda b, i, j: (b, i, j)),
                   pl.BlockSpec((1, tm, LANES), lambda b, i, j: (b, i, 0))],
        out_shape=[jax.ShapeDtypeStruct((BATCH, TOK, PROJ_COLS), BF16),
                   jax.ShapeDtypeStruct((BATCH, TOK, LANES), F32)],
        scratch_shapes=[pltpu.VMEM((tm, D_MODEL), BF16)],
        compiler_params=_params("parallel", "parallel", "arbitrary"),
        name="proj",
    )(xs, mods, mods, gain.reshape(1, D_MODEL), w_main, w_ba)


def _conv(x, w):
    n, c = x.shape
    t = lax.broadcasted_iota(jnp.int32, (n, c), 0)
    is_ctx = t >= SEQ
    local = jnp.where(is_ctx, t - SEQ, t)
    seg_len = jnp.where(is_ctx, CTX_LEN, SEQ)
    y = jnp.zeros_like(x)
    for j in range(CONV_WIDTH):
        s = j - CONV_WIDTH // 2
        if s == 0:
            y = y + x * w[j:j + 1, :]
        else:
            shifted = pltpu.roll(x, (-s) % n, 0)
            ok = jnp.logical_and(local + s >= 0, local + s < seg_len)
            y = y + jnp.where(ok, shifted, 0.0) * w[j:j + 1, :]
    return y


def _gdn_conv_kernel(x_ref, w_ref, o_ref):
    j = pl.program_id(1)
    y = _silu(_conv(x_ref[0].astype(F32), w_ref[...]))
    nrm = lax.rsqrt(jnp.sum(y * y, axis=-1, keepdims=True) + EPS)
    n_head_blocks = GDN_HEADS
    scale = jnp.where(j < n_head_blocks, nrm * GDN_HEAD_DIM ** -0.5, jnp.where(j < 2 * n_head_blocks, nrm, 1.0))
    o_ref[0] = (y * scale).astype(BF16)


def _gdn_conv(p, conv_w):
    nblk = 3 * BRANCH_WIDTH // LANES
    first = COL_QKV // LANES
    return pl.pallas_call(
        _gdn_conv_kernel,
        grid=(BATCH, nblk),
        in_specs=[pl.BlockSpec((1, TOK, LANES), lambda b, j: (b, 0, first + j)),
                  pl.BlockSpec((CONV_WIDTH, LANES), lambda b, j: (0, j))],
        out_specs=pl.BlockSpec((1, TOK, LANES), lambda b, j: (b, 0, j)),
        out_shape=jax.ShapeDtypeStruct((BATCH, TOK, 3 * BRANCH_WIDTH), BF16),
        compiler_params=_params("parallel", "parallel"),
        name="gdn_conv",
    )(p, conv_w)


def _lru_conv_kernel(x_ref, w_ref, b_ref, o_ref):
    o_ref[0] = _conv(x_ref[0].astype(F32), w_ref[...]) + b_ref[...]


def _lru_conv(p, conv_w, conv_b):
    nblk = BRANCH_WIDTH // LANES
    first = COL_LX // LANES
    return pl.pallas_call(
        _lru_conv_kernel,
        grid=(BATCH, nblk),
        in_specs=[pl.BlockSpec((1, TOK, LANES), lambda b, j: (b, 0, first + j)),
                  pl.BlockSpec((CONV_WIDTH, LANES), lambda b, j: (0, j)),
                  pl.BlockSpec((1, LANES), lambda b, j: (0, j))],
        out_specs=pl.BlockSpec((1, TOK, LANES), lambda b, j: (b, 0, j)),
        out_shape=jax.ShapeDtypeStruct((BATCH, TOK, BRANCH_WIDTH), F32),
        compiler_params=_params("parallel", "parallel"),
        name="lru_conv",
    )(p, conv_w, conv_b.reshape(1, BRANCH_WIDTH))


def _gdn_chunk(q_ref, k_ref, v_ref, gb_scr, chunk, state, head, reverse):
    c = GDN_CHUNK
    r0 = pl.multiple_of(chunk * c, c)
    q = q_ref[0, pl.ds(r0, c), :]
    k = k_ref[0, pl.ds(r0, c), :]
    v = v_ref[0, pl.ds(r0, c), :]
    gb = gb_scr[pl.ds(r0, c), :]
    lane = lax.broadcasted_iota(jnp.int32, (c, LANES), 1)
    row = lax.broadcasted_iota(jnp.int32, (c, LANES), 0)
    col = head + (GDN_HEADS if reverse else 0)
    beta = jnp.sum(jnp.where(lane == col, gb, 0.0), axis=-1, keepdims=True)
    g = jnp.sum(jnp.where(lane == col + 2 * GDN_HEADS, gb, 0.0), axis=-1, keepdims=True)
    ii = lax.broadcasted_iota(jnp.int32, (c, c), 0)
    jj = lax.broadcasted_iota(jnp.int32, (c, c), 1)
    if reverse:
        incl, strict, strict_wide = ii <= jj, ii < jj, row < lane
    else:
        incl, strict, strict_wide = ii >= jj, ii > jj, row > lane
    rhs = jnp.where(lane >= c, g, jnp.where(strict_wide, g, 0.0))
    e = _dot(incl.astype(F32), rhs, precision=lax.Precision.HIGHEST)
    decay = jnp.where(incl, jnp.exp(e[:, :c]), 0.0)
    gc = e[:, c:c + 1]
    last = 0 if reverse else c - 1
    gc_last = e[last:last + 1, c:c + 1]
    egc = jnp.exp(gc)
    kf = k.astype(F32)
    kb = kf * beta
    a_mat = jnp.where(strict, _dot_nt(kb.astype(BF16), k) * decay, 0.0)
    eye = (ii == jj).astype(F32)
    pw = -a_mat
    t_mat = eye + pw
    for _ in range(int(math.log2(c)) - 1):
        pw = _dot(pw, pw, precision=lax.Precision.HIGHEST)
        t_mat = t_mat + _dot(t_mat, pw, precision=lax.Precision.HIGHEST)
    rhs2 = jnp.concatenate([v.astype(F32) * beta, kb * egc], axis=1).astype(BF16)
    uw = _dot(t_mat.astype(BF16), rhs2)
    u, w = uw[:, :GDN_HEAD_DIM], uw[:, GDN_HEAD_DIM:]
    qk = jnp.where(incl, _dot_nt(q, k) * decay, 0.0)
    q_dec = q.astype(F32) * egc
    ws_qs = _dot(jnp.concatenate([w, q_dec], axis=0).astype(BF16), state.astype(BF16))
    v_new = u - ws_qs[:c]
    o = ws_qs[c:] + _dot(qk.astype(BF16), v_new.astype(BF16))
    k_dec_t = (kf * jnp.exp(gc_last - gc)).T.astype(BF16)
    state = state * jnp.exp(gc_last) + _dot(k_dec_t, v_new.astype(BF16))
    return state, o


def _gdn_kernel(q_ref, k_ref, v_ref, z_ref, ba_ref, alog_ref, dtb_ref, gain_ref, o_ref, gb_scr, of_scr, ob_scr):
    head = pl.program_id(1)
    c = GDN_CHUNK
    n_lat, n_ctx = SEQ // c, CTX_LEN // c
    ba = ba_ref[0]
    lane = lax.broadcasted_iota(jnp.int32, ba.shape, 1)
    g_all = -jnp.exp(alog_ref[...]) * _softplus(ba + dtb_ref[...])
    gb_scr[...] = jnp.where(lane < 2 * GDN_HEADS, jax.nn.sigmoid(ba), g_all)

    def body(i, carry):
        s_f, s_b = carry
        in_ctx = i < n_ctx
        cf = jnp.where(in_ctx, n_lat + i, i - n_ctx)
        cb = jnp.where(in_ctx, n_lat + n_ctx - 1 - i, n_lat + n_ctx - 1 - i)
        s_f, o_f = _gdn_chunk(q_ref, k_ref, v_ref, gb_scr, cf, s_f, head, False)
        of_scr[pl.ds(pl.multiple_of(cf * c, c), c), :] = o_f
        s_b, o_b = _gdn_chunk(q_ref, k_ref, v_ref, gb_scr, cb, s_b, head, True)
        ob_scr[pl.ds(pl.multiple_of(cb * c, c), c), :] = o_b
        return s_f, s_b

    zero = jnp.zeros((GDN_HEAD_DIM, GDN_HEAD_DIM), F32)
    lax.fori_loop(0, n_lat + n_ctx, body, (zero, zero))
    o = of_scr[...] + ob_scr[...]
    o_ref[0] = (_rms(o, gain_ref[...]) * _silu(z_ref[0].astype(F32))).astype(BF16)


def _gdn(qkv, p, ba, a_log, dt_bias, out_gain):
    def pad_lanes(vals):
        row = jnp.zeros((LANES,), F32).at[2 * GDN_HEADS:4 * GDN_HEADS].set(vals.reshape(-1))
        return row.reshape(1, LANES)

    nh = GDN_HEADS
    zblk = COL_Z // LANES
    blk = lambda off: pl.BlockSpec((1, TOK, LANES), lambda b, h: (b, 0, off + h))
    vec = pl.BlockSpec((1, LANES), lambda b, h: (0, 0))
    return pl.pallas_call(
        _gdn_kernel,
        grid=(BATCH, nh),
        in_specs=[blk(0), blk(nh), blk(2 * nh), blk(zblk),
                  pl.BlockSpec((1, TOK, LANES), lambda b, h: (b, 0, 0)), vec, vec, vec],
        out_specs=pl.BlockSpec((1, TOK, LANES), lambda b, h: (b, 0, h)),
        out_shape=jax.ShapeDtypeStruct((BATCH, TOK, BRANCH_WIDTH), BF16),
        scratch_shapes=[pltpu.VMEM((TOK, LANES), F32)] * 3,
        compiler_params=_params("parallel", "parallel"),
        name="gdn",
    )(qkv, qkv, qkv, p, ba, pad_lanes(a_log), pad_lanes(dt_bias), out_gain.reshape(1, LANES))


def _rms_halves(x, gain):
    lane = lax.broadcasted_iota(jnp.int32, x.shape, 1)
    lo = lane < DIFF_HEAD_DIM
    x2 = x * x
    s_lo = jnp.sum(jnp.where(lo, x2, 0.0), axis=-1, keepdims=True)
    s_hi = jnp.sum(jnp.where(lo, 0.0, x2), axis=-1, keepdims=True)
    ms = jnp.where(lo, s_lo, s_hi) * (1.0 / DIFF_HEAD_DIM)
    return x * lax.rsqrt(ms + EPS) * gain


def _rope(x, cos, sin):
    lane = lax.broadcasted_iota(jnp.int32, x.shape, 1)
    first = (lane & ROPE_PAIRS) == 0
    partner = jnp.where(first, -pltpu.roll(x, LANES - ROPE_PAIRS, 1), pltpu.roll(x, ROPE_PAIRS, 1))
    return x * cos + partner * sin


def _attn_kernel(q_ref, k_ref, v_ref, cosk_ref, sink_ref, cosq_ref, sinq_ref, qg_ref, kg_ref, lv_ref, og_ref,
                 o_ref, kn_scr, *, ctx_block, lam_init):
    qi = pl.program_id(2)

    @pl.when(qi == 0)
    def _():
        kn = _rope(_rms_halves(k_ref[0].astype(F32), kg_ref[...]), cosk_ref[...], sink_ref[...])
        kn_scr[...] = kn.astype(BF16)

    lv = lv_ref[...]
    lam = (jnp.exp(jnp.sum(lv[0:1] * lv[1:2], axis=-1, keepdims=True))
           - jnp.exp(jnp.sum(lv[2:3] * lv[3:4], axis=-1, keepdims=True)) + lam_init)
    q = _rope(_rms_halves(q_ref[0].astype(F32), qg_ref[...]), cosq_ref[...], sinq_ref[...])
    q = q * DIFF_HEAD_DIM ** -0.5
    lane = lax.broadcasted_iota(jnp.int32, q.shape, 1)
    lo = lane < DIFF_HEAD_DIM
    q1 = jnp.where(lo, q, 0.0).astype(BF16)
    q2 = jnp.where(lo, 0.0, q).astype(BF16)

    def attend(kn, v):
        def probs(qc):
            s = _dot_nt(qc, kn)
            p = jnp.exp(s - jnp.max(s, axis=-1, keepdims=True))
            return p, 1.0 / jnp.sum(p, axis=-1, keepdims=True)
        p1, r1 = probs(q1)
        p2, r2 = probs(q2)
        p = p1 * r1 - p2 * (lam * r2)
        o = _dot(p.astype(BF16), v)
        o_ref[0] = (_rms(o, og_ref[...]) * (1.0 - lam_init)).astype(BF16)

    if ctx_block is None:
        attend(kn_scr[...], v_ref[0])
    else:
        @pl.when(qi == ctx_block)
        def _():
            attend(kn_scr[SEQ:, :], v_ref[0, SEQ:, :])

        @pl.when(qi != ctx_block)
        def _():
            attend(kn_scr[...], v_ref[0])


def _rope_tables():
    n_rows = SEQ // GRID_W
    row_id = jnp.broadcast_to(jnp.arange(n_rows, dtype=F32)[:, None], (n_rows, GRID_W)).reshape(-1)
    col_id = jnp.broadcast_to(jnp.arange(GRID_W, dtype=F32)[None, :], (n_rows, GRID_W)).reshape(-1)
    inv_freq = jnp.power(ROPE_BASE, -jnp.arange(ROPE_PAIRS, dtype=F32) / ROPE_PAIRS)
    row_ang = row_id[:, None] * inv_freq
    col_ang = col_id[:, None] * inv_freq
    ang = jnp.concatenate([row_ang, row_ang, col_ang, col_ang], axis=-1)
    ang = jnp.concatenate([ang, ang], axis=-1)
    pad = ((0, CTX_LEN), (0, 0))
    return jnp.pad(jnp.cos(ang), pad, constant_values=1.0), jnp.pad(jnp.sin(ang), pad)


def _diff_attn(p, cos, sin, q_gain, k_gain, lam_vecs, out_gain, lam_init, with_ctx):
    tq = CTX_LEN
    nq = TOK // tq if with_ctx else SEQ // tq
    nh = DIFF_HEADS
    qb, kb, vb = COL_DQ // LANES, COL_DK // LANES, COL_DV // LANES
    full = lambda off: pl.BlockSpec((1, TOK, LANES), lambda b, h, i: (b, 0, off + h))
    tab_full = pl.BlockSpec((TOK, LANES), lambda b, h, i: (0, 0))
    tab_q = pl.BlockSpec((tq, LANES), lambda b, h, i: (i, 0))
    vec = pl.BlockSpec((1, LANES), lambda b, h, i: (0, 0))
    tile2 = lambda g: jnp.concatenate([g, g]).reshape(1, LANES)
    return pl.pallas_call(
        functools.partial(_attn_kernel, ctx_block=SEQ // tq if with_ctx else None, lam_init=lam_init),
        grid=(BATCH, nh, nq),
        in_specs=[pl.BlockSpec((1, tq, LANES), lambda b, h, i: (b, i, qb + h)), full(kb), full(vb),
                  tab_full, tab_full, tab_q, tab_q, vec, vec,
                  pl.BlockSpec((4, DIFF_HEAD_DIM), lambda b, h, i: (0, 0)), vec],
        out_specs=pl.BlockSpec((1, tq, LANES), lambda b, h, i: (b, i, h)),
        out_shape=jax.ShapeDtypeStruct((BATCH, TOK, BRANCH_WIDTH), BF16),
        scratch_shapes=[pltpu.VMEM((TOK, LANES), BF16)],
        compiler_params=_params("parallel", "parallel", "arbitrary"),
        name="diff_attn",
    )(p, p, p, cos, sin, cos, sin, tile2(q_gain), tile2(k_gain), lam_vecs, out_gain.reshape(1, LANES))


LRU_SLAB = 256
LRU_ROWS = 256


def _lru_kernel(xc_ref, y_ref, wg_ref, bg_ref, lam_ref, o_ref, af_scr, bf_scr, ab_scr, bb_scr):
    w = LRU_SLAB
    sp = _softplus(-lam_ref[0])

    def gates(i, _):
        r0 = pl.multiple_of(i * LRU_ROWS, LRU_ROWS)
        xc = xc_ref[0, pl.ds(r0, LRU_ROWS), :]
        pre = _dot(xc.astype(BF16), wg_ref[0]) + bg_ref[0]
        for d, (a_scr, b_scr) in enumerate(((af_scr, bf_scr), (ab_scr, bb_scr))):
            r = jax.nn.sigmoid(pre[:, (2 * d) * w:(2 * d + 1) * w])
            gi = jax.nn.sigmoid(pre[:, (2 * d + 1) * w:(2 * d + 2) * w])
            log_a = -LRU_C * r * sp[d:d + 1]
            a = jnp.exp(log_a)
            a_scr[pl.ds(r0, LRU_ROWS), :] = a
            b_scr[pl.ds(r0, LRU_ROWS), :] = jnp.sqrt(1.0 - a * a) * gi * xc
        return 0

    lax.fori_loop(0, TOK // LRU_ROWS, gates, 0)

    def step(s, carry):
        h_f, h_b = carry
        in_ctx = s < CTX_LEN
        rf = jnp.where(in_ctx, SEQ + s, s - CTX_LEN)
        rb = jnp.where(in_ctx, TOK - 1 - s, TOK - 1 - s)
        h_f = af_scr[pl.ds(rf, 1), :] * h_f + bf_scr[pl.ds(rf, 1), :]
        bf_scr[pl.ds(rf, 1), :] = h_f
        h_b = ab_scr[pl.ds(rb, 1), :] * h_b + bb_scr[pl.ds(rb, 1), :]
        bb_scr[pl.ds(rb, 1), :] = h_b
        return h_f, h_b

    zero = jnp.zeros((1, w), F32)
    lax.fori_loop(0, TOK, step, (zero, zero), unroll=8)
    h = bf_scr[...] + bb_scr[...]
    o_ref[0] = (h * jax.nn.gelu(y_ref[0].astype(F32))).astype(BF16)


def _lru_gate_weights(w_gate, b_gate):
    n_slab = BRANCH_WIDTH // LRU_SLAB
    per = LRU_SLAB // LRU_BLOCK_DIM
    wg = w_gate.reshape(2, 2, n_slab, per, LRU_BLOCK_DIM, LRU_BLOCK_DIM)
    eye = jnp.eye(per, dtype=w_gate.dtype)
    dense = jnp.einsum('dgsnjk,nm->snjdgmk', wg, eye)
    dense = dense.reshape(n_slab, LRU_SLAB, 4 * LRU_SLAB)
    bg = b_gate.reshape(2, 2, n_slab, LRU_SLAB).transpose(2, 0, 1, 3).reshape(n_slab, 1, 4 * LRU_SLAB)
    return dense.astype(BF16), bg


def _lru(xc, p, w_gate, b_gate, lam):
    n_slab = BRANCH_WIDTH // LRU_SLAB
    wg, bg = _lru_gate_weights(w_gate, b_gate)
    lam_s = lam.reshape(2, n_slab, LRU_SLAB).transpose(1, 0, 2)
    yb = COL_LY // LRU_SLAB
    return pl.pallas_call(
        _lru_kernel,
        grid=(BATCH, n_slab),
        in_specs=[pl.BlockSpec((1, TOK, LRU_SLAB), lambda b, s: (b, 0, s)),
                  pl.BlockSpec((1, TOK, LRU_SLAB), lambda b, s: (b, 0, yb + s)),
                  pl.BlockSpec((1, LRU_SLAB, 4 * LRU_SLAB), lambda b, s: (s, 0, 0)),
                  pl.BlockSpec((1, 1, 4 * LRU_SLAB), lambda b, s: (s, 0, 0)),
                  pl.BlockSpec((1, 2, LRU_SLAB), lambda b, s: (s, 0, 0))],
        out_specs=pl.BlockSpec((1, TOK, LRU_SLAB), lambda b, s: (b, 0, s)),
        out_shape=jax.ShapeDtypeStruct((BATCH, TOK, BRANCH_WIDTH), BF16),
        scratch_shapes=[pltpu.VMEM((TOK, LRU_SLAB), F32)] * 4,
        compiler_params=_params("parallel", "parallel"),
        name="lru",
    )(xc, p, wg, bg, lam_s)


def _route(logits):
    lane = lax.broadcasted_iota(jnp.int32, logits.shape, 1)
    lane_f = lane.astype(F32)
    far = float(LANES)
    lg = jnp.where(lane < N_EXPERTS, logits, -jnp.inf)
    ex = jnp.exp(lg - jnp.max(lg, axis=-1, keepdims=True))
    probs = ex / jnp.sum(ex, axis=-1, keepdims=True)
    per_group = N_EXPERTS // N_GROUPS
    grp = lane // per_group

    def top2(vals):
        m1 = jnp.max(vals, axis=-1, keepdims=True)
        i1 = jnp.min(jnp.where(vals == m1, lane_f, far), axis=-1, keepdims=True)
        rest = jnp.where(lane_f == i1, -2.0, vals)
        m2 = jnp.max(rest, axis=-1, keepdims=True)
        i2 = jnp.min(jnp.where(rest == m2, lane_f, far), axis=-1, keepdims=True)
        return m1, i1, m2, i2

    best = jnp.zeros(logits.shape[:1] + (1,), jnp.int32)
    best_score = None
    for g in range(N_GROUPS):
        m1, _, m2, _ = top2(jnp.where(grp == g, probs, -1.0))
        score = m1 + m2
        if best_score is None:
            best_score = score
        else:
            better = score > best_score
            best = jnp.where(better, g, best)
            best_score = jnp.where(better, score, best_score)
    m1, i1, m2, i2 = top2(jnp.where(grp == best, probs, -1.0))
    den = m1 + m2
    return jnp.where(lane_f == i1, m1 / den, 0.0) + jnp.where(lane_f == i2, m2 / den, 0.0)


def _merge_kernel(ya_ref, yb_ref, yc_ref, gates_ref, x_ref, ml_ref, mc_ref, g2_ref, wbr_ref, wout_ref, wr_ref,
                  br_ref, xo_ref, h2_ref, comb_ref, *, tm):
    i = pl.program_id(1)
    acc = None
    for n, y_ref in enumerate((ya_ref, yb_ref, yc_ref)):
        yn = _dot(y_ref[0], wbr_ref[n])
        gate = jax.nn.sigmoid(gates_ref[0, :, n * D_MODEL:(n + 1) * D_MODEL].astype(F32))
        acc = gate * yn if acc is None else acc + gate * yn
    out = _dot(acc.astype(BF16), wout_ref[...])
    is_ctx = _ctx_rows(i, tm, D_MODEL)
    xn = x_ref[0] + jnp.where(is_ctx, mc_ref[0, 2:3, :], ml_ref[0, 2:3, :]) * out
    xo_ref[0] = xn
    h2 = _modulated_norm(xn, g2_ref[...], ml_ref, mc_ref, is_ctx, 3).astype(BF16)
    h2_ref[0] = h2
    comb_ref[0] = _route(_dot(h2, wr_ref[...]) + br_ref[...])


def _merge(ya, yb, yc, p, xs, mods, gain2, w_branch, w_out, w_router, b_router, rows, tm):
    wr = jnp.zeros((D_MODEL, LANES), BF16).at[:, :N_EXPERTS].set(w_router.astype(BF16))
    br = jnp.zeros((1, LANES), F32).at[0, :N_EXPERTS].set(b_router)
    tile = lambda w: pl.BlockSpec((1, tm, w), lambda b, i: (b, i, 0))
    const = lambda shape: pl.BlockSpec(shape, lambda b, i: (0,) * len(shape))
    return pl.pallas_call(
        functools.partial(_merge_kernel, tm=tm),
        grid=(BATCH, rows // tm),
        in_specs=[tile(BRANCH_WIDTH), tile(BRANCH_WIDTH), tile(BRANCH_WIDTH), tile(3 * D_MODEL), tile(D_MODEL),
                  pl.BlockSpec((1, N_MOD, D_MODEL), lambda b, i: (b, 0, 0)),
                  pl.BlockSpec((1, N_MOD, D_MODEL), lambda b, i: (BATCH, 0, 0)),
                  const((1, D_MODEL)), const((3, BRANCH_WIDTH, D_MODEL)), const((D_MODEL, D_MODEL)),
                  const((D_MODEL, LANES)), const((1, LANES))],
        out_specs=[tile(D_MODEL), tile(D_MODEL), tile(LANES)],
        out_shape=[jax.ShapeDtypeStruct((BATCH, rows, D_MODEL), F32),
                   jax.ShapeDtypeStruct((BATCH, rows, D_MODEL), BF16),
                   jax.ShapeDtypeStruct((BATCH, rows, LANES), F32)],
        compiler_params=_params("parallel", "parallel"),
        name="merge",
    )(ya, yb, yc, p, xs, mods, mods, gain2.reshape(1, D_MODEL), w_branch, w_out, wr, br)


def _moe_kernel(h_ref, comb_ref, x_ref, ml_ref, mc_ref, wgu_ref, wd_ref, o_ref, acc_scr, *, tm):
    i = pl.program_id(1)
    e = pl.program_id(2)

    @pl.when(e == 0)
    def _():
        acc_scr[...] = jnp.zeros_like(acc_scr)

    gu = _dot(h_ref[0], wgu_ref[0])
    act = (_silu(gu[:, :EXPERT_FF]) * gu[:, EXPERT_FF:]).astype(BF16)
    y = _dot(act, wd_ref[0])
    comb = comb_ref[0]
    lane = lax.broadcasted_iota(jnp.int32, comb.shape, 1)
    weight = jnp.sum(jnp.where(lane == e, comb, 0.0), axis=-1, keepdims=True)
    acc_scr[...] += weight * y

    @pl.when(e == N_EXPERTS - 1)
    def _():
        is_ctx = _ctx_rows(i, tm, D_MODEL)
        o_ref[0] = x_ref[0] + jnp.where(is_ctx, mc_ref[0, 5:6, :], ml_ref[0, 5:6, :]) * acc_scr[...]


def _moe(h2, comb, xs, mods, w_gate_up, w_down, rows, tm):
    tile = lambda w: pl.BlockSpec((1, tm, w), lambda b, i, e: (b, i, 0))
    return pl.pallas_call(
        functools.partial(_moe_kernel, tm=tm),
        grid=(BATCH, rows // tm, N_EXPERTS),
        in_specs=[tile(D_MODEL), tile(LANES), tile(D_MODEL),
                  pl.BlockSpec((1, N_MOD, D_MODEL), lambda b, i, e: (b, 0, 0)),
                  pl.BlockSpec((1, N_MOD, D_MODEL), lambda b, i, e: (BATCH, 0, 0)),
                  pl.BlockSpec((1, D_MODEL, 2 * EXPERT_FF), lambda b, i, e: (e, 0, 0)),
                  pl.BlockSpec((1, EXPERT_FF, D_MODEL), lambda b, i, e: (e, 0, 0))],
        out_specs=tile(D_MODEL),
        out_shape=jax.ShapeDtypeStruct((BATCH, rows, D_MODEL), F32),
        scratch_shapes=[pltpu.VMEM((tm, D_MODEL), F32)],
        compiler_params=_params("parallel", "parallel", "arbitrary"),
        name="moe",
    )(h2, comb, xs, mods, mods, w_gate_up, w_down)


def _split_w_in(w_in):
    bw = BRANCH_WIDTH
    sizes = (3 * bw, bw, 2 * GDN_HEADS, 2 * GDN_HEADS, bw, bw, bw, bw, bw, 3 * D_MODEL)
    offs = [0]
    for s in sizes:
        offs.append(offs[-1] + s)
    part = lambda i: w_in[:, offs[i]:offs[i + 1]]
    main = jnp.concatenate([part(9), part(0), part(1), part(4), part(5), part(6), part(7), part(8)], axis=1)
    ba = jnp.zeros((D_MODEL, LANES), F32).at[:, :4 * GDN_HEADS].set(jnp.concatenate([part(2), part(3)], axis=1))
    return main.astype(BF16), ba.astype(BF16)


def kernel(x, c, ctx, c_ctx, w_mod, b_mod, norm1_gain, norm2_gain, w_in, gdn_conv_w, gdn_a_log, gdn_dt_bias, gdn_out_gain, diff_q_gain, diff_k_gain, diff_lambda, diff_out_gain, lru_conv_w, lru_conv_b, lru_w_gate, lru_b_gate, lru_lambda, w_branch, w_out, w_router, b_router, w_gate_up, w_down):
    mods = _mods(c, c_ctx, w_mod, b_mod)
    cos, sin = _rope_tables()
    xs = jnp.concatenate([x, ctx], axis=1)
    for layer in range(DEPTH):
        last = layer == DEPTH - 1
        lam_init = 0.8 - 0.6 * math.exp(-0.3 * layer)
        m = mods[layer]
        w_main, w_ba = _split_w_in(w_in[layer])
        p, ba = _project(xs, m, norm1_gain[layer], w_main, w_ba)
        qkv = _gdn_conv(p, gdn_conv_w[layer])
        ya = _gdn(qkv, p, ba, gdn_a_log[layer], gdn_dt_bias[layer], gdn_out_gain[layer])
        yb = _diff_attn(p, cos, sin, diff_q_gain[layer], diff_k_gain[layer], diff_lambda[layer],
                        diff_out_gain[layer], lam_init, with_ctx=not last)
        xc = _lru_conv(p, lru_conv_w[layer], lru_conv_b[layer])
        yc = _lru(xc, p, lru_w_gate[layer], lru_b_gate[layer], lru_lambda[layer])
        rows, tm = (SEQ, 512) if last else (TOK, 768)
        xs, h2, comb = _merge(ya, yb, yc, p, xs, m, norm2_gain[layer], w_branch[layer].astype(BF16),
                              w_out[layer].astype(BF16), w_router, b_router, rows, tm)
        xs = _moe(h2, comb, xs, m, w_gate_up[layer].astype(BF16), w_down[layer].astype(BF16), rows, tm)
    return xs
```

```python
import functools
import math

import jax
import jax.numpy as jnp
from jax import lax
from jax.experimental import pallas as pl
from jax.experimental.pallas import tpu as pltpu

F32 = jnp.float32
BF16 = jnp.bfloat16

D_MODEL = 1024
BATCH = 8
SEQ = 2048
DEPTH = 2
GRID_W = 64
CTX_LEN = 256
TOK = SEQ + CTX_LEN
N_MOD = 6
EPS = 1e-6
CONV_WIDTH = 4
BRANCH_WIDTH = 512
GDN_HEADS = 4
GDN_HEAD_DIM = 128
GDN_CHUNK = 64
DIFF_HEADS = 4
DIFF_HEAD_DIM = 64
ROPE_BASE = 10000.0
ROPE_PAIRS = DIFF_HEAD_DIM // 4
LRU_BLOCKS = 8
LRU_BLOCK_DIM = BRANCH_WIDTH // LRU_BLOCKS
LRU_C = 8.0
N_EXPERTS = 16
N_GROUPS = 4
EXPERT_FF = 512

LANES = 128
VMEM_LIMIT = 56 * 1024 * 1024

COL_GATES = 0
COL_QKV = 3 * D_MODEL
COL_Z = COL_QKV + 3 * BRANCH_WIDTH
COL_DQ = COL_Z + BRANCH_WIDTH
COL_DK = COL_DQ + BRANCH_WIDTH
COL_DV = COL_DK + BRANCH_WIDTH
COL_LX = COL_DV + BRANCH_WIDTH
COL_LY = COL_LX + BRANCH_WIDTH
PROJ_COLS = COL_LY + BRANCH_WIDTH


def _params(*sem):
    return pltpu.CompilerParams(dimension_semantics=sem, vmem_limit_bytes=VMEM_LIMIT)


def _dot(a, b, precision=None):
    return jnp.dot(a, b, preferred_element_type=F32, precision=precision)


def _dot_nt(a, b):
    return lax.dot_general(a, b, (((1,), (1,)), ((), ())), preferred_element_type=F32)


def _silu(x):
    return x * jax.nn.sigmoid(x)


def _softplus(x):
    return jnp.maximum(x, 0.0) + jnp.log(1.0 + jnp.exp(-jnp.abs(x)))


def _rms(x, gain):
    return x * lax.rsqrt(jnp.mean(x * x, axis=-1, keepdims=True) + EPS) * gain


def _mod_kernel(c_ref, w_ref, b_ref, o_ref):
    c = c_ref[...]
    o_ref[0] = _dot(_silu(c), w_ref[0], precision=lax.Precision.HIGHEST) + b_ref[0]


def _mods(c, c_ctx, w_mod, b_mod):
    depth = w_mod.shape[0]
    rows = 16
    cc = jnp.zeros((rows, D_MODEL), F32).at[:BATCH].set(c).at[BATCH].set(c_ctx)
    tn = 1536
    out = pl.pallas_call(
        _mod_kernel,
        grid=(depth, N_MOD * D_MODEL // tn),
        in_specs=[pl.BlockSpec((rows, D_MODEL), lambda l, j: (0, 0)),
                  pl.BlockSpec((1, D_MODEL, tn), lambda l, j: (l, 0, j)),
                  pl.BlockSpec((1, 1, tn), lambda l, j: (l, 0, j))],
        out_specs=pl.BlockSpec((1, rows, tn), lambda l, j: (l, 0, j)),
        out_shape=jax.ShapeDtypeStruct((depth, rows, N_MOD * D_MODEL), F32),
        compiler_params=_params("parallel", "parallel"),
        name="mods",
    )(cc, w_mod, b_mod.reshape(depth, 1, N_MOD * D_MODEL))
    return out.reshape(depth, rows, N_MOD, D_MODEL)


def _modulated_norm(x, gain, ml_ref, mc_ref, is_ctx, shift_idx):
    shift = jnp.where(is_ctx, mc_ref[0, shift_idx:shift_idx + 1, :], ml_ref[0, shift_idx:shift_idx + 1, :])
    scale = jnp.where(is_ctx, mc_ref[0, shift_idx + 1:shift_idx + 2, :], ml_ref[0, shift_idx + 1:shift_idx + 2, :])
    return _rms(x, gain) * (1.0 + scale) + shift


def _ctx_rows(tile, tm, width):
    row = tile * tm + lax.broadcasted_iota(jnp.int32, (tm, width), 0)
    return row >= SEQ


def _proj_kernel(x_ref, ml_ref, mc_ref, g_ref, w_ref, wba_ref, p_ref, ba_ref, h_scr, *, tm):
    i = pl.program_id(1)
    j = pl.program_id(2)

    @pl.when(j == 0)
    def _():
        is_ctx = _ctx_rows(i, tm, D_MODEL)
        h_scr[...] = _modulated_norm(x_ref[0], g_ref[...], ml_ref, mc_ref, is_ctx, 0).astype(BF16)

    h = h_scr[...]
    p_ref[0] = _dot(h, w_ref[...]).astype(BF16)
    ba_ref[0] = _dot(h, wba_ref[...])


def _project(xs, mods, gain, w_main, w_ba):
    tm, tn = 1152, 1280
    return pl.pallas_call(
        functools.partial(_proj_kernel, tm=tm),
        grid=(BATCH, TOK // tm, PROJ_COLS // tn),
        in_specs=[pl.BlockSpec((1, tm, D_MODEL), lambda b, i, j: (b, i, 0)),
                  pl.BlockSpec((1, N_MOD, D_MODEL), lambda b, i, j: (b, 0, 0)),
                  pl.BlockSpec((1, N_MOD, D_MODEL), lambda b, i, j: (BATCH, 0, 0)),
                  pl.BlockSpec((1, D_MODEL), lambda b, i, j: (0, 0)),
                  pl.BlockSpec((D_MODEL, tn), lambda b, i, j: (0, j)),
                  pl.BlockSpec((D_MODEL, LANES), lambda b, i, j: (0, 0))],
        out_specs=[pl.BlockSpec((1, tm, tn), lambda b, i, j: (b, i, j)),
                   pl.BlockSpec((1, tm, LANES), lambda b, i, j: (b, i, 0))],
        out_shape=[jax.ShapeDtypeStruct((BATCH, TOK, PROJ_COLS), BF16),
                   jax.ShapeDtypeStruct((BATCH, TOK, LANES), F32)],
        scratch_shapes=[pltpu.VMEM((tm, D_MODEL), BF16)],
        compiler_params=_params("parallel", "parallel", "arbitrary"),
        name="proj",
    )(xs, mods, mods, gain.reshape(1, D_MODEL), w_main, w_ba)


def _conv(x, w):
    n, c = x.shape
    t = lax.broadcasted_iota(jnp.int32, (n, c), 0)
    is_ctx = t >= SEQ
    local = jnp.where(is_ctx, t - SEQ, t)
    seg_len = jnp.where(is_ctx, CTX_LEN, SEQ)
    y = jnp.zeros_like(x)
    for j in range(CONV_WIDTH):
        s = j - CONV_WIDTH // 2
        if s == 0:
            y = y + x * w[j:j + 1, :]
        else:
            shifted = pltpu.roll(x, (-s) % n, 0)
            ok = jnp.logical_and(local + s >= 0, local + s < seg_len)
            y = y + jnp.where(ok, shifted, 0.0) * w[j:j + 1, :]
    return y


def _gdn_conv_kernel(x_ref, w_ref, o_ref):
    j = pl.program_id(1)
    y = _silu(_conv(x_ref[0].astype(F32), w_ref[...]))
    nrm = lax.rsqrt(jnp.sum(y * y, axis=-1, keepdims=True) + EPS)
    n_head_blocks = GDN_HEADS
    scale = jnp.where(j < n_head_blocks, nrm * GDN_HEAD_DIM ** -0.5, jnp.where(j < 2 * n_head_blocks, nrm, 1.0))
    o_ref[0] = (y * scale).astype(BF16)


def _gdn_conv(p, conv_w):
    nblk = 3 * BRANCH_WIDTH // LANES
    first = COL_QKV // LANES
    return pl.pallas_call(
        _gdn_conv_kernel,
        grid=(BATCH, nblk),
        in_specs=[pl.BlockSpec((1, TOK, LANES), lambda b, j: (b, 0, first + j)),
                  pl.BlockSpec((CONV_WIDTH, LANES), lambda b, j: (0, j))],
        out_specs=pl.BlockSpec((1, TOK, LANES), lambda b, j: (b, 0, j)),
        out_shape=jax.ShapeDtypeStruct((BATCH, TOK, 3 * BRANCH_WIDTH), BF16),
        compiler_params=_params("parallel", "parallel"),
        name="gdn_conv",
    )(p, conv_w)


def _lru_conv_kernel(x_ref, w_ref, b_ref, o_ref):
    o_ref[0] = _conv(x_ref[0].astype(F32), w_ref[...]) + b_ref[...]


def _lru_conv(p, conv_w, conv_b):
    nblk = BRANCH_WIDTH // LANES
    first = COL_LX // LANES
    return pl.pallas_call(
        _lru_conv_kernel,
        grid=(BATCH, nblk),
        in_specs=[pl.BlockSpec((1, TOK, LANES), lambda b, j: (b, 0, first + j)),
                  pl.BlockSpec((CONV_WIDTH, LANES), lambda b, j: (0, j)),
                  pl.BlockSpec((1, LANES), lambda b, j: (0, j))],
        out_specs=pl.BlockSpec((1, TOK, LANES), lambda b, j: (b, 0, j)),
        out_shape=jax.ShapeDtypeStruct((BATCH, TOK, BRANCH_WIDTH), F32),
        compiler_params=_params("parallel", "parallel"),
        name="lru_conv",
    )(p, conv_w, conv_b.reshape(1, BRANCH_WIDTH))


GDN_QM_ROWS = GDN_CHUNK + GDN_HEAD_DIM
GDN_GL_ROWS = 8
GDN_PREP_UNROLL = 4


def _gdn_prepare(q_ref, k_ref, v_ref, gb_scr, qm_scr, nn_scr, o_scr, gl_scr, chunks, head):
    c = GDN_CHUNK
    lane = lax.broadcasted_iota(jnp.int32, (c, LANES), 1)
    row = lax.broadcasted_iota(jnp.int32, (c, LANES), 0)
    ii = lax.broadcasted_iota(jnp.int32, (c, c), 0)
    jj = lax.broadcasted_iota(jnp.int32, (c, c), 1)
    eye = (ii == jj).astype(F32)
    masks = ((ii >= jj, ii > jj, row > lane), (ii <= jj, ii < jj, row < lane))

    loaded = []
    for chunk in chunks:
        r0 = pl.multiple_of(chunk * c, c)
        k = k_ref[0, pl.ds(r0, c), :]
        q = q_ref[0, pl.ds(r0, c), :]
        kq = _dot_nt(jnp.concatenate([k, q], axis=0), k)
        loaded.append((chunk, r0, q, k, kq))

    chains = []
    for chunk, r0, q, k, kq in loaded:
        gb = gb_scr[pl.ds(r0, c), :]
        for d in range(2):
            col = head + d * GDN_HEADS
            beta = jnp.sum(jnp.where(lane == col, gb, 0.0), axis=-1, keepdims=True)
            g = jnp.sum(jnp.where(lane == col + 2 * GDN_HEADS, gb, 0.0), axis=-1, keepdims=True)
            incl, strict, strict_wide = masks[d]
            rhs = jnp.where(lane >= c, g, jnp.where(strict_wide, g, 0.0))
            e = _dot(incl.astype(F32), rhs, precision=lax.Precision.HIGHEST)
            chains.append(dict(chunk=chunk, r0=r0, d=d, q=q, k=k, kq=kq, beta=beta, e=e))

    for ch in chains:
        incl, strict, _ = masks[ch["d"]]
        e = ch["e"]
        decay = jnp.where(incl, jnp.exp(e[:, :c]), 0.0)
        gc = e[:, c:c + 1]
        last = 0 if ch["d"] == 1 else c - 1
        gc_last = e[last:last + 1, c:c + 1]
        ch.update(decay=decay, gc=gc, gc_last=gc_last, egc=jnp.exp(gc))
        ch["a"] = jnp.where(strict, ch["beta"] * ch["kq"][:c] * decay, 0.0)
        ch["t"] = eye
    s = 1
    while s < c:
        pair = jnp.logical_and((ii // (2 * s)) == (jj // (2 * s)), (ii // s) != (jj // s))
        for ch in chains:
            ch["a_off"] = jnp.where(pair, ch["a"], 0.0)
        if s == 1:
            for ch in chains:
                ch["t"] = eye - ch["a_off"]
        else:
            for ch in chains:
                ch["m"] = _dot(ch["t"].astype(BF16), ch["a_off"].astype(BF16))
            for ch in chains:
                ch["t"] = ch["t"] - _dot(ch["m"].astype(BF16), ch["t"].astype(BF16))
        s *= 2
    for ch in chains:
        r0, beta, egc = ch["r0"], ch["beta"], ch["egc"]
        kf = ch["k"].astype(F32)
        vf = v_ref[0, pl.ds(r0, c), :].astype(F32)
        rhs2 = jnp.concatenate([vf * beta, kf * (beta * egc)], axis=1).astype(BF16)
        ch["uw"] = _dot(ch["t"].astype(BF16), rhs2).astype(BF16)
        ch["k_dec_t"] = (kf * jnp.exp(ch["gc_last"] - ch["gc"])).T.astype(BF16)
    for ch in chains:
        incl = masks[ch["d"]][0]
        qk = jnp.where(incl, ch["kq"][c:] * ch["decay"], 0.0).astype(BF16)
        ch["nm"] = _dot(ch["k_dec_t"], ch["uw"])
        ch["ow"] = _dot(qk, ch["uw"])
    for ch in chains:
        chunk, r0, d, nm, ow = ch["chunk"], ch["r0"], ch["d"], ch["nm"], ch["ow"]
        q0 = pl.multiple_of(chunk * GDN_QM_ROWS, 16)
        qm_scr[d, pl.ds(q0, c), :] = (ch["q"].astype(F32) * ch["egc"] - ow[:, GDN_HEAD_DIM:]).astype(BF16)
        qm_scr[d, pl.ds(q0 + c, GDN_HEAD_DIM), :] = nm[:, GDN_HEAD_DIM:].astype(BF16)
        nn_scr[d, pl.ds(pl.multiple_of(chunk * GDN_HEAD_DIM, GDN_HEAD_DIM), GDN_HEAD_DIM), :] = nm[:, :GDN_HEAD_DIM]
        o_scr[d, pl.ds(r0, c), :] = ow[:, :GDN_HEAD_DIM]
        gl_scr[d, pl.ds(pl.multiple_of(chunk * GDN_GL_ROWS, GDN_GL_ROWS), GDN_GL_ROWS), :] = jnp.broadcast_to(
            jnp.exp(ch["gc_last"]), (GDN_GL_ROWS, LANES))


def _gdn_advance(qm_scr, nn_scr, o_scr, gl_scr, d, chunk, state):
    c = GDN_CHUNK
    qm = qm_scr[d, pl.ds(pl.multiple_of(chunk * GDN_QM_ROWS, 16), GDN_QM_ROWS), :]
    r = _dot(qm, state.astype(BF16))
    rows = pl.ds(pl.multiple_of(chunk * c, c), c)
    o_scr[d, rows, :] = o_scr[d, rows, :] + r[:c]
    gl = gl_scr[d, pl.ds(pl.multiple_of(chunk * GDN_GL_ROWS, GDN_GL_ROWS), 1), :]
    n = nn_scr[d, pl.ds(pl.multiple_of(chunk * GDN_HEAD_DIM, GDN_HEAD_DIM), GDN_HEAD_DIM), :]
    return state * gl - r[c:] + n


def _gdn_kernel(q_ref, k_ref, v_ref, z_ref, ba_ref, alog_ref, dtb_ref, gain_ref, o_ref,
                gb_scr, qm_scr, nn_scr, o_scr, gl_scr):
    head = pl.program_id(1)
    c = GDN_CHUNK
    n_lat, n_ctx = SEQ // c, CTX_LEN // c
    n_chunks = n_lat + n_ctx
    ba = ba_ref[0]
    lane = lax.broadcasted_iota(jnp.int32, ba.shape, 1)
    g_all = -jnp.exp(alog_ref[...]) * _softplus(ba + dtb_ref[...])
    gb_scr[...] = jnp.where(lane < 2 * GDN_HEADS, jax.nn.sigmoid(ba), g_all)

    def prepare(i, _):
        chunks = [i * GDN_PREP_UNROLL + j for j in range(GDN_PREP_UNROLL)]
        _gdn_prepare(q_ref, k_ref, v_ref, gb_scr, qm_scr, nn_scr, o_scr, gl_scr, chunks, head)
        return 0

    lax.fori_loop(0, n_chunks // GDN_PREP_UNROLL, prepare, 0)

    def advance(i, carry):
        s_f, s_b = carry
        cf = jnp.where(i < n_ctx, n_lat + i, i - n_ctx)
        cb = n_chunks - 1 - i
        s_f = _gdn_advance(qm_scr, nn_scr, o_scr, gl_scr, 0, cf, s_f)
        s_b = _gdn_advance(qm_scr, nn_scr, o_scr, gl_scr, 1, cb, s_b)
        return s_f, s_b

    zero = jnp.zeros((GDN_HEAD_DIM, GDN_HEAD_DIM), F32)
    lax.fori_loop(0, n_chunks, advance, (zero, zero))
    o = o_scr[0] + o_scr[1]
    o_ref[0] = (_rms(o, gain_ref[...]) * _silu(z_ref[0].astype(F32))).astype(BF16)


def _gdn(qkv, p, ba, a_log, dt_bias, out_gain):
    def pad_lanes(vals):
        row = jnp.zeros((LANES,), F32).at[2 * GDN_HEADS:4 * GDN_HEADS].set(vals.reshape(-1))
        return row.reshape(1, LANES)

    nh = GDN_HEADS
    n_chunks = TOK // GDN_CHUNK
    zblk = COL_Z // LANES
    blk = lambda off: pl.BlockSpec((1, TOK, LANES), lambda b, h: (b, 0, off + h))
    vec = pl.BlockSpec((1, LANES), lambda b, h: (0, 0))
    return pl.pallas_call(
        _gdn_kernel,
        grid=(BATCH, nh),
        in_specs=[blk(0), blk(nh), blk(2 * nh), blk(zblk),
                  pl.BlockSpec((1, TOK, LANES), lambda b, h: (b, 0, 0)), vec, vec, vec],
        out_specs=pl.BlockSpec((1, TOK, LANES), lambda b, h: (b, 0, h)),
        out_shape=jax.ShapeDtypeStruct((BATCH, TOK, BRANCH_WIDTH), BF16),
        scratch_shapes=[pltpu.VMEM((TOK, LANES), F32),
                        pltpu.VMEM((2, n_chunks * GDN_QM_ROWS, LANES), BF16),
                        pltpu.VMEM((2, n_chunks * GDN_HEAD_DIM, LANES), F32),
                        pltpu.VMEM((2, TOK, LANES), F32),
                        pltpu.VMEM((2, n_chunks * GDN_GL_ROWS, LANES), F32)],
        compiler_params=_params("parallel", "parallel"),
        name="gdn",
    )(qkv, qkv, qkv, p, ba, pad_lanes(a_log), pad_lanes(dt_bias), out_gain.reshape(1, LANES))


def _rms_halves(x, gain):
    lane = lax.broadcasted_iota(jnp.int32, x.shape, 1)
    lo = lane < DIFF_HEAD_DIM
    x2 = x * x
    s_lo = jnp.sum(jnp.where(lo, x2, 0.0), axis=-1, keepdims=True)
    s_hi = jnp.sum(jnp.where(lo, 0.0, x2), axis=-1, keepdims=True)
    ms = jnp.where(lo, s_lo, s_hi) * (1.0 / DIFF_HEAD_DIM)
    return x * lax.rsqrt(ms + EPS) * gain


def _rope(x, cos, sin):
    lane = lax.broadcasted_iota(jnp.int32, x.shape, 1)
    first = (lane & ROPE_PAIRS) == 0
    partner = jnp.where(first, -pltpu.roll(x, LANES - ROPE_PAIRS, 1), pltpu.roll(x, ROPE_PAIRS, 1))
    return x * cos + partner * sin


def _attn_kernel(q_ref, k_ref, v_ref, cosk_ref, sink_ref, cosq_ref, sinq_ref, qg_ref, kg_ref, lv_ref, og_ref,
                 o_ref, kn_scr, *, ctx_block, lam_init):
    qi = pl.program_id(2)

    @pl.when(qi == 0)
    def _():
        kn = _rope(_rms_halves(k_ref[0].astype(F32), kg_ref[...]), cosk_ref[...], sink_ref[...])
        kn_scr[...] = kn.astype(BF16)

    lv = lv_ref[...]
    lam = (jnp.exp(jnp.sum(lv[0:1] * lv[1:2], axis=-1, keepdims=True))
           - jnp.exp(jnp.sum(lv[2:3] * lv[3:4], axis=-1, keepdims=True)) + lam_init)
    q = _rope(_rms_halves(q_ref[0].astype(F32), qg_ref[...]), cosq_ref[...], sinq_ref[...])
    q = q * DIFF_HEAD_DIM ** -0.5
    lane = lax.broadcasted_iota(jnp.int32, q.shape, 1)
    lo = lane < DIFF_HEAD_DIM
    q1 = jnp.where(lo, q, 0.0).astype(BF16)
    q2 = jnp.where(lo, 0.0, q).astype(BF16)

    def attend(kn, v):
        def probs(qc):
            s = _dot_nt(qc, kn)
            p = jnp.exp(s - jnp.max(s, axis=-1, keepdims=True))
            return p, 1.0 / jnp.sum(p, axis=-1, keepdims=True)
        p1, r1 = probs(q1)
        p2, r2 = probs(q2)
        p = p1 * r1 - p2 * (lam * r2)
        o = _dot(p.astype(BF16), v)
        o_ref[0] = (_rms(o, og_ref[...]) * (1.0 - lam_init)).astype(BF16)

    if ctx_block is None:
        attend(kn_scr[...], v_ref[0])
    else:
        @pl.when(qi == ctx_block)
        def _():
            attend(kn_scr[SEQ:, :], v_ref[0, SEQ:, :])

        @pl.when(qi != ctx_block)
        def _():
            attend(kn_scr[...], v_ref[0])


def _rope_tables():
    n_rows = SEQ // GRID_W
    row_id = jnp.broadcast_to(jnp.arange(n_rows, dtype=F32)[:, None], (n_rows, GRID_W)).reshape(-1)
    col_id = jnp.broadcast_to(jnp.arange(GRID_W, dtype=F32)[None, :], (n_rows, GRID_W)).reshape(-1)
    inv_freq = jnp.power(ROPE_BASE, -jnp.arange(ROPE_PAIRS, dtype=F32) / ROPE_PAIRS)
    row_ang = row_id[:, None] * inv_freq
    col_ang = col_id[:, None] * inv_freq
    ang = jnp.concatenate([row_ang, row_ang, col_ang, col_ang], axis=-1)
    ang = jnp.concatenate([ang, ang], axis=-1)
    pad = ((0, CTX_LEN), (0, 0))
    return jnp.pad(jnp.cos(ang), pad, constant_values=1.0), jnp.pad(jnp.sin(ang), pad)


def _diff_attn(p, cos, sin, q_gain, k_gain, lam_vecs, out_gain, lam_init, with_ctx):
    tq = CTX_LEN
    nq = TOK // tq if with_ctx else SEQ // tq
    nh = DIFF_HEADS
    qb, kb, vb = COL_DQ // LANES, COL_DK // LANES, COL_DV // LANES
    full = lambda off: pl.BlockSpec((1, TOK, LANES), lambda b, h, i: (b, 0, off + h))
    tab_full = pl.BlockSpec((TOK, LANES), lambda b, h, i: (0, 0))
    tab_q = pl.BlockSpec((tq, LANES), lambda b, h, i: (i, 0))
    vec = pl.BlockSpec((1, LANES), lambda b, h, i: (0, 0))
    tile2 = lambda g: jnp.concatenate([g, g]).reshape(1, LANES)
    return pl.pallas_call(
        functools.partial(_attn_kernel, ctx_block=SEQ // tq if with_ctx else None, lam_init=lam_init),
        grid=(BATCH, nh, nq),
        in_specs=[pl.BlockSpec((1, tq, LANES), lambda b, h, i: (b, i, qb + h)), full(kb), full(vb),
                  tab_full, tab_full, tab_q, tab_q, vec, vec,
                  pl.BlockSpec((4, DIFF_HEAD_DIM), lambda b, h, i: (0, 0)), vec],
        out_specs=pl.BlockSpec((1, tq, LANES), lambda b, h, i: (b, i, h)),
        out_shape=jax.ShapeDtypeStruct((BATCH, TOK, BRANCH_WIDTH), BF16),
        scratch_shapes=[pltpu.VMEM((TOK, LANES), BF16)],
        compiler_params=_params("parallel", "parallel", "arbitrary"),
        name="diff_attn",
    )(p, p, p, cos, sin, cos, sin, tile2(q_gain), tile2(k_gain), lam_vecs, out_gain.reshape(1, LANES))


LRU_SLAB = 256
LRU_ROWS = 256


def _lru_kernel(xc_ref, y_ref, wg_ref, bg_ref, lam_ref, o_ref, af_scr, bf_scr, ab_scr, bb_scr):
    w = LRU_SLAB
    sp = _softplus(-lam_ref[0])

    def gates(i, _):
        r0 = pl.multiple_of(i * LRU_ROWS, LRU_ROWS)
        xc = xc_ref[0, pl.ds(r0, LRU_ROWS), :]
        pre = _dot(xc.astype(BF16), wg_ref[0]) + bg_ref[0]
        for d, (a_scr, b_scr) in enumerate(((af_scr, bf_scr), (ab_scr, bb_scr))):
            r = jax.nn.sigmoid(pre[:, (2 * d) * w:(2 * d + 1) * w])
            gi = jax.nn.sigmoid(pre[:, (2 * d + 1) * w:(2 * d + 2) * w])
            log_a = -LRU_C * r * sp[d:d + 1]
            a = jnp.exp(log_a)
            a_scr[pl.ds(r0, LRU_ROWS), :] = a
            b_scr[pl.ds(r0, LRU_ROWS), :] = jnp.sqrt(1.0 - a * a) * gi * xc
        return 0

    lax.fori_loop(0, TOK // LRU_ROWS, gates, 0)

    def step(s, carry):
        h_f, h_b = carry
        in_ctx = s < CTX_LEN
        rf = jnp.where(in_ctx, SEQ + s, s - CTX_LEN)
        rb = jnp.where(in_ctx, TOK - 1 - s, TOK - 1 - s)
        h_f = af_scr[pl.ds(rf, 1), :] * h_f + bf_scr[pl.ds(rf, 1), :]
        bf_scr[pl.ds(rf, 1), :] = h_f
        h_b = ab_scr[pl.ds(rb, 1), :] * h_b + bb_scr[pl.ds(rb, 1), :]
        bb_scr[pl.ds(rb, 1), :] = h_b
        return h_f, h_b

    zero = jnp.zeros((1, w), F32)
    lax.fori_loop(0, TOK, step, (zero, zero), unroll=8)
    h = bf_scr[...] + bb_scr[...]
    o_ref[0] = (h * jax.nn.gelu(y_ref[0].astype(F32))).astype(BF16)


def _lru_gate_weights(w_gate, b_gate):
    n_slab = BRANCH_WIDTH // LRU_SLAB
    per = LRU_SLAB // LRU_BLOCK_DIM
    wg = w_gate.reshape(2, 2, n_slab, per, LRU_BLOCK_DIM, LRU_BLOCK_DIM)
    eye = jnp.eye(per, dtype=w_gate.dtype)
    dense = jnp.einsum('dgsnjk,nm->snjdgmk', wg, eye)
    dense = dense.reshape(n_slab, LRU_SLAB, 4 * LRU_SLAB)
    bg = b_gate.reshape(2, 2, n_slab, LRU_SLAB).transpose(2, 0, 1, 3).reshape(n_slab, 1, 4 * LRU_SLAB)
    return dense.astype(BF16), bg


def _lru(xc, p, w_gate, b_gate, lam):
    n_slab = BRANCH_WIDTH // LRU_SLAB
    wg, bg = _lru_gate_weights(w_gate, b_gate)
    lam_s = lam.reshape(2, n_slab, LRU_SLAB).transpose(1, 0, 2)
    yb = COL_LY // LRU_SLAB
    return pl.pallas_call(
        _lru_kernel,
        grid=(BATCH, n_slab),
        in_specs=[pl.BlockSpec((1, TOK, LRU_SLAB), lambda b, s: (b, 0, s)),
                  pl.BlockSpec((1, TOK, LRU_SLAB), lambda b, s: (b, 0, yb + s)),
                  pl.BlockSpec((1, LRU_SLAB, 4 * LRU_SLAB), lambda b, s: (s, 0, 0)),
                  pl.BlockSpec((1, 1, 4 * LRU_SLAB), lambda b, s: (s, 0, 0)),
                  pl.BlockSpec((1, 2, LRU_SLAB), lambda b, s: (s, 0, 0))],
        out_specs=pl.BlockSpec((1, TOK, LRU_SLAB), lambda b, s: (b, 0, s)),
        out_shape=jax.ShapeDtypeStruct((BATCH, TOK, BRANCH_WIDTH), BF16),
        scratch_shapes=[pltpu.VMEM((TOK, LRU_SLAB), F32)] * 4,
        compiler_params=_params("parallel", "parallel"),
        name="lru",
    )(xc, p, wg, bg, lam_s)


def _route(logits):
    lane = lax.broadcasted_iota(jnp.int32, logits.shape, 1)
    lane_f = lane.astype(F32)
    far = float(LANES)
    lg = jnp.where(lane < N_EXPERTS, logits, -jnp.inf)
    ex = jnp.exp(lg - jnp.max(lg, axis=-1, keepdims=True))
    probs = ex / jnp.sum(ex, axis=-1, keepdims=True)
    per_group = N_EXPERTS // N_GROUPS
    grp = lane // per_group

    def top2(vals):
        m1 = jnp.max(vals, axis=-1, keepdims=True)
        i1 = jnp.min(jnp.where(vals == m1, lane_f, far), axis=-1, keepdims=True)
        rest = jnp.where(lane_f == i1, -2.0, vals)
        m2 = jnp.max(rest, axis=-1, keepdims=True)
        i2 = jnp.min(jnp.where(rest == m2, lane_f, far), axis=-1, keepdims=True)
        return m1, i1, m2, i2

    best = jnp.zeros(logits.shape[:1] + (1,), jnp.int32)
    best_score = None
    for g in range(N_GROUPS):
        m1, _, m2, _ = top2(jnp.where(grp == g, probs, -1.0))
        score = m1 + m2
        if best_score is None:
            best_score = score
        else:
            better = score > best_score
            best = jnp.where(better, g, best)
            best_score = jnp.where(better, score, best_score)
    m1, i1, m2, i2 = top2(jnp.where(grp == best, probs, -1.0))
    den = m1 + m2
    return jnp.where(lane_f == i1, m1 / den, 0.0) + jnp.where(lane_f == i2, m2 / den, 0.0)


def _merge_kernel(ya_ref, yb_ref, yc_ref, gates_ref, x_ref, ml_ref, mc_ref, g2_ref, wbr_ref, wout_ref, wr_ref,
                  br_ref, xo_ref, h2_ref, comb_ref, *, tm):
    i = pl.program_id(1)
    acc = None
    for n, y_ref in enumerate((ya_ref, yb_ref, yc_ref)):
        yn = _dot(y_ref[0], wbr_ref[n])
        gate = jax.nn.sigmoid(gates_ref[0, :, n * D_MODEL:(n + 1) * D_MODEL].astype(F32))
        acc = gate * yn if acc is None else acc + gate * yn
    out = _dot(acc.astype(BF16), wout_ref[...])
    is_ctx = _ctx_rows(i, tm, D_MODEL)
    xn = x_ref[0] + jnp.where(is_ctx, mc_ref[0, 2:3, :], ml_ref[0, 2:3, :]) * out
    xo_ref[0] = xn
    h2 = _modulated_norm(xn, g2_ref[...], ml_ref, mc_ref, is_ctx, 3).astype(BF16)
    h2_ref[0] = h2
    comb_ref[0] = _route(_dot(h2, wr_ref[...]) + br_ref[...])


def _merge(ya, yb, yc, p, xs, mods, gain2, w_branch, w_out, w_router, b_router, rows, tm):
    wr = jnp.zeros((D_MODEL, LANES), BF16).at[:, :N_EXPERTS].set(w_router.astype(BF16))
    br = jnp.zeros((1, LANES), F32).at[0, :N_EXPERTS].set(b_router)
    tile = lambda w: pl.BlockSpec((1, tm, w), lambda b, i: (b, i, 0))
    const = lambda shape: pl.BlockSpec(shape, lambda b, i: (0,) * len(shape))
    return pl.pallas_call(
        functools.partial(_merge_kernel, tm=tm),
        grid=(BATCH, rows // tm),
        in_specs=[tile(BRANCH_WIDTH), tile(BRANCH_WIDTH), tile(BRANCH_WIDTH), tile(3 * D_MODEL), tile(D_MODEL),
                  pl.BlockSpec((1, N_MOD, D_MODEL), lambda b, i: (b, 0, 0)),
                  pl.BlockSpec((1, N_MOD, D_MODEL), lambda b, i: (BATCH, 0, 0)),
                  const((1, D_MODEL)), const((3, BRANCH_WIDTH, D_MODEL)), const((D_MODEL, D_MODEL)),
                  const((D_MODEL, LANES)), const((1, LANES))],
        out_specs=[tile(D_MODEL), tile(D_MODEL), tile(LANES)],
        out_shape=[jax.ShapeDtypeStruct((BATCH, rows, D_MODEL), F32),
                   jax.ShapeDtypeStruct((BATCH, rows, D_MODEL), BF16),
                   jax.ShapeDtypeStruct((BATCH, rows, LANES), F32)],
        compiler_params=_params("parallel", "parallel"),
        name="merge",
    )(ya, yb, yc, p, xs, mods, mods, gain2.reshape(1, D_MODEL), w_branch, w_out, wr, br)


def _moe_kernel(h_ref, comb_ref, x_ref, ml_ref, mc_ref, wgu_ref, wd_ref, o_ref, acc_scr, *, tm):
    i = pl.program_id(1)
    e = pl.program_id(2)

    @pl.when(e == 0)
    def _():
        acc_scr[...] = jnp.zeros_like(acc_scr)

    gu = _dot(h_ref[0], wgu_ref[0])
    act = (_silu(gu[:, :EXPERT_FF]) * gu[:, EXPERT_FF:]).astype(BF16)
    y = _dot(act, wd_ref[0])
    comb = comb_ref[0]
    lane = lax.broadcasted_iota(jnp.int32, comb.shape, 1)
    weight = jnp.sum(jnp.where(lane == e, comb, 0.0), axis=-1, keepdims=True)
    acc_scr[...] += weight * y

    @pl.when(e == N_EXPERTS - 1)
    def _():
        is_ctx = _ctx_rows(i, tm, D_MODEL)
        o_ref[0] = x_ref[0] + jnp.where(is_ctx, mc_ref[0, 5:6, :], ml_ref[0, 5:6, :]) * acc_scr[...]


def _moe(h2, comb, xs, mods, w_gate_up, w_down, rows, tm):
    tile = lambda w: pl.BlockSpec((1, tm, w), lambda b, i, e: (b, i, 0))
    return pl.pallas_call(
        functools.partial(_moe_kernel, tm=tm),
        grid=(BATCH, rows // tm, N_EXPERTS),
        in_specs=[tile(D_MODEL), tile(LANES), tile(D_MODEL),
                  pl.BlockSpec((1, N_MOD, D_MODEL), lambda b, i, e: (b, 0, 0)),
                  pl.BlockSpec((1, N_MOD, D_MODEL), lambda b, i, e: (BATCH, 0, 0)),
                  pl.BlockSpec((1, D_MODEL, 2 * EXPERT_FF), lambda b, i, e: (e, 0, 0)),
                  pl.BlockSpec((1, EXPERT_FF, D_MODEL), lambda b, i, e: (e, 0, 0))],
        out_specs=tile(D_MODEL),
        out_shape=jax.ShapeDtypeStruct((BATCH, rows, D_MODEL), F32),
        scratch_shapes=[pltpu.VMEM((tm, D_MODEL), F32)],
        compiler_params=_params("parallel", "parallel", "arbitrary"),
        name="moe",
    )(h2, comb, xs, mods, mods, w_gate_up, w_down)


def _split_w_in(w_in):
    bw = BRANCH_WIDTH
    sizes = (3 * bw, bw, 2 * GDN_HEADS, 2 * GDN_HEADS, bw, bw, bw, bw, bw, 3 * D_MODEL)
    offs = [0]
    for s in sizes:
        offs.append(offs[-1] + s)
    part = lambda i: w_in[:, offs[i]:offs[i + 1]]
    main = jnp.concatenate([part(9), part(0), part(1), part(4), part(5), part(6), part(7), part(8)], axis=1)
    ba = jnp.zeros((D_MODEL, LANES), F32).at[:, :4 * GDN_HEADS].set(jnp.concatenate([part(2), part(3)], axis=1))
    return main.astype(BF16), ba.astype(BF16)


def kernel(x, c, ctx, c_ctx, w_mod, b_mod, norm1_gain, norm2_gain, w_in, gdn_conv_w, gdn_a_log, gdn_dt_bias, gdn_out_gain, diff_q_gain, diff_k_gain, diff_lambda, diff_out_gain, lru_conv_w, lru_conv_b, lru_w_gate, lru_b_gate, lru_lambda, w_branch, w_out, w_router, b_router, w_gate_up, w_down):
    mods = _mods(c, c_ctx, w_mod, b_mod)
    cos, sin = _rope_tables()
    xs = jnp.concatenate([x, ctx], axis=1)
    for layer in range(DEPTH):
        last = layer == DEPTH - 1
        lam_init = 0.8 - 0.6 * math.exp(-0.3 * layer)
        m = mods[layer]
        w_main, w_ba = _split_w_in(w_in[layer])
        p, ba = _project(xs, m, norm1_gain[layer], w_main, w_ba)
        qkv = _gdn_conv(p, gdn_conv_w[layer])
        ya = _gdn(qkv, p, ba, gdn_a_log[layer], gdn_dt_bias[layer], gdn_out_gain[layer])
        yb = _diff_attn(p, cos, sin, diff_q_gain[layer], diff_k_gain[layer], diff_lambda[layer],
                        diff_out_gain[layer], lam_init, with_ctx=not last)
        xc = _lru_conv(p, lru_conv_w[layer], lru_conv_b[layer])
        yc = _lru(xc, p, lru_w_gate[layer], lru_b_gate[layer], lru_lambda[layer])
        rows, tm = (SEQ, 512) if last else (TOK, 768)
        xs, h2, comb = _merge(ya, yb, yc, p, xs, m, norm2_gain[layer], w_branch[layer].astype(BF16),
                              w_out[layer].astype(BF16), w_router, b_router, rows, tm)
        xs = _moe(h2, comb, xs, m, w_gate_up[layer].astype(BF16), w_down[layer].astype(BF16), rows, tm)
    return xs
```

```python
import functools
import math

import jax
import jax.numpy as jnp
from jax import lax
from jax.experimental import pallas as pl
from jax.experimental.pallas import tpu as pltpu

F32 = jnp.float32
BF16 = jnp.bfloat16

D_MODEL = 1024
BATCH = 8
SEQ = 2048
DEPTH = 2
GRID_W = 64
CTX_LEN = 256
TOK = SEQ + CTX_LEN
N_MOD = 6
EPS = 1e-6
CONV_WIDTH = 4
BRANCH_WIDTH = 512
GDN_HEADS = 4
GDN_HEAD_DIM = 128
GDN_CHUNK = 64
DIFF_HEADS = 4
DIFF_HEAD_DIM = 64
ROPE_BASE = 10000.0
ROPE_PAIRS = DIFF_HEAD_DIM // 4
LRU_BLOCKS = 8
LRU_BLOCK_DIM = BRANCH_WIDTH // LRU_BLOCKS
LRU_C = 8.0
N_EXPERTS = 16
N_GROUPS = 4
EXPERT_FF = 512

LANES = 128
VMEM_LIMIT = 56 * 1024 * 1024

COL_GATES = 0
COL_QKV = 3 * D_MODEL
COL_Z = COL_QKV + 3 * BRANCH_WIDTH
COL_DQ = COL_Z + BRANCH_WIDTH
COL_DK = COL_DQ + BRANCH_WIDTH
COL_DV = COL_DK + BRANCH_WIDTH
COL_LX = COL_DV + BRANCH_WIDTH
COL_LY = COL_LX + BRANCH_WIDTH
PROJ_COLS = COL_LY + BRANCH_WIDTH


def _params(*sem):
    return pltpu.CompilerParams(dimension_semantics=sem, vmem_limit_bytes=VMEM_LIMIT)


def _dot(a, b, precision=None):
    return jnp.dot(a, b, preferred_element_type=F32, precision=precision)


def _dot_nt(a, b):
    return lax.dot_general(a, b, (((1,), (1,)), ((), ())), preferred_element_type=F32)


def _silu(x):
    return x * jax.nn.sigmoid(x)


def _softplus(x):
    return jnp.maximum(x, 0.0) + jnp.log(1.0 + jnp.exp(-jnp.abs(x)))


def _rms(x, gain):
    return x * lax.rsqrt(jnp.mean(x * x, axis=-1, keepdims=True) + EPS) * gain


def _mod_kernel(c_ref, w_ref, b_ref, o_ref):
    c = c_ref[...]
    o_ref[0] = _dot(_silu(c), w_ref[0], precision=lax.Precision.HIGHEST) + b_ref[0]


def _mods(c, c_ctx, w_mod, b_mod):
    depth = w_mod.shape[0]
    rows = 16
    cc = jnp.zeros((rows, D_MODEL), F32).at[:BATCH].set(c).at[BATCH].set(c_ctx)
    tn = 1536
    out = pl.pallas_call(
        _mod_kernel,
        grid=(depth, N_MOD * D_MODEL // tn),
        in_specs=[pl.BlockSpec((rows, D_MODEL), lambda l, j: (0, 0)),
                  pl.BlockSpec((1, D_MODEL, tn), lambda l, j: (l, 0, j)),
                  pl.BlockSpec((1, 1, tn), lambda l, j: (l, 0, j))],
        out_specs=pl.BlockSpec((1, rows, tn), lambda l, j: (l, 0, j)),
        out_shape=jax.ShapeDtypeStruct((depth, rows, N_MOD * D_MODEL), F32),
        compiler_params=_params("parallel", "parallel"),
        name="mods",
    )(cc, w_mod, b_mod.reshape(depth, 1, N_MOD * D_MODEL))
    return out.reshape(depth, rows, N_MOD, D_MODEL)


def _modulated_norm(x, gain, ml_ref, mc_ref, is_ctx, shift_idx):
    shift = jnp.where(is_ctx, mc_ref[0, shift_idx:shift_idx + 1, :], ml_ref[0, shift_idx:shift_idx + 1, :])
    scale = jnp.where(is_ctx, mc_ref[0, shift_idx + 1:shift_idx + 2, :], ml_ref[0, shift_idx + 1:shift_idx + 2, :])
    return _rms(x, gain) * (1.0 + scale) + shift


def _ctx_rows(tile, tm, width):
    row = tile * tm + lax.broadcasted_iota(jnp.int32, (tm, width), 0)
    return row >= SEQ


def _proj_kernel(x_ref, ml_ref, mc_ref, g_ref, w_ref, wba_ref, p_ref, ba_ref, h_scr, *, tm):
    i = pl.program_id(1)
    j = pl.program_id(2)

    @pl.when(j == 0)
    def _():
        is_ctx = _ctx_rows(i, tm, D_MODEL)
        h_scr[...] = _modulated_norm(x_ref[0], g_ref[...], ml_ref, mc_ref, is_ctx, 0).astype(BF16)

    h = h_scr[...]
    p_ref[0] = _dot(h, w_ref[...]).astype(BF16)
    ba_ref[0] = _dot(h, wba_ref[...])


def _project(xs, mods, gain, w_main, w_ba):
    tm, tn = 1152, 1280
    return pl.pallas_call(
        functools.partial(_proj_kernel, tm=tm),
        grid=(BATCH, TOK // tm, PROJ_COLS // tn),
        in_specs=[pl.BlockSpec((1, tm, D_MODEL), lambda b, i, j: (b, i, 0)),
                  pl.BlockSpec((1, N_MOD, D_MODEL), lambda b, i, j: (b, 0, 0)),
                  pl.BlockSpec((1, N_MOD, D_MODEL), lambda b, i, j: (BATCH, 0, 0)),
                  pl.BlockSpec((1, D_MODEL), lambda b, i, j: (0, 0)),
                  pl.BlockSpec((D_MODEL, tn), lambda b, i, j: (0, j)),
                  pl.BlockSpec((D_MODEL, LANES), lambda b, i, j: (0, 0))],
        out_specs=[pl.BlockSpec((1, tm, tn), lambda b, i, j: (b, i, j)),
                   pl.BlockSpec((1, tm, LANES), lambda b, i, j: (b, i, 0))],
        out_shape=[jax.ShapeDtypeStruct((BATCH, TOK, PROJ_COLS), BF16),
                   jax.ShapeDtypeStruct((BATCH, TOK, LANES), F32)],
        scratch_shapes=[pltpu.VMEM((tm, D_MODEL), BF16)],
        compiler_params=_params("parallel", "parallel", "arbitrary"),
        name="proj",
    )(xs, mods, mods, gain.reshape(1, D_MODEL), w_main, w_ba)


def _conv(x, w):
    n, c = x.shape
    t = lax.broadcasted_iota(jnp.int32, (n, c), 0)
    is_ctx = t >= SEQ
    local = jnp.where(is_ctx, t - SEQ, t)
    seg_len = jnp.where(is_ctx, CTX_LEN, SEQ)
    y = jnp.zeros_like(x)
    for j in range(CONV_WIDTH):
        s = j - CONV_WIDTH // 2
        if s == 0:
            y = y + x * w[j:j + 1, :]
        else:
            shifted = pltpu.roll(x, (-s) % n, 0)
            ok = jnp.logical_and(local + s >= 0, local + s < seg_len)
            y = y + jnp.where(ok, shifted, 0.0) * w[j:j + 1, :]
    return y


def _gdn_conv_kernel(x_ref, w_ref, o_ref):
    j = pl.program_id(1)
    y = _silu(_conv(x_ref[0].astype(F32), w_ref[...]))
    nrm = lax.rsqrt(jnp.sum(y * y, axis=-1, keepdims=True) + EPS)
    n_head_blocks = GDN_HEADS
    scale = jnp.where(j < n_head_blocks, nrm * GDN_HEAD_DIM ** -0.5, jnp.where(j < 2 * n_head_blocks, nrm, 1.0))
    o_ref[0] = (y * scale).astype(BF16)


def _gdn_conv(p, conv_w):
    nblk = 3 * BRANCH_WIDTH // LANES
    first = COL_QKV // LANES
    return pl.pallas_call(
        _gdn_conv_kernel,
        grid=(BATCH, nblk),
        in_specs=[pl.BlockSpec((1, TOK, LANES), lambda b, j: (b, 0, first + j)),
                  pl.BlockSpec((CONV_WIDTH, LANES), lambda b, j: (0, j))],
        out_specs=pl.BlockSpec((1, TOK, LANES), lambda b, j: (b, 0, j)),
        out_shape=jax.ShapeDtypeStruct((BATCH, TOK, 3 * BRANCH_WIDTH), BF16),
        compiler_params=_params("parallel", "parallel"),
        name="gdn_conv",
    )(p, conv_w)


def _lru_conv_kernel(x_ref, w_ref, b_ref, o_ref):
    o_ref[0] = _conv(x_ref[0].astype(F32), w_ref[...]) + b_ref[...]


def _lru_conv(p, conv_w, conv_b):
    nblk = BRANCH_WIDTH // LANES
    first = COL_LX // LANES
    return pl.pallas_call(
        _lru_conv_kernel,
        grid=(BATCH, nblk),
        in_specs=[pl.BlockSpec((1, TOK, LANES), lambda b, j: (b, 0, first + j)),
                  pl.BlockSpec((CONV_WIDTH, LANES), lambda b, j: (0, j)),
                  pl.BlockSpec((1, LANES), lambda b, j: (0, j))],
        out_specs=pl.BlockSpec((1, TOK, LANES), lambda b, j: (b, 0, j)),
        out_shape=jax.ShapeDtypeStruct((BATCH, TOK, BRANCH_WIDTH), F32),
        compiler_params=_params("parallel", "parallel"),
        name="lru_conv",
    )(p, conv_w, conv_b.reshape(1, BRANCH_WIDTH))


GDN_QM_ROWS = GDN_CHUNK + GDN_HEAD_DIM
GDN_GL_ROWS = 8
GDN_PREP_UNROLL = 9


def _gdn_prepare(q_ref, k_ref, v_ref, gb_scr, qm_scr, nn_scr, o_scr, gl_scr, chunks, head):
    c = GDN_CHUNK
    lane = lax.broadcasted_iota(jnp.int32, (c, LANES), 1)
    row = lax.broadcasted_iota(jnp.int32, (c, LANES), 0)
    ii = lax.broadcasted_iota(jnp.int32, (c, c), 0)
    jj = lax.broadcasted_iota(jnp.int32, (c, c), 1)
    eye = (ii == jj).astype(F32)
    masks = ((ii >= jj, ii > jj, row > lane), (ii <= jj, ii < jj, row < lane))

    loaded = []
    for chunk in chunks:
        r0 = pl.multiple_of(chunk * c, c)
        k = k_ref[0, pl.ds(r0, c), :]
        q = q_ref[0, pl.ds(r0, c), :]
        kq = _dot_nt(jnp.concatenate([k, q], axis=0), k)
        loaded.append((chunk, r0, q, k, kq))

    chains = []
    for chunk, r0, q, k, kq in loaded:
        gb = gb_scr[pl.ds(r0, c), :]
        for d in range(2):
            col = head + d * GDN_HEADS
            beta = jnp.sum(jnp.where(lane == col, gb, 0.0), axis=-1, keepdims=True)
            g = jnp.sum(jnp.where(lane == col + 2 * GDN_HEADS, gb, 0.0), axis=-1, keepdims=True)
            incl, strict, strict_wide = masks[d]
            rhs = jnp.where(lane >= c, g, jnp.where(strict_wide, g, 0.0))
            e = _dot(incl.astype(F32), rhs, precision=lax.Precision.HIGHEST)
            chains.append(dict(chunk=chunk, r0=r0, d=d, q=q, k=k, kq=kq, beta=beta, e=e))

    for ch in chains:
        incl, strict, _ = masks[ch["d"]]
        e = ch["e"]
        decay = jnp.where(incl, jnp.exp(e[:, :c]), 0.0)
        gc = e[:, c:c + 1]
        last = 0 if ch["d"] == 1 else c - 1
        gc_last = e[last:last + 1, c:c + 1]
        ch.update(decay=decay, gc=gc, gc_last=gc_last, egc=jnp.exp(gc))
        ch["a"] = jnp.where(strict, ch["beta"] * ch["kq"][:c] * decay, 0.0)
        ch["t"] = eye
    s = 1
    while s < c:
        pair = jnp.logical_and((ii // (2 * s)) == (jj // (2 * s)), (ii // s) != (jj // s))
        for ch in chains:
            ch["a_off"] = jnp.where(pair, ch["a"], 0.0)
        if s == 1:
            for ch in chains:
                ch["t"] = eye - ch["a_off"]
        else:
            for ch in chains:
                ch["m"] = _dot(ch["t"].astype(BF16), ch["a_off"].astype(BF16))
            for ch in chains:
                ch["t"] = ch["t"] - _dot(ch["m"].astype(BF16), ch["t"].astype(BF16))
        s *= 2
    for ch in chains:
        r0, beta, egc = ch["r0"], ch["beta"], ch["egc"]
        kf = ch["k"].astype(F32)
        vf = v_ref[0, pl.ds(r0, c), :].astype(F32)
        rhs2 = jnp.concatenate([vf * beta, kf * (beta * egc)], axis=1).astype(BF16)
        ch["uw"] = _dot(ch["t"].astype(BF16), rhs2).astype(BF16)
        ch["k_dec_t"] = (kf * jnp.exp(ch["gc_last"] - ch["gc"])).T.astype(BF16)
    for ch in chains:
        incl = masks[ch["d"]][0]
        qk = jnp.where(incl, ch["kq"][c:] * ch["decay"], 0.0).astype(BF16)
        ch["nm"] = _dot(ch["k_dec_t"], ch["uw"])
        ch["ow"] = _dot(qk, ch["uw"])
    for ch in chains:
        chunk, r0, d, nm, ow = ch["chunk"], ch["r0"], ch["d"], ch["nm"], ch["ow"]
        q0 = pl.multiple_of(chunk * GDN_QM_ROWS, 16)
        qm_scr[d, pl.ds(q0, c), :] = (ch["q"].astype(F32) * ch["egc"] - ow[:, GDN_HEAD_DIM:]).astype(BF16)
        qm_scr[d, pl.ds(q0 + c, GDN_HEAD_DIM), :] = nm[:, GDN_HEAD_DIM:].astype(BF16)
        nn_scr[d, pl.ds(pl.multiple_of(chunk * GDN_HEAD_DIM, GDN_HEAD_DIM), GDN_HEAD_DIM), :] = nm[:, :GDN_HEAD_DIM]
        o_scr[d, pl.ds(r0, c), :] = ow[:, :GDN_HEAD_DIM]
        gl_scr[d, pl.ds(pl.multiple_of(chunk * GDN_GL_ROWS, GDN_GL_ROWS), GDN_GL_ROWS), :] = jnp.broadcast_to(
            jnp.exp(ch["gc_last"]), (GDN_GL_ROWS, LANES))


def _gdn_advance(qm_scr, nn_scr, o_scr, gl_scr, d, chunk, state):
    c = GDN_CHUNK
    qm = qm_scr[d, pl.ds(pl.multiple_of(chunk * GDN_QM_ROWS, 16), GDN_QM_ROWS), :]
    r = _dot(qm, state.astype(BF16))
    rows = pl.ds(pl.multiple_of(chunk * c, c), c)
    o_scr[d, rows, :] = o_scr[d, rows, :] + r[:c]
    gl = gl_scr[d, pl.ds(pl.multiple_of(chunk * GDN_GL_ROWS, GDN_GL_ROWS), 1), :]
    n = nn_scr[d, pl.ds(pl.multiple_of(chunk * GDN_HEAD_DIM, GDN_HEAD_DIM), GDN_HEAD_DIM), :]
    return state * gl - r[c:] + n


def _gdn_kernel(q_ref, k_ref, v_ref, z_ref, ba_ref, alog_ref, dtb_ref, gain_ref, o_ref,
                gb_scr, qm_scr, nn_scr, o_scr, gl_scr):
    head = pl.program_id(1)
    c = GDN_CHUNK
    n_lat, n_ctx = SEQ // c, CTX_LEN // c
    n_chunks = n_lat + n_ctx
    ba = ba_ref[0]
    lane = lax.broadcasted_iota(jnp.int32, ba.shape, 1)
    g_all = -jnp.exp(alog_ref[...]) * _softplus(ba + dtb_ref[...])
    gb_scr[...] = jnp.where(lane < 2 * GDN_HEADS, jax.nn.sigmoid(ba), g_all)

    def prepare(i, _):
        chunks = [i * GDN_PREP_UNROLL + j for j in range(GDN_PREP_UNROLL)]
        _gdn_prepare(q_ref, k_ref, v_ref, gb_scr, qm_scr, nn_scr, o_scr, gl_scr, chunks, head)
        return 0

    lax.fori_loop(0, n_chunks // GDN_PREP_UNROLL, prepare, 0)

    def advance(i, carry):
        s_f, s_b = carry
        cf = jnp.where(i < n_ctx, n_lat + i, i - n_ctx)
        cb = n_chunks - 1 - i
        s_f = _gdn_advance(qm_scr, nn_scr, o_scr, gl_scr, 0, cf, s_f)
        s_b = _gdn_advance(qm_scr, nn_scr, o_scr, gl_scr, 1, cb, s_b)
        return s_f, s_b

    zero = jnp.zeros((GDN_HEAD_DIM, GDN_HEAD_DIM), F32)
    lax.fori_loop(0, n_chunks, advance, (zero, zero))
    o = o_scr[0] + o_scr[1]
    o_ref[0] = (_rms(o, gain_ref[...]) * _silu(z_ref[0].astype(F32))).astype(BF16)


def _gdn(qkv, p, ba, a_log, dt_bias, out_gain):
    def pad_lanes(vals):
        row = jnp.zeros((LANES,), F32).at[2 * GDN_HEADS:4 * GDN_HEADS].set(vals.reshape(-1))
        return row.reshape(1, LANES)

    nh = GDN_HEADS
    n_chunks = TOK // GDN_CHUNK
    zblk = COL_Z // LANES
    blk = lambda off: pl.BlockSpec((1, TOK, LANES), lambda b, h: (b, 0, off + h))
    vec = pl.BlockSpec((1, LANES), lambda b, h: (0, 0))
    return pl.pallas_call(
        _gdn_kernel,
        grid=(BATCH, nh),
        in_specs=[blk(0), blk(nh), blk(2 * nh), blk(zblk),
                  pl.BlockSpec((1, TOK, LANES), lambda b, h: (b, 0, 0)), vec, vec, vec],
        out_specs=pl.BlockSpec((1, TOK, LANES), lambda b, h: (b, 0, h)),
        out_shape=jax.ShapeDtypeStruct((BATCH, TOK, BRANCH_WIDTH), BF16),
        scratch_shapes=[pltpu.VMEM((TOK, LANES), F32),
                        pltpu.VMEM((2, n_chunks * GDN_QM_ROWS, LANES), BF16),
                        pltpu.VMEM((2, n_chunks * GDN_HEAD_DIM, LANES), F32),
                        pltpu.VMEM((2, TOK, LANES), F32),
                        pltpu.VMEM((2, n_chunks * GDN_GL_ROWS, LANES), F32)],
        compiler_params=_params("parallel", "parallel"),
        name="gdn",
    )(qkv, qkv, qkv, p, ba, pad_lanes(a_log), pad_lanes(dt_bias), out_gain.reshape(1, LANES))


def _rms_halves(x, gain):
    lane = lax.broadcasted_iota(jnp.int32, x.shape, 1)
    lo = lane < DIFF_HEAD_DIM
    x2 = x * x
    s_lo = jnp.sum(jnp.where(lo, x2, 0.0), axis=-1, keepdims=True)
    s_hi = jnp.sum(jnp.where(lo, 0.0, x2), axis=-1, keepdims=True)
    ms = jnp.where(lo, s_lo, s_hi) * (1.0 / DIFF_HEAD_DIM)
    return x * lax.rsqrt(ms + EPS) * gain


def _rope(x, cos, sin):
    lane = lax.broadcasted_iota(jnp.int32, x.shape, 1)
    first = (lane & ROPE_PAIRS) == 0
    partner = jnp.where(first, -pltpu.roll(x, LANES - ROPE_PAIRS, 1), pltpu.roll(x, ROPE_PAIRS, 1))
    return x * cos + partner * sin


def _attn_kernel(q_ref, k_ref, v_ref, cosk_ref, sink_ref, cosq_ref, sinq_ref, qg_ref, kg_ref, lv_ref, og_ref,
                 o_ref, kn_scr, *, ctx_block, lam_init):
    qi = pl.program_id(2)

    @pl.when(qi == 0)
    def _():
        kn = _rope(_rms_halves(k_ref[0].astype(F32), kg_ref[...]), cosk_ref[...], sink_ref[...])
        kn_scr[...] = kn.astype(BF16)

    lv = lv_ref[...]
    lam = (jnp.exp(jnp.sum(lv[0:1] * lv[1:2], axis=-1, keepdims=True))
           - jnp.exp(jnp.sum(lv[2:3] * lv[3:4], axis=-1, keepdims=True)) + lam_init)
    q = _rope(_rms_halves(q_ref[0].astype(F32), qg_ref[...]), cosq_ref[...], sinq_ref[...])
    q = q * (DIFF_HEAD_DIM ** -0.5 * math.log2(math.e))
    lane = lax.broadcasted_iota(jnp.int32, q.shape, 1)
    lo = lane < DIFF_HEAD_DIM
    q1 = jnp.where(lo, q, 0.0).astype(BF16)
    q2 = jnp.where(lo, 0.0, q).astype(BF16)

    def attend(kn, v):
        def half(s):
            p = jnp.exp2(s - jnp.max(s, axis=-1, keepdims=True))
            return _dot(p.astype(BF16), v), jnp.sum(p, axis=-1, keepdims=True)
        s1 = _dot_nt(q1, kn)
        s2 = _dot_nt(q2, kn)
        a1, l1 = half(s1)
        a2, l2 = half(s2)
        o = a1 * (1.0 / l1) - a2 * (lam / l2)
        o_ref[0] = (_rms(o, og_ref[...]) * (1.0 - lam_init)).astype(BF16)

    if ctx_block is None:
        attend(kn_scr[...], v_ref[0])
    else:
        @pl.when(qi == ctx_block)
        def _():
            attend(kn_scr[SEQ:, :], v_ref[0, SEQ:, :])

        @pl.when(qi != ctx_block)
        def _():
            attend(kn_scr[...], v_ref[0])


def _rope_tables():
    n_rows = SEQ // GRID_W
    row_id = jnp.broadcast_to(jnp.arange(n_rows, dtype=F32)[:, None], (n_rows, GRID_W)).reshape(-1)
    col_id = jnp.broadcast_to(jnp.arange(GRID_W, dtype=F32)[None, :], (n_rows, GRID_W)).reshape(-1)
    inv_freq = jnp.power(ROPE_BASE, -jnp.arange(ROPE_PAIRS, dtype=F32) / ROPE_PAIRS)
    row_ang = row_id[:, None] * inv_freq
    col_ang = col_id[:, None] * inv_freq
    ang = jnp.concatenate([row_ang, row_ang, col_ang, col_ang], axis=-1)
    ang = jnp.concatenate([ang, ang], axis=-1)
    pad = ((0, CTX_LEN), (0, 0))
    return jnp.pad(jnp.cos(ang), pad, constant_values=1.0), jnp.pad(jnp.sin(ang), pad)


def _diff_attn(p, cos, sin, q_gain, k_gain, lam_vecs, out_gain, lam_init, with_ctx):
    tq = CTX_LEN
    nq = TOK // tq if with_ctx else SEQ // tq
    nh = DIFF_HEADS
    qb, kb, vb = COL_DQ // LANES, COL_DK // LANES, COL_DV // LANES
    full = lambda off: pl.BlockSpec((1, TOK, LANES), lambda b, h, i: (b, 0, off + h))
    tab_full = pl.BlockSpec((TOK, LANES), lambda b, h, i: (0, 0))
    tab_q = pl.BlockSpec((tq, LANES), lambda b, h, i: (i, 0))
    vec = pl.BlockSpec((1, LANES), lambda b, h, i: (0, 0))
    tile2 = lambda g: jnp.concatenate([g, g]).reshape(1, LANES)
    return pl.pallas_call(
        functools.partial(_attn_kernel, ctx_block=SEQ // tq if with_ctx else None, lam_init=lam_init),
        grid=(BATCH, nh, nq),
        in_specs=[pl.BlockSpec((1, tq, LANES), lambda b, h, i: (b, i, qb + h)), full(kb), full(vb),
                  tab_full, tab_full, tab_q, tab_q, vec, vec,
                  pl.BlockSpec((4, DIFF_HEAD_DIM), lambda b, h, i: (0, 0)), vec],
        out_specs=pl.BlockSpec((1, tq, LANES), lambda b, h, i: (b, i, h)),
        out_shape=jax.ShapeDtypeStruct((BATCH, nq * tq, BRANCH_WIDTH), BF16),
        scratch_shapes=[pltpu.VMEM((TOK, LANES), BF16)],
        compiler_params=_params("parallel", "parallel", "arbitrary"),
        name="diff_attn",
    )(p, p, p, cos, sin, cos, sin, tile2(q_gain), tile2(k_gain), lam_vecs, out_gain.reshape(1, LANES))


LRU_SLAB = 256
LRU_ROWS = 256
LRU_SCAN_BLOCK = 8


def _lru_kernel(xc_ref, y_ref, wg_ref, bg_ref, lam_ref, o_ref, af_scr, bf_scr, ab_scr, bb_scr):
    w = LRU_SLAB
    blk = LRU_SCAN_BLOCK
    sp = _softplus(-lam_ref[0])
    sub = lax.broadcasted_iota(jnp.int32, (LRU_ROWS, w), 0) % blk

    def gates(i, _):
        r0 = pl.multiple_of(i * LRU_ROWS, LRU_ROWS)
        xc = xc_ref[0, pl.ds(r0, LRU_ROWS), :]
        pre = _dot(xc.astype(BF16), wg_ref[0]) + bg_ref[0]
        for d, (a_scr, b_scr) in enumerate(((af_scr, bf_scr), (ab_scr, bb_scr))):
            r = jax.nn.sigmoid(pre[:, (2 * d) * w:(2 * d + 1) * w])
            gi = jax.nn.sigmoid(pre[:, (2 * d + 1) * w:(2 * d + 2) * w])
            log_a = -LRU_C * r * sp[d:d + 1]
            a = jnp.exp(log_a)
            b = jnp.sqrt(1.0 - a * a) * gi * xc
            shift = 1
            while shift < blk:
                if d == 0:
                    ok, roll_by = sub >= shift, shift
                else:
                    ok, roll_by = sub < blk - shift, LRU_ROWS - shift
                b = jnp.where(ok, a * pltpu.roll(b, roll_by, 0) + b, b)
                a = jnp.where(ok, a * pltpu.roll(a, roll_by, 0), a)
                shift *= 2
            a_scr[pl.ds(r0, LRU_ROWS), :] = a
            b_scr[pl.ds(r0, LRU_ROWS), :] = b
        return 0

    lax.fori_loop(0, TOK // LRU_ROWS, gates, 0)

    n_blk, n_lat_blk, n_ctx_blk = TOK // blk, SEQ // blk, CTX_LEN // blk

    def step(s, carry):
        h_f, h_b = carry
        rows_f = pl.ds(pl.multiple_of(jnp.where(s < n_ctx_blk, n_lat_blk + s, s - n_ctx_blk) * blk, blk), blk)
        rows_b = pl.ds(pl.multiple_of((n_blk - 1 - s) * blk, blk), blk)
        hf = af_scr[rows_f, :] * h_f + bf_scr[rows_f, :]
        bf_scr[rows_f, :] = hf
        hb = ab_scr[rows_b, :] * h_b + bb_scr[rows_b, :]
        bb_scr[rows_b, :] = hb
        return hf[blk - 1:blk, :], hb[0:1, :]

    zero = jnp.zeros((1, w), F32)
    lax.fori_loop(0, n_blk, step, (zero, zero), unroll=4)
    h = bf_scr[...] + bb_scr[...]
    o_ref[0] = (h * jax.nn.gelu(y_ref[0].astype(F32))).astype(BF16)


def _lru_gate_weights(w_gate, b_gate):
    n_slab = BRANCH_WIDTH // LRU_SLAB
    per = LRU_SLAB // LRU_BLOCK_DIM
    wg = w_gate.reshape(2, 2, n_slab, per, LRU_BLOCK_DIM, LRU_BLOCK_DIM)
    eye = jnp.eye(per, dtype=w_gate.dtype)
    dense = jnp.einsum('dgsnjk,nm->snjdgmk', wg, eye)
    dense = dense.reshape(n_slab, LRU_SLAB, 4 * LRU_SLAB)
    bg = b_gate.reshape(2, 2, n_slab, LRU_SLAB).transpose(2, 0, 1, 3).reshape(n_slab, 1, 4 * LRU_SLAB)
    return dense.astype(BF16), bg


def _lru(xc, p, w_gate, b_gate, lam):
    n_slab = BRANCH_WIDTH // LRU_SLAB
    wg, bg = _lru_gate_weights(w_gate, b_gate)
    lam_s = lam.reshape(2, n_slab, LRU_SLAB).transpose(1, 0, 2)
    yb = COL_LY // LRU_SLAB
    return pl.pallas_call(
        _lru_kernel,
        grid=(BATCH, n_slab),
        in_specs=[pl.BlockSpec((1, TOK, LRU_SLAB), lambda b, s: (b, 0, s)),
                  pl.BlockSpec((1, TOK, LRU_SLAB), lambda b, s: (b, 0, yb + s)),
                  pl.BlockSpec((1, LRU_SLAB, 4 * LRU_SLAB), lambda b, s: (s, 0, 0)),
                  pl.BlockSpec((1, 1, 4 * LRU_SLAB), lambda b, s: (s, 0, 0)),
                  pl.BlockSpec((1, 2, LRU_SLAB), lambda b, s: (s, 0, 0))],
        out_specs=pl.BlockSpec((1, TOK, LRU_SLAB), lambda b, s: (b, 0, s)),
        out_shape=jax.ShapeDtypeStruct((BATCH, TOK, BRANCH_WIDTH), BF16),
        scratch_shapes=[pltpu.VMEM((TOK, LRU_SLAB), F32)] * 4,
        compiler_params=_params("parallel", "parallel"),
        name="lru",
    )(xc, p, wg, bg, lam_s)


def _route(logits):
    lane = lax.broadcasted_iota(jnp.int32, logits.shape, 1)
    lane_f = lane.astype(F32)
    far = float(LANES)
    lg = jnp.where(lane < N_EXPERTS, logits, -jnp.inf)
    ex = jnp.exp(lg - jnp.max(lg, axis=-1, keepdims=True))
    probs = ex / jnp.sum(ex, axis=-1, keepdims=True)
    per_group = N_EXPERTS // N_GROUPS
    grp = lane // per_group

    def top2(vals):
        m1 = jnp.max(vals, axis=-1, keepdims=True)
        i1 = jnp.min(jnp.where(vals == m1, lane_f, far), axis=-1, keepdims=True)
        rest = jnp.where(lane_f == i1, -2.0, vals)
        m2 = jnp.max(rest, axis=-1, keepdims=True)
        i2 = jnp.min(jnp.where(rest == m2, lane_f, far), axis=-1, keepdims=True)
        return m1, i1, m2, i2

    best = jnp.zeros(logits.shape[:1] + (1,), jnp.int32)
    best_score = None
    for g in range(N_GROUPS):
        m1, _, m2, _ = top2(jnp.where(grp == g, probs, -1.0))
        score = m1 + m2
        if best_score is None:
            best_score = score
        else:
            better = score > best_score
            best = jnp.where(better, g, best)
            best_score = jnp.where(better, score, best_score)
    m1, i1, m2, i2 = top2(jnp.where(grp == best, probs, -1.0))
    den = m1 + m2
    return jnp.where(lane_f == i1, m1 / den, 0.0) + jnp.where(lane_f == i2, m2 / den, 0.0)


def _merge_kernel(ya_ref, yb_ref, yc_ref, gates_ref, x_ref, ml_ref, mc_ref, g2_ref, wbr_ref, wout_ref, wr_ref,
                  br_ref, xo_ref, h2_ref, comb_ref, *, tm):
    i = pl.program_id(1)
    acc = None
    for n, y_ref in enumerate((ya_ref, yb_ref, yc_ref)):
        yn = _dot(y_ref[0], wbr_ref[n])
        gate = jax.nn.sigmoid(gates_ref[0, :, n * D_MODEL:(n + 1) * D_MODEL].astype(F32))
        acc = gate * yn if acc is None else acc + gate * yn
    out = _dot(acc.astype(BF16), wout_ref[...])
    is_ctx = _ctx_rows(i, tm, D_MODEL)
    xn = x_ref[0] + jnp.where(is_ctx, mc_ref[0, 2:3, :], ml_ref[0, 2:3, :]) * out
    xo_ref[0] = xn
    h2 = _modulated_norm(xn, g2_ref[...], ml_ref, mc_ref, is_ctx, 3).astype(BF16)
    h2_ref[0] = h2
    comb_ref[0] = _route(_dot(h2, wr_ref[...]) + br_ref[...])


def _merge(ya, yb, yc, p, xs, mods, gain2, w_branch, w_out, w_router, b_router, rows, tm):
    wr = jnp.zeros((D_MODEL, LANES), BF16).at[:, :N_EXPERTS].set(w_router.astype(BF16))
    br = jnp.zeros((1, LANES), F32).at[0, :N_EXPERTS].set(b_router)
    tile = lambda w: pl.BlockSpec((1, tm, w), lambda b, i: (b, i, 0))
    const = lambda shape: pl.BlockSpec(shape, lambda b, i: (0,) * len(shape))
    return pl.pallas_call(
        functools.partial(_merge_kernel, tm=tm),
        grid=(BATCH, rows // tm),
        in_specs=[tile(BRANCH_WIDTH), tile(BRANCH_WIDTH), tile(BRANCH_WIDTH), tile(3 * D_MODEL), tile(D_MODEL),
                  pl.BlockSpec((1, N_MOD, D_MODEL), lambda b, i: (b, 0, 0)),
                  pl.BlockSpec((1, N_MOD, D_MODEL), lambda b, i: (BATCH, 0, 0)),
                  const((1, D_MODEL)), const((3, BRANCH_WIDTH, D_MODEL)), const((D_MODEL, D_MODEL)),
                  const((D_MODEL, LANES)), const((1, LANES))],
        out_specs=[tile(D_MODEL), tile(D_MODEL), tile(LANES)],
        out_shape=[jax.ShapeDtypeStruct((BATCH, rows, D_MODEL), F32),
                   jax.ShapeDtypeStruct((BATCH, rows, D_MODEL), BF16),
                   jax.ShapeDtypeStruct((BATCH, rows, LANES), F32)],
        compiler_params=_params("parallel", "parallel"),
        name="merge",
    )(ya, yb, yc, p, xs, mods, mods, gain2.reshape(1, D_MODEL), w_branch, w_out, wr, br)


def _moe_kernel(h_ref, comb_ref, x_ref, ml_ref, mc_ref, wgu_ref, wd_ref, o_ref, acc_scr, *, tm):
    i = pl.program_id(1)
    e = pl.program_id(2)

    @pl.when(e == 0)
    def _():
        acc_scr[...] = jnp.zeros_like(acc_scr)

    gu = _dot(h_ref[0], wgu_ref[0])
    act = (_silu(gu[:, :EXPERT_FF]) * gu[:, EXPERT_FF:]).astype(BF16)
    y = _dot(act, wd_ref[0])
    comb = comb_ref[0]
    lane = lax.broadcasted_iota(jnp.int32, comb.shape, 1)
    weight = jnp.sum(jnp.where(lane == e, comb, 0.0), axis=-1, keepdims=True)
    acc_scr[...] += weight * y

    @pl.when(e == N_EXPERTS - 1)
    def _():
        is_ctx = _ctx_rows(i, tm, D_MODEL)
        o_ref[0] = x_ref[0] + jnp.where(is_ctx, mc_ref[0, 5:6, :], ml_ref[0, 5:6, :]) * acc_scr[...]


def _moe(h2, comb, xs, mods, w_gate_up, w_down, rows, tm):
    tile = lambda w: pl.BlockSpec((1, tm, w), lambda b, i, e: (b, i, 0))
    return pl.pallas_call(
        functools.partial(_moe_kernel, tm=tm),
        grid=(BATCH, rows // tm, N_EXPERTS),
        in_specs=[tile(D_MODEL), tile(LANES), tile(D_MODEL),
                  pl.BlockSpec((1, N_MOD, D_MODEL), lambda b, i, e: (b, 0, 0)),
                  pl.BlockSpec((1, N_MOD, D_MODEL), lambda b, i, e: (BATCH, 0, 0)),
                  pl.BlockSpec((1, D_MODEL, 2 * EXPERT_FF), lambda b, i, e: (e, 0, 0)),
                  pl.BlockSpec((1, EXPERT_FF, D_MODEL), lambda b, i, e: (e, 0, 0))],
        out_specs=tile(D_MODEL),
        out_shape=jax.ShapeDtypeStruct((BATCH, rows, D_MODEL), F32),
        scratch_shapes=[pltpu.VMEM((tm, D_MODEL), F32)],
        compiler_params=_params("parallel", "parallel", "arbitrary"),
        name="moe",
    )(h2, comb, xs, mods, mods, w_gate_up, w_down)


def _split_w_in(w_in):
    bw = BRANCH_WIDTH
    sizes = (3 * bw, bw, 2 * GDN_HEADS, 2 * GDN_HEADS, bw, bw, bw, bw, bw, 3 * D_MODEL)
    offs = [0]
    for s in sizes:
        offs.append(offs[-1] + s)
    part = lambda i: w_in[:, offs[i]:offs[i + 1]]
    main = jnp.concatenate([part(9), part(0), part(1), part(4), part(5), part(6), part(7), part(8)], axis=1)
    ba = jnp.zeros((D_MODEL, LANES), F32).at[:, :4 * GDN_HEADS].set(jnp.concatenate([part(2), part(3)], axis=1))
    return main.astype(BF16), ba.astype(BF16)


def kernel(x, c, ctx, c_ctx, w_mod, b_mod, norm1_gain, norm2_gain, w_in, gdn_conv_w, gdn_a_log, gdn_dt_bias, gdn_out_gain, diff_q_gain, diff_k_gain, diff_lambda, diff_out_gain, lru_conv_w, lru_conv_b, lru_w_gate, lru_b_gate, lru_lambda, w_branch, w_out, w_router, b_router, w_gate_up, w_down):
    mods = _mods(c, c_ctx, w_mod, b_mod)
    cos, sin = _rope_tables()
    xs = jnp.concatenate([x, ctx], axis=1)
    for layer in range(DEPTH):
        last = layer == DEPTH - 1
        lam_init = 0.8 - 0.6 * math.exp(-0.3 * layer)
        m = mods[layer]
        w_main, w_ba = _split_w_in(w_in[layer])
        p, ba = _project(xs, m, norm1_gain[layer], w_main, w_ba)
        qkv = _gdn_conv(p, gdn_conv_w[layer])
        ya = _gdn(qkv, p, ba, gdn_a_log[layer], gdn_dt_bias[layer], gdn_out_gain[layer])
        yb = _diff_attn(p, cos, sin, diff_q_gain[layer], diff_k_gain[layer], diff_lambda[layer],
                        diff_out_gain[layer], lam_init, with_ctx=not last)
        xc = _lru_conv(p, lru_conv_w[layer], lru_conv_b[layer])
        yc = _lru(xc, p, lru_w_gate[layer], lru_b_gate[layer], lru_lambda[layer])
        rows, tm = (SEQ, 512) if last else (TOK, 768)
        xs, h2, comb = _merge(ya, yb, yc, p, xs, m, norm2_gain[layer], w_branch[layer].astype(BF16),
                              w_out[layer].astype(BF16), w_router, b_router, rows, tm)
        xs = _moe(h2, comb, xs, m, w_gate_up[layer].astype(BF16), w_down[layer].astype(BF16), rows, tm)
    return xs
```

```python
import functools
import math

import jax
import jax.numpy as jnp
from jax import lax
from jax.experimental import pallas as pl
from jax.experimental.pallas import tpu as pltpu
from jax.experimental.pallas import tpu_sc as plsc

F32 = jnp.float32
BF16 = jnp.bfloat16

D_MODEL = 1024
BATCH = 8
SEQ = 2048
DEPTH = 2
GRID_W = 64
CTX_LEN = 256
TOK = SEQ + CTX_LEN
N_MOD = 6
EPS = 1e-6
CONV_WIDTH = 4
BRANCH_WIDTH = 512
GDN_HEADS = 4
GDN_HEAD_DIM = 128
GDN_CHUNK = 64
DIFF_HEADS = 4
DIFF_HEAD_DIM = 64
ROPE_BASE = 10000.0
ROPE_PAIRS = DIFF_HEAD_DIM // 4
LRU_BLOCKS = 8
LRU_BLOCK_DIM = BRANCH_WIDTH // LRU_BLOCKS
LRU_C = 8.0
N_EXPERTS = 16
N_GROUPS = 4
EXPERT_FF = 512

LANES = 128
VMEM_LIMIT = 56 * 1024 * 1024

COL_GATES = 0
COL_QKV = 3 * D_MODEL
COL_Z = COL_QKV + 3 * BRANCH_WIDTH
COL_DQ = COL_Z + BRANCH_WIDTH
COL_DK = COL_DQ + BRANCH_WIDTH
COL_DV = COL_DK + BRANCH_WIDTH
COL_LX = COL_DV + BRANCH_WIDTH
COL_LY = COL_LX + BRANCH_WIDTH
PROJ_COLS = COL_LY + BRANCH_WIDTH


def _params(*sem):
    return pltpu.CompilerParams(dimension_semantics=sem, vmem_limit_bytes=VMEM_LIMIT)


def _dot(a, b, precision=None):
    return jnp.dot(a, b, preferred_element_type=F32, precision=precision)


def _dot_nt(a, b):
    return lax.dot_general(a, b, (((1,), (1,)), ((), ())), preferred_element_type=F32)


def _silu(x):
    return x * jax.nn.sigmoid(x)


def _softplus(x):
    return jnp.maximum(x, 0.0) + jnp.log(1.0 + jnp.exp(-jnp.abs(x)))


def _rms(x, gain):
    return x * lax.rsqrt(jnp.mean(x * x, axis=-1, keepdims=True) + EPS) * gain


def _mod_kernel(c_ref, w_ref, b_ref, o_ref):
    c = c_ref[...]
    o_ref[0] = _dot(_silu(c), w_ref[0], precision=lax.Precision.HIGHEST) + b_ref[0]


def _mods(c, c_ctx, w_mod, b_mod):
    depth = w_mod.shape[0]
    rows = 16
    cc = jnp.zeros((rows, D_MODEL), F32).at[:BATCH].set(c).at[BATCH].set(c_ctx)
    tn = 1536
    out = pl.pallas_call(
        _mod_kernel,
        grid=(depth, N_MOD * D_MODEL // tn),
        in_specs=[pl.BlockSpec((rows, D_MODEL), lambda l, j: (0, 0)),
                  pl.BlockSpec((1, D_MODEL, tn), lambda l, j: (l, 0, j)),
                  pl.BlockSpec((1, 1, tn), lambda l, j: (l, 0, j))],
        out_specs=pl.BlockSpec((1, rows, tn), lambda l, j: (l, 0, j)),
        out_shape=jax.ShapeDtypeStruct((depth, rows, N_MOD * D_MODEL), F32),
        compiler_params=_params("parallel", "parallel"),
        name="mods",
    )(cc, w_mod, b_mod.reshape(depth, 1, N_MOD * D_MODEL))
    return out.reshape(depth, rows, N_MOD, D_MODEL)


def _modulated_norm(x, gain, ml_ref, mc_ref, is_ctx, shift_idx):
    shift = jnp.where(is_ctx, mc_ref[0, shift_idx:shift_idx + 1, :], ml_ref[0, shift_idx:shift_idx + 1, :])
    scale = jnp.where(is_ctx, mc_ref[0, shift_idx + 1:shift_idx + 2, :], ml_ref[0, shift_idx + 1:shift_idx + 2, :])
    return _rms(x, gain) * (1.0 + scale) + shift


def _ctx_rows(tile, tm, width):
    row = tile * tm + lax.broadcasted_iota(jnp.int32, (tm, width), 0)
    return row >= SEQ


def _proj_kernel(x_ref, ml_ref, mc_ref, g_ref, w_ref, wba_ref, p_ref, ba_ref, h_scr, *, tm):
    i = pl.program_id(1)
    j = pl.program_id(2)

    @pl.when(j == 0)
    def _():
        is_ctx = _ctx_rows(i, tm, D_MODEL)
        h_scr[...] = _modulated_norm(x_ref[0], g_ref[...], ml_ref, mc_ref, is_ctx, 0).astype(BF16)

    h = h_scr[...]
    p_ref[0] = _dot(h, w_ref[...]).astype(BF16)
    ba_ref[0] = _dot(h, wba_ref[...])


def _project(xs, mods, gain, w_main, w_ba):
    tm, tn = 1152, 1280
    return pl.pallas_call(
        functools.partial(_proj_kernel, tm=tm),
        grid=(BATCH, TOK // tm, PROJ_COLS // tn),
        in_specs=[pl.BlockSpec((1, tm, D_MODEL), lambda b, i, j: (b, i, 0)),
                  pl.BlockSpec((1, N_MOD, D_MODEL), lambda b, i, j: (b, 0, 0)),
                  pl.BlockSpec((1, N_MOD, D_MODEL), lambda b, i, j: (BATCH, 0, 0)),
                  pl.BlockSpec((1, D_MODEL), lambda b, i, j: (0, 0)),
                  pl.BlockSpec((D_MODEL, tn), lambda b, i, j: (0, j)),
                  pl.BlockSpec((D_MODEL, LANES), lambda b, i, j: (0, 0))],
        out_specs=[pl.BlockSpec((1, tm, tn), lambda b, i, j: (b, i, j)),
                   pl.BlockSpec((1, tm, LANES), lambda b, i, j: (b, i, 0))],
        out_shape=[jax.ShapeDtypeStruct((BATCH, TOK, PROJ_COLS), BF16),
                   jax.ShapeDtypeStruct((BATCH, TOK, LANES), F32)],
        scratch_shapes=[pltpu.VMEM((tm, D_MODEL), BF16)],
        compiler_params=_params("parallel", "parallel", "arbitrary"),
        name="proj",
    )(xs, mods, mods, gain.reshape(1, D_MODEL), w_main, w_ba)


def _conv(x, w):
    n, c = x.shape
    t = lax.broadcasted_iota(jnp.int32, (n, c), 0)
    is_ctx = t >= SEQ
    local = jnp.where(is_ctx, t - SEQ, t)
    seg_len = jnp.where(is_ctx, CTX_LEN, SEQ)
    y = jnp.zeros_like(x)
    for j in range(CONV_WIDTH):
        s = j - CONV_WIDTH // 2
        if s == 0:
            y = y + x * w[j:j + 1, :]
        else:
            shifted = pltpu.roll(x, (-s) % n, 0)
            ok = jnp.logical_and(local + s >= 0, local + s < seg_len)
            y = y + jnp.where(ok, shifted, 0.0) * w[j:j + 1, :]
    return y


def _gdn_conv_kernel(x_ref, w_ref, o_ref):
    j = pl.program_id(1)
    y = _silu(_conv(x_ref[0].astype(F32), w_ref[...]))
    nrm = lax.rsqrt(jnp.sum(y * y, axis=-1, keepdims=True) + EPS)
    n_head_blocks = GDN_HEADS
    scale = jnp.where(j < n_head_blocks, nrm * GDN_HEAD_DIM ** -0.5, jnp.where(j < 2 * n_head_blocks, nrm, 1.0))
    o_ref[0] = (y * scale).astype(BF16)


def _gdn_conv(p, conv_w):
    nblk = 3 * BRANCH_WIDTH // LANES
    first = COL_QKV // LANES
    return pl.pallas_call(
        _gdn_conv_kernel,
        grid=(BATCH, nblk),
        in_specs=[pl.BlockSpec((1, TOK, LANES), lambda b, j: (b, 0, first + j)),
                  pl.BlockSpec((CONV_WIDTH, LANES), lambda b, j: (0, j))],
        out_specs=pl.BlockSpec((1, TOK, LANES), lambda b, j: (b, 0, j)),
        out_shape=jax.ShapeDtypeStruct((BATCH, TOK, 3 * BRANCH_WIDTH), BF16),
        compiler_params=_params("parallel", "parallel"),
        name="gdn_conv",
    )(p, conv_w)


def _lru_conv_kernel(x_ref, w_ref, b_ref, o_ref):
    o_ref[0] = _conv(x_ref[0].astype(F32), w_ref[...]) + b_ref[...]


def _lru_conv(p, conv_w, conv_b):
    nblk = BRANCH_WIDTH // LANES
    first = COL_LX // LANES
    return pl.pallas_call(
        _lru_conv_kernel,
        grid=(BATCH, nblk),
        in_specs=[pl.BlockSpec((1, TOK, LANES), lambda b, j: (b, 0, first + j)),
                  pl.BlockSpec((CONV_WIDTH, LANES), lambda b, j: (0, j)),
                  pl.BlockSpec((1, LANES), lambda b, j: (0, j))],
        out_specs=pl.BlockSpec((1, TOK, LANES), lambda b, j: (b, 0, j)),
        out_shape=jax.ShapeDtypeStruct((BATCH, TOK, BRANCH_WIDTH), F32),
        compiler_params=_params("parallel", "parallel"),
        name="lru_conv",
    )(p, conv_w, conv_b.reshape(1, BRANCH_WIDTH))


GDN_QM_ROWS = GDN_CHUNK + GDN_HEAD_DIM
GDN_GL_ROWS = 8
GDN_PREP_UNROLL = 9


def _gdn_prepare(q_ref, k_ref, v_ref, gb_scr, qm_scr, nn_scr, o_scr, gl_scr, chunks, head):
    c = GDN_CHUNK
    lane = lax.broadcasted_iota(jnp.int32, (c, LANES), 1)
    row = lax.broadcasted_iota(jnp.int32, (c, LANES), 0)
    ii = lax.broadcasted_iota(jnp.int32, (c, c), 0)
    jj = lax.broadcasted_iota(jnp.int32, (c, c), 1)
    eye = (ii == jj).astype(F32)
    masks = ((ii >= jj, ii > jj, row > lane), (ii <= jj, ii < jj, row < lane))

    loaded = []
    for chunk in chunks:
        r0 = pl.multiple_of(chunk * c, c)
        k = k_ref[0, pl.ds(r0, c), :]
        q = q_ref[0, pl.ds(r0, c), :]
        kq = _dot_nt(jnp.concatenate([k, q], axis=0), k)
        loaded.append((chunk, r0, q, k, kq))

    chains = []
    for chunk, r0, q, k, kq in loaded:
        gb = gb_scr[pl.ds(r0, c), :]
        for d in range(2):
            col = head + d * GDN_HEADS
            beta = jnp.sum(jnp.where(lane == col, gb, 0.0), axis=-1, keepdims=True)
            g = jnp.sum(jnp.where(lane == col + 2 * GDN_HEADS, gb, 0.0), axis=-1, keepdims=True)
            incl, strict, strict_wide = masks[d]
            rhs = jnp.where(lane >= c, g, jnp.where(strict_wide, g, 0.0))
            e = _dot(incl.astype(F32), rhs, precision=lax.Precision.HIGHEST)
            chains.append(dict(chunk=chunk, r0=r0, d=d, q=q, k=k, kq=kq, beta=beta, e=e))

    for ch in chains:
        incl, strict, _ = masks[ch["d"]]
        e = ch["e"]
        decay = jnp.where(incl, jnp.exp(e[:, :c]), 0.0)
        gc = e[:, c:c + 1]
        last = 0 if ch["d"] == 1 else c - 1
        gc_last = e[last:last + 1, c:c + 1]
        ch.update(decay=decay, gc=gc, gc_last=gc_last, egc=jnp.exp(gc))
        ch["a"] = jnp.where(strict, ch["beta"] * ch["kq"][:c] * decay, 0.0)
        ch["t"] = eye
    s = 1
    while s < c:
        pair = jnp.logical_and((ii // (2 * s)) == (jj // (2 * s)), (ii // s) != (jj // s))
        for ch in chains:
            ch["a_off"] = jnp.where(pair, ch["a"], 0.0)
        if s == 1:
            for ch in chains:
                ch["t"] = eye - ch["a_off"]
        else:
            for ch in chains:
                ch["m"] = _dot(ch["t"].astype(BF16), ch["a_off"].astype(BF16))
            for ch in chains:
                ch["t"] = ch["t"] - _dot(ch["m"].astype(BF16), ch["t"].astype(BF16))
        s *= 2
    for ch in chains:
        r0, beta, egc = ch["r0"], ch["beta"], ch["egc"]
        kf = ch["k"].astype(F32)
        vf = v_ref[0, pl.ds(r0, c), :].astype(F32)
        rhs2 = jnp.concatenate([vf * beta, kf * (beta * egc)], axis=1).astype(BF16)
        ch["uw"] = _dot(ch["t"].astype(BF16), rhs2).astype(BF16)
        ch["k_dec_t"] = (kf * jnp.exp(ch["gc_last"] - ch["gc"])).T.astype(BF16)
    for ch in chains:
        incl = masks[ch["d"]][0]
        qk = jnp.where(incl, ch["kq"][c:] * ch["decay"], 0.0).astype(BF16)
        ch["nm"] = _dot(ch["k_dec_t"], ch["uw"])
        ch["ow"] = _dot(qk, ch["uw"])
    for ch in chains:
        chunk, r0, d, nm, ow = ch["chunk"], ch["r0"], ch["d"], ch["nm"], ch["ow"]
        q0 = pl.multiple_of(chunk * GDN_QM_ROWS, 16)
        qm_scr[d, pl.ds(q0, c), :] = (ch["q"].astype(F32) * ch["egc"] - ow[:, GDN_HEAD_DIM:]).astype(BF16)
        qm_scr[d, pl.ds(q0 + c, GDN_HEAD_DIM), :] = nm[:, GDN_HEAD_DIM:].astype(BF16)
        nn_scr[d, pl.ds(pl.multiple_of(chunk * GDN_HEAD_DIM, GDN_HEAD_DIM), GDN_HEAD_DIM), :] = nm[:, :GDN_HEAD_DIM]
        o_scr[d, pl.ds(r0, c), :] = ow[:, :GDN_HEAD_DIM]
        gl_scr[d, pl.ds(pl.multiple_of(chunk * GDN_GL_ROWS, GDN_GL_ROWS), GDN_GL_ROWS), :] = jnp.broadcast_to(
            jnp.exp(ch["gc_last"]), (GDN_GL_ROWS, LANES))


def _gdn_advance(qm_scr, nn_scr, o_scr, gl_scr, d, chunk, state):
    c = GDN_CHUNK
    qm = qm_scr[d, pl.ds(pl.multiple_of(chunk * GDN_QM_ROWS, 16), GDN_QM_ROWS), :]
    r = _dot(qm, state.astype(BF16))
    rows = pl.ds(pl.multiple_of(chunk * c, c), c)
    o_scr[d, rows, :] = o_scr[d, rows, :] + r[:c]
    gl = gl_scr[d, pl.ds(pl.multiple_of(chunk * GDN_GL_ROWS, GDN_GL_ROWS), 1), :]
    n = nn_scr[d, pl.ds(pl.multiple_of(chunk * GDN_HEAD_DIM, GDN_HEAD_DIM), GDN_HEAD_DIM), :]
    return state * gl - r[c:] + n


def _gdn_kernel(q_ref, k_ref, v_ref, z_ref, ba_ref, alog_ref, dtb_ref, gain_ref, o_ref,
                gb_scr, qm_scr, nn_scr, o_scr, gl_scr):
    head = pl.program_id(1)
    c = GDN_CHUNK
    n_lat, n_ctx = SEQ // c, CTX_LEN // c
    n_chunks = n_lat + n_ctx
    ba = ba_ref[0]
    lane = lax.broadcasted_iota(jnp.int32, ba.shape, 1)
    g_all = -jnp.exp(alog_ref[...]) * _softplus(ba + dtb_ref[...])
    gb_scr[...] = jnp.where(lane < 2 * GDN_HEADS, jax.nn.sigmoid(ba), g_all)

    def prepare(i, _):
        chunks = [i * GDN_PREP_UNROLL + j for j in range(GDN_PREP_UNROLL)]
        _gdn_prepare(q_ref, k_ref, v_ref, gb_scr, qm_scr, nn_scr, o_scr, gl_scr, chunks, head)
        return 0

    lax.fori_loop(0, n_chunks // GDN_PREP_UNROLL, prepare, 0)

    def advance(i, carry):
        s_f, s_b = carry
        cf = jnp.where(i < n_ctx, n_lat + i, i - n_ctx)
        cb = n_chunks - 1 - i
        s_f = _gdn_advance(qm_scr, nn_scr, o_scr, gl_scr, 0, cf, s_f)
        s_b = _gdn_advance(qm_scr, nn_scr, o_scr, gl_scr, 1, cb, s_b)
        return s_f, s_b

    zero = jnp.zeros((GDN_HEAD_DIM, GDN_HEAD_DIM), F32)
    lax.fori_loop(0, n_chunks, advance, (zero, zero))
    o = o_scr[0] + o_scr[1]
    o_ref[0] = (_rms(o, gain_ref[...]) * _silu(z_ref[0].astype(F32))).astype(BF16)


def _gdn(qkv, p, ba, a_log, dt_bias, out_gain):
    def pad_lanes(vals):
        row = jnp.zeros((LANES,), F32).at[2 * GDN_HEADS:4 * GDN_HEADS].set(vals.reshape(-1))
        return row.reshape(1, LANES)

    nh = GDN_HEADS
    n_chunks = TOK // GDN_CHUNK
    zblk = COL_Z // LANES
    blk = lambda off: pl.BlockSpec((1, TOK, LANES), lambda b, h: (b, 0, off + h))
    vec = pl.BlockSpec((1, LANES), lambda b, h: (0, 0))
    return pl.pallas_call(
        _gdn_kernel,
        grid=(BATCH, nh),
        in_specs=[blk(0), blk(nh), blk(2 * nh), blk(zblk),
                  pl.BlockSpec((1, TOK, LANES), lambda b, h: (b, 0, 0)), vec, vec, vec],
        out_specs=pl.BlockSpec((1, TOK, LANES), lambda b, h: (b, 0, h)),
        out_shape=jax.ShapeDtypeStruct((BATCH, TOK, BRANCH_WIDTH), BF16),
        scratch_shapes=[pltpu.VMEM((TOK, LANES), F32),
                        pltpu.VMEM((2, n_chunks * GDN_QM_ROWS, LANES), BF16),
                        pltpu.VMEM((2, n_chunks * GDN_HEAD_DIM, LANES), F32),
                        pltpu.VMEM((2, TOK, LANES), F32),
                        pltpu.VMEM((2, n_chunks * GDN_GL_ROWS, LANES), F32)],
        compiler_params=_params("parallel", "parallel"),
        name="gdn",
    )(qkv, qkv, qkv, p, ba, pad_lanes(a_log), pad_lanes(dt_bias), out_gain.reshape(1, LANES))


def _rms_halves(x, gain):
    lane = lax.broadcasted_iota(jnp.int32, x.shape, 1)
    lo = lane < DIFF_HEAD_DIM
    x2 = x * x
    s_lo = jnp.sum(jnp.where(lo, x2, 0.0), axis=-1, keepdims=True)
    s_hi = jnp.sum(jnp.where(lo, 0.0, x2), axis=-1, keepdims=True)
    ms = jnp.where(lo, s_lo, s_hi) * (1.0 / DIFF_HEAD_DIM)
    return x * lax.rsqrt(ms + EPS) * gain


def _rope(x, cos, sin):
    lane = lax.broadcasted_iota(jnp.int32, x.shape, 1)
    first = (lane & ROPE_PAIRS) == 0
    partner = jnp.where(first, -pltpu.roll(x, LANES - ROPE_PAIRS, 1), pltpu.roll(x, ROPE_PAIRS, 1))
    return x * cos + partner * sin


def _attn_kernel(q_ref, k_ref, v_ref, cosk_ref, sink_ref, cosq_ref, sinq_ref, qg_ref, kg_ref, lv_ref, og_ref,
                 o_ref, kn_scr, *, ctx_block, lam_init):
    qi = pl.program_id(2)

    @pl.when(qi == 0)
    def _():
        kn = _rope(_rms_halves(k_ref[0].astype(F32), kg_ref[...]), cosk_ref[...], sink_ref[...])
        kn_scr[...] = kn.astype(BF16)

    lv = lv_ref[...]
    lam = (jnp.exp(jnp.sum(lv[0:1] * lv[1:2], axis=-1, keepdims=True))
           - jnp.exp(jnp.sum(lv[2:3] * lv[3:4], axis=-1, keepdims=True)) + lam_init)
    q = _rope(_rms_halves(q_ref[0].astype(F32), qg_ref[...]), cosq_ref[...], sinq_ref[...])
    q = q * (DIFF_HEAD_DIM ** -0.5 * math.log2(math.e))
    lane = lax.broadcasted_iota(jnp.int32, q.shape, 1)
    lo = lane < DIFF_HEAD_DIM
    q1 = jnp.where(lo, q, 0.0).astype(BF16)
    q2 = jnp.where(lo, 0.0, q).astype(BF16)

    def attend(kn, v):
        def half(s):
            p = jnp.exp2(s - jnp.max(s, axis=-1, keepdims=True))
            return _dot(p.astype(BF16), v), jnp.sum(p, axis=-1, keepdims=True)
        s1 = _dot_nt(q1, kn)
        s2 = _dot_nt(q2, kn)
        a1, l1 = half(s1)
        a2, l2 = half(s2)
        o = a1 * (1.0 / l1) - a2 * (lam / l2)
        o_ref[0] = (_rms(o, og_ref[...]) * (1.0 - lam_init)).astype(BF16)

    if ctx_block is None:
        attend(kn_scr[...], v_ref[0])
    else:
        @pl.when(qi == ctx_block)
        def _():
            attend(kn_scr[SEQ:, :], v_ref[0, SEQ:, :])

        @pl.when(qi != ctx_block)
        def _():
            attend(kn_scr[...], v_ref[0])


def _rope_tables():
    n_rows = SEQ // GRID_W
    row_id = jnp.broadcast_to(jnp.arange(n_rows, dtype=F32)[:, None], (n_rows, GRID_W)).reshape(-1)
    col_id = jnp.broadcast_to(jnp.arange(GRID_W, dtype=F32)[None, :], (n_rows, GRID_W)).reshape(-1)
    inv_freq = jnp.power(ROPE_BASE, -jnp.arange(ROPE_PAIRS, dtype=F32) / ROPE_PAIRS)
    row_ang = row_id[:, None] * inv_freq
    col_ang = col_id[:, None] * inv_freq
    ang = jnp.concatenate([row_ang, row_ang, col_ang, col_ang], axis=-1)
    ang = jnp.concatenate([ang, ang], axis=-1)
    pad = ((0, CTX_LEN), (0, 0))
    return jnp.pad(jnp.cos(ang), pad, constant_values=1.0), jnp.pad(jnp.sin(ang), pad)


def _diff_attn(p, cos, sin, q_gain, k_gain, lam_vecs, out_gain, lam_init, with_ctx):
    tq = CTX_LEN
    nq = TOK // tq if with_ctx else SEQ // tq
    nh = DIFF_HEADS
    qb, kb, vb = COL_DQ // LANES, COL_DK // LANES, COL_DV // LANES
    full = lambda off: pl.BlockSpec((1, TOK, LANES), lambda b, h, i: (b, 0, off + h))
    tab_full = pl.BlockSpec((TOK, LANES), lambda b, h, i: (0, 0))
    tab_q = pl.BlockSpec((tq, LANES), lambda b, h, i: (i, 0))
    vec = pl.BlockSpec((1, LANES), lambda b, h, i: (0, 0))
    tile2 = lambda g: jnp.concatenate([g, g]).reshape(1, LANES)
    return pl.pallas_call(
        functools.partial(_attn_kernel, ctx_block=SEQ // tq if with_ctx else None, lam_init=lam_init),
        grid=(BATCH, nh, nq),
        in_specs=[pl.BlockSpec((1, tq, LANES), lambda b, h, i: (b, i, qb + h)), full(kb), full(vb),
                  tab_full, tab_full, tab_q, tab_q, vec, vec,
                  pl.BlockSpec((4, DIFF_HEAD_DIM), lambda b, h, i: (0, 0)), vec],
        out_specs=pl.BlockSpec((1, tq, LANES), lambda b, h, i: (b, i, h)),
        out_shape=jax.ShapeDtypeStruct((BATCH, nq * tq, BRANCH_WIDTH), BF16),
        scratch_shapes=[pltpu.VMEM((TOK, LANES), BF16)],
        compiler_params=_params("parallel", "parallel", "arbitrary"),
        name="diff_attn",
    )(p, p, p, cos, sin, cos, sin, tile2(q_gain), tile2(k_gain), lam_vecs, out_gain.reshape(1, LANES))


LRU_SLAB = 256
LRU_ROWS = 256
LRU_SCAN_BLOCK = 8


def _lru_kernel(xc_ref, y_ref, wg_ref, bg_ref, lam_ref, o_ref, af_scr, bf_scr, ab_scr, bb_scr):
    w = LRU_SLAB
    blk = LRU_SCAN_BLOCK
    sp = _softplus(-lam_ref[0])
    sub = lax.broadcasted_iota(jnp.int32, (LRU_ROWS, w), 0) % blk

    def gates(i, _):
        r0 = pl.multiple_of(i * LRU_ROWS, LRU_ROWS)
        xc = xc_ref[0, pl.ds(r0, LRU_ROWS), :]
        pre = _dot(xc.astype(BF16), wg_ref[0]) + bg_ref[0]
        for d, (a_scr, b_scr) in enumerate(((af_scr, bf_scr), (ab_scr, bb_scr))):
            r = jax.nn.sigmoid(pre[:, (2 * d) * w:(2 * d + 1) * w])
            gi = jax.nn.sigmoid(pre[:, (2 * d + 1) * w:(2 * d + 2) * w])
            log_a = -LRU_C * r * sp[d:d + 1]
            a = jnp.exp(log_a)
            b = jnp.sqrt(1.0 - a * a) * gi * xc
            shift = 1
            while shift < blk:
                if d == 0:
                    ok, roll_by = sub >= shift, shift
                else:
                    ok, roll_by = sub < blk - shift, LRU_ROWS - shift
                b = jnp.where(ok, a * pltpu.roll(b, roll_by, 0) + b, b)
                a = jnp.where(ok, a * pltpu.roll(a, roll_by, 0), a)
                shift *= 2
            a_scr[pl.ds(r0, LRU_ROWS), :] = a
            b_scr[pl.ds(r0, LRU_ROWS), :] = b
        return 0

    lax.fori_loop(0, TOK // LRU_ROWS, gates, 0)

    n_blk, n_lat_blk, n_ctx_blk = TOK // blk, SEQ // blk, CTX_LEN // blk

    def step(s, carry):
        h_f, h_b = carry
        rows_f = pl.ds(pl.multiple_of(jnp.where(s < n_ctx_blk, n_lat_blk + s, s - n_ctx_blk) * blk, blk), blk)
        rows_b = pl.ds(pl.multiple_of((n_blk - 1 - s) * blk, blk), blk)
        hf = af_scr[rows_f, :] * h_f + bf_scr[rows_f, :]
        bf_scr[rows_f, :] = hf
        hb = ab_scr[rows_b, :] * h_b + bb_scr[rows_b, :]
        bb_scr[rows_b, :] = hb
        return hf[blk - 1:blk, :], hb[0:1, :]

    zero = jnp.zeros((1, w), F32)
    lax.fori_loop(0, n_blk, step, (zero, zero), unroll=4)
    h = bf_scr[...] + bb_scr[...]
    o_ref[0] = (h * jax.nn.gelu(y_ref[0].astype(F32))).astype(BF16)


def _lru_gate_weights(w_gate, b_gate):
    n_slab = BRANCH_WIDTH // LRU_SLAB
    per = LRU_SLAB // LRU_BLOCK_DIM
    wg = w_gate.reshape(2, 2, n_slab, per, LRU_BLOCK_DIM, LRU_BLOCK_DIM)
    eye = jnp.eye(per, dtype=w_gate.dtype)
    dense = jnp.einsum('dgsnjk,nm->snjdgmk', wg, eye)
    dense = dense.reshape(n_slab, LRU_SLAB, 4 * LRU_SLAB)
    bg = b_gate.reshape(2, 2, n_slab, LRU_SLAB).transpose(2, 0, 1, 3).reshape(n_slab, 1, 4 * LRU_SLAB)
    return dense.astype(BF16), bg


def _lru(xc, p, w_gate, b_gate, lam):
    n_slab = BRANCH_WIDTH // LRU_SLAB
    wg, bg = _lru_gate_weights(w_gate, b_gate)
    lam_s = lam.reshape(2, n_slab, LRU_SLAB).transpose(1, 0, 2)
    yb = COL_LY // LRU_SLAB
    return pl.pallas_call(
        _lru_kernel,
        grid=(BATCH, n_slab),
        in_specs=[pl.BlockSpec((1, TOK, LRU_SLAB), lambda b, s: (b, 0, s)),
                  pl.BlockSpec((1, TOK, LRU_SLAB), lambda b, s: (b, 0, yb + s)),
                  pl.BlockSpec((1, LRU_SLAB, 4 * LRU_SLAB), lambda b, s: (s, 0, 0)),
                  pl.BlockSpec((1, 1, 4 * LRU_SLAB), lambda b, s: (s, 0, 0)),
                  pl.BlockSpec((1, 2, LRU_SLAB), lambda b, s: (s, 0, 0))],
        out_specs=pl.BlockSpec((1, TOK, LRU_SLAB), lambda b, s: (b, 0, s)),
        out_shape=jax.ShapeDtypeStruct((BATCH, TOK, BRANCH_WIDTH), BF16),
        scratch_shapes=[pltpu.VMEM((TOK, LRU_SLAB), F32)] * 4,
        compiler_params=_params("parallel", "parallel"),
        name="lru",
    )(xc, p, wg, bg, lam_s)


def _route(logits):
    lane = lax.broadcasted_iota(jnp.int32, logits.shape, 1)
    lane_f = lane.astype(F32)
    far = float(LANES)
    lg = jnp.where(lane < N_EXPERTS, logits, -jnp.inf)
    ex = jnp.exp(lg - jnp.max(lg, axis=-1, keepdims=True))
    probs = ex / jnp.sum(ex, axis=-1, keepdims=True)
    per_group = N_EXPERTS // N_GROUPS
    grp = lane // per_group

    def top2(vals):
        m1 = jnp.max(vals, axis=-1, keepdims=True)
        i1 = jnp.min(jnp.where(vals == m1, lane_f, far), axis=-1, keepdims=True)
        rest = jnp.where(lane_f == i1, -2.0, vals)
        m2 = jnp.max(rest, axis=-1, keepdims=True)
        i2 = jnp.min(jnp.where(rest == m2, lane_f, far), axis=-1, keepdims=True)
        return m1, i1, m2, i2

    best = jnp.zeros(logits.shape[:1] + (1,), jnp.int32)
    best_score = None
    for g in range(N_GROUPS):
        m1, _, m2, _ = top2(jnp.where(grp == g, probs, -1.0))
        score = m1 + m2
        if best_score is None:
            best_score = score
        else:
            better = score > best_score
            best = jnp.where(better, g, best)
            best_score = jnp.where(better, score, best_score)
    m1, i1, m2, i2 = top2(jnp.where(grp == best, probs, -1.0))
    den = m1 + m2
    return i1, i2, m1 / den, m2 / den


ROUTE_E, ROUTE_W, ROUTE_RANK = 0, 2, 4


def _pack_bf16_pairs(x):
    n = x.shape[1] // 2
    xb = x.astype(BF16).astype(F32)
    lo = pltpu.bitcast(xb[:, :n], jnp.uint32)
    hi = pltpu.bitcast(xb[:, n:], jnp.uint32)
    return (lo >> 16) | (hi & jnp.uint32(0xFFFF0000))


def _unpack_bf16_pairs(p):
    lo = pltpu.bitcast(p << 16, F32)
    hi = pltpu.bitcast(p & jnp.uint32(0xFFFF0000), F32)
    return jnp.concatenate([lo, hi], axis=1)


def _merge_kernel(ya_ref, yb_ref, yc_ref, gates_ref, x_ref, ml_ref, mc_ref, g2_ref, wbr_ref, wout_ref, wr_ref,
                  br_ref, xo_ref, h2_ref, route_ref, cnt_ref, cnt_scr, *, tm):
    i = pl.program_id(1)

    @pl.when(jnp.logical_and(pl.program_id(0) == 0, i == 0))
    def _():
        cnt_scr[...] = jnp.zeros_like(cnt_scr)

    acc = None
    for n, y_ref in enumerate((ya_ref, yb_ref, yc_ref)):
        yn = _dot(y_ref[0], wbr_ref[n])
        gate = jax.nn.sigmoid(gates_ref[0, :, n * D_MODEL:(n + 1) * D_MODEL].astype(F32))
        acc = gate * yn if acc is None else acc + gate * yn
    out = _dot(acc.astype(BF16), wout_ref[...])
    is_ctx = _ctx_rows(i, tm, D_MODEL)
    xn = x_ref[0] + jnp.where(is_ctx, mc_ref[0, 2:3, :], ml_ref[0, 2:3, :]) * out
    xo_ref[0] = xn
    h2 = _modulated_norm(xn, g2_ref[...], ml_ref, mc_ref, is_ctx, 3)
    h2_ref[0] = _pack_bf16_pairs(h2)
    i1, i2, w1, w2 = _route(_dot(h2.astype(BF16), wr_ref[...]) + br_ref[...])
    lane = lax.broadcasted_iota(jnp.int32, (tm, LANES), 1)
    lane_f = lane.astype(F32)
    chosen = jnp.where(jnp.logical_or(lane_f == i1, lane_f == i2), 1.0, 0.0)
    earlier = (lax.broadcasted_iota(jnp.int32, (tm, tm), 0) > lax.broadcasted_iota(jnp.int32, (tm, tm), 1))
    before = _dot(jnp.where(earlier, 1.0, 0.0).astype(BF16), chosen.astype(BF16)) + cnt_scr[...]
    rank1 = jnp.sum(jnp.where(lane_f == i1, before, 0.0), axis=-1, keepdims=True)
    rank2 = jnp.sum(jnp.where(lane_f == i2, before, 0.0), axis=-1, keepdims=True)
    cnt_scr[...] += jnp.sum(chosen, axis=0, keepdims=True)
    cnt_ref[...] = cnt_scr[...]
    record = jnp.zeros((tm, LANES), F32)
    for pos, val in enumerate((i1, i2, w1, w2, rank1, rank2)):
        record = jnp.where(lane == pos, val, record)
    route_ref[0] = record


def _merge(ya, yb, yc, p, xs, mods, gain2, w_branch, w_out, w_router, b_router, rows, tm):
    wr = jnp.zeros((D_MODEL, LANES), BF16).at[:, :N_EXPERTS].set(w_router.astype(BF16))
    br = jnp.zeros((1, LANES), F32).at[0, :N_EXPERTS].set(b_router)
    tile = lambda w: pl.BlockSpec((1, tm, w), lambda b, i: (b, i, 0))
    const = lambda shape: pl.BlockSpec(shape, lambda b, i: (0,) * len(shape))
    return pl.pallas_call(
        functools.partial(_merge_kernel, tm=tm),
        grid=(BATCH, rows // tm),
        in_specs=[tile(BRANCH_WIDTH), tile(BRANCH_WIDTH), tile(BRANCH_WIDTH), tile(3 * D_MODEL), tile(D_MODEL),
                  pl.BlockSpec((1, N_MOD, D_MODEL), lambda b, i: (b, 0, 0)),
                  pl.BlockSpec((1, N_MOD, D_MODEL), lambda b, i: (BATCH, 0, 0)),
                  const((1, D_MODEL)), const((3, BRANCH_WIDTH, D_MODEL)), const((D_MODEL, D_MODEL)),
                  const((D_MODEL, LANES)), const((1, LANES))],
        out_specs=[tile(D_MODEL), tile(D_MODEL // 2), tile(LANES), const((1, LANES))],
        out_shape=[jax.ShapeDtypeStruct((BATCH, rows, D_MODEL), F32),
                   jax.ShapeDtypeStruct((BATCH, rows, D_MODEL // 2), jnp.uint32),
                   jax.ShapeDtypeStruct((BATCH, rows, LANES), F32),
                   jax.ShapeDtypeStruct((1, LANES), F32)],
        scratch_shapes=[pltpu.VMEM((1, LANES), F32)],
        compiler_params=_params("arbitrary", "arbitrary"),
        name="merge",
    )(ya, yb, yc, p, xs, mods, mods, gain2.reshape(1, D_MODEL), w_branch, w_out, wr, br)


MOE_TILE = 512
SC_GATHER_ROWS = 64


def _sc_gather(table, idx):
    info = plsc.get_sparse_core_info()
    n_workers = info.num_cores * info.num_subcores
    n_rows, width = idx.shape[0], table.shape[1]
    per_worker = n_rows // n_workers
    assert per_worker * n_workers == n_rows and per_worker % SC_GATHER_ROWS == 0
    mesh = plsc.VectorSubcoreMesh(core_axis_name="c", subcore_axis_name="s")

    @functools.partial(
        pl.kernel, mesh=mesh, out_type=jax.ShapeDtypeStruct((n_rows, width), table.dtype),
        scratch_types=[pltpu.VMEM((SC_GATHER_ROWS,), jnp.int32),
                       pltpu.VMEM((SC_GATHER_ROWS, width), table.dtype),
                       pltpu.SemaphoreType.DMA],
        name="sc_gather")
    def gather(table_hbm, idx_hbm, out_hbm, idx_v, rows_v, sem):
        worker = lax.axis_index("s") * info.num_cores + lax.axis_index("c")
        base = worker * per_worker

        @pl.loop(0, per_worker // SC_GATHER_ROWS)
        def _(j):
            off = base + j * SC_GATHER_ROWS
            pltpu.sync_copy(idx_hbm.at[pl.ds(off, SC_GATHER_ROWS)], idx_v)
            pltpu.async_copy(table_hbm.at[idx_v], rows_v, sem).wait()
            pltpu.sync_copy(rows_v, out_hbm.at[pl.ds(off, SC_GATHER_ROWS)])

    return gather(table, idx)


def _experts_kernel(tile_expert_ref, n_valid_ref, x_ref, wgu_ref, wd_ref, o_ref):
    j = pl.program_id(0)

    @pl.when(j < n_valid_ref[0])
    def _():
        x = _unpack_bf16_pairs(x_ref[...]).astype(BF16)
        gu = _dot(x, wgu_ref[0])
        act = (_silu(gu[:, :EXPERT_FF]) * gu[:, EXPERT_FF:]).astype(BF16)
        o_ref[...] = _pack_bf16_pairs(_dot(act, wd_ref[0]))

    @pl.when(j >= n_valid_ref[0])
    def _():
        o_ref[...] = jnp.zeros_like(o_ref)


def _experts(x_sorted, tile_expert, n_valid, w_gate_up, w_down):
    n_tiles = x_sorted.shape[0] // MOE_TILE
    half = D_MODEL // 2
    return pl.pallas_call(
        _experts_kernel,
        grid_spec=pltpu.PrefetchScalarGridSpec(
            num_scalar_prefetch=2, grid=(n_tiles,),
            in_specs=[pl.BlockSpec((MOE_TILE, half), lambda j, te, nv: (j, 0)),
                      pl.BlockSpec((1, D_MODEL, 2 * EXPERT_FF), lambda j, te, nv: (te[j], 0, 0)),
                      pl.BlockSpec((1, EXPERT_FF, D_MODEL), lambda j, te, nv: (te[j], 0, 0))],
            out_specs=pl.BlockSpec((MOE_TILE, half), lambda j, te, nv: (j, 0))),
        out_shape=jax.ShapeDtypeStruct((x_sorted.shape[0], half), jnp.uint32),
        compiler_params=_params("arbitrary"),
        name="experts",
    )(tile_expert, n_valid, x_sorted, w_gate_up, w_down)


def _combine_kernel(y1_ref, y2_ref, route_ref, x_ref, ml_ref, mc_ref, o_ref, *, tm):
    i = pl.program_id(1)
    route = route_ref[0]
    w1 = route[:, ROUTE_W:ROUTE_W + 1]
    w2 = route[:, ROUTE_W + 1:ROUTE_W + 2]
    moe = w1 * _unpack_bf16_pairs(y1_ref[0, 0]) + w2 * _unpack_bf16_pairs(y2_ref[0, 0])
    is_ctx = _ctx_rows(i, tm, D_MODEL)
    o_ref[0] = x_ref[0] + jnp.where(is_ctx, mc_ref[0, 5:6, :], ml_ref[0, 5:6, :]) * moe


def _combine(y_pairs, route, xs, mods, rows, tm):
    half = D_MODEL // 2
    tile = lambda w: pl.BlockSpec((1, tm, w), lambda b, i: (b, i, 0))
    slot = lambda s: pl.BlockSpec((1, 1, tm, half), lambda b, i: (s, b, i, 0))
    return pl.pallas_call(
        functools.partial(_combine_kernel, tm=tm),
        grid=(BATCH, rows // tm),
        in_specs=[slot(0), slot(1), tile(LANES), tile(D_MODEL),
                  pl.BlockSpec((1, N_MOD, D_MODEL), lambda b, i: (b, 0, 0)),
                  pl.BlockSpec((1, N_MOD, D_MODEL), lambda b, i: (BATCH, 0, 0))],
        out_specs=tile(D_MODEL),
        out_shape=jax.ShapeDtypeStruct((BATCH, rows, D_MODEL), F32),
        compiler_params=_params("parallel", "parallel"),
        name="combine",
    )(y_pairs, y_pairs, route, xs, mods, mods)


def _moe(h2, route, counts, xs, mods, w_gate_up, w_down, rows, tm):
    n_tok = BATCH * rows
    half = D_MODEL // 2
    n_sorted = 2 * n_tok + N_EXPERTS * MOE_TILE
    n_tiles = n_sorted // MOE_TILE
    rec = route.reshape(n_tok, LANES)
    expert = rec[:, ROUTE_E:ROUTE_E + 2].astype(jnp.int32)
    rank = rec[:, ROUTE_RANK:ROUTE_RANK + 2].astype(jnp.int32)
    count = counts[0, :N_EXPERTS].astype(jnp.int32)
    padded = (count + MOE_TILE - 1) // MOE_TILE * MOE_TILE
    end = jnp.cumsum(padded)
    dest = ((end - padded)[expert] + rank).T.reshape(2 * n_tok)
    token = jnp.tile(jnp.arange(n_tok, dtype=jnp.int32), 2)
    source = jnp.zeros((n_sorted,), jnp.int32).at[dest].set(token, unique_indices=True)
    tile_start = jnp.arange(n_tiles, dtype=jnp.int32) * MOE_TILE
    tile_expert = jnp.minimum(jnp.searchsorted(end, tile_start, side='right'), N_EXPERTS - 1).astype(jnp.int32)
    n_valid = (end[-1:] // MOE_TILE).astype(jnp.int32)
    x_sorted = _sc_gather(h2.reshape(n_tok, half), source)
    y_sorted = _experts(x_sorted, tile_expert, n_valid, w_gate_up, w_down)
    y_pairs = _sc_gather(y_sorted, dest).reshape(2, BATCH, rows, half)
    return _combine(y_pairs, route, xs, mods, rows, tm)


def _split_w_in(w_in):
    bw = BRANCH_WIDTH
    sizes = (3 * bw, bw, 2 * GDN_HEADS, 2 * GDN_HEADS, bw, bw, bw, bw, bw, 3 * D_MODEL)
    offs = [0]
    for s in sizes:
        offs.append(offs[-1] + s)
    part = lambda i: w_in[:, offs[i]:offs[i + 1]]
    main = jnp.concatenate([part(9), part(0), part(1), part(4), part(5), part(6), part(7), part(8)], axis=1)
    ba = jnp.zeros((D_MODEL, LANES), F32).at[:, :4 * GDN_HEADS].set(jnp.concatenate([part(2), part(3)], axis=1))
    return main.astype(BF16), ba.astype(BF16)


def kernel(x, c, ctx, c_ctx, w_mod, b_mod, norm1_gain, norm2_gain, w_in, gdn_conv_w, gdn_a_log, gdn_dt_bias, gdn_out_gain, diff_q_gain, diff_k_gain, diff_lambda, diff_out_gain, lru_conv_w, lru_conv_b, lru_w_gate, lru_b_gate, lru_lambda, w_branch, w_out, w_router, b_router, w_gate_up, w_down):
    mods = _mods(c, c_ctx, w_mod, b_mod)
    cos, sin = _rope_tables()
    xs = jnp.concatenate([x, ctx], axis=1)
    for layer in range(DEPTH):
        last = layer == DEPTH - 1
        lam_init = 0.8 - 0.6 * math.exp(-0.3 * layer)
        m = mods[layer]
        w_main, w_ba = _split_w_in(w_in[layer])
        p, ba = _project(xs, m, norm1_gain[layer], w_main, w_ba)
        qkv = _gdn_conv(p, gdn_conv_w[layer])
        ya = _gdn(qkv, p, ba, gdn_a_log[layer], gdn_dt_bias[layer], gdn_out_gain[layer])
        yb = _diff_attn(p, cos, sin, diff_q_gain[layer], diff_k_gain[layer], diff_lambda[layer],
                        diff_out_gain[layer], lam_init, with_ctx=not last)
        xc = _lru_conv(p, lru_conv_w[layer], lru_conv_b[layer])
        yc = _lru(xc, p, lru_w_gate[layer], lru_b_gate[layer], lru_lambda[layer])
        rows, tm = (SEQ, 512) if last else (TOK, 768)
        xs, h2, route, counts = _merge(ya, yb, yc, p, xs, m, norm2_gain[layer], w_branch[layer].astype(BF16),
                                       w_out[layer].astype(BF16), w_router, b_router, rows, tm)
        xs = _moe(h2, route, counts, xs, m, w_gate_up[layer].astype(BF16), w_down[layer].astype(BF16), rows, tm)
    return xs
```

```python
import functools
import math

import jax
import jax.numpy as jnp
from jax import lax
from jax.experimental import pallas as pl
from jax.experimental.pallas import tpu as pltpu
from jax.experimental.pallas import tpu_sc as plsc

F32 = jnp.float32
BF16 = jnp.bfloat16

D_MODEL = 1024
BATCH = 8
SEQ = 2048
DEPTH = 2
GRID_W = 64
CTX_LEN = 256
TOK = SEQ + CTX_LEN
N_MOD = 6
EPS = 1e-6
CONV_WIDTH = 4
BRANCH_WIDTH = 512
GDN_HEADS = 4
GDN_HEAD_DIM = 128
GDN_CHUNK = 64
DIFF_HEADS = 4
DIFF_HEAD_DIM = 64
ROPE_BASE = 10000.0
ROPE_PAIRS = DIFF_HEAD_DIM // 4
LRU_BLOCKS = 8
LRU_BLOCK_DIM = BRANCH_WIDTH // LRU_BLOCKS
LRU_C = 8.0
N_EXPERTS = 16
N_GROUPS = 4
EXPERT_FF = 512

LANES = 128
VMEM_LIMIT = 56 * 1024 * 1024

COL_GATES = 0
COL_QKV = 3 * D_MODEL
COL_Z = COL_QKV + 3 * BRANCH_WIDTH
COL_DQ = COL_Z + BRANCH_WIDTH
COL_DK = COL_DQ + BRANCH_WIDTH
COL_DV = COL_DK + BRANCH_WIDTH
COL_LX = COL_DV + BRANCH_WIDTH
COL_LY = COL_LX + BRANCH_WIDTH
PROJ_COLS = COL_LY + BRANCH_WIDTH


def _params(*sem):
    return pltpu.CompilerParams(dimension_semantics=sem, vmem_limit_bytes=VMEM_LIMIT)


def _dot(a, b, precision=None):
    return jnp.dot(a, b, preferred_element_type=F32, precision=precision)


def _dot_nt(a, b):
    return lax.dot_general(a, b, (((1,), (1,)), ((), ())), preferred_element_type=F32)


def _silu(x):
    return x * jax.nn.sigmoid(x)


def _softplus(x):
    return jnp.maximum(x, 0.0) + jnp.log(1.0 + jnp.exp(-jnp.abs(x)))


def _rms(x, gain):
    return x * lax.rsqrt(jnp.mean(x * x, axis=-1, keepdims=True) + EPS) * gain


def _mod_kernel(c_ref, w_ref, b_ref, o_ref):
    c = c_ref[...]
    o_ref[0] = _dot(_silu(c), w_ref[0], precision=lax.Precision.HIGHEST) + b_ref[0]


def _mods(c, c_ctx, w_mod, b_mod):
    depth = w_mod.shape[0]
    rows = 16
    cc = jnp.zeros((rows, D_MODEL), F32).at[:BATCH].set(c).at[BATCH].set(c_ctx)
    tn = 1536
    out = pl.pallas_call(
        _mod_kernel,
        grid=(depth, N_MOD * D_MODEL // tn),
        in_specs=[pl.BlockSpec((rows, D_MODEL), lambda l, j: (0, 0)),
                  pl.BlockSpec((1, D_MODEL, tn), lambda l, j: (l, 0, j)),
                  pl.BlockSpec((1, 1, tn), lambda l, j: (l, 0, j))],
        out_specs=pl.BlockSpec((1, rows, tn), lambda l, j: (l, 0, j)),
        out_shape=jax.ShapeDtypeStruct((depth, rows, N_MOD * D_MODEL), F32),
        compiler_params=_params("parallel", "parallel"),
        name="mods",
    )(cc, w_mod, b_mod.reshape(depth, 1, N_MOD * D_MODEL))
    return out.reshape(depth, rows, N_MOD, D_MODEL)


def _modulated_norm(x, gain, ml_ref, mc_ref, is_ctx, shift_idx):
    shift = jnp.where(is_ctx, mc_ref[0, shift_idx:shift_idx + 1, :], ml_ref[0, shift_idx:shift_idx + 1, :])
    scale = jnp.where(is_ctx, mc_ref[0, shift_idx + 1:shift_idx + 2, :], ml_ref[0, shift_idx + 1:shift_idx + 2, :])
    return _rms(x, gain) * (1.0 + scale) + shift


def _ctx_rows(tile, tm, width):
    row = tile * tm + lax.broadcasted_iota(jnp.int32, (tm, width), 0)
    return row >= SEQ


def _proj_kernel(x_ref, ml_ref, mc_ref, g_ref, w_ref, wba_ref, p_ref, ba_ref, h_scr, *, tm):
    i = pl.program_id(1)
    j = pl.program_id(2)

    @pl.when(j == 0)
    def _():
        is_ctx = _ctx_rows(i, tm, D_MODEL)
        h_scr[...] = _modulated_norm(x_ref[0], g_ref[...], ml_ref, mc_ref, is_ctx, 0).astype(BF16)

    h = h_scr[...]
    p_ref[0] = _dot(h, w_ref[...]).astype(BF16)
    ba_ref[0] = _dot(h, wba_ref[...])


def _project(xs, mods, gain, w_main, w_ba):
    tm, tn = 1152, 1280
    return pl.pallas_call(
        functools.partial(_proj_kernel, tm=tm),
        grid=(BATCH, TOK // tm, PROJ_COLS // tn),
        in_specs=[pl.BlockSpec((1, tm, D_MODEL), lambda b, i, j: (b, i, 0)),
                  pl.BlockSpec((1, N_MOD, D_MODEL), lambda b, i, j: (b, 0, 0)),
                  pl.BlockSpec((1, N_MOD, D_MODEL), lambda b, i, j: (BATCH, 0, 0)),
                  pl.BlockSpec((1, D_MODEL), lambda b, i, j: (0, 0)),
                  pl.BlockSpec((D_MODEL, tn), lambda b, i, j: (0, j)),
                  pl.BlockSpec((D_MODEL, LANES), lambda b, i, j: (0, 0))],
        out_specs=[pl.BlockSpec((1, tm, tn), lambda b, i, j: (b, i, j)),
                   pl.BlockSpec((1, tm, LANES), lambda b, i, j: (b, i, 0))],
        out_shape=[jax.ShapeDtypeStruct((BATCH, TOK, PROJ_COLS), BF16),
                   jax.ShapeDtypeStruct((BATCH, TOK, LANES), F32)],
        scratch_shapes=[pltpu.VMEM((tm, D_MODEL), BF16)],
        compiler_params=_params("parallel", "parallel", "arbitrary"),
        name="proj",
    )(xs, mods, mods, gain.reshape(1, D_MODEL), w_main, w_ba)


def _conv(x, w):
    n, c = x.shape
    t = lax.broadcasted_iota(jnp.int32, (n, c), 0)
    is_ctx = t >= SEQ
    local = jnp.where(is_ctx, t - SEQ, t)
    seg_len = jnp.where(is_ctx, CTX_LEN, SEQ)
    y = jnp.zeros_like(x)
    for j in range(CONV_WIDTH):
        s = j - CONV_WIDTH // 2
        if s == 0:
            y = y + x * w[j:j + 1, :]
        else:
            shifted = pltpu.roll(x, (-s) % n, 0)
            ok = jnp.logical_and(local + s >= 0, local + s < seg_len)
            y = y + jnp.where(ok, shifted, 0.0) * w[j:j + 1, :]
    return y


def _gdn_conv_kernel(x_ref, w_ref, o_ref):
    j = pl.program_id(1)
    y = _silu(_conv(x_ref[0].astype(F32), w_ref[...]))
    nrm = lax.rsqrt(jnp.sum(y * y, axis=-1, keepdims=True) + EPS)
    n_head_blocks = GDN_HEADS
    scale = jnp.where(j < n_head_blocks, nrm * GDN_HEAD_DIM ** -0.5, jnp.where(j < 2 * n_head_blocks, nrm, 1.0))
    o_ref[0] = (y * scale).astype(BF16)


def _gdn_conv(p, conv_w):
    nblk = 3 * BRANCH_WIDTH // LANES
    first = COL_QKV // LANES
    return pl.pallas_call(
        _gdn_conv_kernel,
        grid=(BATCH, nblk),
        in_specs=[pl.BlockSpec((1, TOK, LANES), lambda b, j: (b, 0, first + j)),
                  pl.BlockSpec((CONV_WIDTH, LANES), lambda b, j: (0, j))],
        out_specs=pl.BlockSpec((1, TOK, LANES), lambda b, j: (b, 0, j)),
        out_shape=jax.ShapeDtypeStruct((BATCH, TOK, 3 * BRANCH_WIDTH), BF16),
        compiler_params=_params("parallel", "parallel"),
        name="gdn_conv",
    )(p, conv_w)


def _lru_conv_kernel(x_ref, w_ref, b_ref, o_ref):
    o_ref[0] = _conv(x_ref[0].astype(F32), w_ref[...]) + b_ref[...]


def _lru_conv(p, conv_w, conv_b):
    nblk = BRANCH_WIDTH // LANES
    first = COL_LX // LANES
    return pl.pallas_call(
        _lru_conv_kernel,
        grid=(BATCH, nblk),
        in_specs=[pl.BlockSpec((1, TOK, LANES), lambda b, j: (b, 0, first + j)),
                  pl.BlockSpec((CONV_WIDTH, LANES), lambda b, j: (0, j)),
                  pl.BlockSpec((1, LANES), lambda b, j: (0, j))],
        out_specs=pl.BlockSpec((1, TOK, LANES), lambda b, j: (b, 0, j)),
        out_shape=jax.ShapeDtypeStruct((BATCH, TOK, BRANCH_WIDTH), F32),
        compiler_params=_params("parallel", "parallel"),
        name="lru_conv",
    )(p, conv_w, conv_b.reshape(1, BRANCH_WIDTH))


GDN_QM_ROWS = GDN_CHUNK + GDN_HEAD_DIM
GDN_GL_ROWS = 8
GDN_PREP_UNROLL = 9


def _gdn_prepare(q_ref, k_ref, v_ref, gb_scr, qm_scr, nn_scr, o_scr, gl_scr, chunks, head):
    c = GDN_CHUNK
    lane = lax.broadcasted_iota(jnp.int32, (c, LANES), 1)
    row = lax.broadcasted_iota(jnp.int32, (c, LANES), 0)
    ii = lax.broadcasted_iota(jnp.int32, (c, c), 0)
    jj = lax.broadcasted_iota(jnp.int32, (c, c), 1)
    eye = (ii == jj).astype(F32)
    masks = ((ii >= jj, ii > jj, row > lane), (ii <= jj, ii < jj, row < lane))

    loaded = []
    for chunk in chunks:
        r0 = pl.multiple_of(chunk * c, c)
        k = k_ref[0, pl.ds(r0, c), :]
        q = q_ref[0, pl.ds(r0, c), :]
        kq = _dot_nt(jnp.concatenate([k, q], axis=0), k)
        loaded.append((chunk, r0, q, k, kq))

    chains = []
    for chunk, r0, q, k, kq in loaded:
        gb = gb_scr[pl.ds(r0, c), :]
        for d in range(2):
            col = head + d * GDN_HEADS
            beta = jnp.sum(jnp.where(lane == col, gb, 0.0), axis=-1, keepdims=True)
            g = jnp.sum(jnp.where(lane == col + 2 * GDN_HEADS, gb, 0.0), axis=-1, keepdims=True)
            incl, strict, strict_wide = masks[d]
            rhs = jnp.where(lane >= c, g, jnp.where(strict_wide, g, 0.0))
            e = _dot(incl.astype(F32), rhs, precision=lax.Precision.HIGHEST)
            chains.append(dict(chunk=chunk, r0=r0, d=d, q=q, k=k, kq=kq, beta=beta, e=e))

    for ch in chains:
        incl, strict, _ = masks[ch["d"]]
        e = ch["e"]
        decay = jnp.where(incl, jnp.exp(e[:, :c]), 0.0)
        gc = e[:, c:c + 1]
        last = 0 if ch["d"] == 1 else c - 1
        gc_last = e[last:last + 1, c:c + 1]
        ch.update(decay=decay, gc=gc, gc_last=gc_last, egc=jnp.exp(gc))
        ch["a"] = jnp.where(strict, ch["beta"] * ch["kq"][:c] * decay, 0.0)
        ch["t"] = eye
    s = 1
    while s < c:
        pair = jnp.logical_and((ii // (2 * s)) == (jj // (2 * s)), (ii // s) != (jj // s))
        for ch in chains:
            ch["a_off"] = jnp.where(pair, ch["a"], 0.0)
        if s == 1:
            for ch in chains:
                ch["t"] = eye - ch["a_off"]
        else:
            for ch in chains:
                ch["m"] = _dot(ch["t"].astype(BF16), ch["a_off"].astype(BF16))
            for ch in chains:
                ch["t"] = ch["t"] - _dot(ch["m"].astype(BF16), ch["t"].astype(BF16))
        s *= 2
    for ch in chains:
        r0, beta, egc = ch["r0"], ch["beta"], ch["egc"]
        kf = ch["k"].astype(F32)
        vf = v_ref[0, pl.ds(r0, c), :].astype(F32)
        rhs2 = jnp.concatenate([vf * beta, kf * (beta * egc)], axis=1).astype(BF16)
        ch["uw"] = _dot(ch["t"].astype(BF16), rhs2).astype(BF16)
        ch["k_dec_t"] = (kf * jnp.exp(ch["gc_last"] - ch["gc"])).T.astype(BF16)
    for ch in chains:
        incl = masks[ch["d"]][0]
        qk = jnp.where(incl, ch["kq"][c:] * ch["decay"], 0.0).astype(BF16)
        ch["nm"] = _dot(ch["k_dec_t"], ch["uw"])
        ch["ow"] = _dot(qk, ch["uw"])
    for ch in chains:
        chunk, r0, d, nm, ow = ch["chunk"], ch["r0"], ch["d"], ch["nm"], ch["ow"]
        q0 = pl.multiple_of(chunk * GDN_QM_ROWS, 16)
        qm_scr[d, pl.ds(q0, c), :] = (ch["q"].astype(F32) * ch["egc"] - ow[:, GDN_HEAD_DIM:]).astype(BF16)
        qm_scr[d, pl.ds(q0 + c, GDN_HEAD_DIM), :] = nm[:, GDN_HEAD_DIM:].astype(BF16)
        nn_scr[d, pl.ds(pl.multiple_of(chunk * GDN_HEAD_DIM, GDN_HEAD_DIM), GDN_HEAD_DIM), :] = nm[:, :GDN_HEAD_DIM]
        o_scr[d, pl.ds(r0, c), :] = ow[:, :GDN_HEAD_DIM]
        gl_scr[d, pl.ds(pl.multiple_of(chunk * GDN_GL_ROWS, GDN_GL_ROWS), GDN_GL_ROWS), :] = jnp.broadcast_to(
            jnp.exp(ch["gc_last"]), (GDN_GL_ROWS, LANES))


def _gdn_advance(qm_scr, nn_scr, o_scr, gl_scr, d, chunk, state):
    c = GDN_CHUNK
    qm = qm_scr[d, pl.ds(pl.multiple_of(chunk * GDN_QM_ROWS, 16), GDN_QM_ROWS), :]
    r = _dot(qm, state.astype(BF16))
    rows = pl.ds(pl.multiple_of(chunk * c, c), c)
    o_scr[d, rows, :] = o_scr[d, rows, :] + r[:c]
    gl = gl_scr[d, pl.ds(pl.multiple_of(chunk * GDN_GL_ROWS, GDN_GL_ROWS), 1), :]
    n = nn_scr[d, pl.ds(pl.multiple_of(chunk * GDN_HEAD_DIM, GDN_HEAD_DIM), GDN_HEAD_DIM), :]
    return state * gl - r[c:] + n


def _gdn_kernel(q_ref, k_ref, v_ref, z_ref, ba_ref, alog_ref, dtb_ref, gain_ref, o_ref,
                gb_scr, qm_scr, nn_scr, o_scr, gl_scr):
    head = pl.program_id(1)
    c = GDN_CHUNK
    n_lat, n_ctx = SEQ // c, CTX_LEN // c
    n_chunks = n_lat + n_ctx
    ba = ba_ref[0]
    lane = lax.broadcasted_iota(jnp.int32, ba.shape, 1)
    g_all = -jnp.exp(alog_ref[...]) * _softplus(ba + dtb_ref[...])
    gb_scr[...] = jnp.where(lane < 2 * GDN_HEADS, jax.nn.sigmoid(ba), g_all)

    def prepare(i, _):
        chunks = [i * GDN_PREP_UNROLL + j for j in range(GDN_PREP_UNROLL)]
        _gdn_prepare(q_ref, k_ref, v_ref, gb_scr, qm_scr, nn_scr, o_scr, gl_scr, chunks, head)
        return 0

    lax.fori_loop(0, n_chunks // GDN_PREP_UNROLL, prepare, 0)

    def advance(i, carry):
        s_f, s_b = carry
        cf = jnp.where(i < n_ctx, n_lat + i, i - n_ctx)
        cb = n_chunks - 1 - i
        s_f = _gdn_advance(qm_scr, nn_scr, o_scr, gl_scr, 0, cf, s_f)
        s_b = _gdn_advance(qm_scr, nn_scr, o_scr, gl_scr, 1, cb, s_b)
        return s_f, s_b

    zero = jnp.zeros((GDN_HEAD_DIM, GDN_HEAD_DIM), F32)
    lax.fori_loop(0, n_chunks, advance, (zero, zero))
    o = o_scr[0] + o_scr[1]
    o_ref[0] = (_rms(o, gain_ref[...]) * _silu(z_ref[0].astype(F32))).astype(BF16)


def _gdn(qkv, p, ba, a_log, dt_bias, out_gain):
    def pad_lanes(vals):
        row = jnp.zeros((LANES,), F32).at[2 * GDN_HEADS:4 * GDN_HEADS].set(vals.reshape(-1))
        return row.reshape(1, LANES)

    nh = GDN_HEADS
    n_chunks = TOK // GDN_CHUNK
    zblk = COL_Z // LANES
    blk = lambda off: pl.BlockSpec((1, TOK, LANES), lambda b, h: (b, 0, off + h))
    vec = pl.BlockSpec((1, LANES), lambda b, h: (0, 0))
    return pl.pallas_call(
        _gdn_kernel,
        grid=(BATCH, nh),
        in_specs=[blk(0), blk(nh), blk(2 * nh), blk(zblk),
                  pl.BlockSpec((1, TOK, LANES), lambda b, h: (b, 0, 0)), vec, vec, vec],
        out_specs=pl.BlockSpec((1, TOK, LANES), lambda b, h: (b, 0, h)),
        out_shape=jax.ShapeDtypeStruct((BATCH, TOK, BRANCH_WIDTH), BF16),
        scratch_shapes=[pltpu.VMEM((TOK, LANES), F32),
                        pltpu.VMEM((2, n_chunks * GDN_QM_ROWS, LANES), BF16),
                        pltpu.VMEM((2, n_chunks * GDN_HEAD_DIM, LANES), F32),
                        pltpu.VMEM((2, TOK, LANES), F32),
                        pltpu.VMEM((2, n_chunks * GDN_GL_ROWS, LANES), F32)],
        compiler_params=_params("parallel", "parallel"),
        name="gdn",
    )(qkv, qkv, qkv, p, ba, pad_lanes(a_log), pad_lanes(dt_bias), out_gain.reshape(1, LANES))


def _rms_halves(x, gain):
    lane = lax.broadcasted_iota(jnp.int32, x.shape, 1)
    lo = lane < DIFF_HEAD_DIM
    x2 = x * x
    s_lo = jnp.sum(jnp.where(lo, x2, 0.0), axis=-1, keepdims=True)
    s_hi = jnp.sum(jnp.where(lo, 0.0, x2), axis=-1, keepdims=True)
    ms = jnp.where(lo, s_lo, s_hi) * (1.0 / DIFF_HEAD_DIM)
    return x * lax.rsqrt(ms + EPS) * gain


def _rope(x, cos, sin):
    lane = lax.broadcasted_iota(jnp.int32, x.shape, 1)
    first = (lane & ROPE_PAIRS) == 0
    partner = jnp.where(first, -pltpu.roll(x, LANES - ROPE_PAIRS, 1), pltpu.roll(x, ROPE_PAIRS, 1))
    return x * cos + partner * sin


def _attn_kernel(q_ref, k_ref, v_ref, cosk_ref, sink_ref, cosq_ref, sinq_ref, qg_ref, kg_ref, lv_ref, og_ref,
                 o_ref, kn_scr, *, ctx_block, lam_init):
    qi = pl.program_id(2)

    @pl.when(qi == 0)
    def _():
        kn = _rope(_rms_halves(k_ref[0].astype(F32), kg_ref[...]), cosk_ref[...], sink_ref[...])
        kn_scr[...] = kn.astype(BF16)

    lv = lv_ref[...]
    lam = (jnp.exp(jnp.sum(lv[0:1] * lv[1:2], axis=-1, keepdims=True))
           - jnp.exp(jnp.sum(lv[2:3] * lv[3:4], axis=-1, keepdims=True)) + lam_init)
    q = _rope(_rms_halves(q_ref[0].astype(F32), qg_ref[...]), cosq_ref[...], sinq_ref[...])
    q = q * (DIFF_HEAD_DIM ** -0.5 * math.log2(math.e))
    lane = lax.broadcasted_iota(jnp.int32, q.shape, 1)
    lo = lane < DIFF_HEAD_DIM
    q1 = jnp.where(lo, q, 0.0).astype(BF16)
    q2 = jnp.where(lo, 0.0, q).astype(BF16)

    def attend(kn, v):
        def half(s):
            p = jnp.exp2(s - jnp.max(s, axis=-1, keepdims=True))
            return _dot(p.astype(BF16), v), jnp.sum(p, axis=-1, keepdims=True)
        s1 = _dot_nt(q1, kn)
        s2 = _dot_nt(q2, kn)
        a1, l1 = half(s1)
        a2, l2 = half(s2)
        o = a1 * (1.0 / l1) - a2 * (lam / l2)
        o_ref[0] = (_rms(o, og_ref[...]) * (1.0 - lam_init)).astype(BF16)

    if ctx_block is None:
        attend(kn_scr[...], v_ref[0])
    else:
        @pl.when(qi == ctx_block)
        def _():
            attend(kn_scr[SEQ:, :], v_ref[0, SEQ:, :])

        @pl.when(qi != ctx_block)
        def _():
            attend(kn_scr[...], v_ref[0])


def _rope_tables():
    n_rows = SEQ // GRID_W
    row_id = jnp.broadcast_to(jnp.arange(n_rows, dtype=F32)[:, None], (n_rows, GRID_W)).reshape(-1)
    col_id = jnp.broadcast_to(jnp.arange(GRID_W, dtype=F32)[None, :], (n_rows, GRID_W)).reshape(-1)
    inv_freq = jnp.power(ROPE_BASE, -jnp.arange(ROPE_PAIRS, dtype=F32) / ROPE_PAIRS)
    row_ang = row_id[:, None] * inv_freq
    col_ang = col_id[:, None] * inv_freq
    ang = jnp.concatenate([row_ang, row_ang, col_ang, col_ang], axis=-1)
    ang = jnp.concatenate([ang, ang], axis=-1)
    pad = ((0, CTX_LEN), (0, 0))
    return jnp.pad(jnp.cos(ang), pad, constant_values=1.0), jnp.pad(jnp.sin(ang), pad)


def _diff_attn(p, cos, sin, q_gain, k_gain, lam_vecs, out_gain, lam_init, with_ctx):
    tq = CTX_LEN
    nq = TOK // tq if with_ctx else SEQ // tq
    nh = DIFF_HEADS
    qb, kb, vb = COL_DQ // LANES, COL_DK // LANES, COL_DV // LANES
    full = lambda off: pl.BlockSpec((1, TOK, LANES), lambda b, h, i: (b, 0, off + h))
    tab_full = pl.BlockSpec((TOK, LANES), lambda b, h, i: (0, 0))
    tab_q = pl.BlockSpec((tq, LANES), lambda b, h, i: (i, 0))
    vec = pl.BlockSpec((1, LANES), lambda b, h, i: (0, 0))
    tile2 = lambda g: jnp.concatenate([g, g]).reshape(1, LANES)
    return pl.pallas_call(
        functools.partial(_attn_kernel, ctx_block=SEQ // tq if with_ctx else None, lam_init=lam_init),
        grid=(BATCH, nh, nq),
        in_specs=[pl.BlockSpec((1, tq, LANES), lambda b, h, i: (b, i, qb + h)), full(kb), full(vb),
                  tab_full, tab_full, tab_q, tab_q, vec, vec,
                  pl.BlockSpec((4, DIFF_HEAD_DIM), lambda b, h, i: (0, 0)), vec],
        out_specs=pl.BlockSpec((1, tq, LANES), lambda b, h, i: (b, i, h)),
        out_shape=jax.ShapeDtypeStruct((BATCH, nq * tq, BRANCH_WIDTH), BF16),
        scratch_shapes=[pltpu.VMEM((TOK, LANES), BF16)],
        compiler_params=_params("parallel", "parallel", "arbitrary"),
        name="diff_attn",
    )(p, p, p, cos, sin, cos, sin, tile2(q_gain), tile2(k_gain), lam_vecs, out_gain.reshape(1, LANES))


LRU_SLAB = 256
LRU_ROWS = 256
LRU_SCAN_BLOCK = 8


def _lru_kernel(xc_ref, y_ref, wg_ref, bg_ref, lam_ref, o_ref, af_scr, bf_scr, ab_scr, bb_scr):
    w = LRU_SLAB
    blk = LRU_SCAN_BLOCK
    sp = _softplus(-lam_ref[0])
    sub = lax.broadcasted_iota(jnp.int32, (LRU_ROWS, w), 0) % blk

    def gates(i, _):
        r0 = pl.multiple_of(i * LRU_ROWS, LRU_ROWS)
        xc = xc_ref[0, pl.ds(r0, LRU_ROWS), :]
        pre = _dot(xc.astype(BF16), wg_ref[0]) + bg_ref[0]
        for d, (a_scr, b_scr) in enumerate(((af_scr, bf_scr), (ab_scr, bb_scr))):
            r = jax.nn.sigmoid(pre[:, (2 * d) * w:(2 * d + 1) * w])
            gi = jax.nn.sigmoid(pre[:, (2 * d + 1) * w:(2 * d + 2) * w])
            log_a = -LRU_C * r * sp[d:d + 1]
            a = jnp.exp(log_a)
            b = jnp.sqrt(1.0 - a * a) * gi * xc
            shift = 1
            while shift < blk:
                if d == 0:
                    ok, roll_by = sub >= shift, shift
                else:
                    ok, roll_by = sub < blk - shift, LRU_ROWS - shift
                b = jnp.where(ok, a * pltpu.roll(b, roll_by, 0) + b, b)
                a = jnp.where(ok, a * pltpu.roll(a, roll_by, 0), a)
                shift *= 2
            a_scr[pl.ds(r0, LRU_ROWS), :] = a
            b_scr[pl.ds(r0, LRU_ROWS), :] = b
        return 0

    lax.fori_loop(0, TOK // LRU_ROWS, gates, 0)

    n_blk, n_lat_blk, n_ctx_blk = TOK // blk, SEQ // blk, CTX_LEN // blk

    def step(s, carry):
        h_f, h_b = carry
        rows_f = pl.ds(pl.multiple_of(jnp.where(s < n_ctx_blk, n_lat_blk + s, s - n_ctx_blk) * blk, blk), blk)
        rows_b = pl.ds(pl.multiple_of((n_blk - 1 - s) * blk, blk), blk)
        hf = af_scr[rows_f, :] * h_f + bf_scr[rows_f, :]
        bf_scr[rows_f, :] = hf
        hb = ab_scr[rows_b, :] * h_b + bb_scr[rows_b, :]
        bb_scr[rows_b, :] = hb
        return hf[blk - 1:blk, :], hb[0:1, :]

    zero = jnp.zeros((1, w), F32)
    lax.fori_loop(0, n_blk, step, (zero, zero), unroll=4)
    h = bf_scr[...] + bb_scr[...]
    o_ref[0] = (h * jax.nn.gelu(y_ref[0].astype(F32))).astype(BF16)


def _lru_gate_weights(w_gate, b_gate):
    n_slab = BRANCH_WIDTH // LRU_SLAB
    per = LRU_SLAB // LRU_BLOCK_DIM
    wg = w_gate.reshape(2, 2, n_slab, per, LRU_BLOCK_DIM, LRU_BLOCK_DIM)
    eye = jnp.eye(per, dtype=w_gate.dtype)
    dense = jnp.einsum('dgsnjk,nm->snjdgmk', wg, eye)
    dense = dense.reshape(n_slab, LRU_SLAB, 4 * LRU_SLAB)
    bg = b_gate.reshape(2, 2, n_slab, LRU_SLAB).transpose(2, 0, 1, 3).reshape(n_slab, 1, 4 * LRU_SLAB)
    return dense.astype(BF16), bg


def _lru(xc, p, w_gate, b_gate, lam):
    n_slab = BRANCH_WIDTH // LRU_SLAB
    wg, bg = _lru_gate_weights(w_gate, b_gate)
    lam_s = lam.reshape(2, n_slab, LRU_SLAB).transpose(1, 0, 2)
    yb = COL_LY // LRU_SLAB
    return pl.pallas_call(
        _lru_kernel,
        grid=(BATCH, n_slab),
        in_specs=[pl.BlockSpec((1, TOK, LRU_SLAB), lambda b, s: (b, 0, s)),
                  pl.BlockSpec((1, TOK, LRU_SLAB), lambda b, s: (b, 0, yb + s)),
                  pl.BlockSpec((1, LRU_SLAB, 4 * LRU_SLAB), lambda b, s: (s, 0, 0)),
                  pl.BlockSpec((1, 1, 4 * LRU_SLAB), lambda b, s: (s, 0, 0)),
                  pl.BlockSpec((1, 2, LRU_SLAB), lambda b, s: (s, 0, 0))],
        out_specs=pl.BlockSpec((1, TOK, LRU_SLAB), lambda b, s: (b, 0, s)),
        out_shape=jax.ShapeDtypeStruct((BATCH, TOK, BRANCH_WIDTH), BF16),
        scratch_shapes=[pltpu.VMEM((TOK, LRU_SLAB), F32)] * 4,
        compiler_params=_params("parallel", "parallel"),
        name="lru",
    )(xc, p, wg, bg, lam_s)


def _route(logits):
    lane = lax.broadcasted_iota(jnp.int32, logits.shape, 1)
    lane_f = lane.astype(F32)
    far = float(LANES)
    lg = jnp.where(lane < N_EXPERTS, logits, -jnp.inf)
    ex = jnp.exp(lg - jnp.max(lg, axis=-1, keepdims=True))
    probs = ex / jnp.sum(ex, axis=-1, keepdims=True)
    per_group = N_EXPERTS // N_GROUPS
    grp = lane // per_group

    def top2(vals):
        m1 = jnp.max(vals, axis=-1, keepdims=True)
        i1 = jnp.min(jnp.where(vals == m1, lane_f, far), axis=-1, keepdims=True)
        rest = jnp.where(lane_f == i1, -2.0, vals)
        m2 = jnp.max(rest, axis=-1, keepdims=True)
        i2 = jnp.min(jnp.where(rest == m2, lane_f, far), axis=-1, keepdims=True)
        return m1, i1, m2, i2

    best = jnp.zeros(logits.shape[:1] + (1,), jnp.int32)
    best_score = None
    for g in range(N_GROUPS):
        m1, _, m2, _ = top2(jnp.where(grp == g, probs, -1.0))
        score = m1 + m2
        if best_score is None:
            best_score = score
        else:
            better = score > best_score
            best = jnp.where(better, g, best)
            best_score = jnp.where(better, score, best_score)
    m1, i1, m2, i2 = top2(jnp.where(grp == best, probs, -1.0))
    den = m1 + m2
    return i1, i2, m1 / den, m2 / den


ROUTE_E, ROUTE_W, ROUTE_RANK = 0, 2, 4


def _pack_bf16_pairs(x):
    n = x.shape[1] // 2
    xb = x.astype(BF16).astype(F32)
    lo = pltpu.bitcast(xb[:, :n], jnp.uint32)
    hi = pltpu.bitcast(xb[:, n:], jnp.uint32)
    return (lo >> 16) | (hi & jnp.uint32(0xFFFF0000))


def _unpack_bf16_pairs(p):
    lo = pltpu.bitcast(p << 16, F32)
    hi = pltpu.bitcast(p & jnp.uint32(0xFFFF0000), F32)
    return jnp.concatenate([lo, hi], axis=1)


def _merge_kernel(ya_ref, yb_ref, yc_ref, gates_ref, x_ref, ml_ref, mc_ref, g2_ref, wbr_ref, wout_ref, wr_ref,
                  br_ref, xo_ref, h2_ref, route_ref, cnt_ref, cnt_scr, *, tm):
    i = pl.program_id(1)

    @pl.when(jnp.logical_and(pl.program_id(0) == 0, i == 0))
    def _():
        cnt_scr[...] = jnp.zeros_like(cnt_scr)

    acc = None
    for n, y_ref in enumerate((ya_ref, yb_ref, yc_ref)):
        yn = _dot(y_ref[0], wbr_ref[n])
        gate = jax.nn.sigmoid(gates_ref[0, :, n * D_MODEL:(n + 1) * D_MODEL].astype(F32))
        acc = gate * yn if acc is None else acc + gate * yn
    out = _dot(acc.astype(BF16), wout_ref[...])
    is_ctx = _ctx_rows(i, tm, D_MODEL)
    xn = x_ref[0] + jnp.where(is_ctx, mc_ref[0, 2:3, :], ml_ref[0, 2:3, :]) * out
    xo_ref[0] = xn
    h2 = _modulated_norm(xn, g2_ref[...], ml_ref, mc_ref, is_ctx, 3)
    h2_ref[0] = _pack_bf16_pairs(h2)
    i1, i2, w1, w2 = _route(_dot(h2.astype(BF16), wr_ref[...]) + br_ref[...])
    lane = lax.broadcasted_iota(jnp.int32, (tm, LANES), 1)
    lane_f = lane.astype(F32)
    chosen = jnp.where(jnp.logical_or(lane_f == i1, lane_f == i2), 1.0, 0.0)
    earlier = (lax.broadcasted_iota(jnp.int32, (tm, tm), 0) > lax.broadcasted_iota(jnp.int32, (tm, tm), 1))
    before = _dot(jnp.where(earlier, 1.0, 0.0).astype(BF16), chosen.astype(BF16)) + cnt_scr[...]
    rank1 = jnp.sum(jnp.where(lane_f == i1, before, 0.0), axis=-1, keepdims=True)
    rank2 = jnp.sum(jnp.where(lane_f == i2, before, 0.0), axis=-1, keepdims=True)
    cnt_scr[...] += jnp.sum(chosen, axis=0, keepdims=True)
    cnt_ref[...] = cnt_scr[...]
    record = jnp.zeros((tm, LANES), F32)
    for pos, val in enumerate((i1, i2, w1, w2, rank1, rank2)):
        record = jnp.where(lane == pos, val, record)
    route_ref[0] = record


def _merge(ya, yb, yc, p, xs, mods, gain2, w_branch, w_out, w_router, b_router, rows, tm):
    wr = jnp.zeros((D_MODEL, LANES), BF16).at[:, :N_EXPERTS].set(w_router.astype(BF16))
    br = jnp.zeros((1, LANES), F32).at[0, :N_EXPERTS].set(b_router)
    tile = lambda w: pl.BlockSpec((1, tm, w), lambda b, i: (b, i, 0))
    const = lambda shape: pl.BlockSpec(shape, lambda b, i: (0,) * len(shape))
    return pl.pallas_call(
        functools.partial(_merge_kernel, tm=tm),
        grid=(BATCH, rows // tm),
        in_specs=[tile(BRANCH_WIDTH), tile(BRANCH_WIDTH), tile(BRANCH_WIDTH), tile(3 * D_MODEL), tile(D_MODEL),
                  pl.BlockSpec((1, N_MOD, D_MODEL), lambda b, i: (b, 0, 0)),
                  pl.BlockSpec((1, N_MOD, D_MODEL), lambda b, i: (BATCH, 0, 0)),
                  const((1, D_MODEL)), const((3, BRANCH_WIDTH, D_MODEL)), const((D_MODEL, D_MODEL)),
                  const((D_MODEL, LANES)), const((1, LANES))],
        out_specs=[tile(D_MODEL), tile(D_MODEL // 2), tile(LANES), const((1, LANES))],
        out_shape=[jax.ShapeDtypeStruct((BATCH, rows, D_MODEL), F32),
                   jax.ShapeDtypeStruct((BATCH, rows, D_MODEL // 2), jnp.uint32),
                   jax.ShapeDtypeStruct((BATCH, rows, LANES), F32),
                   jax.ShapeDtypeStruct((1, LANES), F32)],
        scratch_shapes=[pltpu.VMEM((1, LANES), F32)],
        compiler_params=_params("arbitrary", "arbitrary"),
        name="merge",
    )(ya, yb, yc, p, xs, mods, mods, gain2.reshape(1, D_MODEL), w_branch, w_out, wr, br)


MOE_TILE = 512
SC_GATHER_ROWS = 64


def _sc_gather(table, idx):
    info = plsc.get_sparse_core_info()
    n_workers = info.num_cores * info.num_subcores
    n_rows, width = idx.shape[0], table.shape[1]
    per_worker = n_rows // n_workers
    assert per_worker * n_workers == n_rows and per_worker % SC_GATHER_ROWS == 0
    mesh = plsc.VectorSubcoreMesh(core_axis_name="c", subcore_axis_name="s")

    @functools.partial(
        pl.kernel, mesh=mesh, out_type=jax.ShapeDtypeStruct((n_rows, width), table.dtype),
        scratch_types=[pltpu.VMEM((SC_GATHER_ROWS,), jnp.int32),
                       pltpu.VMEM((SC_GATHER_ROWS, width), table.dtype),
                       pltpu.SemaphoreType.DMA],
        name="sc_gather")
    def gather(table_hbm, idx_hbm, out_hbm, idx_v, rows_v, sem):
        worker = lax.axis_index("s") * info.num_cores + lax.axis_index("c")
        base = worker * per_worker

        @pl.loop(0, per_worker // SC_GATHER_ROWS)
        def _(j):
            off = base + j * SC_GATHER_ROWS
            pltpu.sync_copy(idx_hbm.at[pl.ds(off, SC_GATHER_ROWS)], idx_v)
            pltpu.async_copy(table_hbm.at[idx_v], rows_v, sem).wait()
            pltpu.sync_copy(rows_v, out_hbm.at[pl.ds(off, SC_GATHER_ROWS)])

    return gather(table, idx)


SC_SCATTER_ROWS = 128


def _sc_scatter(table, dest, n_out):
    info = plsc.get_sparse_core_info()
    n_workers = info.num_cores * info.num_subcores
    n_tok, width = table.shape
    n_assign = dest.shape[0]
    per_worker = n_assign // n_workers
    assert per_worker * n_workers == n_assign and per_worker % SC_SCATTER_ROWS == 0 and n_tok % SC_SCATTER_ROWS == 0
    mesh = plsc.VectorSubcoreMesh(core_axis_name="c", subcore_axis_name="s")

    @functools.partial(
        pl.kernel, mesh=mesh, out_type=jax.ShapeDtypeStruct((n_out, width), table.dtype),
        scratch_types=[pltpu.VMEM((SC_SCATTER_ROWS,), jnp.int32),
                       pltpu.VMEM((SC_SCATTER_ROWS, width), table.dtype),
                       pltpu.SemaphoreType.DMA],
        name="sc_scatter")
    def scatter(table_hbm, dest_hbm, out_hbm, idx_v, rows_v, sem):
        worker = lax.axis_index("s") * info.num_cores + lax.axis_index("c")
        base = worker * per_worker

        @pl.loop(0, per_worker // SC_SCATTER_ROWS)
        def _(j):
            off = base + j * SC_SCATTER_ROWS
            pltpu.sync_copy(dest_hbm.at[pl.ds(off, SC_SCATTER_ROWS)], idx_v)
            pltpu.sync_copy(table_hbm.at[pl.ds(lax.rem(off, n_tok), SC_SCATTER_ROWS)], rows_v)
            pltpu.async_copy(rows_v, out_hbm.at[idx_v], sem).wait()

    return scatter(table, dest)


def _experts_kernel(tile_expert_ref, n_valid_ref, x_ref, wgu_ref, wd_ref, o_ref, wgu_scr, wd_scr):
    j = pl.program_id(0)
    valid = j < n_valid_ref[0]
    fresh = jnp.logical_or(j == 0, tile_expert_ref[j] != tile_expert_ref[jnp.maximum(j - 1, 0)])

    @pl.when(jnp.logical_and(valid, fresh))
    def _():
        wgu_scr[...] = wgu_ref[0].astype(BF16)
        wd_scr[...] = wd_ref[0].astype(BF16)

    @pl.when(valid)
    def _():
        x = _unpack_bf16_pairs(x_ref[...]).astype(BF16)
        gu = _dot(x, wgu_scr[...])
        act = (_silu(gu[:, :EXPERT_FF]) * gu[:, EXPERT_FF:]).astype(BF16)
        o_ref[...] = _pack_bf16_pairs(_dot(act, wd_scr[...]))

    @pl.when(jnp.logical_not(valid))
    def _():
        o_ref[...] = jnp.zeros_like(o_ref)


def _experts(x_sorted, tile_expert, n_valid, w_gate_up, w_down):
    n_tiles = x_sorted.shape[0] // MOE_TILE
    half = D_MODEL // 2
    return pl.pallas_call(
        _experts_kernel,
        grid_spec=pltpu.PrefetchScalarGridSpec(
            num_scalar_prefetch=2, grid=(n_tiles,),
            in_specs=[pl.BlockSpec((MOE_TILE, half), lambda j, te, nv: (j, 0)),
                      pl.BlockSpec((1, D_MODEL, 2 * EXPERT_FF), lambda j, te, nv: (te[j], 0, 0)),
                      pl.BlockSpec((1, EXPERT_FF, D_MODEL), lambda j, te, nv: (te[j], 0, 0))],
            out_specs=pl.BlockSpec((MOE_TILE, half), lambda j, te, nv: (j, 0)),
            scratch_shapes=[pltpu.VMEM((D_MODEL, 2 * EXPERT_FF), BF16), pltpu.VMEM((EXPERT_FF, D_MODEL), BF16)]),
        out_shape=jax.ShapeDtypeStruct((x_sorted.shape[0], half), jnp.uint32),
        compiler_params=_params("arbitrary"),
        name="experts",
    )(tile_expert, n_valid, x_sorted, w_gate_up, w_down)


def _combine_kernel(y1_ref, y2_ref, route_ref, x_ref, ml_ref, mc_ref, o_ref, *, tm):
    i = pl.program_id(1)
    route = route_ref[0]
    w1 = route[:, ROUTE_W:ROUTE_W + 1]
    w2 = route[:, ROUTE_W + 1:ROUTE_W + 2]
    moe = w1 * _unpack_bf16_pairs(y1_ref[0, 0]) + w2 * _unpack_bf16_pairs(y2_ref[0, 0])
    is_ctx = _ctx_rows(i, tm, D_MODEL)
    o_ref[0] = x_ref[0] + jnp.where(is_ctx, mc_ref[0, 5:6, :], ml_ref[0, 5:6, :]) * moe


def _combine(y_pairs, route, xs, mods, rows, tm):
    half = D_MODEL // 2
    tile = lambda w: pl.BlockSpec((1, tm, w), lambda b, i: (b, i, 0))
    slot = lambda s: pl.BlockSpec((1, 1, tm, half), lambda b, i: (s, b, i, 0))
    return pl.pallas_call(
        functools.partial(_combine_kernel, tm=tm),
        grid=(BATCH, rows // tm),
        in_specs=[slot(0), slot(1), tile(LANES), tile(D_MODEL),
                  pl.BlockSpec((1, N_MOD, D_MODEL), lambda b, i: (b, 0, 0)),
                  pl.BlockSpec((1, N_MOD, D_MODEL), lambda b, i: (BATCH, 0, 0))],
        out_specs=tile(D_MODEL),
        out_shape=jax.ShapeDtypeStruct((BATCH, rows, D_MODEL), F32),
        compiler_params=_params("parallel", "parallel"),
        name="combine",
    )(y_pairs, y_pairs, route, xs, mods, mods)


def _moe(h2, route, counts, xs, mods, w_gate_up, w_down, rows, tm):
    n_tok = BATCH * rows
    half = D_MODEL // 2
    n_sorted = 2 * n_tok + N_EXPERTS * MOE_TILE
    n_tiles = n_sorted // MOE_TILE
    rec = route.reshape(n_tok, LANES)
    expert = rec[:, ROUTE_E:ROUTE_E + 2].astype(jnp.int32)
    rank = rec[:, ROUTE_RANK:ROUTE_RANK + 2].astype(jnp.int32)
    count = counts[0, :N_EXPERTS].astype(jnp.int32)
    padded = (count + MOE_TILE - 1) // MOE_TILE * MOE_TILE
    end = jnp.cumsum(padded)
    start = end - padded
    first = jnp.sum(jnp.where(expert[:, :1] == jnp.arange(N_EXPERTS), start, 0), axis=1)
    second = jnp.sum(jnp.where(expert[:, 1:] == jnp.arange(N_EXPERTS), start, 0), axis=1)
    dest = jnp.concatenate([first + rank[:, 0], second + rank[:, 1]])
    tile_start = jnp.arange(n_tiles, dtype=jnp.int32) * MOE_TILE
    tile_expert = jnp.minimum(jnp.sum(tile_start[:, None] >= end[None, :], axis=1), N_EXPERTS - 1).astype(jnp.int32)
    n_valid = (end[-1:] // MOE_TILE).astype(jnp.int32)
    x_sorted = _sc_scatter(h2.reshape(n_tok, half), dest, n_sorted)
    y_sorted = _experts(x_sorted, tile_expert, n_valid, w_gate_up, w_down)
    y_pairs = _sc_gather(y_sorted, dest).reshape(2, BATCH, rows, half)
    return _combine(y_pairs, route, xs, mods, rows, tm)


def _split_w_in(w_in):
    bw = BRANCH_WIDTH
    sizes = (3 * bw, bw, 2 * GDN_HEADS, 2 * GDN_HEADS, bw, bw, bw, bw, bw, 3 * D_MODEL)
    offs = [0]
    for s in sizes:
        offs.append(offs[-1] + s)
    part = lambda i: w_in[:, offs[i]:offs[i + 1]]
    main = jnp.concatenate([part(9), part(0), part(1), part(4), part(5), part(6), part(7), part(8)], axis=1)
    ba = jnp.zeros((D_MODEL, LANES), F32).at[:, :4 * GDN_HEADS].set(jnp.concatenate([part(2), part(3)], axis=1))
    return main.astype(BF16), ba.astype(BF16)


def kernel(x, c, ctx, c_ctx, w_mod, b_mod, norm1_gain, norm2_gain, w_in, gdn_conv_w, gdn_a_log, gdn_dt_bias, gdn_out_gain, diff_q_gain, diff_k_gain, diff_lambda, diff_out_gain, lru_conv_w, lru_conv_b, lru_w_gate, lru_b_gate, lru_lambda, w_branch, w_out, w_router, b_router, w_gate_up, w_down):
    mods = _mods(c, c_ctx, w_mod, b_mod)
    cos, sin = _rope_tables()
    xs = jnp.concatenate([x, ctx], axis=1)
    for layer in range(DEPTH):
        last = layer == DEPTH - 1
        lam_init = 0.8 - 0.6 * math.exp(-0.3 * layer)
        m = mods[layer]
        w_main, w_ba = _split_w_in(w_in[layer])
        p, ba = _project(xs, m, norm1_gain[layer], w_main, w_ba)
        qkv = _gdn_conv(p, gdn_conv_w[layer])
        ya = _gdn(qkv, p, ba, gdn_a_log[layer], gdn_dt_bias[layer], gdn_out_gain[layer])
        yb = _diff_attn(p, cos, sin, diff_q_gain[layer], diff_k_gain[layer], diff_lambda[layer],
                        diff_out_gain[layer], lam_init, with_ctx=not last)
        xc = _lru_conv(p, lru_conv_w[layer], lru_conv_b[layer])
        yc = _lru(xc, p, lru_w_gate[layer], lru_b_gate[layer], lru_lambda[layer])
        rows, tm = (SEQ, 512) if last else (TOK, 768)
        xs, h2, route, counts = _merge(ya, yb, yc, p, xs, m, norm2_gain[layer], w_branch[layer].astype(BF16),
                                       w_out[layer].astype(BF16), w_router, b_router, rows, tm)
        xs = _moe(h2, route, counts, xs, m, w_gate_up[layer], w_down[layer], rows, tm)
    return xs
```

```python
import functools
import math

import jax
import jax.numpy as jnp
from jax import lax
from jax.experimental import pallas as pl
from jax.experimental.pallas import tpu as pltpu
from jax.experimental.pallas import tpu_sc as plsc

F32 = jnp.float32
BF16 = jnp.bfloat16

D_MODEL = 1024
BATCH = 8
SEQ = 2048
DEPTH = 2
GRID_W = 64
CTX_LEN = 256
TOK = SEQ + CTX_LEN
N_MOD = 6
EPS = 1e-6
CONV_WIDTH = 4
BRANCH_WIDTH = 512
GDN_HEADS = 4
GDN_HEAD_DIM = 128
GDN_CHUNK = 64
DIFF_HEADS = 4
DIFF_HEAD_DIM = 64
ROPE_BASE = 10000.0
ROPE_PAIRS = DIFF_HEAD_DIM // 4
LRU_BLOCKS = 8
LRU_BLOCK_DIM = BRANCH_WIDTH // LRU_BLOCKS
LRU_C = 8.0
N_EXPERTS = 16
N_GROUPS = 4
EXPERT_FF = 512

LANES = 128
VMEM_LIMIT = 56 * 1024 * 1024

COL_GATES = 0
COL_QKV = 3 * D_MODEL
COL_Z = COL_QKV + 3 * BRANCH_WIDTH
COL_DQ = COL_Z + BRANCH_WIDTH
COL_DK = COL_DQ + BRANCH_WIDTH
COL_DV = COL_DK + BRANCH_WIDTH
COL_LX = COL_DV + BRANCH_WIDTH
COL_LY = COL_LX + BRANCH_WIDTH
PROJ_COLS = COL_LY + BRANCH_WIDTH


def _params(*sem):
    return pltpu.CompilerParams(dimension_semantics=sem, vmem_limit_bytes=VMEM_LIMIT)


def _dot(a, b, precision=None):
    return jnp.dot(a, b, preferred_element_type=F32, precision=precision)


def _dot_nt(a, b):
    return lax.dot_general(a, b, (((1,), (1,)), ((), ())), preferred_element_type=F32)


def _silu(x):
    return x * jax.nn.sigmoid(x)


def _softplus(x):
    return jnp.maximum(x, 0.0) + jnp.log(1.0 + jnp.exp(-jnp.abs(x)))


def _rms(x, gain):
    return x * lax.rsqrt(jnp.mean(x * x, axis=-1, keepdims=True) + EPS) * gain


def _mod_kernel(c_ref, w_ref, b_ref, o_ref):
    c = c_ref[...]
    o_ref[0] = _dot(_silu(c), w_ref[0], precision=lax.Precision.HIGHEST) + b_ref[0]


def _mods(c, c_ctx, w_mod, b_mod):
    depth = w_mod.shape[0]
    rows = 16
    cc = jnp.zeros((rows, D_MODEL), F32).at[:BATCH].set(c).at[BATCH].set(c_ctx)
    tn = 1536
    out = pl.pallas_call(
        _mod_kernel,
        grid=(depth, N_MOD * D_MODEL // tn),
        in_specs=[pl.BlockSpec((rows, D_MODEL), lambda l, j: (0, 0)),
                  pl.BlockSpec((1, D_MODEL, tn), lambda l, j: (l, 0, j)),
                  pl.BlockSpec((1, 1, tn), lambda l, j: (l, 0, j))],
        out_specs=pl.BlockSpec((1, rows, tn), lambda l, j: (l, 0, j)),
        out_shape=jax.ShapeDtypeStruct((depth, rows, N_MOD * D_MODEL), F32),
        compiler_params=_params("parallel", "parallel"),
        name="mods",
    )(cc, w_mod, b_mod.reshape(depth, 1, N_MOD * D_MODEL))
    return out.reshape(depth, rows, N_MOD, D_MODEL)


def _modulated_norm(x, gain, ml_ref, mc_ref, is_ctx, shift_idx):
    shift = jnp.where(is_ctx, mc_ref[0, shift_idx:shift_idx + 1, :], ml_ref[0, shift_idx:shift_idx + 1, :])
    scale = jnp.where(is_ctx, mc_ref[0, shift_idx + 1:shift_idx + 2, :], ml_ref[0, shift_idx + 1:shift_idx + 2, :])
    return _rms(x, gain) * (1.0 + scale) + shift


def _ctx_rows(tile, tm, width):
    row = tile * tm + lax.broadcasted_iota(jnp.int32, (tm, width), 0)
    return row >= SEQ


def _proj_kernel(x_ref, ml_ref, mc_ref, g_ref, w_ref, wba_ref, p_ref, ba_ref, h_scr, *, tm):
    i = pl.program_id(1)
    j = pl.program_id(2)

    @pl.when(j == 0)
    def _():
        is_ctx = _ctx_rows(i, tm, D_MODEL)
        h_scr[...] = _modulated_norm(x_ref[0], g_ref[...], ml_ref, mc_ref, is_ctx, 0).astype(BF16)

    h = h_scr[...]
    p_ref[0] = _dot(h, w_ref[...]).astype(BF16)
    ba_ref[0] = _dot(h, wba_ref[...])


def _project(xs, mods, gain, w_main, w_ba):
    tm, tn = 1152, 1280
    return pl.pallas_call(
        functools.partial(_proj_kernel, tm=tm),
        grid=(BATCH, TOK // tm, PROJ_COLS // tn),
        in_specs=[pl.BlockSpec((1, tm, D_MODEL), lambda b, i, j: (b, i, 0)),
                  pl.BlockSpec((1, N_MOD, D_MODEL), lambda b, i, j: (b, 0, 0)),
                  pl.BlockSpec((1, N_MOD, D_MODEL), lambda b, i, j: (BATCH, 0, 0)),
                  pl.BlockSpec((1, D_MODEL), lambda b, i, j: (0, 0)),
                  pl.BlockSpec((D_MODEL, tn), lambda b, i, j: (0, j)),
                  pl.BlockSpec((D_MODEL, LANES), lambda b, i, j: (0, 0))],
        out_specs=[pl.BlockSpec((1, tm, tn), lambda b, i, j: (b, i, j)),
                   pl.BlockSpec((1, tm, LANES), lambda b, i, j: (b, i, 0))],
        out_shape=[jax.ShapeDtypeStruct((BATCH, TOK, PROJ_COLS), BF16),
                   jax.ShapeDtypeStruct((BATCH, TOK, LANES), F32)],
        scratch_shapes=[pltpu.VMEM((tm, D_MODEL), BF16)],
        compiler_params=_params("parallel", "parallel", "arbitrary"),
        name="proj",
    )(xs, mods, mods, gain.reshape(1, D_MODEL), w_main, w_ba)


def _conv(x, w):
    n, c = x.shape
    t = lax.broadcasted_iota(jnp.int32, (n, c), 0)
    is_ctx = t >= SEQ
    local = jnp.where(is_ctx, t - SEQ, t)
    seg_len = jnp.where(is_ctx, CTX_LEN, SEQ)
    y = jnp.zeros_like(x)
    for j in range(CONV_WIDTH):
        s = j - CONV_WIDTH // 2
        if s == 0:
            y = y + x * w[j:j + 1, :]
        else:
            shifted = pltpu.roll(x, (-s) % n, 0)
            ok = jnp.logical_and(local + s >= 0, local + s < seg_len)
            y = y + jnp.where(ok, shifted, 0.0) * w[j:j + 1, :]
    return y


def _gdn_conv_kernel(x_ref, w_ref, o_ref):
    j = pl.program_id(1)
    y = _silu(_conv(x_ref[0].astype(F32), w_ref[...]))
    nrm = lax.rsqrt(jnp.sum(y * y, axis=-1, keepdims=True) + EPS)
    n_head_blocks = GDN_HEADS
    scale = jnp.where(j < n_head_blocks, nrm * GDN_HEAD_DIM ** -0.5, jnp.where(j < 2 * n_head_blocks, nrm, 1.0))
    o_ref[0] = (y * scale).astype(BF16)


def _gdn_conv(p, conv_w):
    nblk = 3 * BRANCH_WIDTH // LANES
    first = COL_QKV // LANES
    return pl.pallas_call(
        _gdn_conv_kernel,
        grid=(BATCH, nblk),
        in_specs=[pl.BlockSpec((1, TOK, LANES), lambda b, j: (b, 0, first + j)),
                  pl.BlockSpec((CONV_WIDTH, LANES), lambda b, j: (0, j))],
        out_specs=pl.BlockSpec((1, TOK, LANES), lambda b, j: (b, 0, j)),
        out_shape=jax.ShapeDtypeStruct((BATCH, TOK, 3 * BRANCH_WIDTH), BF16),
        compiler_params=_params("parallel", "parallel"),
        name="gdn_conv",
    )(p, conv_w)


def _lru_conv_kernel(x_ref, w_ref, b_ref, o_ref):
    o_ref[0] = _conv(x_ref[0].astype(F32), w_ref[...]) + b_ref[...]


def _lru_conv(p, conv_w, conv_b):
    nblk = BRANCH_WIDTH // LANES
    first = COL_LX // LANES
    return pl.pallas_call(
        _lru_conv_kernel,
        grid=(BATCH, nblk),
        in_specs=[pl.BlockSpec((1, TOK, LANES), lambda b, j: (b, 0, first + j)),
                  pl.BlockSpec((CONV_WIDTH, LANES), lambda b, j: (0, j)),
                  pl.BlockSpec((1, LANES), lambda b, j: (0, j))],
        out_specs=pl.BlockSpec((1, TOK, LANES), lambda b, j: (b, 0, j)),
        out_shape=jax.ShapeDtypeStruct((BATCH, TOK, BRANCH_WIDTH), F32),
        compiler_params=_params("parallel", "parallel"),
        name="lru_conv",
    )(p, conv_w, conv_b.reshape(1, BRANCH_WIDTH))


GDN_QM_ROWS = GDN_CHUNK + GDN_HEAD_DIM
GDN_GL_ROWS = 8
GDN_HEADS_PER_STEP = 2
GDN_PREP_UNROLL = 9


def _gdn_prepare(q_ref, k_ref, v_ref, gb_scr, qm_scr, nn_scr, o_scr, gl_scr, chunks, head, local):
    c = GDN_CHUNK
    cols = slice(local * LANES, (local + 1) * LANES)
    lane = lax.broadcasted_iota(jnp.int32, (c, LANES), 1)
    row = lax.broadcasted_iota(jnp.int32, (c, LANES), 0)
    ii = lax.broadcasted_iota(jnp.int32, (c, c), 0)
    jj = lax.broadcasted_iota(jnp.int32, (c, c), 1)
    eye = (ii == jj).astype(F32)
    masks =((ii >= jj, ii > jj, row > lane), (ii <= jj, ii < jj, row < lane))

    loaded = []
    for chunk in chunks:
        r0 = pl.multiple_of(chunk * c, c)
        k = k_ref[0, pl.ds(r0, c), cols]
        q = q_ref[0, pl.ds(r0, c), cols]
        kq = _dot_nt(jnp.concatenate([k, q], axis=0), k)
        loaded.append((chunk, r0, q, k, kq))

    chains = []
    for chunk, r0, q, k, kq in loaded:
        gb = gb_scr[pl.ds(r0, c), :]
        for d in range(2):
            col = head + d * GDN_HEADS
            beta = jnp.sum(jnp.where(lane == col, gb, 0.0), axis=-1, keepdims=True)
            g = jnp.sum(jnp.where(lane == col + 2 * GDN_HEADS, gb, 0.0), axis=-1, keepdims=True)
            incl, strict, strict_wide = masks[d]
            rhs = jnp.where(lane >= c, g, jnp.where(strict_wide, g, 0.0))
            e = _dot(incl.astype(F32), rhs, precision=lax.Precision.HIGHEST)
            chains.append(dict(chunk=chunk, r0=r0, d=d, q=q, k=k, kq=kq, beta=beta, e=e))

    for ch in chains:
        incl, strict, _ = masks[ch["d"]]
        e = ch["e"]
        decay = jnp.where(incl, jnp.exp(e[:, :c]), 0.0)
        gc = e[:, c:c + 1]
        last = 0 if ch["d"] == 1 else c - 1
        gc_last = e[last:last + 1, c:c + 1]
        ch.update(decay=decay, gc=gc, gc_last=gc_last, egc=jnp.exp(gc))
        ch["a"] = jnp.where(strict, ch["beta"] * ch["kq"][:c] * decay, 0.0)
        ch["t"] = eye
    s = 1
    while s < c:
        pair = jnp.logical_and((ii // (2 * s)) == (jj // (2 * s)), (ii // s) != (jj // s))
        for ch in chains:
            ch["a_off"] = jnp.where(pair, ch["a"], 0.0)
        if s == 1:
            for ch in chains:
                ch["t"] = eye - ch["a_off"]
        else:
            for ch in chains:
                ch["m"] = _dot(ch["t"].astype(BF16), ch["a_off"].astype(BF16))
            for ch in chains:
                ch["t"] = ch["t"] - _dot(ch["m"].astype(BF16), ch["t"].astype(BF16))
        s *= 2
    for ch in chains:
        r0, beta, egc = ch["r0"], ch["beta"], ch["egc"]
        kf = ch["k"].astype(F32)
        vf = v_ref[0, pl.ds(r0, c), cols].astype(F32)
        rhs2 = jnp.concatenate([vf * beta, kf * (beta * egc)], axis=1).astype(BF16)
        ch["uw"] = _dot(ch["t"].astype(BF16), rhs2).astype(BF16)
        ch["k_dec_t"] = (kf * jnp.exp(ch["gc_last"] - ch["gc"])).T.astype(BF16)
    for ch in chains:
        incl = masks[ch["d"]][0]
        qk = jnp.where(incl, ch["kq"][c:] * ch["decay"], 0.0).astype(BF16)
        ch["nm"] = _dot(ch["k_dec_t"], ch["uw"])
        ch["ow"] = _dot(qk, ch["uw"])
    for ch in chains:
        chunk, r0, nm, ow = ch["chunk"], ch["r0"], ch["nm"], ch["ow"]
        s = 2 * local + ch["d"]
        q0 = pl.multiple_of(chunk * GDN_QM_ROWS, 16)
        qm_scr[s, pl.ds(q0, c), :] = (ch["q"].astype(F32) * ch["egc"] - ow[:, GDN_HEAD_DIM:]).astype(BF16)
        qm_scr[s, pl.ds(q0 + c, GDN_HEAD_DIM), :] = nm[:, GDN_HEAD_DIM:].astype(BF16)
        nn_scr[s, pl.ds(pl.multiple_of(chunk * GDN_HEAD_DIM, GDN_HEAD_DIM), GDN_HEAD_DIM), :] = nm[:, :GDN_HEAD_DIM]
        o_scr[s, pl.ds(r0, c), :] = ow[:, :GDN_HEAD_DIM]
        gl_scr[s, pl.ds(pl.multiple_of(chunk * GDN_GL_ROWS, GDN_GL_ROWS), GDN_GL_ROWS), :] = jnp.broadcast_to(
            jnp.exp(ch["gc_last"]), (GDN_GL_ROWS, LANES))


def _gdn_advance(qm_scr, nn_scr, o_scr, gl_scr, d, chunk, state):
    c = GDN_CHUNK
    qm = qm_scr[d, pl.ds(pl.multiple_of(chunk * GDN_QM_ROWS, 16), GDN_QM_ROWS), :]
    r = _dot(qm, state.astype(BF16))
    rows = pl.ds(pl.multiple_of(chunk * c, c), c)
    o_scr[d, rows, :] = o_scr[d, rows, :] + r[:c]
    gl = gl_scr[d, pl.ds(pl.multiple_of(chunk * GDN_GL_ROWS, GDN_GL_ROWS), 1), :]
    n = nn_scr[d, pl.ds(pl.multiple_of(chunk * GDN_HEAD_DIM, GDN_HEAD_DIM), GDN_HEAD_DIM), :]
    return state * gl - r[c:] + n


def _gdn_kernel(q_ref, k_ref, v_ref, z_ref, ba_ref, alog_ref, dtb_ref, gain_ref, o_ref,
                gb_scr, qm_scr, nn_scr, o_scr, gl_scr):
    first_head = pl.program_id(1) * GDN_HEADS_PER_STEP
    c = GDN_CHUNK
    n_lat, n_ctx = SEQ // c, CTX_LEN // c
    n_chunks = n_lat + n_ctx
    ba = ba_ref[0]
    lane = lax.broadcasted_iota(jnp.int32, ba.shape, 1)
    g_all = -jnp.exp(alog_ref[...]) * _softplus(ba + dtb_ref[...])
    gb_scr[...] = jnp.where(lane < 2 * GDN_HEADS, jax.nn.sigmoid(ba), g_all)

    for local in range(GDN_HEADS_PER_STEP):
        def prepare(i, _, local=local):
            chunks = [i * GDN_PREP_UNROLL + j for j in range(GDN_PREP_UNROLL)]
            _gdn_prepare(q_ref, k_ref, v_ref, gb_scr, qm_scr, nn_scr, o_scr, gl_scr, chunks, first_head + local, local)
            return 0

        lax.fori_loop(0, n_chunks // GDN_PREP_UNROLL, prepare, 0)

    def advance(i, states):
        cf = jnp.where(i < n_ctx, n_lat + i, i - n_ctx)
        cb = n_chunks - 1 - i
        return tuple(_gdn_advance(qm_scr, nn_scr, o_scr, gl_scr, s, cb if s % 2 else cf, state)
                     for s, state in enumerate(states))

    zero = jnp.zeros((GDN_HEAD_DIM, GDN_HEAD_DIM), F32)
    lax.fori_loop(0, n_chunks, advance, (zero,) * (2 * GDN_HEADS_PER_STEP))
    for local in range(GDN_HEADS_PER_STEP):
        cols = slice(local * LANES, (local + 1) * LANES)
        o = o_scr[2 * local] + o_scr[2 * local + 1]
        o_ref[0, :, cols] = (_rms(o, gain_ref[...]) * _silu(z_ref[0, :, cols].astype(F32))).astype(BF16)


def _gdn(qkv, p, ba, a_log, dt_bias, out_gain):
    def pad_lanes(vals):
        row = jnp.zeros((LANES,), F32).at[2 * GDN_HEADS:4 * GDN_HEADS].set(vals.reshape(-1))
        return row.reshape(1, LANES)

    n_steps = GDN_HEADS // GDN_HEADS_PER_STEP
    n_chunks = TOK // GDN_CHUNK
    width = GDN_HEADS_PER_STEP * LANES
    zblk = COL_Z // width
    n_chain = 2 * GDN_HEADS_PER_STEP
    blk = lambda off: pl.BlockSpec((1, TOK, width), lambda b, h: (b, 0, off + h))
    vec = pl.BlockSpec((1, LANES), lambda b, h: (0, 0))
    return pl.pallas_call(
        _gdn_kernel,
        grid=(BATCH, n_steps),
        in_specs=[blk(0), blk(n_steps), blk(2 * n_steps), blk(zblk),
                  pl.BlockSpec((1, TOK, LANES), lambda b, h: (b, 0, 0)), vec, vec, vec],
        out_specs=pl.BlockSpec((1, TOK, width), lambda b, h: (b, 0, h)),
        out_shape=jax.ShapeDtypeStruct((BATCH, TOK, BRANCH_WIDTH), BF16),
        scratch_shapes=[pltpu.VMEM((TOK, LANES), F32),
                        pltpu.VMEM((n_chain, n_chunks * GDN_QM_ROWS, LANES), BF16),
                        pltpu.VMEM((n_chain, n_chunks * GDN_HEAD_DIM, LANES), F32),
                        pltpu.VMEM((n_chain, TOK, LANES), F32),
                        pltpu.VMEM((n_chain, n_chunks * GDN_GL_ROWS, LANES), F32)],
        compiler_params=_params("parallel", "parallel"),
        name="gdn",
    )(qkv, qkv, qkv, p, ba, pad_lanes(a_log), pad_lanes(dt_bias), out_gain.reshape(1, LANES))


def _rms_halves(x, gain):
    lane = lax.broadcasted_iota(jnp.int32, x.shape, 1)
    lo = lane < DIFF_HEAD_DIM
    x2 = x * x
    s_lo = jnp.sum(jnp.where(lo, x2, 0.0), axis=-1, keepdims=True)
    s_hi = jnp.sum(jnp.where(lo, 0.0, x2), axis=-1, keepdims=True)
    ms = jnp.where(lo, s_lo, s_hi) * (1.0 / DIFF_HEAD_DIM)
    return x * lax.rsqrt(ms + EPS) * gain


def _rope(x, cos, sin):
    lane = lax.broadcasted_iota(jnp.int32, x.shape, 1)
    first = (lane & ROPE_PAIRS) == 0
    partner = jnp.where(first, -pltpu.roll(x, LANES - ROPE_PAIRS, 1), pltpu.roll(x, ROPE_PAIRS, 1))
    return x * cos + partner * sin


ATTN_ROW_GROUPS = 2


def _attn_kernel(q_ref, k_ref, v_ref, cosk_ref, sink_ref, cosq_ref, sinq_ref, qg_ref, kg_ref, lv_ref, og_ref,
                 o_ref, kn_scr, *, ctx_block, lam_init):
    qi = pl.program_id(2)

    @pl.when(qi == 0)
    def _():
        kn = _rope(_rms_halves(k_ref[0].astype(F32), kg_ref[...]), cosk_ref[...], sink_ref[...])
        kn_scr[...] = kn.astype(BF16)

    lv = lv_ref[...]
    lam = (jnp.exp(jnp.sum(lv[0:1] * lv[1:2], axis=-1, keepdims=True))
           - jnp.exp(jnp.sum(lv[2:3] * lv[3:4], axis=-1, keepdims=True)) + lam_init)
    q = _rope(_rms_halves(q_ref[0].astype(F32), qg_ref[...]), cosq_ref[...], sinq_ref[...])
    q = q * (DIFF_HEAD_DIM ** -0.5 * math.log2(math.e))
    lane = lax.broadcasted_iota(jnp.int32, q.shape, 1)
    lo = lane < DIFF_HEAD_DIM
    q1 = jnp.where(lo, q, 0.0).astype(BF16)
    q2 = jnp.where(lo, 0.0, q).astype(BF16)

    def attend(kn, v):
        def half(s):
            p = jnp.exp2(s - jnp.max(s, axis=-1, keepdims=True))
            return _dot(p.astype(BF16), v), jnp.sum(p, axis=-1, keepdims=True)
        rows = q1.shape[0] // ATTN_ROW_GROUPS
        scores = [(_dot_nt(q1[g * rows:(g + 1) * rows], kn), _dot_nt(q2[g * rows:(g + 1) * rows], kn))
                  for g in range(ATTN_ROW_GROUPS)]
        for g, (s1, s2) in enumerate(scores):
            a1, l1 = half(s1)
            a2, l2 = half(s2)
            o = a1 * (1.0 / l1) - a2 * (lam / l2)
            o_ref[0, g * rows:(g + 1) * rows, :] = (_rms(o, og_ref[...]) * (1.0 - lam_init)).astype(BF16)

    if ctx_block is None:
        attend(kn_scr[...], v_ref[0])
    else:
        @pl.when(qi == ctx_block)
        def _():
            attend(kn_scr[SEQ:, :], v_ref[0, SEQ:, :])

        @pl.when(qi != ctx_block)
        def _():
            attend(kn_scr[...], v_ref[0])


def _rope_tables():
    n_rows = SEQ // GRID_W
    row_id = jnp.broadcast_to(jnp.arange(n_rows, dtype=F32)[:, None], (n_rows, GRID_W)).reshape(-1)
    col_id = jnp.broadcast_to(jnp.arange(GRID_W, dtype=F32)[None, :], (n_rows, GRID_W)).reshape(-1)
    inv_freq = jnp.power(ROPE_BASE, -jnp.arange(ROPE_PAIRS, dtype=F32) / ROPE_PAIRS)
    row_ang = row_id[:, None] * inv_freq
    col_ang = col_id[:, None] * inv_freq
    ang = jnp.concatenate([row_ang, row_ang, col_ang, col_ang], axis=-1)
    ang = jnp.concatenate([ang, ang], axis=-1)
    pad = ((0, CTX_LEN), (0, 0))
    return jnp.pad(jnp.cos(ang), pad, constant_values=1.0), jnp.pad(jnp.sin(ang), pad)


def _diff_attn(p, cos, sin, q_gain, k_gain, lam_vecs, out_gain, lam_init, with_ctx):
    tq = CTX_LEN
    nq = TOK // tq if with_ctx else SEQ // tq
    nh = DIFF_HEADS
    qb, kb, vb = COL_DQ // LANES, COL_DK // LANES, COL_DV // LANES
    full = lambda off: pl.BlockSpec((1, TOK, LANES), lambda b, h, i: (b, 0, off + h))
    tab_full = pl.BlockSpec((TOK, LANES), lambda b, h, i: (0, 0))
    tab_q = pl.BlockSpec((tq, LANES), lambda b, h, i: (i, 0))
    vec = pl.BlockSpec((1, LANES), lambda b, h, i: (0, 0))
    tile2 = lambda g: jnp.concatenate([g, g]).reshape(1, LANES)
    return pl.pallas_call(
        functools.partial(_attn_kernel, ctx_block=SEQ // tq if with_ctx else None, lam_init=lam_init),
        grid=(BATCH, nh, nq),
        in_specs=[pl.BlockSpec((1, tq, LANES), lambda b, h, i: (b, i, qb + h)), full(kb), full(vb),
                  tab_full, tab_full, tab_q, tab_q, vec, vec,
                  pl.BlockSpec((4, DIFF_HEAD_DIM), lambda b, h, i: (0, 0)), vec],
        out_specs=pl.BlockSpec((1, tq, LANES), lambda b, h, i: (b, i, h)),
        out_shape=jax.ShapeDtypeStruct((BATCH, nq * tq, BRANCH_WIDTH), BF16),
        scratch_shapes=[pltpu.VMEM((TOK, LANES), BF16)],
        compiler_params=_params("parallel", "parallel", "arbitrary"),
        name="diff_attn",
    )(p, p, p, cos, sin, cos, sin, tile2(q_gain), tile2(k_gain), lam_vecs, out_gain.reshape(1, LANES))


LRU_SLAB = 256
LRU_ROWS = 256
LRU_SCAN_BLOCK = 8


def _lru_kernel(xc_ref, y_ref, wg_ref, bg_ref, lam_ref, o_ref, af_scr, bf_scr, ab_scr, bb_scr):
    w = LRU_SLAB
    blk = LRU_SCAN_BLOCK
    sp = _softplus(-lam_ref[0])
    sub = lax.broadcasted_iota(jnp.int32, (LRU_ROWS, w), 0) % blk

    def gates(i, _):
        r0 = pl.multiple_of(i * LRU_ROWS, LRU_ROWS)
        xc = xc_ref[0, pl.ds(r0, LRU_ROWS), :]
        pre = _dot(xc.astype(BF16), wg_ref[0]) + bg_ref[0]
        for d, (a_scr, b_scr) in enumerate(((af_scr, bf_scr), (ab_scr, bb_scr))):
            r = jax.nn.sigmoid(pre[:, (2 * d) * w:(2 * d + 1) * w])
            gi = jax.nn.sigmoid(pre[:, (2 * d + 1) * w:(2 * d + 2) * w])
            log_a = -LRU_C * r * sp[d:d + 1]
            a = jnp.exp(log_a)
            b = jnp.sqrt(1.0 - a * a) * gi * xc
            shift = 1
            while shift < blk:
                if d == 0:
                    ok, roll_by = sub >= shift, shift
                else:
                    ok, roll_by = sub < blk - shift, LRU_ROWS - shift
                b = jnp.where(ok, a * pltpu.roll(b, roll_by, 0) + b, b)
                a = jnp.where(ok, a * pltpu.roll(a, roll_by, 0), a)
                shift *= 2
            a_scr[pl.ds(r0, LRU_ROWS), :] = a
            b_scr[pl.ds(r0, LRU_ROWS), :] = b
        return 0

    lax.fori_loop(0, TOK // LRU_ROWS, gates, 0)

    n_blk, n_lat_blk, n_ctx_blk = TOK // blk, SEQ // blk, CTX_LEN // blk

    def step(s, carry):
        h_f, h_b = carry
        rows_f = pl.ds(pl.multiple_of(jnp.where(s < n_ctx_blk, n_lat_blk + s, s - n_ctx_blk) * blk, blk), blk)
        rows_b = pl.ds(pl.multiple_of((n_blk - 1 - s) * blk, blk), blk)
        hf = af_scr[rows_f, :] * h_f + bf_scr[rows_f, :]
        bf_scr[rows_f, :] = hf
        hb = ab_scr[rows_b, :] * h_b + bb_scr[rows_b, :]
        bb_scr[rows_b, :] = hb
        return hf[blk - 1:blk, :], hb[0:1, :]

    zero = jnp.zeros((1, w), F32)
    lax.fori_loop(0, n_blk, step, (zero, zero), unroll=4)
    h = bf_scr[...] + bb_scr[...]
    o_ref[0] = (h * jax.nn.gelu(y_ref[0].astype(F32))).astype(BF16)


def _lru_gate_weights(w_gate, b_gate):
    n_slab = BRANCH_WIDTH // LRU_SLAB
    per = LRU_SLAB // LRU_BLOCK_DIM
    wg = w_gate.reshape(2, 2, n_slab, per, LRU_BLOCK_DIM, LRU_BLOCK_DIM)
    eye = jnp.eye(per, dtype=w_gate.dtype)
    dense = jnp.einsum('dgsnjk,nm->snjdgmk', wg, eye)
    dense = dense.reshape(n_slab, LRU_SLAB, 4 * LRU_SLAB)
    bg = b_gate.reshape(2, 2, n_slab, LRU_SLAB).transpose(2, 0, 1, 3).reshape(n_slab, 1, 4 * LRU_SLAB)
    return dense.astype(BF16), bg


def _lru(xc, p, w_gate, b_gate, lam):
    n_slab = BRANCH_WIDTH // LRU_SLAB
    wg, bg = _lru_gate_weights(w_gate, b_gate)
    lam_s = lam.reshape(2, n_slab, LRU_SLAB).transpose(1, 0, 2)
    yb = COL_LY // LRU_SLAB
    return pl.pallas_call(
        _lru_kernel,
        grid=(BATCH, n_slab),
        in_specs=[pl.BlockSpec((1, TOK, LRU_SLAB), lambda b, s: (b, 0, s)),
                  pl.BlockSpec((1, TOK, LRU_SLAB), lambda b, s: (b, 0, yb + s)),
                  pl.BlockSpec((1, LRU_SLAB, 4 * LRU_SLAB), lambda b, s: (s, 0, 0)),
                  pl.BlockSpec((1, 1, 4 * LRU_SLAB), lambda b, s: (s, 0, 0)),
                  pl.BlockSpec((1, 2, LRU_SLAB), lambda b, s: (s, 0, 0))],
        out_specs=pl.BlockSpec((1, TOK, LRU_SLAB), lambda b, s: (b, 0, s)),
        out_shape=jax.ShapeDtypeStruct((BATCH, TOK, BRANCH_WIDTH), BF16),
        scratch_shapes=[pltpu.VMEM((TOK, LRU_SLAB), F32)] * 4,
        compiler_params=_params("parallel", "parallel"),
        name="lru",
    )(xc, p, wg, bg, lam_s)


def _route(logits):
    lane = lax.broadcasted_iota(jnp.int32, logits.shape, 1)
    lane_f = lane.astype(F32)
    far = float(LANES)
    lg = jnp.where(lane < N_EXPERTS, logits, -jnp.inf)
    ex = jnp.exp(lg - jnp.max(lg, axis=-1, keepdims=True))
    probs = ex / jnp.sum(ex, axis=-1, keepdims=True)
    per_group = N_EXPERTS // N_GROUPS
    grp = lane // per_group

    def top2(vals):
        m1 = jnp.max(vals, axis=-1, keepdims=True)
        i1 = jnp.min(jnp.where(vals == m1, lane_f, far), axis=-1, keepdims=True)
        rest = jnp.where(lane_f == i1, -2.0, vals)
        m2 = jnp.max(rest, axis=-1, keepdims=True)
        i2 = jnp.min(jnp.where(rest == m2, lane_f, far), axis=-1, keepdims=True)
        return m1, i1, m2, i2

    best = jnp.zeros(logits.shape[:1] + (1,), jnp.int32)
    best_score = None
    for g in range(N_GROUPS):
        m1, _, m2, _ = top2(jnp.where(grp == g, probs, -1.0))
        score = m1 + m2
        if best_score is None:
            best_score = score
        else:
            better = score > best_score
            best = jnp.where(better, g, best)
            best_score = jnp.where(better, score, best_score)
    m1, i1, m2, i2 = top2(jnp.where(grp == best, probs, -1.0))
    den = m1 + m2
    return i1, i2, m1 / den, m2 / den


ROUTE_E, ROUTE_W, ROUTE_RANK = 0, 2, 4


def _pack_bf16_pairs(x):
    n = x.shape[1] // 2
    xb = x.astype(BF16).astype(F32)
    lo = pltpu.bitcast(xb[:, :n], jnp.uint32)
    hi = pltpu.bitcast(xb[:, n:], jnp.uint32)
    return (lo >> 16) | (hi & jnp.uint32(0xFFFF0000))


def _unpack_bf16_pairs(p):
    lo = pltpu.bitcast(p << 16, F32)
    hi = pltpu.bitcast(p & jnp.uint32(0xFFFF0000), F32)
    return jnp.concatenate([lo, hi], axis=1)


def _merge_kernel(ya_ref, yb_ref, yc_ref, gates_ref, x_ref, ml_ref, mc_ref, g2_ref, wbr_ref, wout_ref, wr_ref,
                  br_ref, xo_ref, h2_ref, route_ref, cnt_ref, cnt_scr, *, tm):
    i = pl.program_id(1)

    @pl.when(jnp.logical_and(pl.program_id(0) == 0, i == 0))
    def _():
        cnt_scr[...] = jnp.zeros_like(cnt_scr)

    acc = None
    for n, y_ref in enumerate((ya_ref, yb_ref, yc_ref)):
        yn = _dot(y_ref[0], wbr_ref[n])
        gate = jax.nn.sigmoid(gates_ref[0, :, n * D_MODEL:(n + 1) * D_MODEL].astype(F32))
        acc = gate * yn if acc is None else acc + gate * yn
    out = _dot(acc.astype(BF16), wout_ref[...])
    is_ctx = _ctx_rows(i, tm, D_MODEL)
    xn = x_ref[0] + jnp.where(is_ctx, mc_ref[0, 2:3, :], ml_ref[0, 2:3, :]) * out
    xo_ref[0] = xn
    h2 = _modulated_norm(xn, g2_ref[...], ml_ref, mc_ref, is_ctx, 3)
    h2_ref[0] = _pack_bf16_pairs(h2)
    i1, i2, w1, w2 = _route(_dot(h2.astype(BF16), wr_ref[...]) + br_ref[...])
    lane = lax.broadcasted_iota(jnp.int32, (tm, LANES), 1)
    lane_f = lane.astype(F32)
    chosen = jnp.where(jnp.logical_or(lane_f == i1, lane_f == i2), 1.0, 0.0)
    earlier = (lax.broadcasted_iota(jnp.int32, (tm, tm), 0) > lax.broadcasted_iota(jnp.int32, (tm, tm), 1))
    before = _dot(jnp.where(earlier, 1.0, 0.0).astype(BF16), chosen.astype(BF16)) + cnt_scr[...]
    rank1 = jnp.sum(jnp.where(lane_f == i1, before, 0.0), axis=-1, keepdims=True)
    rank2 = jnp.sum(jnp.where(lane_f == i2, before, 0.0), axis=-1, keepdims=True)
    cnt_scr[...] += jnp.sum(chosen, axis=0, keepdims=True)
    cnt_ref[...] = cnt_scr[...]
    record = jnp.zeros((tm, LANES), F32)
    for pos, val in enumerate((i1, i2, w1, w2, rank1, rank2)):
        record = jnp.where(lane == pos, val, record)
    route_ref[0] = record


def _merge(ya, yb, yc, p, xs, mods, gain2, w_branch, w_out, w_router, b_router, rows, tm):
    wr = jnp.zeros((D_MODEL, LANES), BF16).at[:, :N_EXPERTS].set(w_router.astype(BF16))
    br = jnp.zeros((1, LANES), F32).at[0, :N_EXPERTS].set(b_router)
    tile = lambda w: pl.BlockSpec((1, tm, w), lambda b, i: (b, i, 0))
    const = lambda shape: pl.BlockSpec(shape, lambda b, i: (0,) * len(shape))
    return pl.pallas_call(
        functools.partial(_merge_kernel, tm=tm),
        grid=(BATCH, rows // tm),
        in_specs=[tile(BRANCH_WIDTH), tile(BRANCH_WIDTH), tile(BRANCH_WIDTH), tile(3 * D_MODEL), tile(D_MODEL),
                  pl.BlockSpec((1, N_MOD, D_MODEL), lambda b, i: (b, 0, 0)),
                  pl.BlockSpec((1, N_MOD, D_MODEL), lambda b, i: (BATCH, 0, 0)),
                  const((1, D_MODEL)), const((3, BRANCH_WIDTH, D_MODEL)), const((D_MODEL, D_MODEL)),
                  const((D_MODEL, LANES)), const((1, LANES))],
        out_specs=[tile(D_MODEL), tile(D_MODEL // 2), tile(LANES), const((1, LANES))],
        out_shape=[jax.ShapeDtypeStruct((BATCH, rows, D_MODEL), F32),
                   jax.ShapeDtypeStruct((BATCH, rows, D_MODEL // 2), jnp.uint32),
                   jax.ShapeDtypeStruct((BATCH, rows, LANES), F32),
                   jax.ShapeDtypeStruct((1, LANES), F32)],
        scratch_shapes=[pltpu.VMEM((1, LANES), F32)],
        compiler_params=_params("arbitrary", "arbitrary"),
        name="merge",
    )(ya, yb, yc, p, xs, mods, mods, gain2.reshape(1, D_MODEL), w_branch, w_out, wr, br)


MOE_TILE = 512
SC_GATHER_ROWS = 64


def _sc_gather(table, idx):
    info = plsc.get_sparse_core_info()
    n_workers = info.num_cores * info.num_subcores
    n_rows, width = idx.shape[0], table.shape[1]
    per_worker = n_rows // n_workers
    assert per_worker * n_workers == n_rows and per_worker % SC_GATHER_ROWS == 0
    mesh = plsc.VectorSubcoreMesh(core_axis_name="c", subcore_axis_name="s")

    @functools.partial(
        pl.kernel, mesh=mesh, out_type=jax.ShapeDtypeStruct((n_rows, width), table.dtype),
        scratch_types=[pltpu.VMEM((SC_GATHER_ROWS,), jnp.int32),
                       pltpu.VMEM((SC_GATHER_ROWS, width), table.dtype),
                       pltpu.SemaphoreType.DMA],
        name="sc_gather")
    def gather(table_hbm, idx_hbm, out_hbm, idx_v, rows_v, sem):
        worker = lax.axis_index("s") * info.num_cores + lax.axis_index("c")
        base = worker * per_worker

        @pl.loop(0, per_worker // SC_GATHER_ROWS)
        def _(j):
            off = base + j * SC_GATHER_ROWS
            pltpu.sync_copy(idx_hbm.at[pl.ds(off, SC_GATHER_ROWS)], idx_v)
            pltpu.async_copy(table_hbm.at[idx_v], rows_v, sem).wait()
            pltpu.sync_copy(rows_v, out_hbm.at[pl.ds(off, SC_GATHER_ROWS)])

    return gather(table, idx)


SC_SCATTER_ROWS = 128


def _sc_scatter(table, dest, n_out):
    info = plsc.get_sparse_core_info()
    n_workers = info.num_cores * info.num_subcores
    n_tok, width = table.shape
    n_assign = dest.shape[0]
    per_worker = n_assign // n_workers
    assert per_worker * n_workers == n_assign and per_worker % SC_SCATTER_ROWS == 0 and n_tok % SC_SCATTER_ROWS == 0
    mesh = plsc.VectorSubcoreMesh(core_axis_name="c", subcore_axis_name="s")

    @functools.partial(
        pl.kernel, mesh=mesh, out_type=jax.ShapeDtypeStruct((n_out, width), table.dtype),
        scratch_types=[pltpu.VMEM((SC_SCATTER_ROWS,), jnp.int32),
                       pltpu.VMEM((SC_SCATTER_ROWS, width), table.dtype),
                       pltpu.SemaphoreType.DMA],
        name="sc_scatter")
    def scatter(table_hbm, dest_hbm, out_hbm, idx_v, rows_v, sem):
        worker = lax.axis_index("s") * info.num_cores + lax.axis_index("c")
        base = worker * per_worker

        @pl.loop(0, per_worker // SC_SCATTER_ROWS)
        def _(j):
            off = base + j * SC_SCATTER_ROWS
            pltpu.sync_copy(dest_hbm.at[pl.ds(off, SC_SCATTER_ROWS)], idx_v)
            pltpu.sync_copy(table_hbm.at[pl.ds(lax.rem(off, n_tok), SC_SCATTER_ROWS)], rows_v)
            pltpu.async_copy(rows_v, out_hbm.at[idx_v], sem).wait()

    return scatter(table, dest)


def _experts_kernel(tile_expert_ref, n_valid_ref, x_ref, wgu_ref, wd_ref, o_ref, wgu_scr, wd_scr):
    j = pl.program_id(0)
    valid = j < n_valid_ref[0]
    fresh = jnp.logical_or(j == 0, tile_expert_ref[j] != tile_expert_ref[jnp.maximum(j - 1, 0)])

    @pl.when(jnp.logical_and(valid, fresh))
    def _():
        wgu_scr[...] = wgu_ref[0].astype(BF16)
        wd_scr[...] = wd_ref[0].astype(BF16)

    @pl.when(valid)
    def _():
        x = _unpack_bf16_pairs(x_ref[...]).astype(BF16)
        gu = _dot(x, wgu_scr[...])
        act = (_silu(gu[:, :EXPERT_FF]) * gu[:, EXPERT_FF:]).astype(BF16)
        o_ref[...] = _pack_bf16_pairs(_dot(act, wd_scr[...]))

    @pl.when(jnp.logical_not(valid))
    def _():
        o_ref[...] = jnp.zeros_like(o_ref)


def _experts(x_sorted, tile_expert, n_valid, w_gate_up, w_down):
    n_tiles = x_sorted.shape[0] // MOE_TILE
    half = D_MODEL // 2
    return pl.pallas_call(
        _experts_kernel,
        grid_spec=pltpu.PrefetchScalarGridSpec(
            num_scalar_prefetch=2, grid=(n_tiles,),
            in_specs=[pl.BlockSpec((MOE_TILE, half), lambda j, te, nv: (j, 0)),
                      pl.BlockSpec((1, D_MODEL, 2 * EXPERT_FF), lambda j, te, nv: (te[j], 0, 0)),
                      pl.BlockSpec((1, EXPERT_FF, D_MODEL), lambda j, te, nv: (te[j], 0, 0))],
            out_specs=pl.BlockSpec((MOE_TILE, half), lambda j, te, nv: (j, 0)),
            scratch_shapes=[pltpu.VMEM((D_MODEL, 2 * EXPERT_FF), BF16), pltpu.VMEM((EXPERT_FF, D_MODEL), BF16)]),
        out_shape=jax.ShapeDtypeStruct((x_sorted.shape[0], half), jnp.uint32),
        compiler_params=_params("arbitrary"),
        name="experts",
    )(tile_expert, n_valid, x_sorted, w_gate_up, w_down)


def _combine_kernel(y1_ref, y2_ref, route_ref, x_ref, ml_ref, mc_ref, o_ref, *, tm):
    i = pl.program_id(1)
    route = route_ref[0]
    w1 = route[:, ROUTE_W:ROUTE_W + 1]
    w2 = route[:, ROUTE_W + 1:ROUTE_W + 2]
    moe = w1 * _unpack_bf16_pairs(y1_ref[0, 0]) + w2 * _unpack_bf16_pairs(y2_ref[0, 0])
    is_ctx = _ctx_rows(i, tm, D_MODEL)
    o_ref[0] = x_ref[0] + jnp.where(is_ctx, mc_ref[0, 5:6, :], ml_ref[0, 5:6, :]) * moe


def _combine(y_pairs, route, xs, mods, rows, tm):
    half = D_MODEL // 2
    tile = lambda w: pl.BlockSpec((1, tm, w), lambda b, i: (b, i, 0))
    slot = lambda s: pl.BlockSpec((1, 1, tm, half), lambda b, i: (s, b, i, 0))
    return pl.pallas_call(
        functools.partial(_combine_kernel, tm=tm),
        grid=(BATCH, rows // tm),
        in_specs=[slot(0), slot(1), tile(LANES), tile(D_MODEL),
                  pl.BlockSpec((1, N_MOD, D_MODEL), lambda b, i: (b, 0, 0)),
                  pl.BlockSpec((1, N_MOD, D_MODEL), lambda b, i: (BATCH, 0, 0))],
        out_specs=tile(D_MODEL),
        out_shape=jax.ShapeDtypeStruct((BATCH, rows, D_MODEL), F32),
        compiler_params=_params("parallel", "parallel"),
        name="combine",
    )(y_pairs, y_pairs, route, xs, mods, mods)


def _moe(h2, route, counts, xs, mods, w_gate_up, w_down, rows, tm):
    n_tok = BATCH * rows
    half = D_MODEL // 2
    n_sorted = 2 * n_tok + N_EXPERTS * MOE_TILE
    n_tiles = n_sorted // MOE_TILE
    rec = route.reshape(n_tok, LANES)
    expert = rec[:, ROUTE_E:ROUTE_E + 2].astype(jnp.int32)
    rank = rec[:, ROUTE_RANK:ROUTE_RANK + 2].astype(jnp.int32)
    count = counts[0, :N_EXPERTS].astype(jnp.int32)
    padded = (count + MOE_TILE - 1) // MOE_TILE * MOE_TILE
    end = jnp.cumsum(padded)
    start = end - padded
    first = jnp.sum(jnp.where(expert[:, :1] == jnp.arange(N_EXPERTS), start, 0), axis=1)
    second = jnp.sum(jnp.where(expert[:, 1:] == jnp.arange(N_EXPERTS), start, 0), axis=1)
    dest = jnp.concatenate([first + rank[:, 0], second + rank[:, 1]])
    tile_start = jnp.arange(n_tiles, dtype=jnp.int32) * MOE_TILE
    tile_expert = jnp.minimum(jnp.sum(tile_start[:, None] >= end[None, :], axis=1), N_EXPERTS - 1).astype(jnp.int32)
    n_valid = (end[-1:] // MOE_TILE).astype(jnp.int32)
    x_sorted = _sc_scatter(h2.reshape(n_tok, half), dest, n_sorted)
    y_sorted = _experts(x_sorted, tile_expert, n_valid, w_gate_up, w_down)
    y_pairs = _sc_gather(y_sorted, dest).reshape(2, BATCH, rows, half)
    return _combine(y_pairs, route, xs, mods, rows, tm)


def _split_w_in(w_in):
    bw = BRANCH_WIDTH
    sizes = (3 * bw, bw, 2 * GDN_HEADS, 2 * GDN_HEADS, bw, bw, bw, bw, bw, 3 * D_MODEL)
    offs = [0]
    for s in sizes:
        offs.append(offs[-1] + s)
    part = lambda i: w_in[:, offs[i]:offs[i + 1]]
    main = jnp.concatenate([part(9), part(0), part(1), part(4), part(5), part(6), part(7), part(8)], axis=1)
    ba = jnp.zeros((D_MODEL, LANES), F32).at[:, :4 * GDN_HEADS].set(jnp.concatenate([part(2), part(3)], axis=1))
    return main.astype(BF16), ba.astype(BF16)


def kernel(x, c, ctx, c_ctx, w_mod, b_mod, norm1_gain, norm2_gain, w_in, gdn_conv_w, gdn_a_log, gdn_dt_bias, gdn_out_gain, diff_q_gain, diff_k_gain, diff_lambda, diff_out_gain, lru_conv_w, lru_conv_b, lru_w_gate, lru_b_gate, lru_lambda, w_branch, w_out, w_router, b_router, w_gate_up, w_down):
    mods = _mods(c, c_ctx, w_mod, b_mod)
    cos, sin = _rope_tables()
    xs = jnp.concatenate([x, ctx], axis=1)
    for layer in range(DEPTH):
        last = layer == DEPTH - 1
        lam_init = 0.8 - 0.6 * math.exp(-0.3 * layer)
        m = mods[layer]
        w_main, w_ba = _split_w_in(w_in[layer])
        p, ba = _project(xs, m, norm1_gain[layer], w_main, w_ba)
        qkv = _gdn_conv(p, gdn_conv_w[layer])
        ya = _gdn(qkv, p, ba, gdn_a_log[layer], gdn_dt_bias[layer], gdn_out_gain[layer])
        yb = _diff_attn(p, cos, sin, diff_q_gain[layer], diff_k_gain[layer], diff_lambda[layer],
                        diff_out_gain[layer], lam_init, with_ctx=not last)
        xc = _lru_conv(p, lru_conv_w[layer], lru_conv_b[layer])
        yc = _lru(xc, p, lru_w_gate[layer], lru_b_gate[layer], lru_lambda[layer])
        rows, tm = (SEQ, 512) if last else (TOK, 768)
        xs, h2, route, counts = _merge(ya, yb, yc, p, xs, m, norm2_gain[layer], w_branch[layer].astype(BF16),
                                       w_out[layer].astype(BF16), w_router, b_router, rows, tm)
        xs = _moe(h2, route, counts, xs, m, w_gate_up[layer], w_down[layer], rows, tm)
    return xs
```

```python
import functools
import math

import jax
import jax.numpy as jnp
from jax import lax
from jax.experimental import pallas as pl
from jax.experimental.pallas import tpu as pltpu
from jax.experimental.pallas import tpu_sc as plsc

F32 = jnp.float32
BF16 = jnp.bfloat16

D_MODEL = 1024
BATCH = 8
SEQ = 2048
DEPTH = 2
GRID_W = 64
CTX_LEN = 256
TOK = SEQ + CTX_LEN
N_MOD = 6
EPS = 1e-6
CONV_WIDTH = 4
BRANCH_WIDTH = 512
GDN_HEADS = 4
GDN_HEAD_DIM = 128
GDN_CHUNK = 64
DIFF_HEADS = 4
DIFF_HEAD_DIM = 64
ROPE_BASE = 10000.0
ROPE_PAIRS = DIFF_HEAD_DIM // 4
LRU_BLOCKS = 8
LRU_BLOCK_DIM = BRANCH_WIDTH // LRU_BLOCKS
LRU_C = 8.0
N_EXPERTS = 16
N_GROUPS = 4
EXPERT_FF = 512

LANES = 128
VMEM_LIMIT = 56 * 1024 * 1024

COL_GATES = 0
COL_QKV = 3 * D_MODEL
COL_Z = COL_QKV + 3 * BRANCH_WIDTH
COL_DQ = COL_Z + BRANCH_WIDTH
COL_DK = COL_DQ + BRANCH_WIDTH
COL_DV = COL_DK + BRANCH_WIDTH
COL_LX = COL_DV + BRANCH_WIDTH
COL_LY = COL_LX + BRANCH_WIDTH
PROJ_COLS = COL_LY + BRANCH_WIDTH


def _params(*sem):
    return pltpu.CompilerParams(dimension_semantics=sem, vmem_limit_bytes=VMEM_LIMIT)


def _dot(a, b, precision=None):
    return jnp.dot(a, b, preferred_element_type=F32, precision=precision)


def _dot_nt(a, b):
    return lax.dot_general(a, b, (((1,), (1,)), ((), ())), preferred_element_type=F32)


def _silu(x):
    return x * jax.nn.sigmoid(x)


def _softplus(x):
    return jnp.maximum(x, 0.0) + jnp.log(1.0 + jnp.exp(-jnp.abs(x)))


def _rms(x, gain):
    return x * lax.rsqrt(jnp.mean(x * x, axis=-1, keepdims=True) + EPS) * gain


def _mod_kernel(c_ref, w_ref, b_ref, o_ref):
    c = c_ref[...]
    o_ref[0] = _dot(_silu(c), w_ref[0], precision=lax.Precision.HIGHEST) + b_ref[0]


def _mods(c, c_ctx, w_mod, b_mod):
    depth = w_mod.shape[0]
    rows = 16
    cc = jnp.zeros((rows, D_MODEL), F32).at[:BATCH].set(c).at[BATCH].set(c_ctx)
    tn = 1536
    out = pl.pallas_call(
        _mod_kernel,
        grid=(depth, N_MOD * D_MODEL // tn),
        in_specs=[pl.BlockSpec((rows, D_MODEL), lambda l, j: (0, 0)),
                  pl.BlockSpec((1, D_MODEL, tn), lambda l, j: (l, 0, j)),
                  pl.BlockSpec((1, 1, tn), lambda l, j: (l, 0, j))],
        out_specs=pl.BlockSpec((1, rows, tn), lambda l, j: (l, 0, j)),
        out_shape=jax.ShapeDtypeStruct((depth, rows, N_MOD * D_MODEL), F32),
        compiler_params=_params("parallel", "parallel"),
        name="mods",
    )(cc, w_mod, b_mod.reshape(depth, 1, N_MOD * D_MODEL))
    return out.reshape(depth, rows, N_MOD, D_MODEL)


def _modulated_norm(x, gain, ml_ref, mc_ref, is_ctx, shift_idx):
    shift = jnp.where(is_ctx, mc_ref[0, shift_idx:shift_idx + 1, :], ml_ref[0, shift_idx:shift_idx + 1, :])
    scale = jnp.where(is_ctx, mc_ref[0, shift_idx + 1:shift_idx + 2, :], ml_ref[0, shift_idx + 1:shift_idx + 2, :])
    return _rms(x, gain) * (1.0 + scale) + shift


def _ctx_rows(tile, tm, width):
    row = tile * tm + lax.broadcasted_iota(jnp.int32, (tm, width), 0)
    return row >= SEQ


def _proj_kernel(x_ref, ml_ref, mc_ref, g_ref, w_ref, wba_ref, p_ref, ba_ref, h_scr, *, tm):
    i = pl.program_id(1)
    j = pl.program_id(2)

    @pl.when(j == 0)
    def _():
        is_ctx = _ctx_rows(i, tm, D_MODEL)
        h_scr[...] = _modulated_norm(x_ref[0], g_ref[...], ml_ref, mc_ref, is_ctx, 0).astype(BF16)

    h = h_scr[...]
    p_ref[0] = _dot(h, w_ref[0]).astype(BF16)
    ba_ref[0] = _dot(h, wba_ref[0])


def _project(xs, mods, gain, w_main, w_ba, layer):
    tm, tn = 1152, 1280
    return pl.pallas_call(
        functools.partial(_proj_kernel, tm=tm),
        grid=(BATCH, TOK // tm, PROJ_COLS // tn),
        in_specs=[pl.BlockSpec((1, tm, D_MODEL), lambda b, i, j: (b, i, 0)),
                  pl.BlockSpec((1, N_MOD, D_MODEL), lambda b, i, j: (b, 0, 0)),
                  pl.BlockSpec((1, N_MOD, D_MODEL), lambda b, i, j: (BATCH, 0, 0)),
                  pl.BlockSpec((1, D_MODEL), lambda b, i, j: (0, 0)),
                  pl.BlockSpec((1, D_MODEL, tn), lambda b, i, j: (layer, 0, j)),
                  pl.BlockSpec((1, D_MODEL, LANES), lambda b, i, j: (layer, 0, 0))],
        out_specs=[pl.BlockSpec((1, tm, tn), lambda b, i, j: (b, i, j)),
                   pl.BlockSpec((1, tm, LANES), lambda b, i, j: (b, i, 0))],
        out_shape=[jax.ShapeDtypeStruct((BATCH, TOK, PROJ_COLS), BF16),
                   jax.ShapeDtypeStruct((BATCH, TOK, LANES), F32)],
        scratch_shapes=[pltpu.VMEM((tm, D_MODEL), BF16)],
        compiler_params=_params("parallel", "parallel", "arbitrary"),
        name="proj",
    )(xs, mods, mods, gain.reshape(1, D_MODEL), w_main, w_ba)


def _conv(x, w):
    n, c = x.shape
    t = lax.broadcasted_iota(jnp.int32, (n, c), 0)
    is_ctx = t >= SEQ
    local = jnp.where(is_ctx, t - SEQ, t)
    seg_len = jnp.where(is_ctx, CTX_LEN, SEQ)
    y = jnp.zeros_like(x)
    for j in range(CONV_WIDTH):
        s = j - CONV_WIDTH // 2
        if s == 0:
            y = y + x * w[j:j + 1, :]
        else:
            shifted = pltpu.roll(x, (-s) % n, 0)
            ok = jnp.logical_and(local + s >= 0, local + s < seg_len)
            y = y + jnp.where(ok, shifted, 0.0) * w[j:j + 1, :]
    return y


def _gdn_conv_kernel(x_ref, w_ref, o_ref):
    j = pl.program_id(1)
    y = _silu(_conv(x_ref[0].astype(F32), w_ref[...]))
    nrm = lax.rsqrt(jnp.sum(y * y, axis=-1, keepdims=True) + EPS)
    n_head_blocks = GDN_HEADS
    scale = jnp.where(j < n_head_blocks, nrm * GDN_HEAD_DIM ** -0.5, jnp.where(j < 2 * n_head_blocks, nrm, 1.0))
    o_ref[0] = (y * scale).astype(BF16)


def _gdn_conv(p, conv_w):
    nblk = 3 * BRANCH_WIDTH // LANES
    first = COL_QKV // LANES
    return pl.pallas_call(
        _gdn_conv_kernel,
        grid=(BATCH, nblk),
        in_specs=[pl.BlockSpec((1, TOK, LANES), lambda b, j: (b, 0, first + j)),
                  pl.BlockSpec((CONV_WIDTH, LANES), lambda b, j: (0, j))],
        out_specs=pl.BlockSpec((1, TOK, LANES), lambda b, j: (b, 0, j)),
        out_shape=jax.ShapeDtypeStruct((BATCH, TOK, 3 * BRANCH_WIDTH), BF16),
        compiler_params=_params("parallel", "parallel"),
        name="gdn_conv",
    )(p, conv_w)


def _lru_conv_kernel(x_ref, w_ref, b_ref, o_ref):
    o_ref[0] = _conv(x_ref[0].astype(F32), w_ref[...]) + b_ref[...]


def _lru_conv(p, conv_w, conv_b):
    nblk = BRANCH_WIDTH // LANES
    first = COL_LX // LANES
    return pl.pallas_call(
        _lru_conv_kernel,
        grid=(BATCH, nblk),
        in_specs=[pl.BlockSpec((1, TOK, LANES), lambda b, j: (b, 0, first + j)),
                  pl.BlockSpec((CONV_WIDTH, LANES), lambda b, j: (0, j)),
                  pl.BlockSpec((1, LANES), lambda b, j: (0, j))],
        out_specs=pl.BlockSpec((1, TOK, LANES), lambda b, j: (b, 0, j)),
        out_shape=jax.ShapeDtypeStruct((BATCH, TOK, BRANCH_WIDTH), F32),
        compiler_params=_params("parallel", "parallel"),
        name="lru_conv",
    )(p, conv_w, conv_b.reshape(1, BRANCH_WIDTH))


GDN_QM_ROWS = GDN_CHUNK + GDN_HEAD_DIM
GDN_GL_ROWS = 8
GDN_HEADS_PER_STEP = 2
GDN_PREP_UNROLL = 9


def _gdn_prepare(q_ref, k_ref, v_ref, gb_scr, qm_scr, nn_scr, o_scr, gl_scr, chunks, head, local):
    c = GDN_CHUNK
    cols = slice(local * LANES, (local + 1) * LANES)
    lane = lax.broadcasted_iota(jnp.int32, (c, LANES), 1)
    row = lax.broadcasted_iota(jnp.int32, (c, LANES), 0)
    ii = lax.broadcasted_iota(jnp.int32, (c, c), 0)
    jj = lax.broadcasted_iota(jnp.int32, (c, c), 1)
    eye = (ii == jj).astype(F32)
    masks =((ii >= jj, ii > jj, row > lane), (ii <= jj, ii < jj, row < lane))

    loaded = []
    for chunk in chunks:
        r0 = pl.multiple_of(chunk * c, c)
        k = k_ref[0, pl.ds(r0, c), cols]
        q = q_ref[0, pl.ds(r0, c), cols]
        kq = _dot_nt(jnp.concatenate([k, q], axis=0), k)
        loaded.append((chunk, r0, q, k, kq))

    chains = []
    for chunk, r0, q, k, kq in loaded:
        gb = gb_scr[pl.ds(r0, c), :]
        for d in range(2):
            col = head + d * GDN_HEADS
            beta = jnp.sum(jnp.where(lane == col, gb, 0.0), axis=-1, keepdims=True)
            g = jnp.sum(jnp.where(lane == col + 2 * GDN_HEADS, gb, 0.0), axis=-1, keepdims=True)
            incl, strict, strict_wide = masks[d]
            rhs = jnp.where(lane >= c, g, jnp.where(strict_wide, g, 0.0))
            e = _dot(incl.astype(F32), rhs, precision=lax.Precision.HIGHEST)
            chains.append(dict(chunk=chunk, r0=r0, d=d, q=q, k=k, kq=kq, beta=beta, e=e))

    for ch in chains:
        incl, strict, _ = masks[ch["d"]]
        e = ch["e"]
        decay = jnp.where(incl, jnp.exp(e[:, :c]), 0.0)
        gc = e[:, c:c + 1]
        last = 0 if ch["d"] == 1 else c - 1
        gc_last = e[last:last + 1, c:c + 1]
        ch.update(decay=decay, gc=gc, gc_last=gc_last, egc=jnp.exp(gc))
        ch["a"] = jnp.where(strict, ch["beta"] * ch["kq"][:c] * decay, 0.0)
        ch["t"] = eye
    s = 1
    while s < c:
        pair = jnp.logical_and((ii // (2 * s)) == (jj // (2 * s)), (ii // s) != (jj // s))
        for ch in chains:
            ch["a_off"] = jnp.where(pair, ch["a"], 0.0)
        if s == 1:
            for ch in chains:
                ch["t"] = eye - ch["a_off"]
        else:
            for ch in chains:
                ch["m"] = _dot(ch["t"].astype(BF16), ch["a_off"].astype(BF16))
            for ch in chains:
                ch["t"] = ch["t"] - _dot(ch["m"].astype(BF16), ch["t"].astype(BF16))
        s *= 2
    for ch in chains:
        r0, beta, egc = ch["r0"], ch["beta"], ch["egc"]
        kf = ch["k"].astype(F32)
        vf = v_ref[0, pl.ds(r0, c), cols].astype(F32)
        rhs2 = jnp.concatenate([vf * beta, kf * (beta * egc)], axis=1).astype(BF16)
        ch["uw"] = _dot(ch["t"].astype(BF16), rhs2).astype(BF16)
        ch["k_dec_t"] = (kf * jnp.exp(ch["gc_last"] - ch["gc"])).T.astype(BF16)
    for ch in chains:
        incl = masks[ch["d"]][0]
        qk = jnp.where(incl, ch["kq"][c:] * ch["decay"], 0.0).astype(BF16)
        ch["nm"] = _dot(ch["k_dec_t"], ch["uw"])
        ch["ow"] = _dot(qk, ch["uw"])
    for ch in chains:
        chunk, r0, nm, ow = ch["chunk"], ch["r0"], ch["nm"], ch["ow"]
        s = 2 * local + ch["d"]
        q0 = pl.multiple_of(chunk * GDN_QM_ROWS, 16)
        qm_scr[s, pl.ds(q0, c), :] = (ch["q"].astype(F32) * ch["egc"] - ow[:, GDN_HEAD_DIM:]).astype(BF16)
        qm_scr[s, pl.ds(q0 + c, GDN_HEAD_DIM), :] = nm[:, GDN_HEAD_DIM:].astype(BF16)
        nn_scr[s, pl.ds(pl.multiple_of(chunk * GDN_HEAD_DIM, GDN_HEAD_DIM), GDN_HEAD_DIM), :] = nm[:, :GDN_HEAD_DIM]
        o_scr[s, pl.ds(r0, c), :] = ow[:, :GDN_HEAD_DIM]
        gl_scr[s, pl.ds(pl.multiple_of(chunk * GDN_GL_ROWS, GDN_GL_ROWS), GDN_GL_ROWS), :] = jnp.broadcast_to(
            jnp.exp(ch["gc_last"]), (GDN_GL_ROWS, LANES))


def _gdn_advance(qm_scr, nn_scr, o_scr, gl_scr, d, chunk, state):
    c = GDN_CHUNK
    qm = qm_scr[d, pl.ds(pl.multiple_of(chunk * GDN_QM_ROWS, 16), GDN_QM_ROWS), :]
    r = _dot(qm, state.astype(BF16))
    rows = pl.ds(pl.multiple_of(chunk * c, c), c)
    o_scr[d, rows, :] = o_scr[d, rows, :] + r[:c]
    gl = gl_scr[d, pl.ds(pl.multiple_of(chunk * GDN_GL_ROWS, GDN_GL_ROWS), 1), :]
    n = nn_scr[d, pl.ds(pl.multiple_of(chunk * GDN_HEAD_DIM, GDN_HEAD_DIM), GDN_HEAD_DIM), :]
    return state * gl - r[c:] + n


def _gdn_kernel(q_ref, k_ref, v_ref, z_ref, ba_ref, alog_ref, dtb_ref, gain_ref, o_ref,
                gb_scr, qm_scr, nn_scr, o_scr, gl_scr):
    first_head = pl.program_id(1) * GDN_HEADS_PER_STEP
    c = GDN_CHUNK
    n_lat, n_ctx = SEQ // c, CTX_LEN // c
    n_chunks = n_lat + n_ctx
    ba = ba_ref[0]
    lane = lax.broadcasted_iota(jnp.int32, ba.shape, 1)
    g_all = -jnp.exp(alog_ref[...]) * _softplus(ba + dtb_ref[...])
    gb_scr[...] = jnp.where(lane < 2 * GDN_HEADS, jax.nn.sigmoid(ba), g_all)

    for local in range(GDN_HEADS_PER_STEP):
        def prepare(i, _, local=local):
            chunks = [i * GDN_PREP_UNROLL + j for j in range(GDN_PREP_UNROLL)]
            _gdn_prepare(q_ref, k_ref, v_ref, gb_scr, qm_scr, nn_scr, o_scr, gl_scr, chunks, first_head + local, local)
            return 0

        lax.fori_loop(0, n_chunks // GDN_PREP_UNROLL, prepare, 0)

    def advance(i, states):
        cf = jnp.where(i < n_ctx, n_lat + i, i - n_ctx)
        cb = n_chunks - 1 - i
        return tuple(_gdn_advance(qm_scr, nn_scr, o_scr, gl_scr, s, cb if s % 2 else cf, state)
                     for s, state in enumerate(states))

    zero = jnp.zeros((GDN_HEAD_DIM, GDN_HEAD_DIM), F32)
    lax.fori_loop(0, n_chunks, advance, (zero,) * (2 * GDN_HEADS_PER_STEP))
    for local in range(GDN_HEADS_PER_STEP):
        cols = slice(local * LANES, (local + 1) * LANES)
        o = o_scr[2 * local] + o_scr[2 * local + 1]
        o_ref[0, :, cols] = (_rms(o, gain_ref[...]) * _silu(z_ref[0, :, cols].astype(F32))).astype(BF16)


def _gdn(qkv, p, ba, a_log, dt_bias, out_gain):
    def pad_lanes(vals):
        row = jnp.zeros((LANES,), F32).at[2 * GDN_HEADS:4 * GDN_HEADS].set(vals.reshape(-1))
        return row.reshape(1, LANES)

    n_steps = GDN_HEADS // GDN_HEADS_PER_STEP
    n_chunks = TOK // GDN_CHUNK
    width = GDN_HEADS_PER_STEP * LANES
    zblk = COL_Z // width
    n_chain = 2 * GDN_HEADS_PER_STEP
    blk = lambda off: pl.BlockSpec((1, TOK, width), lambda b, h: (b, 0, off + h))
    vec = pl.BlockSpec((1, LANES), lambda b, h: (0, 0))
    return pl.pallas_call(
        _gdn_kernel,
        grid=(BATCH, n_steps),
        in_specs=[blk(0), blk(n_steps), blk(2 * n_steps), blk(zblk),
                  pl.BlockSpec((1, TOK, LANES), lambda b, h: (b, 0, 0)), vec, vec, vec],
        out_specs=pl.BlockSpec((1, TOK, width), lambda b, h: (b, 0, h)),
        out_shape=jax.ShapeDtypeStruct((BATCH, TOK, BRANCH_WIDTH), BF16),
        scratch_shapes=[pltpu.VMEM((TOK, LANES), F32),
                        pltpu.VMEM((n_chain, n_chunks * GDN_QM_ROWS, LANES), BF16),
                        pltpu.VMEM((n_chain, n_chunks * GDN_HEAD_DIM, LANES), F32),
                        pltpu.VMEM((n_chain, TOK, LANES), F32),
                        pltpu.VMEM((n_chain, n_chunks * GDN_GL_ROWS, LANES), F32)],
        compiler_params=_params("parallel", "parallel"),
        name="gdn",
    )(qkv, qkv, qkv, p, ba, pad_lanes(a_log), pad_lanes(dt_bias), out_gain.reshape(1, LANES))


def _rms_halves(x, gain):
    lane = lax.broadcasted_iota(jnp.int32, x.shape, 1)
    lo = lane < DIFF_HEAD_DIM
    x2 = x * x
    s_lo = jnp.sum(jnp.where(lo, x2, 0.0), axis=-1, keepdims=True)
    s_hi = jnp.sum(jnp.where(lo, 0.0, x2), axis=-1, keepdims=True)
    ms = jnp.where(lo, s_lo, s_hi) * (1.0 / DIFF_HEAD_DIM)
    return x * lax.rsqrt(ms + EPS) * gain


def _rope(x, cos, sin):
    lane = lax.broadcasted_iota(jnp.int32, x.shape, 1)
    first = (lane & ROPE_PAIRS) == 0
    partner = jnp.where(first, -pltpu.roll(x, LANES - ROPE_PAIRS, 1), pltpu.roll(x, ROPE_PAIRS, 1))
    return x * cos + partner * sin


ATTN_Q_BLOCK = 512
ATTN_GROUP_ROWS = 128


def _attn_kernel(q_ref, k_ref, v_ref, cosk_ref, sink_ref, cosq_ref, sinq_ref, qg_ref, kg_ref, lv_ref, og_ref,
                 o_ref, kn_scr, *, ctx_block, lam_init):
    qi = pl.program_id(2)

    @pl.when(qi == 0)
    def _():
        kn = _rope(_rms_halves(k_ref[0].astype(F32), kg_ref[...]), cosk_ref[...], sink_ref[...])
        kn_scr[...] = kn.astype(BF16)

    lv = lv_ref[...]
    lam = (jnp.exp(jnp.sum(lv[0:1] * lv[1:2], axis=-1, keepdims=True))
           - jnp.exp(jnp.sum(lv[2:3] * lv[3:4], axis=-1, keepdims=True)) + lam_init)
    def attend(n_rows, kn, v):
        q = _rope(_rms_halves(q_ref[0, :n_rows, :].astype(F32), qg_ref[...]), cosq_ref[:n_rows, :], sinq_ref[:n_rows, :])
        q = q * (DIFF_HEAD_DIM ** -0.5 * math.log2(math.e))
        lane = lax.broadcasted_iota(jnp.int32, q.shape, 1)
        lo = lane < DIFF_HEAD_DIM
        q1 = jnp.where(lo, q, 0.0).astype(BF16)
        q2 = jnp.where(lo, 0.0, q).astype(BF16)

        def half(s):
            p = jnp.exp2(s - jnp.max(s, axis=-1, keepdims=True))
            return _dot(p.astype(BF16), v), jnp.sum(p, axis=-1, keepdims=True)
        rows = ATTN_GROUP_ROWS
        scores = [(_dot_nt(q1[r:r + rows], kn), _dot_nt(q2[r:r + rows], kn)) for r in range(0, n_rows, rows)]
        for g, (s1, s2) in enumerate(scores):
            a1, l1 = half(s1)
            a2, l2 = half(s2)
            o = a1 * (1.0 / l1) - a2 * (lam / l2)
            o_ref[0, g * rows:(g + 1) * rows, :] = (_rms(o, og_ref[...]) * (1.0 - lam_init)).astype(BF16)

    if ctx_block is None:
        attend(q_ref.shape[1], kn_scr[...], v_ref[0])
    else:
        @pl.when(qi == ctx_block)
        def _():
            attend(CTX_LEN, kn_scr[SEQ:, :], v_ref[0, SEQ:, :])

        @pl.when(qi != ctx_block)
        def _():
            attend(q_ref.shape[1], kn_scr[...], v_ref[0])


def _rope_tables():
    n_rows = SEQ // GRID_W
    row_id = jnp.broadcast_to(jnp.arange(n_rows, dtype=F32)[:, None], (n_rows, GRID_W)).reshape(-1)
    col_id = jnp.broadcast_to(jnp.arange(GRID_W, dtype=F32)[None, :], (n_rows, GRID_W)).reshape(-1)
    inv_freq = jnp.power(ROPE_BASE, -jnp.arange(ROPE_PAIRS, dtype=F32) / ROPE_PAIRS)
    row_ang = row_id[:, None] * inv_freq
    col_ang = col_id[:, None] * inv_freq
    ang = jnp.concatenate([row_ang, row_ang, col_ang, col_ang], axis=-1)
    ang = jnp.concatenate([ang, ang], axis=-1)
    pad = ((0, CTX_LEN), (0, 0))
    return jnp.pad(jnp.cos(ang), pad, constant_values=1.0), jnp.pad(jnp.sin(ang), pad)


def _diff_attn(p, cos, sin, q_gain, k_gain, lam_vecs, out_gain, lam_init, with_ctx):
    tq = ATTN_Q_BLOCK
    n_rows = TOK if with_ctx else SEQ
    nq = pl.cdiv(n_rows, tq)
    nh = DIFF_HEADS
    qb, kb, vb = COL_DQ // LANES, COL_DK // LANES, COL_DV // LANES
    full = lambda off: pl.BlockSpec((1, TOK, LANES), lambda b, h, i: (b, 0, off + h))
    tab_full = pl.BlockSpec((TOK, LANES), lambda b, h, i: (0, 0))
    tab_q = pl.BlockSpec((tq, LANES), lambda b, h, i: (i, 0))
    vec = pl.BlockSpec((1, LANES), lambda b, h, i: (0, 0))
    tile2 = lambda g: jnp.concatenate([g, g]).reshape(1, LANES)
    return pl.pallas_call(
        functools.partial(_attn_kernel, ctx_block=SEQ // tq if with_ctx else None, lam_init=lam_init),
        grid=(BATCH, nh, nq),
        in_specs=[pl.BlockSpec((1, tq, LANES), lambda b, h, i: (b, i, qb + h)), full(kb), full(vb),
                  tab_full, tab_full, tab_q, tab_q, vec, vec,
                  pl.BlockSpec((4, DIFF_HEAD_DIM), lambda b, h, i: (0, 0)), vec],
        out_specs=pl.BlockSpec((1, tq, LANES), lambda b, h, i: (b, i, h)),
        out_shape=jax.ShapeDtypeStruct((BATCH, n_rows, BRANCH_WIDTH), BF16),
        scratch_shapes=[pltpu.VMEM((TOK, LANES), BF16)],
        compiler_params=_params("parallel", "parallel", "arbitrary"),
        name="diff_attn",
    )(p, p, p, cos, sin, cos, sin, tile2(q_gain), tile2(k_gain), lam_vecs, out_gain.reshape(1, LANES))


LRU_SLAB = 256
LRU_ROWS = 256
LRU_SCAN_BLOCK = 8


def _lru_kernel(xc_ref, y_ref, wg_ref, bg_ref, lam_ref, o_ref, af_scr, bf_scr, ab_scr, bb_scr):
    w = LRU_SLAB
    blk = LRU_SCAN_BLOCK
    sp = _softplus(-lam_ref[0])
    sub = lax.broadcasted_iota(jnp.int32, (LRU_ROWS, w), 0) % blk

    def gates(i, _):
        r0 = pl.multiple_of(i * LRU_ROWS, LRU_ROWS)
        xc = xc_ref[0, pl.ds(r0, LRU_ROWS), :]
        pre = _dot(xc.astype(BF16), wg_ref[0]) + bg_ref[0]
        for d, (a_scr, b_scr) in enumerate(((af_scr, bf_scr), (ab_scr, bb_scr))):
            r = jax.nn.sigmoid(pre[:, (2 * d) * w:(2 * d + 1) * w])
            gi = jax.nn.sigmoid(pre[:, (2 * d + 1) * w:(2 * d + 2) * w])
            log_a = -LRU_C * r * sp[d:d + 1]
            a = jnp.exp(log_a)
            b = jnp.sqrt(1.0 - a * a) * gi * xc
            shift = 1
            while shift < blk:
                if d == 0:
                    ok, roll_by = sub >= shift, shift
                else:
                    ok, roll_by = sub < blk - shift, LRU_ROWS - shift
                b = jnp.where(ok, a * pltpu.roll(b, roll_by, 0) + b, b)
                a = jnp.where(ok, a * pltpu.roll(a, roll_by, 0), a)
                shift *= 2
            a_scr[pl.ds(r0, LRU_ROWS), :] = a
            b_scr[pl.ds(r0, LRU_ROWS), :] = b
        return 0

    lax.fori_loop(0, TOK // LRU_ROWS, gates, 0)

    n_blk, n_lat_blk, n_ctx_blk = TOK // blk, SEQ // blk, CTX_LEN // blk

    def step(s, carry):
        h_f, h_b = carry
        rows_f = pl.ds(pl.multiple_of(jnp.where(s < n_ctx_blk, n_lat_blk + s, s - n_ctx_blk) * blk, blk), blk)
        rows_b = pl.ds(pl.multiple_of((n_blk - 1 - s) * blk, blk), blk)
        hf = af_scr[rows_f, :] * h_f + bf_scr[rows_f, :]
        bf_scr[rows_f, :] = hf
        hb = ab_scr[rows_b, :] * h_b + bb_scr[rows_b, :]
        bb_scr[rows_b, :] = hb
        return hf[blk - 1:blk, :], hb[0:1, :]

    zero = jnp.zeros((1, w), F32)
    lax.fori_loop(0, n_blk, step, (zero, zero), unroll=4)
    h = bf_scr[...] + bb_scr[...]
    o_ref[0] = (h * jax.nn.gelu(y_ref[0].astype(F32))).astype(BF16)


def _lru_gate_weights(w_gate, b_gate):
    n_slab = BRANCH_WIDTH // LRU_SLAB
    per = LRU_SLAB // LRU_BLOCK_DIM
    wg = w_gate.reshape(2, 2, n_slab, per, LRU_BLOCK_DIM, LRU_BLOCK_DIM)
    eye = jnp.eye(per, dtype=w_gate.dtype)
    dense = jnp.einsum('dgsnjk,nm->snjdgmk', wg, eye)
    dense = dense.reshape(n_slab, LRU_SLAB, 4 * LRU_SLAB)
    bg = b_gate.reshape(2, 2, n_slab, LRU_SLAB).transpose(2, 0, 1, 3).reshape(n_slab, 1, 4 * LRU_SLAB)
    return dense.astype(BF16), bg


def _lru(xc, p, w_gate, b_gate, lam):
    n_slab = BRANCH_WIDTH // LRU_SLAB
    wg, bg = _lru_gate_weights(w_gate, b_gate)
    lam_s = lam.reshape(2, n_slab, LRU_SLAB).transpose(1, 0, 2)
    yb = COL_LY // LRU_SLAB
    return pl.pallas_call(
        _lru_kernel,
        grid=(BATCH, n_slab),
        in_specs=[pl.BlockSpec((1, TOK, LRU_SLAB), lambda b, s: (b, 0, s)),
                  pl.BlockSpec((1, TOK, LRU_SLAB), lambda b, s: (b, 0, yb + s)),
                  pl.BlockSpec((1, LRU_SLAB, 4 * LRU_SLAB), lambda b, s: (s, 0, 0)),
                  pl.BlockSpec((1, 1, 4 * LRU_SLAB), lambda b, s: (s, 0, 0)),
                  pl.BlockSpec((1, 2, LRU_SLAB), lambda b, s: (s, 0, 0))],
        out_specs=pl.BlockSpec((1, TOK, LRU_SLAB), lambda b, s: (b, 0, s)),
        out_shape=jax.ShapeDtypeStruct((BATCH, TOK, BRANCH_WIDTH), BF16),
        scratch_shapes=[pltpu.VMEM((TOK, LRU_SLAB), F32)] * 4,
        compiler_params=_params("parallel", "parallel"),
        name="lru",
    )(xc, p, wg, bg, lam_s)


def _route(logits):
    lane = lax.broadcasted_iota(jnp.int32, logits.shape, 1)
    lane_f = lane.astype(F32)
    far = float(LANES)
    lg = jnp.where(lane < N_EXPERTS, logits, -jnp.inf)
    ex = jnp.exp(lg - jnp.max(lg, axis=-1, keepdims=True))
    probs = ex / jnp.sum(ex, axis=-1, keepdims=True)
    per_group = N_EXPERTS // N_GROUPS
    grp = lane // per_group

    def top2(vals):
        m1 = jnp.max(vals, axis=-1, keepdims=True)
        i1 = jnp.min(jnp.where(vals == m1, lane_f, far), axis=-1, keepdims=True)
        rest = jnp.where(lane_f == i1, -2.0, vals)
        m2 = jnp.max(rest, axis=-1, keepdims=True)
        i2 = jnp.min(jnp.where(rest == m2, lane_f, far), axis=-1, keepdims=True)
        return m1, i1, m2, i2

    best = jnp.zeros(logits.shape[:1] + (1,), jnp.int32)
    best_score = None
    for g in range(N_GROUPS):
        m1, _, m2, _ = top2(jnp.where(grp == g, probs, -1.0))
        score = m1 + m2
        if best_score is None:
            best_score = score
        else:
            better = score > best_score
            best = jnp.where(better, g, best)
            best_score = jnp.where(better, score, best_score)
    m1, i1, m2, i2 = top2(jnp.where(grp == best, probs, -1.0))
    den = m1 + m2
    return i1, i2, m1 / den, m2 / den


ROUTE_E, ROUTE_W, ROUTE_RANK = 0, 2, 4


def _pack_bf16_pairs(x):
    n = x.shape[1] // 2
    xb = x.astype(BF16).astype(F32)
    lo = pltpu.bitcast(xb[:, :n], jnp.uint32)
    hi = pltpu.bitcast(xb[:, n:], jnp.uint32)
    return (lo >> 16) | (hi & jnp.uint32(0xFFFF0000))


def _unpack_bf16_pairs(p):
    lo = pltpu.bitcast(p << 16, F32)
    hi = pltpu.bitcast(p & jnp.uint32(0xFFFF0000), F32)
    return jnp.concatenate([lo, hi], axis=1)


def _merge_kernel(ya_ref, yb_ref, yc_ref, gates_ref, x_ref, ml_ref, mc_ref, g2_ref, wbr_ref, wout_ref, wr_ref,
                  br_ref, xo_ref, h2_ref, route_ref, cnt_ref, cnt_scr, *, tm):
    i = pl.program_id(1)

    @pl.when(jnp.logical_and(pl.program_id(0) == 0, i == 0))
    def _():
        cnt_scr[...] = jnp.zeros_like(cnt_scr)

    acc = None
    for n, y_ref in enumerate((ya_ref, yb_ref, yc_ref)):
        yn = _dot(y_ref[0], wbr_ref[n])
        gate = jax.nn.sigmoid(gates_ref[0, :, n * D_MODEL:(n + 1) * D_MODEL].astype(F32))
        acc = gate * yn if acc is None else acc + gate * yn
    out = _dot(acc.astype(BF16), wout_ref[...])
    is_ctx = _ctx_rows(i, tm, D_MODEL)
    xn = x_ref[0] + jnp.where(is_ctx, mc_ref[0, 2:3, :], ml_ref[0, 2:3, :]) * out
    xo_ref[0] = xn
    h2 = _modulated_norm(xn, g2_ref[...], ml_ref, mc_ref, is_ctx, 3)
    h2_ref[0] = _pack_bf16_pairs(h2)
    i1, i2, w1, w2 = _route(_dot(h2.astype(BF16), wr_ref[...]) + br_ref[...])
    lane = lax.broadcasted_iota(jnp.int32, (tm, LANES), 1)
    lane_f = lane.astype(F32)
    chosen = jnp.where(jnp.logical_or(lane_f == i1, lane_f == i2), 1.0, 0.0)
    earlier = (lax.broadcasted_iota(jnp.int32, (tm, tm), 0) > lax.broadcasted_iota(jnp.int32, (tm, tm), 1))
    before = _dot(jnp.where(earlier, 1.0, 0.0).astype(BF16), chosen.astype(BF16)) + cnt_scr[...]
    rank1 = jnp.sum(jnp.where(lane_f == i1, before, 0.0), axis=-1, keepdims=True)
    rank2 = jnp.sum(jnp.where(lane_f == i2, before, 0.0), axis=-1, keepdims=True)
    cnt_scr[...] += jnp.sum(chosen, axis=0, keepdims=True)
    cnt_ref[...] = cnt_scr[...]
    record = jnp.zeros((tm, LANES), F32)
    for pos, val in enumerate((i1, i2, w1, w2, rank1, rank2)):
        record = jnp.where(lane == pos, val, record)
    route_ref[0] = record


def _merge(ya, yb, yc, p, xs, mods, gain2, w_branch, w_out, w_router, b_router, rows, tm):
    wr = jnp.zeros((D_MODEL, LANES), BF16).at[:, :N_EXPERTS].set(w_router.astype(BF16))
    br = jnp.zeros((1, LANES), F32).at[0, :N_EXPERTS].set(b_router)
    tile = lambda w: pl.BlockSpec((1, tm, w), lambda b, i: (b, i, 0))
    const = lambda shape: pl.BlockSpec(shape, lambda b, i: (0,) * len(shape))
    return pl.pallas_call(
        functools.partial(_merge_kernel, tm=tm),
        grid=(BATCH, rows // tm),
        in_specs=[tile(BRANCH_WIDTH), tile(BRANCH_WIDTH), tile(BRANCH_WIDTH), tile(3 * D_MODEL), tile(D_MODEL),
                  pl.BlockSpec((1, N_MOD, D_MODEL), lambda b, i: (b, 0, 0)),
                  pl.BlockSpec((1, N_MOD, D_MODEL), lambda b, i: (BATCH, 0, 0)),
                  const((1, D_MODEL)), const((3, BRANCH_WIDTH, D_MODEL)), const((D_MODEL, D_MODEL)),
                  const((D_MODEL, LANES)), const((1, LANES))],
        out_specs=[tile(D_MODEL), tile(D_MODEL // 2), tile(LANES), const((1, LANES))],
        out_shape=[jax.ShapeDtypeStruct((BATCH, rows, D_MODEL), F32),
                   jax.ShapeDtypeStruct((BATCH, rows, D_MODEL // 2), jnp.uint32),
                   jax.ShapeDtypeStruct((BATCH, rows, LANES), F32),
                   jax.ShapeDtypeStruct((1, LANES), F32)],
        scratch_shapes=[pltpu.VMEM((1, LANES), F32)],
        compiler_params=_params("arbitrary", "arbitrary"),
        name="merge",
    )(ya, yb, yc, p, xs, mods, mods, gain2.reshape(1, D_MODEL), w_branch, w_out, wr, br)


MOE_TILE = 512
SC_GATHER_ROWS = 64


def _sc_gather(table, idx):
    info = plsc.get_sparse_core_info()
    n_workers = info.num_cores * info.num_subcores
    n_rows, width = idx.shape[0], table.shape[1]
    per_worker = n_rows // n_workers
    assert per_worker * n_workers == n_rows and per_worker % SC_GATHER_ROWS == 0
    mesh = plsc.VectorSubcoreMesh(core_axis_name="c", subcore_axis_name="s")

    @functools.partial(
        pl.kernel, mesh=mesh, out_type=jax.ShapeDtypeStruct((n_rows, width), table.dtype),
        scratch_types=[pltpu.VMEM((SC_GATHER_ROWS,), jnp.int32),
                       pltpu.VMEM((SC_GATHER_ROWS, width), table.dtype),
                       pltpu.SemaphoreType.DMA],
        name="sc_gather")
    def gather(table_hbm, idx_hbm, out_hbm, idx_v, rows_v, sem):
        worker = lax.axis_index("s") * info.num_cores + lax.axis_index("c")
        base = worker * per_worker

        @pl.loop(0, per_worker // SC_GATHER_ROWS)
        def _(j):
            off = base + j * SC_GATHER_ROWS
            pltpu.sync_copy(idx_hbm.at[pl.ds(off, SC_GATHER_ROWS)], idx_v)
            pltpu.async_copy(table_hbm.at[idx_v], rows_v, sem).wait()
            pltpu.sync_copy(rows_v, out_hbm.at[pl.ds(off, SC_GATHER_ROWS)])

    return gather(table, idx)


SC_SCATTER_ROWS = 128


def _sc_scatter(table, dest, n_out):
    info = plsc.get_sparse_core_info()
    n_workers = info.num_cores * info.num_subcores
    n_tok, width = table.shape
    n_assign = dest.shape[0]
    per_worker = n_assign // n_workers
    assert per_worker * n_workers == n_assign and per_worker % SC_SCATTER_ROWS == 0 and n_tok % SC_SCATTER_ROWS == 0
    mesh = plsc.VectorSubcoreMesh(core_axis_name="c", subcore_axis_name="s")

    @functools.partial(
        pl.kernel, mesh=mesh, out_type=jax.ShapeDtypeStruct((n_out, width), table.dtype),
        scratch_types=[pltpu.VMEM((SC_SCATTER_ROWS,), jnp.int32),
                       pltpu.VMEM((SC_SCATTER_ROWS, width), table.dtype),
                       pltpu.SemaphoreType.DMA],
        name="sc_scatter")
    def scatter(table_hbm, dest_hbm, out_hbm, idx_v, rows_v, sem):
        worker = lax.axis_index("s") * info.num_cores + lax.axis_index("c")
        base = worker * per_worker

        @pl.loop(0, per_worker // SC_SCATTER_ROWS)
        def _(j):
            off = base + j * SC_SCATTER_ROWS
            pltpu.sync_copy(dest_hbm.at[pl.ds(off, SC_SCATTER_ROWS)], idx_v)
            pltpu.sync_copy(table_hbm.at[pl.ds(lax.rem(off, n_tok), SC_SCATTER_ROWS)], rows_v)
            pltpu.async_copy(rows_v, out_hbm.at[idx_v], sem).wait()

    return scatter(table, dest)


def _experts_kernel(tile_expert_ref, n_valid_ref, x_ref, wgu_ref, wd_ref, o_ref, wgu_scr, wd_scr):
    j = pl.program_id(0)
    valid = j < n_valid_ref[0]
    fresh = jnp.logical_or(j == 0, tile_expert_ref[j] != tile_expert_ref[jnp.maximum(j - 1, 0)])

    @pl.when(jnp.logical_and(valid, fresh))
    def _():
        wgu_scr[...] = wgu_ref[0].astype(BF16)
        wd_scr[...] = wd_ref[0].astype(BF16)

    @pl.when(valid)
    def _():
        x = _unpack_bf16_pairs(x_ref[...]).astype(BF16)
        gu = _dot(x, wgu_scr[...])
        act = (_silu(gu[:, :EXPERT_FF]) * gu[:, EXPERT_FF:]).astype(BF16)
        o_ref[...] = _pack_bf16_pairs(_dot(act, wd_scr[...]))

    @pl.when(jnp.logical_not(valid))
    def _():
        o_ref[...] = jnp.zeros_like(o_ref)


def _experts(x_sorted, tile_expert, n_valid, w_gate_up, w_down):
    n_tiles = x_sorted.shape[0] // MOE_TILE
    half = D_MODEL // 2
    return pl.pallas_call(
        _experts_kernel,
        grid_spec=pltpu.PrefetchScalarGridSpec(
            num_scalar_prefetch=2, grid=(n_tiles,),
            in_specs=[pl.BlockSpec((MOE_TILE, half), lambda j, te, nv: (j, 0)),
                      pl.BlockSpec((1, D_MODEL, 2 * EXPERT_FF), lambda j, te, nv: (te[j], 0, 0)),
                      pl.BlockSpec((1, EXPERT_FF, D_MODEL), lambda j, te, nv: (te[j], 0, 0))],
            out_specs=pl.BlockSpec((MOE_TILE, half), lambda j, te, nv: (j, 0)),
            scratch_shapes=[pltpu.VMEM((D_MODEL, 2 * EXPERT_FF), BF16), pltpu.VMEM((EXPERT_FF, D_MODEL), BF16)]),
        out_shape=jax.ShapeDtypeStruct((x_sorted.shape[0], half), jnp.uint32),
        compiler_params=_params("arbitrary"),
        name="experts",
    )(tile_expert, n_valid, x_sorted, w_gate_up, w_down)


def _combine_kernel(y1_ref, y2_ref, route_ref, x_ref, ml_ref, mc_ref, o_ref, *, tm):
    i = pl.program_id(1)
    route = route_ref[0]
    w1 = route[:, ROUTE_W:ROUTE_W + 1]
    w2 = route[:, ROUTE_W + 1:ROUTE_W + 2]
    moe = w1 * _unpack_bf16_pairs(y1_ref[0, 0]) + w2 * _unpack_bf16_pairs(y2_ref[0, 0])
    is_ctx = _ctx_rows(i, tm, D_MODEL)
    o_ref[0] = x_ref[0] + jnp.where(is_ctx, mc_ref[0, 5:6, :], ml_ref[0, 5:6, :]) * moe


def _combine(y_pairs, route, xs, mods, rows, tm):
    half = D_MODEL // 2
    tile = lambda w: pl.BlockSpec((1, tm, w), lambda b, i: (b, i, 0))
    slot = lambda s: pl.BlockSpec((1, 1, tm, half), lambda b, i: (s, b, i, 0))
    return pl.pallas_call(
        functools.partial(_combine_kernel, tm=tm),
        grid=(BATCH, rows // tm),
        in_specs=[slot(0), slot(1), tile(LANES), tile(D_MODEL),
                  pl.BlockSpec((1, N_MOD, D_MODEL), lambda b, i: (b, 0, 0)),
                  pl.BlockSpec((1, N_MOD, D_MODEL), lambda b, i: (BATCH, 0, 0))],
        out_specs=tile(D_MODEL),
        out_shape=jax.ShapeDtypeStruct((BATCH, rows, D_MODEL), F32),
        compiler_params=_params("parallel", "parallel"),
        name="combine",
    )(y_pairs, y_pairs, route, xs, mods, mods)


def _moe(h2, route, counts, xs, mods, w_gate_up, w_down, rows, tm):
    n_tok = BATCH * rows
    half = D_MODEL // 2
    n_sorted = 2 * n_tok + N_EXPERTS * MOE_TILE
    n_tiles = n_sorted // MOE_TILE
    rec = route.reshape(n_tok, LANES)
    expert = rec[:, ROUTE_E:ROUTE_E + 2].astype(jnp.int32)
    rank = rec[:, ROUTE_RANK:ROUTE_RANK + 2].astype(jnp.int32)
    count = counts[0, :N_EXPERTS].astype(jnp.int32)
    padded = (count + MOE_TILE - 1) // MOE_TILE * MOE_TILE
    end = jnp.cumsum(padded)
    start = end - padded
    first = jnp.sum(jnp.where(expert[:, :1] == jnp.arange(N_EXPERTS), start, 0), axis=1)
    second = jnp.sum(jnp.where(expert[:, 1:] == jnp.arange(N_EXPERTS), start, 0), axis=1)
    dest = jnp.concatenate([first + rank[:, 0], second + rank[:, 1]])
    tile_start = jnp.arange(n_tiles, dtype=jnp.int32) * MOE_TILE
    tile_expert = jnp.minimum(jnp.sum(tile_start[:, None] >= end[None, :], axis=1), N_EXPERTS - 1).astype(jnp.int32)
    n_valid = (end[-1:] // MOE_TILE).astype(jnp.int32)
    x_sorted = _sc_scatter(h2.reshape(n_tok, half), dest, n_sorted)
    y_sorted = _experts(x_sorted, tile_expert, n_valid, w_gate_up, w_down)
    y_pairs = _sc_gather(y_sorted, dest).reshape(2, BATCH, rows, half)
    return _combine(y_pairs, route, xs, mods, rows, tm)


W_IN_BLOCK = 512
W_IN_NARROW = 4 * GDN_HEADS
W_IN_NARROW_AT = 4 * BRANCH_WIDTH


def _reorder_kernel(a_ref, b_ref, o_ref):
    i = pl.program_id(1)
    n_gate = 3 * D_MODEL // W_IN_BLOCK
    aligned = jnp.logical_and(i >= n_gate, i < n_gate + W_IN_NARROW_AT // W_IN_BLOCK)

    @pl.when(aligned)
    def _():
        o_ref[0] = a_ref[0].astype(BF16)

    @pl.when(jnp.logical_not(aligned))
    def _():
        o_ref[0] = jnp.concatenate([a_ref[0][:, W_IN_NARROW:], b_ref[0][:, :W_IN_NARROW]], axis=1).astype(BF16)


def _split_w_in(w_in):
    depth = w_in.shape[0]
    n_gate = 3 * D_MODEL // W_IN_BLOCK
    src = lambda i: jnp.where(i < n_gate, i + (PROJ_COLS - 3 * D_MODEL) // W_IN_BLOCK, i - n_gate)
    main = pl.pallas_call(
        _reorder_kernel,
        grid=(depth, PROJ_COLS // W_IN_BLOCK),
        in_specs=[pl.BlockSpec((1, D_MODEL, W_IN_BLOCK), lambda l, i: (l, 0, src(i))),
                  pl.BlockSpec((1, D_MODEL, W_IN_BLOCK), lambda l, i: (l, 0, src(i) + 1))],
        out_specs=pl.BlockSpec((1, D_MODEL, W_IN_BLOCK), lambda l, i: (l, 0, i)),
        out_shape=jax.ShapeDtypeStruct((depth, D_MODEL, PROJ_COLS), BF16),
        compiler_params=_params("parallel", "parallel"),
        name="reorder_w_in",
    )(w_in, w_in)
    narrow = w_in[:, :, W_IN_NARROW_AT:W_IN_NARROW_AT + W_IN_NARROW]
    ba = jnp.zeros((depth, D_MODEL, LANES), F32).at[:, :, :W_IN_NARROW].set(narrow)
    return main, ba.astype(BF16)


def kernel(x, c, ctx, c_ctx, w_mod, b_mod, norm1_gain, norm2_gain, w_in, gdn_conv_w, gdn_a_log, gdn_dt_bias, gdn_out_gain, diff_q_gain, diff_k_gain, diff_lambda, diff_out_gain, lru_conv_w, lru_conv_b, lru_w_gate, lru_b_gate, lru_lambda, w_branch, w_out, w_router, b_router, w_gate_up, w_down):
    mods = _mods(c, c_ctx, w_mod, b_mod)
    cos, sin = _rope_tables()
    xs = jnp.concatenate([x, ctx], axis=1)
    w_main, w_ba = _split_w_in(w_in)
    for layer in range(DEPTH):
        last = layer == DEPTH - 1
        lam_init = 0.8 - 0.6 * math.exp(-0.3 * layer)
        m = mods[layer]
        p, ba = _project(xs, m, norm1_gain[layer], w_main, w_ba, layer)
        qkv = _gdn_conv(p, gdn_conv_w[layer])
        ya = _gdn(qkv, p, ba, gdn_a_log[layer], gdn_dt_bias[layer], gdn_out_gain[layer])
        yb = _diff_attn(p, cos, sin, diff_q_gain[layer], diff_k_gain[layer], diff_lambda[layer],
                        diff_out_gain[layer], lam_init, with_ctx=not last)
        xc = _lru_conv(p, lru_conv_w[layer], lru_conv_b[layer])
        yc = _lru(xc, p, lru_w_gate[layer], lru_b_gate[layer], lru_lambda[layer])
        rows, tm = (SEQ, 512) if last else (TOK, 768)
        xs, h2, route, counts = _merge(ya, yb, yc, p, xs, m, norm2_gain[layer], w_branch[layer].astype(BF16),
                                       w_out[layer].astype(BF16), w_router, b_router, rows, tm)
        xs = _moe(h2, route, counts, xs, m, w_gate_up[layer], w_down[layer], rows, tm)
    return xs
```

```python
import functools
import math

import jax
import jax.numpy as jnp
from jax import lax
from jax.experimental import pallas as pl
from jax.experimental.pallas import tpu as pltpu
from jax.experimental.pallas import tpu_sc as plsc

F32 = jnp.float32
BF16 = jnp.bfloat16

D_MODEL = 1024
BATCH = 8
SEQ = 2048
DEPTH = 2
GRID_W = 64
CTX_LEN = 256
TOK = SEQ + CTX_LEN
N_MOD = 6
EPS = 1e-6
CONV_WIDTH = 4
BRANCH_WIDTH = 512
GDN_HEADS = 4
GDN_HEAD_DIM = 128
GDN_CHUNK = 64
DIFF_HEADS = 4
DIFF_HEAD_DIM = 64
ROPE_BASE = 10000.0
ROPE_PAIRS = DIFF_HEAD_DIM // 4
LRU_BLOCKS = 8
LRU_BLOCK_DIM = BRANCH_WIDTH // LRU_BLOCKS
LRU_C = 8.0
N_EXPERTS = 16
N_GROUPS = 4
EXPERT_FF = 512

LANES = 128
VMEM_LIMIT = 56 * 1024 * 1024

COL_GATES = 0
COL_QKV = 3 * D_MODEL
COL_Z = COL_QKV + 3 * BRANCH_WIDTH
COL_DQ = COL_Z + BRANCH_WIDTH
COL_DK = COL_DQ + BRANCH_WIDTH
COL_DV = COL_DK + BRANCH_WIDTH
COL_LX = COL_DV + BRANCH_WIDTH
COL_LY = COL_LX + BRANCH_WIDTH
PROJ_COLS = COL_LY + BRANCH_WIDTH


def _params(*sem):
    return pltpu.CompilerParams(dimension_semantics=sem, vmem_limit_bytes=VMEM_LIMIT)


def _dot(a, b, precision=None):
    return jnp.dot(a, b, preferred_element_type=F32, precision=precision)


def _dot_nt(a, b):
    return lax.dot_general(a, b, (((1,), (1,)), ((), ())), preferred_element_type=F32)


def _silu(x):
    return x * jax.nn.sigmoid(x)


def _softplus(x):
    return jnp.maximum(x, 0.0) + jnp.log(1.0 + jnp.exp(-jnp.abs(x)))


def _rms(x, gain):
    return x * lax.rsqrt(jnp.mean(x * x, axis=-1, keepdims=True) + EPS) * gain


def _mod_kernel(c_ref, w_ref, b_ref, o_ref):
    c = c_ref[...]
    o_ref[0] = _dot(_silu(c), w_ref[0], precision=lax.Precision.HIGHEST) + b_ref[0]


def _mods(c, c_ctx, w_mod, b_mod):
    depth = w_mod.shape[0]
    rows = 16
    cc = jnp.zeros((rows, D_MODEL), F32).at[:BATCH].set(c).at[BATCH].set(c_ctx)
    tn = 1536
    out = pl.pallas_call(
        _mod_kernel,
        grid=(depth, N_MOD * D_MODEL // tn),
        in_specs=[pl.BlockSpec((rows, D_MODEL), lambda l, j: (0, 0)),
                  pl.BlockSpec((1, D_MODEL, tn), lambda l, j: (l, 0, j)),
                  pl.BlockSpec((1, 1, tn), lambda l, j: (l, 0, j))],
        out_specs=pl.BlockSpec((1, rows, tn), lambda l, j: (l, 0, j)),
        out_shape=jax.ShapeDtypeStruct((depth, rows, N_MOD * D_MODEL), F32),
        compiler_params=_params("parallel", "parallel"),
        name="mods",
    )(cc, w_mod, b_mod.reshape(depth, 1, N_MOD * D_MODEL))
    return out.reshape(depth, rows, N_MOD, D_MODEL)


def _modulated_norm(x, gain, ml_ref, mc_ref, is_ctx, shift_idx):
    shift = jnp.where(is_ctx, mc_ref[0, shift_idx:shift_idx + 1, :], ml_ref[0, shift_idx:shift_idx + 1, :])
    scale = jnp.where(is_ctx, mc_ref[0, shift_idx + 1:shift_idx + 2, :], ml_ref[0, shift_idx + 1:shift_idx + 2, :])
    return _rms(x, gain) * (1.0 + scale) + shift


def _ctx_rows(tile, tm, width):
    row = tile * tm + lax.broadcasted_iota(jnp.int32, (tm, width), 0)
    return row >= SEQ


def _proj_kernel(x_ref, ml_ref, mc_ref, g_ref, w_ref, wba_ref, p_ref, ba_ref, h_scr, *, tm):
    i = pl.program_id(1)
    j = pl.program_id(2)

    @pl.when(j == 0)
    def _():
        is_ctx = _ctx_rows(i, tm, D_MODEL)
        h_scr[...] = _modulated_norm(x_ref[0], g_ref[...], ml_ref, mc_ref, is_ctx, 0).astype(BF16)

    h = h_scr[...]
    p_ref[0] = _dot(h, w_ref[0]).astype(BF16)
    ba_ref[0] = _dot(h, wba_ref[0])


def _project(xs, mods, gain, w_main, w_ba, layer):
    tm, tn = 1152, 1280
    return pl.pallas_call(
        functools.partial(_proj_kernel, tm=tm),
        grid=(BATCH, TOK // tm, PROJ_COLS // tn),
        in_specs=[pl.BlockSpec((1, tm, D_MODEL), lambda b, i, j: (b, i, 0)),
                  pl.BlockSpec((1, N_MOD, D_MODEL), lambda b, i, j: (b, 0, 0)),
                  pl.BlockSpec((1, N_MOD, D_MODEL), lambda b, i, j: (BATCH, 0, 0)),
                  pl.BlockSpec((1, D_MODEL), lambda b, i, j: (0, 0)),
                  pl.BlockSpec((1, D_MODEL, tn), lambda b, i, j: (layer, 0, j)),
                  pl.BlockSpec((1, D_MODEL, LANES), lambda b, i, j: (layer, 0, 0))],
        out_specs=[pl.BlockSpec((1, tm, tn), lambda b, i, j: (b, i, j)),
                   pl.BlockSpec((1, tm, LANES), lambda b, i, j: (b, i, 0))],
        out_shape=[jax.ShapeDtypeStruct((BATCH, TOK, PROJ_COLS), BF16),
                   jax.ShapeDtypeStruct((BATCH, TOK, LANES), F32)],
        scratch_shapes=[pltpu.VMEM((tm, D_MODEL), BF16)],
        compiler_params=_params("parallel", "parallel", "arbitrary"),
        name="proj",
    )(xs, mods, mods, gain.reshape(1, D_MODEL), w_main, w_ba)


def _conv(x, w):
    n, c = x.shape
    t = lax.broadcasted_iota(jnp.int32, (n, c), 0)
    is_ctx = t >= SEQ
    local = jnp.where(is_ctx, t - SEQ, t)
    seg_len = jnp.where(is_ctx, CTX_LEN, SEQ)
    y = jnp.zeros_like(x)
    for j in range(CONV_WIDTH):
        s = j - CONV_WIDTH // 2
        if s == 0:
            y = y + x * w[j:j + 1, :]
        else:
            shifted = pltpu.roll(x, (-s) % n, 0)
            ok = jnp.logical_and(local + s >= 0, local + s < seg_len)
            y = y + jnp.where(ok, shifted, 0.0) * w[j:j + 1, :]
    return y


def _gdn_conv_kernel(x_ref, w_ref, o_ref):
    j = pl.program_id(1)
    y = _silu(_conv(x_ref[0].astype(F32), w_ref[...]))
    nrm = lax.rsqrt(jnp.sum(y * y, axis=-1, keepdims=True) + EPS)
    n_head_blocks = GDN_HEADS
    scale = jnp.where(j < n_head_blocks, nrm * GDN_HEAD_DIM ** -0.5, jnp.where(j < 2 * n_head_blocks, nrm, 1.0))
    o_ref[0] = (y * scale).astype(BF16)


def _gdn_conv(p, conv_w):
    nblk = 3 * BRANCH_WIDTH // LANES
    first = COL_QKV // LANES
    return pl.pallas_call(
        _gdn_conv_kernel,
        grid=(BATCH, nblk),
        in_specs=[pl.BlockSpec((1, TOK, LANES), lambda b, j: (b, 0, first + j)),
                  pl.BlockSpec((CONV_WIDTH, LANES), lambda b, j: (0, j))],
        out_specs=pl.BlockSpec((1, TOK, LANES), lambda b, j: (b, 0, j)),
        out_shape=jax.ShapeDtypeStruct((BATCH, TOK, 3 * BRANCH_WIDTH), BF16),
        compiler_params=_params("parallel", "parallel"),
        name="gdn_conv",
    )(p, conv_w)


def _lru_conv_kernel(x_ref, w_ref, b_ref, o_ref):
    o_ref[0] = _conv(x_ref[0].astype(F32), w_ref[...]) + b_ref[...]


def _lru_conv(p, conv_w, conv_b):
    nblk = BRANCH_WIDTH // LANES
    first = COL_LX // LANES
    return pl.pallas_call(
        _lru_conv_kernel,
        grid=(BATCH, nblk),
        in_specs=[pl.BlockSpec((1, TOK, LANES), lambda b, j: (b, 0, first + j)),
                  pl.BlockSpec((CONV_WIDTH, LANES), lambda b, j: (0, j)),
                  pl.BlockSpec((1, LANES), lambda b, j: (0, j))],
        out_specs=pl.BlockSpec((1, TOK, LANES), lambda b, j: (b, 0, j)),
        out_shape=jax.ShapeDtypeStruct((BATCH, TOK, BRANCH_WIDTH), F32),
        compiler_params=_params("parallel", "parallel"),
        name="lru_conv",
    )(p, conv_w, conv_b.reshape(1, BRANCH_WIDTH))


GDN_QM_ROWS = GDN_CHUNK + GDN_HEAD_DIM
GDN_GL_ROWS = 8
GDN_HEADS_PER_STEP = 2
GDN_PREP_UNROLL = 9


def _gdn_prepare(q_ref, k_ref, v_ref, gb_scr, qm_scr, nn_scr, o_scr, gl_scr, chunks, head, local):
    c = GDN_CHUNK
    cols = slice(local * LANES, (local + 1) * LANES)
    lane = lax.broadcasted_iota(jnp.int32, (c, LANES), 1)
    row = lax.broadcasted_iota(jnp.int32, (c, LANES), 0)
    ii = lax.broadcasted_iota(jnp.int32, (c, c), 0)
    jj = lax.broadcasted_iota(jnp.int32, (c, c), 1)
    eye = (ii == jj).astype(F32)
    masks =((ii >= jj, ii > jj, row > lane), (ii <= jj, ii < jj, row < lane))

    loaded = []
    for chunk in chunks:
        r0 = pl.multiple_of(chunk * c, c)
        k = k_ref[0, pl.ds(r0, c), cols]
        q = q_ref[0, pl.ds(r0, c), cols]
        kq = _dot_nt(jnp.concatenate([k, q], axis=0), k)
        loaded.append((chunk, r0, q, k, kq))

    chains = []
    for chunk, r0, q, k, kq in loaded:
        gb = gb_scr[pl.ds(r0, c), :]
        for d in range(2):
            col = head + d * GDN_HEADS
            beta = jnp.sum(jnp.where(lane == col, gb, 0.0), axis=-1, keepdims=True)
            g = jnp.sum(jnp.where(lane == col + 2 * GDN_HEADS, gb, 0.0), axis=-1, keepdims=True)
            incl, strict, strict_wide = masks[d]
            rhs = jnp.where(lane >= c, g, jnp.where(strict_wide, g, 0.0))
            e = _dot(incl.astype(F32), rhs, precision=lax.Precision.HIGHEST)
            chains.append(dict(chunk=chunk, r0=r0, d=d, q=q, k=k, kq=kq, beta=beta, e=e))

    for ch in chains:
        incl, strict, _ = masks[ch["d"]]
        e = ch["e"]
        decay = jnp.where(incl, jnp.exp(e[:, :c]), 0.0)
        gc = e[:, c:c + 1]
        last = 0 if ch["d"] == 1 else c - 1
        gc_last = e[last:last + 1, c:c + 1]
        ch.update(decay=decay, gc=gc, gc_last=gc_last, egc=jnp.exp(gc))
        ch["a"] = jnp.where(strict, ch["beta"] * ch["kq"][:c] * decay, 0.0)
        ch["t"] = eye
    s = 1
    while s < c:
        pair = jnp.logical_and((ii // (2 * s)) == (jj // (2 * s)), (ii // s) != (jj // s))
        for ch in chains:
            ch["a_off"] = jnp.where(pair, ch["a"], 0.0)
        if s == 1:
            for ch in chains:
                ch["t"] = eye - ch["a_off"]
        else:
            for ch in chains:
                ch["m"] = _dot(ch["t"].astype(BF16), ch["a_off"].astype(BF16))
            for ch in chains:
                ch["t"] = ch["t"] - _dot(ch["m"].astype(BF16), ch["t"].astype(BF16))
        s *= 2
    for ch in chains:
        r0, beta, egc = ch["r0"], ch["beta"], ch["egc"]
        kf = ch["k"].astype(F32)
        vf = v_ref[0, pl.ds(r0, c), cols].astype(F32)
        rhs2 = jnp.concatenate([vf * beta, kf * (beta * egc)], axis=1).astype(BF16)
        ch["uw"] = _dot(ch["t"].astype(BF16), rhs2).astype(BF16)
        ch["k_dec_t"] = (kf * jnp.exp(ch["gc_last"] - ch["gc"])).T.astype(BF16)
    for ch in chains:
        incl = masks[ch["d"]][0]
        qk = jnp.where(incl, ch["kq"][c:] * ch["decay"], 0.0).astype(BF16)
        ch["nm"] = _dot(ch["k_dec_t"], ch["uw"])
        ch["ow"] = _dot(qk, ch["uw"])
    for ch in chains:
        chunk, r0, nm, ow = ch["chunk"], ch["r0"], ch["nm"], ch["ow"]
        s = 2 * local + ch["d"]
        q0 = pl.multiple_of(chunk * GDN_QM_ROWS, 16)
        qm_scr[s, pl.ds(q0, c), :] = (ch["q"].astype(F32) * ch["egc"] - ow[:, GDN_HEAD_DIM:]).astype(BF16)
        qm_scr[s, pl.ds(q0 + c, GDN_HEAD_DIM), :] = nm[:, GDN_HEAD_DIM:].astype(BF16)
        nn_scr[s, pl.ds(pl.multiple_of(chunk * GDN_HEAD_DIM, GDN_HEAD_DIM), GDN_HEAD_DIM), :] = nm[:, :GDN_HEAD_DIM]
        o_scr[s, pl.ds(r0, c), :] = ow[:, :GDN_HEAD_DIM]
        gl_scr[s, pl.ds(pl.multiple_of(chunk * GDN_GL_ROWS, GDN_GL_ROWS), GDN_GL_ROWS), :] = jnp.broadcast_to(
            jnp.exp(ch["gc_last"]), (GDN_GL_ROWS, LANES))


def _gdn_advance(qm_scr, nn_scr, o_scr, gl_scr, d, chunk, state):
    c = GDN_CHUNK
    qm = qm_scr[d, pl.ds(pl.multiple_of(chunk * GDN_QM_ROWS, 16), GDN_QM_ROWS), :]
    r = _dot(qm, state.astype(BF16))
    rows = pl.ds(pl.multiple_of(chunk * c, c), c)
    o_scr[d, rows, :] = o_scr[d, rows, :] + r[:c]
    gl = gl_scr[d, pl.ds(pl.multiple_of(chunk * GDN_GL_ROWS, GDN_GL_ROWS), 1), :]
    n = nn_scr[d, pl.ds(pl.multiple_of(chunk * GDN_HEAD_DIM, GDN_HEAD_DIM), GDN_HEAD_DIM), :]
    return state * gl - r[c:] + n


def _gdn_kernel(q_ref, k_ref, v_ref, z_ref, ba_ref, alog_ref, dtb_ref, gain_ref, o_ref,
                gb_scr, qm_scr, nn_scr, o_scr, gl_scr):
    first_head = pl.program_id(1) * GDN_HEADS_PER_STEP
    c = GDN_CHUNK
    n_lat, n_ctx = SEQ // c, CTX_LEN // c
    n_chunks = n_lat + n_ctx
    ba = ba_ref[0]
    lane = lax.broadcasted_iota(jnp.int32, ba.shape, 1)
    g_all = -jnp.exp(alog_ref[...]) * _softplus(ba + dtb_ref[...])
    gb_scr[...] = jnp.where(lane < 2 * GDN_HEADS, jax.nn.sigmoid(ba), g_all)

    for local in range(GDN_HEADS_PER_STEP):
        def prepare(i, _, local=local):
            chunks = [i * GDN_PREP_UNROLL + j for j in range(GDN_PREP_UNROLL)]
            _gdn_prepare(q_ref, k_ref, v_ref, gb_scr, qm_scr, nn_scr, o_scr, gl_scr, chunks, first_head + local, local)
            return 0

        lax.fori_loop(0, n_chunks // GDN_PREP_UNROLL, prepare, 0)

    def advance(i, states):
        cf = jnp.where(i < n_ctx, n_lat + i, i - n_ctx)
        cb = n_chunks - 1 - i
        return tuple(_gdn_advance(qm_scr, nn_scr, o_scr, gl_scr, s, cb if s % 2 else cf, state)
                     for s, state in enumerate(states))

    zero = jnp.zeros((GDN_HEAD_DIM, GDN_HEAD_DIM), F32)
    lax.fori_loop(0, n_chunks, advance, (zero,) * (2 * GDN_HEADS_PER_STEP))
    for local in range(GDN_HEADS_PER_STEP):
        cols = slice(local * LANES, (local + 1) * LANES)
        o = o_scr[2 * local] + o_scr[2 * local + 1]
        o_ref[0, :, cols] = (_rms(o, gain_ref[...]) * _silu(z_ref[0, :, cols].astype(F32))).astype(BF16)


def _gdn(qkv, p, ba, a_log, dt_bias, out_gain):
    def pad_lanes(vals):
        row = jnp.zeros((LANES,), F32).at[2 * GDN_HEADS:4 * GDN_HEADS].set(vals.reshape(-1))
        return row.reshape(1, LANES)

    n_steps = GDN_HEADS // GDN_HEADS_PER_STEP
    n_chunks = TOK // GDN_CHUNK
    width = GDN_HEADS_PER_STEP * LANES
    zblk = COL_Z // width
    n_chain = 2 * GDN_HEADS_PER_STEP
    blk = lambda off: pl.BlockSpec((1, TOK, width), lambda b, h: (b, 0, off + h))
    vec = pl.BlockSpec((1, LANES), lambda b, h: (0, 0))
    return pl.pallas_call(
        _gdn_kernel,
        grid=(BATCH, n_steps),
        in_specs=[blk(0), blk(n_steps), blk(2 * n_steps), blk(zblk),
                  pl.BlockSpec((1, TOK, LANES), lambda b, h: (b, 0, 0)), vec, vec, vec],
        out_specs=pl.BlockSpec((1, TOK, width), lambda b, h: (b, 0, h)),
        out_shape=jax.ShapeDtypeStruct((BATCH, TOK, BRANCH_WIDTH), BF16),
        scratch_shapes=[pltpu.VMEM((TOK, LANES), F32),
                        pltpu.VMEM((n_chain, n_chunks * GDN_QM_ROWS, LANES), BF16),
                        pltpu.VMEM((n_chain, n_chunks * GDN_HEAD_DIM, LANES), F32),
                        pltpu.VMEM((n_chain, TOK, LANES), F32),
                        pltpu.VMEM((n_chain, n_chunks * GDN_GL_ROWS, LANES), F32)],
        compiler_params=_params("parallel", "parallel"),
        name="gdn",
    )(qkv, qkv, qkv, p, ba, pad_lanes(a_log), pad_lanes(dt_bias), out_gain.reshape(1, LANES))


def _rms_halves(x, gain):
    lane = lax.broadcasted_iota(jnp.int32, x.shape, 1)
    lo = lane < DIFF_HEAD_DIM
    x2 = x * x
    s_lo = jnp.sum(jnp.where(lo, x2, 0.0), axis=-1, keepdims=True)
    s_hi = jnp.sum(jnp.where(lo, 0.0, x2), axis=-1, keepdims=True)
    ms = jnp.where(lo, s_lo, s_hi) * (1.0 / DIFF_HEAD_DIM)
    return x * lax.rsqrt(ms + EPS) * gain


def _rope(x, cos, sin):
    lane = lax.broadcasted_iota(jnp.int32, x.shape, 1)
    first = (lane & ROPE_PAIRS) == 0
    partner = jnp.where(first, -pltpu.roll(x, LANES - ROPE_PAIRS, 1), pltpu.roll(x, ROPE_PAIRS, 1))
    return x * cos + partner * sin


ATTN_Q_BLOCK = 512
ATTN_GROUP_ROWS = 128


def _attn_kernel(q_ref, k_ref, v_ref, cosk_ref, sink_ref, cosq_ref, sinq_ref, qg_ref, kg_ref, lv_ref, og_ref,
                 o_ref, kn_scr, *, ctx_block, lam_init):
    qi = pl.program_id(2)

    @pl.when(qi == 0)
    def _():
        kn = _rope(_rms_halves(k_ref[0].astype(F32), kg_ref[...]), cosk_ref[...], sink_ref[...])
        kn_scr[...] = kn.astype(BF16)

    lv = lv_ref[...]
    lam = (jnp.exp(jnp.sum(lv[0:1] * lv[1:2], axis=-1, keepdims=True))
           - jnp.exp(jnp.sum(lv[2:3] * lv[3:4], axis=-1, keepdims=True)) + lam_init)
    def attend(n_rows, kn, v):
        q = _rope(_rms_halves(q_ref[0, :n_rows, :].astype(F32), qg_ref[...]), cosq_ref[:n_rows, :], sinq_ref[:n_rows, :])
        q = q * (DIFF_HEAD_DIM ** -0.5 * math.log2(math.e))
        lane = lax.broadcasted_iota(jnp.int32, q.shape, 1)
        lo = lane < DIFF_HEAD_DIM
        q1 = jnp.where(lo, q, 0.0).astype(BF16)
        q2 = jnp.where(lo, 0.0, q).astype(BF16)

        def half(s):
            p = jnp.exp2(s - jnp.max(s, axis=-1, keepdims=True))
            return _dot(p.astype(BF16), v), jnp.sum(p, axis=-1, keepdims=True)
        rows = ATTN_GROUP_ROWS
        scores = [(_dot_nt(q1[r:r + rows], kn), _dot_nt(q2[r:r + rows], kn)) for r in range(0, n_rows, rows)]
        for g, (s1, s2) in enumerate(scores):
            a1, l1 = half(s1)
            a2, l2 = half(s2)
            o = a1 * (1.0 / l1) - a2 * (lam / l2)
            o_ref[0, g * rows:(g + 1) * rows, :] = (_rms(o, og_ref[...]) * (1.0 - lam_init)).astype(BF16)

    if ctx_block is None:
        attend(q_ref.shape[1], kn_scr[...], v_ref[0])
    else:
        @pl.when(qi == ctx_block)
        def _():
            attend(CTX_LEN, kn_scr[SEQ:, :], v_ref[0, SEQ:, :])

        @pl.when(qi != ctx_block)
        def _():
            attend(q_ref.shape[1], kn_scr[...], v_ref[0])


def _rope_tables():
    n_rows = SEQ // GRID_W
    row_id = jnp.broadcast_to(jnp.arange(n_rows, dtype=F32)[:, None], (n_rows, GRID_W)).reshape(-1)
    col_id = jnp.broadcast_to(jnp.arange(GRID_W, dtype=F32)[None, :], (n_rows, GRID_W)).reshape(-1)
    inv_freq = jnp.power(ROPE_BASE, -jnp.arange(ROPE_PAIRS, dtype=F32) / ROPE_PAIRS)
    row_ang = row_id[:, None] * inv_freq
    col_ang = col_id[:, None] * inv_freq
    ang = jnp.concatenate([row_ang, row_ang, col_ang, col_ang], axis=-1)
    ang = jnp.concatenate([ang, ang], axis=-1)
    pad = ((0, CTX_LEN), (0, 0))
    return jnp.pad(jnp.cos(ang), pad, constant_values=1.0), jnp.pad(jnp.sin(ang), pad)


def _diff_attn(p, cos, sin, q_gain, k_gain, lam_vecs, out_gain, lam_init, with_ctx):
    tq = ATTN_Q_BLOCK
    n_rows = TOK if with_ctx else SEQ
    nq = pl.cdiv(n_rows, tq)
    nh = DIFF_HEADS
    qb, kb, vb = COL_DQ // LANES, COL_DK // LANES, COL_DV // LANES
    full = lambda off: pl.BlockSpec((1, TOK, LANES), lambda b, h, i: (b, 0, off + h))
    tab_full = pl.BlockSpec((TOK, LANES), lambda b, h, i: (0, 0))
    tab_q = pl.BlockSpec((tq, LANES), lambda b, h, i: (i, 0))
    vec = pl.BlockSpec((1, LANES), lambda b, h, i: (0, 0))
    tile2 = lambda g: jnp.concatenate([g, g]).reshape(1, LANES)
    return pl.pallas_call(
        functools.partial(_attn_kernel, ctx_block=SEQ // tq if with_ctx else None, lam_init=lam_init),
        grid=(BATCH, nh, nq),
        in_specs=[pl.BlockSpec((1, tq, LANES), lambda b, h, i: (b, i, qb + h)), full(kb), full(vb),
                  tab_full, tab_full, tab_q, tab_q, vec, vec,
                  pl.BlockSpec((4, DIFF_HEAD_DIM), lambda b, h, i: (0, 0)), vec],
        out_specs=pl.BlockSpec((1, tq, LANES), lambda b, h, i: (b, i, h)),
        out_shape=jax.ShapeDtypeStruct((BATCH, n_rows, BRANCH_WIDTH), BF16),
        scratch_shapes=[pltpu.VMEM((TOK, LANES), BF16)],
        compiler_params=_params("parallel", "parallel", "arbitrary"),
        name="diff_attn",
    )(p, p, p, cos, sin, cos, sin, tile2(q_gain), tile2(k_gain), lam_vecs, out_gain.reshape(1, LANES))


LRU_SLAB = 256
LRU_ROWS = 256
LRU_SCAN_BLOCK = 8


def _lru_kernel(xc_ref, y_ref, wg_ref, bg_ref, lam_ref, o_ref, af_scr, bf_scr, ab_scr, bb_scr):
    w = LRU_SLAB
    blk = LRU_SCAN_BLOCK
    sp = _softplus(-lam_ref[0])
    sub = lax.broadcasted_iota(jnp.int32, (LRU_ROWS, w), 0) % blk

    def gates(i, _):
        r0 = pl.multiple_of(i * LRU_ROWS, LRU_ROWS)
        xc = xc_ref[0, pl.ds(r0, LRU_ROWS), :]
        pre = _dot(xc.astype(BF16), wg_ref[0]) + bg_ref[0]
        for d, (a_scr, b_scr) in enumerate(((af_scr, bf_scr), (ab_scr, bb_scr))):
            r = jax.nn.sigmoid(pre[:, (2 * d) * w:(2 * d + 1) * w])
            gi = jax.nn.sigmoid(pre[:, (2 * d + 1) * w:(2 * d + 2) * w])
            log_a = -LRU_C * r * sp[d:d + 1]
            a = jnp.exp(log_a)
            b = jnp.sqrt(1.0 - a * a) * gi * xc
            shift = 1
            while shift < blk:
                if d == 0:
                    ok, roll_by = sub >= shift, shift
                else:
                    ok, roll_by = sub < blk - shift, LRU_ROWS - shift
                b = jnp.where(ok, a * pltpu.roll(b, roll_by, 0) + b, b)
                a = jnp.where(ok, a * pltpu.roll(a, roll_by, 0), a)
                shift *= 2
            a_scr[pl.ds(r0, LRU_ROWS), :] = a
            b_scr[pl.ds(r0, LRU_ROWS), :] = b
        return 0

    lax.fori_loop(0, TOK // LRU_ROWS, gates, 0)

    n_blk, n_lat_blk, n_ctx_blk = TOK // blk, SEQ // blk, CTX_LEN // blk

    def step(s, carry):
        h_f, h_b = carry
        rows_f = pl.ds(pl.multiple_of(jnp.where(s < n_ctx_blk, n_lat_blk + s, s - n_ctx_blk) * blk, blk), blk)
        rows_b = pl.ds(pl.multiple_of((n_blk - 1 - s) * blk, blk), blk)
        hf = af_scr[rows_f, :] * h_f + bf_scr[rows_f, :]
        bf_scr[rows_f, :] = hf
        hb = ab_scr[rows_b, :] * h_b + bb_scr[rows_b, :]
        bb_scr[rows_b, :] = hb
        return hf[blk - 1:blk, :], hb[0:1, :]

    zero = jnp.zeros((1, w), F32)
    lax.fori_loop(0, n_blk, step, (zero, zero), unroll=4)
    h = bf_scr[...] + bb_scr[...]
    o_ref[0] = (h * jax.nn.gelu(y_ref[0].astype(F32))).astype(BF16)


def _lru_gate_weights(w_gate, b_gate):
    n_slab = BRANCH_WIDTH // LRU_SLAB
    per = LRU_SLAB // LRU_BLOCK_DIM
    wg = w_gate.reshape(2, 2, n_slab, per, LRU_BLOCK_DIM, LRU_BLOCK_DIM)
    eye = jnp.eye(per, dtype=w_gate.dtype)
    dense = jnp.einsum('dgsnjk,nm->snjdgmk', wg, eye)
    dense = dense.reshape(n_slab, LRU_SLAB, 4 * LRU_SLAB)
    bg = b_gate.reshape(2, 2, n_slab, LRU_SLAB).transpose(2, 0, 1, 3).reshape(n_slab, 1, 4 * LRU_SLAB)
    return dense.astype(BF16), bg


def _lru(xc, p, w_gate, b_gate, lam):
    n_slab = BRANCH_WIDTH // LRU_SLAB
    wg, bg = _lru_gate_weights(w_gate, b_gate)
    lam_s = lam.reshape(2, n_slab, LRU_SLAB).transpose(1, 0, 2)
    yb = COL_LY // LRU_SLAB
    return pl.pallas_call(
        _lru_kernel,
        grid=(BATCH, n_slab),
        in_specs=[pl.BlockSpec((1, TOK, LRU_SLAB), lambda b, s: (b, 0, s)),
                  pl.BlockSpec((1, TOK, LRU_SLAB), lambda b, s: (b, 0, yb + s)),
                  pl.BlockSpec((1, LRU_SLAB, 4 * LRU_SLAB), lambda b, s: (s, 0, 0)),
                  pl.BlockSpec((1, 1, 4 * LRU_SLAB), lambda b, s: (s, 0, 0)),
                  pl.BlockSpec((1, 2, LRU_SLAB), lambda b, s: (s, 0, 0))],
        out_specs=pl.BlockSpec((1, TOK, LRU_SLAB), lambda b, s: (b, 0, s)),
        out_shape=jax.ShapeDtypeStruct((BATCH, TOK, BRANCH_WIDTH), BF16),
        scratch_shapes=[pltpu.VMEM((TOK, LRU_SLAB), F32)] * 4,
        compiler_params=_params("parallel", "parallel"),
        name="lru",
    )(xc, p, wg, bg, lam_s)


def _route(logits):
    lane = lax.broadcasted_iota(jnp.int32, logits.shape, 1)
    lane_f = lane.astype(F32)
    far = float(LANES)
    lg = jnp.where(lane < N_EXPERTS, logits, -jnp.inf)
    ex = jnp.exp(lg - jnp.max(lg, axis=-1, keepdims=True))
    probs = ex / jnp.sum(ex, axis=-1, keepdims=True)
    per_group = N_EXPERTS // N_GROUPS
    grp = lane // per_group

    def top2(vals):
        m1 = jnp.max(vals, axis=-1, keepdims=True)
        i1 = jnp.min(jnp.where(vals == m1, lane_f, far), axis=-1, keepdims=True)
        rest = jnp.where(lane_f == i1, -2.0, vals)
        m2 = jnp.max(rest, axis=-1, keepdims=True)
        i2 = jnp.min(jnp.where(rest == m2, lane_f, far), axis=-1, keepdims=True)
        return m1, i1, m2, i2

    best = jnp.zeros(logits.shape[:1] + (1,), jnp.int32)
    best_score = None
    for g in range(N_GROUPS):
        m1, _, m2, _ = top2(jnp.where(grp == g, probs, -1.0))
        score = m1 + m2
        if best_score is None:
            best_score = score
        else:
            better = score > best_score
            best = jnp.where(better, g, best)
            best_score = jnp.where(better, score, best_score)
    m1, i1, m2, i2 = top2(jnp.where(grp == best, probs, -1.0))
    den = m1 + m2
    return i1, i2, m1 / den, m2 / den


ROUTE_E, ROUTE_W, ROUTE_RANK = 0, 2, 4


def _pack_bf16_pairs(x):
    n = x.shape[1] // 2
    xb = x.astype(BF16).astype(F32)
    lo = pltpu.bitcast(xb[:, :n], jnp.uint32)
    hi = pltpu.bitcast(xb[:, n:], jnp.uint32)
    return (lo >> 16) | (hi & jnp.uint32(0xFFFF0000))


def _unpack_bf16_pairs(p):
    lo = pltpu.bitcast(p << 16, F32)
    hi = pltpu.bitcast(p & jnp.uint32(0xFFFF0000), F32)
    return jnp.concatenate([lo, hi], axis=1)


def _merge_kernel(ya_ref, yb_ref, yc_ref, gates_ref, x_ref, ml_ref, mc_ref, g2_ref, wbr_ref, wout_ref, wr_ref,
                  br_ref, xo_ref, h2_ref, route_ref, cnt_ref, cnt_scr, *, tm):
    i = pl.program_id(1)

    @pl.when(jnp.logical_and(pl.program_id(0) == 0, i == 0))
    def _():
        cnt_scr[...] = jnp.zeros_like(cnt_scr)

    acc = None
    for n, y_ref in enumerate((ya_ref, yb_ref, yc_ref)):
        yn = _dot(y_ref[0], wbr_ref[n])
        gate = jax.nn.sigmoid(gates_ref[0, :, n * D_MODEL:(n + 1) * D_MODEL].astype(F32))
        acc = gate * yn if acc is None else acc + gate * yn
    out = _dot(acc.astype(BF16), wout_ref[...])
    is_ctx = _ctx_rows(i, tm, D_MODEL)
    xn = x_ref[0] + jnp.where(is_ctx, mc_ref[0, 2:3, :], ml_ref[0, 2:3, :]) * out
    xo_ref[0] = xn
    h2 = _modulated_norm(xn, g2_ref[...], ml_ref, mc_ref, is_ctx, 3)
    h2_ref[0] = _pack_bf16_pairs(h2)
    i1, i2, w1, w2 = _route(_dot(h2.astype(BF16), wr_ref[...]) + br_ref[...])
    lane = lax.broadcasted_iota(jnp.int32, (tm, LANES), 1)
    lane_f = lane.astype(F32)
    chosen = jnp.where(jnp.logical_or(lane_f == i1, lane_f == i2), 1.0, 0.0)
    earlier = (lax.broadcasted_iota(jnp.int32, (tm, tm), 0) > lax.broadcasted_iota(jnp.int32, (tm, tm), 1))
    before = _dot(jnp.where(earlier, 1.0, 0.0).astype(BF16), chosen.astype(BF16)) + cnt_scr[...]
    rank1 = jnp.sum(jnp.where(lane_f == i1, before, 0.0), axis=-1, keepdims=True)
    rank2 = jnp.sum(jnp.where(lane_f == i2, before, 0.0), axis=-1, keepdims=True)
    cnt_scr[...] += jnp.sum(chosen, axis=0, keepdims=True)
    cnt_ref[...] = cnt_scr[...]
    record = jnp.zeros((tm, LANES), F32)
    for pos, val in enumerate((i1, i2, w1, w2, rank1, rank2)):
        record = jnp.where(lane == pos, val, record)
    route_ref[0] = record


def _merge(ya, yb, yc, p, xs, mods, gain2, w_branch, w_out, w_router, b_router, rows, tm):
    wr = jnp.zeros((D_MODEL, LANES), BF16).at[:, :N_EXPERTS].set(w_router.astype(BF16))
    br = jnp.zeros((1, LANES), F32).at[0, :N_EXPERTS].set(b_router)
    tile = lambda w: pl.BlockSpec((1, tm, w), lambda b, i: (b, i, 0))
    const = lambda shape: pl.BlockSpec(shape, lambda b, i: (0,) * len(shape))
    return pl.pallas_call(
        functools.partial(_merge_kernel, tm=tm),
        grid=(BATCH, rows // tm),
        in_specs=[tile(BRANCH_WIDTH), tile(BRANCH_WIDTH), tile(BRANCH_WIDTH), tile(3 * D_MODEL), tile(D_MODEL),
                  pl.BlockSpec((1, N_MOD, D_MODEL), lambda b, i: (b, 0, 0)),
                  pl.BlockSpec((1, N_MOD, D_MODEL), lambda b, i: (BATCH, 0, 0)),
                  const((1, D_MODEL)), const((3, BRANCH_WIDTH, D_MODEL)), const((D_MODEL, D_MODEL)),
                  const((D_MODEL, LANES)), const((1, LANES))],
        out_specs=[tile(D_MODEL), tile(D_MODEL // 2), tile(LANES), const((1, LANES))],
        out_shape=[jax.ShapeDtypeStruct((BATCH, rows, D_MODEL), F32),
                   jax.ShapeDtypeStruct((BATCH, rows, D_MODEL // 2), jnp.uint32),
                   jax.ShapeDtypeStruct((BATCH, rows, LANES), F32),
                   jax.ShapeDtypeStruct((1, LANES), F32)],
        scratch_shapes=[pltpu.VMEM((1, LANES), F32)],
        compiler_params=_params("arbitrary", "arbitrary"),
        name="merge",
    )(ya, yb, yc, p, xs, mods, mods, gain2.reshape(1, D_MODEL), w_branch, w_out, wr, br)


MOE_TILE = 512
SC_GATHER_ROWS = 64


def _sc_gather(table, idx):
    info = plsc.get_sparse_core_info()
    n_workers = info.num_cores * info.num_subcores
    n_rows, width = idx.shape[0], table.shape[1]
    per_worker = n_rows // n_workers
    assert per_worker * n_workers == n_rows and per_worker % SC_GATHER_ROWS == 0
    mesh = plsc.VectorSubcoreMesh(core_axis_name="c", subcore_axis_name="s")

    @functools.partial(
        pl.kernel, mesh=mesh, out_type=jax.ShapeDtypeStruct((n_rows, width), table.dtype),
        scratch_types=[pltpu.VMEM((SC_GATHER_ROWS,), jnp.int32),
                       pltpu.VMEM((SC_GATHER_ROWS, width), table.dtype),
                       pltpu.SemaphoreType.DMA],
        name="sc_gather")
    def gather(table_hbm, idx_hbm, out_hbm, idx_v, rows_v, sem):
        worker = lax.axis_index("s") * info.num_cores + lax.axis_index("c")
        base = worker * per_worker

        @pl.loop(0, per_worker // SC_GATHER_ROWS)
        def _(j):
            off = base + j * SC_GATHER_ROWS
            pltpu.sync_copy(idx_hbm.at[pl.ds(off, SC_GATHER_ROWS)], idx_v)
            pltpu.async_copy(table_hbm.at[idx_v], rows_v, sem).wait()
            pltpu.sync_copy(rows_v, out_hbm.at[pl.ds(off, SC_GATHER_ROWS)])

    return gather(table, idx)


SC_SCATTER_ROWS = 128


def _sc_scatter(table, dest, n_out):
    info = plsc.get_sparse_core_info()
    n_workers = info.num_cores * info.num_subcores
    n_tok, width = table.shape
    n_assign = dest.shape[0]
    per_worker = n_assign // n_workers
    assert per_worker * n_workers == n_assign and per_worker % SC_SCATTER_ROWS == 0 and n_tok % SC_SCATTER_ROWS == 0
    mesh = plsc.VectorSubcoreMesh(core_axis_name="c", subcore_axis_name="s")

    @functools.partial(
        pl.kernel, mesh=mesh, out_type=jax.ShapeDtypeStruct((n_out, width), table.dtype),
        scratch_types=[pltpu.VMEM((SC_SCATTER_ROWS,), jnp.int32),
                       pltpu.VMEM((SC_SCATTER_ROWS, width), table.dtype),
                       pltpu.SemaphoreType.DMA],
        name="sc_scatter")
    def scatter(table_hbm, dest_hbm, out_hbm, idx_v, rows_v, sem):
        worker = lax.axis_index("s") * info.num_cores + lax.axis_index("c")
        base = worker * per_worker

        @pl.loop(0, per_worker // SC_SCATTER_ROWS)
        def _(j):
            off = base + j * SC_SCATTER_ROWS
            pltpu.sync_copy(dest_hbm.at[pl.ds(off, SC_SCATTER_ROWS)], idx_v)
            pltpu.sync_copy(table_hbm.at[pl.ds(lax.rem(off, n_tok), SC_SCATTER_ROWS)], rows_v)
            pltpu.async_copy(rows_v, out_hbm.at[idx_v], sem).wait()

    return scatter(table, dest)


def _experts_kernel(tile_expert_ref, n_valid_ref, x_ref, wgu_ref, wd_ref, o_ref, wgu_scr, wd_scr):
    j = pl.program_id(0)
    valid = j < n_valid_ref[0]
    fresh = jnp.logical_or(j == 0, tile_expert_ref[j] != tile_expert_ref[jnp.maximum(j - 1, 0)])

    @pl.when(jnp.logical_and(valid, fresh))
    def _():
        wgu_scr[...] = wgu_ref[0, 0].astype(BF16)
        wd_scr[...] = wd_ref[0, 0].astype(BF16)

    @pl.when(valid)
    def _():
        x = _unpack_bf16_pairs(x_ref[...]).astype(BF16)
        gu = _dot(x, wgu_scr[...])
        act = (_silu(gu[:, :EXPERT_FF]) * gu[:, EXPERT_FF:]).astype(BF16)
        o_ref[...] = _pack_bf16_pairs(_dot(act, wd_scr[...]))

    @pl.when(jnp.logical_not(valid))
    def _():
        o_ref[...] = jnp.zeros_like(o_ref)


def _experts(x_sorted, tile_expert, n_valid, w_gate_up, w_down, layer):
    n_tiles = x_sorted.shape[0] // MOE_TILE
    half = D_MODEL // 2
    return pl.pallas_call(
        _experts_kernel,
        grid_spec=pltpu.PrefetchScalarGridSpec(
            num_scalar_prefetch=2, grid=(n_tiles,),
            in_specs=[pl.BlockSpec((MOE_TILE, half), lambda j, te, nv: (j, 0)),
                      pl.BlockSpec((1, 1, D_MODEL, 2 * EXPERT_FF), lambda j, te, nv: (layer, te[j], 0, 0)),
                      pl.BlockSpec((1, 1, EXPERT_FF, D_MODEL), lambda j, te, nv: (layer, te[j], 0, 0))],
            out_specs=pl.BlockSpec((MOE_TILE, half), lambda j, te, nv: (j, 0)),
            scratch_shapes=[pltpu.VMEM((D_MODEL, 2 * EXPERT_FF), BF16), pltpu.VMEM((EXPERT_FF, D_MODEL), BF16)]),
        out_shape=jax.ShapeDtypeStruct((x_sorted.shape[0], half), jnp.uint32),
        compiler_params=_params("arbitrary"),
        name="experts",
    )(tile_expert, n_valid, x_sorted, w_gate_up, w_down)


def _combine_kernel(y1_ref, y2_ref, route_ref, x_ref, ml_ref, mc_ref, o_ref, *, tm):
    i = pl.program_id(1)
    route = route_ref[0]
    w1 = route[:, ROUTE_W:ROUTE_W + 1]
    w2 = route[:, ROUTE_W + 1:ROUTE_W + 2]
    moe = w1 * _unpack_bf16_pairs(y1_ref[0, 0]) + w2 * _unpack_bf16_pairs(y2_ref[0, 0])
    is_ctx = _ctx_rows(i, tm, D_MODEL)
    o_ref[0] = x_ref[0] + jnp.where(is_ctx, mc_ref[0, 5:6, :], ml_ref[0, 5:6, :]) * moe


def _combine(y_pairs, route, xs, mods, rows, tm):
    half = D_MODEL // 2
    tile = lambda w: pl.BlockSpec((1, tm, w), lambda b, i: (b, i, 0))
    slot = lambda s: pl.BlockSpec((1, 1, tm, half), lambda b, i: (s, b, i, 0))
    return pl.pallas_call(
        functools.partial(_combine_kernel, tm=tm),
        grid=(BATCH, rows // tm),
        in_specs=[slot(0), slot(1), tile(LANES), tile(D_MODEL),
                  pl.BlockSpec((1, N_MOD, D_MODEL), lambda b, i: (b, 0, 0)),
                  pl.BlockSpec((1, N_MOD, D_MODEL), lambda b, i: (BATCH, 0, 0))],
        out_specs=tile(D_MODEL),
        out_shape=jax.ShapeDtypeStruct((BATCH, rows, D_MODEL), F32),
        compiler_params=_params("parallel", "parallel"),
        name="combine",
    )(y_pairs, y_pairs, route, xs, mods, mods)


def _moe(h2, route, counts, xs, mods, w_gate_up, w_down, layer, rows, tm):
    n_tok = BATCH * rows
    half = D_MODEL // 2
    n_sorted = 2 * n_tok + N_EXPERTS * MOE_TILE
    n_tiles = n_sorted // MOE_TILE
    rec = route.reshape(n_tok, LANES)
    expert = rec[:, ROUTE_E:ROUTE_E + 2].astype(jnp.int32)
    rank = rec[:, ROUTE_RANK:ROUTE_RANK + 2].astype(jnp.int32)
    count = counts[0, :N_EXPERTS].astype(jnp.int32)
    padded = (count + MOE_TILE - 1) // MOE_TILE * MOE_TILE
    end = jnp.cumsum(padded)
    start = end - padded
    first = jnp.sum(jnp.where(expert[:, :1] == jnp.arange(N_EXPERTS), start, 0), axis=1)
    second = jnp.sum(jnp.where(expert[:, 1:] == jnp.arange(N_EXPERTS), start, 0), axis=1)
    dest = jnp.concatenate([first + rank[:, 0], second + rank[:, 1]])
    tile_start = jnp.arange(n_tiles, dtype=jnp.int32) * MOE_TILE
    tile_expert = jnp.minimum(jnp.sum(tile_start[:, None] >= end[None, :], axis=1), N_EXPERTS - 1).astype(jnp.int32)
    n_valid = (end[-1:] // MOE_TILE).astype(jnp.int32)
    x_sorted = _sc_scatter(h2.reshape(n_tok, half), dest, n_sorted)
    y_sorted = _experts(x_sorted, tile_expert, n_valid, w_gate_up, w_down, layer)
    y_pairs = _sc_gather(y_sorted, dest).reshape(2, BATCH, rows, half)
    return _combine(y_pairs, route, xs, mods, rows, tm)


def _split_w_in(w_in):
    bw = BRANCH_WIDTH
    sizes = (3 * bw, bw, 2 * GDN_HEADS, 2 * GDN_HEADS, bw, bw, bw, bw, bw, 3 * D_MODEL)
    offs = [0]
    for s in sizes:
        offs.append(offs[-1] + s)
    part = lambda i: w_in[:, :, offs[i]:offs[i + 1]]
    main = jnp.concatenate([part(9), part(0), part(1), part(4), part(5), part(6), part(7), part(8)], axis=2)
    ba = jnp.zeros(w_in.shape[:2] + (LANES,), F32).at[:, :, :4 * GDN_HEADS].set(
        jnp.concatenate([part(2), part(3)], axis=2))
    return main.astype(BF16), ba.astype(BF16)


def kernel(x, c, ctx, c_ctx, w_mod, b_mod, norm1_gain, norm2_gain, w_in, gdn_conv_w, gdn_a_log, gdn_dt_bias, gdn_out_gain, diff_q_gain, diff_k_gain, diff_lambda, diff_out_gain, lru_conv_w, lru_conv_b, lru_w_gate, lru_b_gate, lru_lambda, w_branch, w_out, w_router, b_router, w_gate_up, w_down):
    mods = _mods(c, c_ctx, w_mod, b_mod)
    cos, sin = _rope_tables()
    xs = jnp.concatenate([x, ctx], axis=1)
    w_main, w_ba = _split_w_in(w_in)
    for layer in range(DEPTH):
        last = layer == DEPTH - 1
        lam_init = 0.8 - 0.6 * math.exp(-0.3 * layer)
        m = mods[layer]
        p, ba = _project(xs, m, norm1_gain[layer], w_main, w_ba, layer)
        qkv = _gdn_conv(p, gdn_conv_w[layer])
        ya = _gdn(qkv, p, ba, gdn_a_log[layer], gdn_dt_bias[layer], gdn_out_gain[layer])
        yb = _diff_attn(p, cos, sin, diff_q_gain[layer], diff_k_gain[layer], diff_lambda[layer],
                        diff_out_gain[layer], lam_init, with_ctx=not last)
        xc = _lru_conv(p, lru_conv_w[layer], lru_conv_b[layer])
        yc = _lru(xc, p, lru_w_gate[layer], lru_b_gate[layer], lru_lambda[layer])
        rows, tm = (SEQ, 512) if last else (TOK, 768)
        xs, h2, route, counts = _merge(ya, yb, yc, p, xs, m, norm2_gain[layer], w_branch[layer].astype(BF16),
                                       w_out[layer].astype(BF16), w_router, b_router, rows, tm)
        xs = _moe(h2, route, counts, xs, m, w_gate_up, w_down, layer, rows, tm)
    return xs
```

```python
import functools
import math

import jax
import jax.numpy as jnp
from jax import lax
from jax.experimental import pallas as pl
from jax.experimental.pallas import tpu as pltpu
from jax.experimental.pallas import tpu_sc as plsc

F32 = jnp.float32
BF16 = jnp.bfloat16

D_MODEL = 1024
BATCH = 8
SEQ = 2048
DEPTH = 2
GRID_W = 64
CTX_LEN = 256
TOK = SEQ + CTX_LEN
N_MOD = 6
EPS = 1e-6
CONV_WIDTH = 4
BRANCH_WIDTH = 512
GDN_HEADS = 4
GDN_HEAD_DIM = 128
GDN_CHUNK = 64
DIFF_HEADS = 4
DIFF_HEAD_DIM = 64
ROPE_BASE = 10000.0
ROPE_PAIRS = DIFF_HEAD_DIM // 4
LRU_BLOCKS = 8
LRU_BLOCK_DIM = BRANCH_WIDTH // LRU_BLOCKS
LRU_C = 8.0
N_EXPERTS = 16
N_GROUPS = 4
EXPERT_FF = 512

LANES = 128
VMEM_LIMIT = 56 * 1024 * 1024

COL_GATES = 0
COL_QKV = 3 * D_MODEL
COL_Z = COL_QKV + 3 * BRANCH_WIDTH
COL_DQ = COL_Z + BRANCH_WIDTH
COL_DK = COL_DQ + BRANCH_WIDTH
COL_DV = COL_DK + BRANCH_WIDTH
COL_LX = COL_DV + BRANCH_WIDTH
COL_LY = COL_LX + BRANCH_WIDTH
PROJ_COLS = COL_LY + BRANCH_WIDTH


def _params(*sem):
    return pltpu.CompilerParams(dimension_semantics=sem, vmem_limit_bytes=VMEM_LIMIT)


def _dot(a, b, precision=None):
    return jnp.dot(a, b, preferred_element_type=F32, precision=precision)


def _dot_nt(a, b):
    return lax.dot_general(a, b, (((1,), (1,)), ((), ())), preferred_element_type=F32)


_sigmoid = jax.nn.sigmoid


def _silu(x):
    return x * _sigmoid(x)


def _softplus(x):
    return jnp.maximum(x, 0.0) + jnp.log(1.0 + jnp.exp(-jnp.abs(x)))


def _rms(x, gain):
    return x * lax.rsqrt(jnp.mean(x * x, axis=-1, keepdims=True) + EPS) * gain


def _mod_kernel(c_ref, w_ref, b_ref, o_ref):
    c = c_ref[...]
    o_ref[0] = _dot(_silu(c), w_ref[0], precision=lax.Precision.HIGHEST) + b_ref[0]


def _mods(c, c_ctx, w_mod, b_mod):
    depth = w_mod.shape[0]
    rows = 16
    cc = jnp.zeros((rows, D_MODEL), F32).at[:BATCH].set(c).at[BATCH].set(c_ctx)
    tn = 1536
    out = pl.pallas_call(
        _mod_kernel,
        grid=(depth, N_MOD * D_MODEL // tn),
        in_specs=[pl.BlockSpec((rows, D_MODEL), lambda l, j: (0, 0)),
                  pl.BlockSpec((1, D_MODEL, tn), lambda l, j: (l, 0, j)),
                  pl.BlockSpec((1, 1, tn), lambda l, j: (l, 0, j))],
        out_specs=pl.BlockSpec((1, rows, tn), lambda l, j: (l, 0, j)),
        out_shape=jax.ShapeDtypeStruct((depth, rows, N_MOD * D_MODEL), F32),
        compiler_params=_params("parallel", "parallel"),
        name="mods",
    )(cc, w_mod, b_mod.reshape(depth, 1, N_MOD * D_MODEL))
    return out.reshape(depth, rows, N_MOD, D_MODEL)


def _modulated_norm(x, gain, ml_ref, mc_ref, is_ctx, shift_idx):
    shift = jnp.where(is_ctx, mc_ref[0, shift_idx:shift_idx + 1, :], ml_ref[0, shift_idx:shift_idx + 1, :])
    scale = jnp.where(is_ctx, mc_ref[0, shift_idx + 1:shift_idx + 2, :], ml_ref[0, shift_idx + 1:shift_idx + 2, :])
    return _rms(x, gain) * (1.0 + scale) + shift


def _ctx_rows(tile, tm, width):
    row = tile * tm + lax.broadcasted_iota(jnp.int32, (tm, width), 0)
    return row >= SEQ


def _proj_kernel(x_ref, ml_ref, mc_ref, g_ref, w_ref, wba_ref, p_ref, ba_ref, h_scr, *, tm):
    i = pl.program_id(1)
    j = pl.program_id(2)

    @pl.when(j == 0)
    def _():
        is_ctx = _ctx_rows(i, tm, D_MODEL)
        h_scr[...] = _modulated_norm(x_ref[0], g_ref[...], ml_ref, mc_ref, is_ctx, 0).astype(BF16)

    h = h_scr[...]
    p_ref[0] = _dot(h, w_ref[0]).astype(BF16)
    ba_ref[0] = _dot(h, wba_ref[0])


def _project(xs, mods, gain, w_main, w_ba, layer):
    tm, tn = 1152, 1280
    return pl.pallas_call(
        functools.partial(_proj_kernel, tm=tm),
        grid=(BATCH, TOK // tm, PROJ_COLS // tn),
        in_specs=[pl.BlockSpec((1, tm, D_MODEL), lambda b, i, j: (b, i, 0)),
                  pl.BlockSpec((1, N_MOD, D_MODEL), lambda b, i, j: (b, 0, 0)),
                  pl.BlockSpec((1, N_MOD, D_MODEL), lambda b, i, j: (BATCH, 0, 0)),
                  pl.BlockSpec((1, D_MODEL), lambda b, i, j: (0, 0)),
                  pl.BlockSpec((1, D_MODEL, tn), lambda b, i, j: (layer, 0, j)),
                  pl.BlockSpec((1, D_MODEL, LANES), lambda b, i, j: (layer, 0, 0))],
        out_specs=[pl.BlockSpec((1, tm, tn), lambda b, i, j: (b, i, j)),
                   pl.BlockSpec((1, tm, LANES), lambda b, i, j: (b, i, 0))],
        out_shape=[jax.ShapeDtypeStruct((BATCH, TOK, PROJ_COLS), BF16),
                   jax.ShapeDtypeStruct((BATCH, TOK, LANES), F32)],
        scratch_shapes=[pltpu.VMEM((tm, D_MODEL), BF16)],
        compiler_params=_params("parallel", "parallel", "arbitrary"),
        name="proj",
    )(xs, mods, mods, gain.reshape(1, D_MODEL), w_main, w_ba)


def _conv(x, w):
    n, c = x.shape
    t = lax.broadcasted_iota(jnp.int32, (n, c), 0)
    is_ctx = t >= SEQ
    local = jnp.where(is_ctx, t - SEQ, t)
    seg_len = jnp.where(is_ctx, CTX_LEN, SEQ)
    y = jnp.zeros_like(x)
    for j in range(CONV_WIDTH):
        s = j - CONV_WIDTH // 2
        if s == 0:
            y = y + x * w[j:j + 1, :]
        else:
            shifted = pltpu.roll(x, (-s) % n, 0)
            ok = jnp.logical_and(local + s >= 0, local + s < seg_len)
            y = y + jnp.where(ok, shifted, 0.0) * w[j:j + 1, :]
    return y


def _gdn_conv_kernel(x_ref, w_ref, o_ref):
    j = pl.program_id(1)
    for h in range(GDN_HEADS):
        cols = slice(h * GDN_HEAD_DIM, (h + 1) * GDN_HEAD_DIM)
        y = _silu(_conv(x_ref[0, :, cols].astype(F32), w_ref[:, cols]))
        nrm = lax.rsqrt(jnp.sum(y * y, axis=-1, keepdims=True) + EPS)
        scale = jnp.where(j == 0, nrm * GDN_HEAD_DIM ** -0.5, jnp.where(j == 1, nrm, 1.0))
        o_ref[0, :, cols] = (y * scale).astype(BF16)


def _gdn_conv(p, conv_w):
    first = COL_QKV // BRANCH_WIDTH
    return pl.pallas_call(
        _gdn_conv_kernel,
        grid=(BATCH, 3),
        in_specs=[pl.BlockSpec((1, TOK, BRANCH_WIDTH), lambda b, j: (b, 0, first + j)),
                  pl.BlockSpec((CONV_WIDTH, BRANCH_WIDTH), lambda b, j: (0, j))],
        out_specs=pl.BlockSpec((1, TOK, BRANCH_WIDTH), lambda b, j: (b, 0, j)),
        out_shape=jax.ShapeDtypeStruct((BATCH, TOK, 3 * BRANCH_WIDTH), BF16),
        compiler_params=_params("parallel", "parallel"),
        name="gdn_conv",
    )(p, conv_w)


GDN_QM_ROWS = GDN_CHUNK + GDN_HEAD_DIM
GDN_GL_ROWS = 8
GDN_HEADS_PER_STEP = 2
GDN_PREP_UNROLL = 9


def _gdn_prepare(q_ref, k_ref, v_ref, gb_scr, qm_scr, nn_scr, o_scr, gl_scr, chunks, head, local):
    c = GDN_CHUNK
    cols = slice(local * LANES, (local + 1) * LANES)
    lane = lax.broadcasted_iota(jnp.int32, (c, LANES), 1)
    row = lax.broadcasted_iota(jnp.int32, (c, LANES), 0)
    ii = lax.broadcasted_iota(jnp.int32, (c, c), 0)
    jj = lax.broadcasted_iota(jnp.int32, (c, c), 1)
    eye = (ii == jj).astype(F32)
    masks =((ii >= jj, ii > jj, row > lane), (ii <= jj, ii < jj, row < lane))

    loaded = []
    for chunk in chunks:
        r0 = pl.multiple_of(chunk * c, c)
        k = k_ref[0, pl.ds(r0, c), cols]
        q = q_ref[0, pl.ds(r0, c), cols]
        kq = _dot_nt(jnp.concatenate([k, q], axis=0), k)
        loaded.append((chunk, r0, q, k, kq))

    chains = []
    for chunk, r0, q, k, kq in loaded:
        gb = gb_scr[pl.ds(r0, c), :]
        for d in range(2):
            col = head + d * GDN_HEADS
            beta = jnp.sum(jnp.where(lane == col, gb, 0.0), axis=-1, keepdims=True)
            g = jnp.sum(jnp.where(lane == col + 2 * GDN_HEADS, gb, 0.0), axis=-1, keepdims=True)
            incl, strict, strict_wide = masks[d]
            rhs = jnp.where(lane >= c, g, jnp.where(strict_wide, g, 0.0))
            e = _dot(incl.astype(F32), rhs, precision=lax.Precision.HIGHEST)
            chains.append(dict(chunk=chunk, r0=r0, d=d, q=q, k=k, kq=kq, beta=beta, e=e))

    for ch in chains:
        incl, strict, _ = masks[ch["d"]]
        e = ch["e"]
        decay = jnp.where(incl, jnp.exp(e[:, :c]), 0.0)
        gc = e[:, c:c + 1]
        last = 0 if ch["d"] == 1 else c - 1
        gc_last = e[last:last + 1, c:c + 1]
        ch.update(decay=decay, gc=gc, gc_last=gc_last, egc=jnp.exp(gc))
        ch["a"] = jnp.where(strict, ch["beta"] * ch["kq"][:c] * decay, 0.0)
        ch["t"] = eye
    s = 1
    while s < c:
        pair = jnp.logical_and((ii // (2 * s)) == (jj // (2 * s)), (ii // s) != (jj // s))
        for ch in chains:
            ch["a_off"] = jnp.where(pair, ch["a"], 0.0)
        if s == 1:
            for ch in chains:
                ch["t"] = eye - ch["a_off"]
        else:
            for ch in chains:
                ch["m"] = _dot(ch["t"].astype(BF16), ch["a_off"].astype(BF16))
            for ch in chains:
                ch["t"] = ch["t"] - _dot(ch["m"].astype(BF16), ch["t"].astype(BF16))
        s *= 2
    for ch in chains:
        r0, beta, egc = ch["r0"], ch["beta"], ch["egc"]
        kf = ch["k"].astype(F32)
        vf = v_ref[0, pl.ds(r0, c), cols].astype(F32)
        rhs2 = jnp.concatenate([vf * beta, kf * (beta * egc)], axis=1).astype(BF16)
        ch["uw"] = _dot(ch["t"].astype(BF16), rhs2).astype(BF16)
        ch["k_dec_t"] = (kf * jnp.exp(ch["gc_last"] - ch["gc"])).T.astype(BF16)
    for ch in chains:
        incl = masks[ch["d"]][0]
        qk = jnp.where(incl, ch["kq"][c:] * ch["decay"], 0.0).astype(BF16)
        ch["nm"] = _dot(ch["k_dec_t"], ch["uw"])
        ch["ow"] = _dot(qk, ch["uw"])
    for ch in chains:
        chunk, r0, nm, ow = ch["chunk"], ch["r0"], ch["nm"], ch["ow"]
        s = 2 * local + ch["d"]
        q0 = pl.multiple_of(chunk * GDN_QM_ROWS, 16)
        qm_scr[s, pl.ds(q0, c), :] = (ch["q"].astype(F32) * ch["egc"] - ow[:, GDN_HEAD_DIM:]).astype(BF16)
        qm_scr[s, pl.ds(q0 + c, GDN_HEAD_DIM), :] = nm[:, GDN_HEAD_DIM:].astype(BF16)
        nn_scr[s, pl.ds(pl.multiple_of(chunk * GDN_HEAD_DIM, GDN_HEAD_DIM), GDN_HEAD_DIM), :] = nm[:, :GDN_HEAD_DIM]
        o_scr[s, pl.ds(r0, c), :] = ow[:, :GDN_HEAD_DIM]
        gl_scr[s, pl.ds(pl.multiple_of(chunk * GDN_GL_ROWS, GDN_GL_ROWS), GDN_GL_ROWS), :] = jnp.broadcast_to(
            jnp.exp(ch["gc_last"]), (GDN_GL_ROWS, LANES))


def _gdn_advance(qm_scr, nn_scr, o_scr, gl_scr, d, chunk, state):
    c = GDN_CHUNK
    qm = qm_scr[d, pl.ds(pl.multiple_of(chunk * GDN_QM_ROWS, 16), GDN_QM_ROWS), :]
    r = _dot(qm, state.astype(BF16))
    rows = pl.ds(pl.multiple_of(chunk * c, c), c)
    o_scr[d, rows, :] = o_scr[d, rows, :] + r[:c]
    gl = gl_scr[d, pl.ds(pl.multiple_of(chunk * GDN_GL_ROWS, GDN_GL_ROWS), 1), :]
    n = nn_scr[d, pl.ds(pl.multiple_of(chunk * GDN_HEAD_DIM, GDN_HEAD_DIM), GDN_HEAD_DIM), :]
    return state * gl - r[c:] + n


def _gdn_kernel(q_ref, k_ref, v_ref, z_ref, ba_ref, alog_ref, dtb_ref, gain_ref, o_ref,
                gb_scr, qm_scr, nn_scr, o_scr, gl_scr):
    first_head = pl.program_id(1) * GDN_HEADS_PER_STEP
    c = GDN_CHUNK
    n_lat, n_ctx = SEQ // c, CTX_LEN // c
    n_chunks = n_lat + n_ctx
    ba = ba_ref[0]
    lane = lax.broadcasted_iota(jnp.int32, ba.shape, 1)
    g_all = -jnp.exp(alog_ref[...]) * _softplus(ba + dtb_ref[...])
    gb_scr[...] = jnp.where(lane < 2 * GDN_HEADS, _sigmoid(ba), g_all)

    for local in range(GDN_HEADS_PER_STEP):
        def prepare(i, _, local=local):
            chunks = [i * GDN_PREP_UNROLL + j for j in range(GDN_PREP_UNROLL)]
            _gdn_prepare(q_ref, k_ref, v_ref, gb_scr, qm_scr, nn_scr, o_scr, gl_scr, chunks, first_head + local, local)
            return 0

        lax.fori_loop(0, n_chunks // GDN_PREP_UNROLL, prepare, 0)

    def advance(i, states):
        cf = jnp.where(i < n_ctx, n_lat + i, i - n_ctx)
        cb = n_chunks - 1 - i
        return tuple(_gdn_advance(qm_scr, nn_scr, o_scr, gl_scr, s, cb if s % 2 else cf, state)
                     for s, state in enumerate(states))

    zero = jnp.zeros((GDN_HEAD_DIM, GDN_HEAD_DIM), F32)
    lax.fori_loop(0, n_chunks, advance, (zero,) * (2 * GDN_HEADS_PER_STEP))
    for local in range(GDN_HEADS_PER_STEP):
        cols = slice(local * LANES, (local + 1) * LANES)
        o = o_scr[2 * local] + o_scr[2 * local + 1]
        o_ref[0, :, cols] = (_rms(o, gain_ref[...]) * _silu(z_ref[0, :, cols].astype(F32))).astype(BF16)


def _gdn(qkv, p, ba, a_log, dt_bias, out_gain):
    def pad_lanes(vals):
        row = jnp.zeros((LANES,), F32).at[2 * GDN_HEADS:4 * GDN_HEADS].set(vals.reshape(-1))
        return row.reshape(1, LANES)

    n_steps = GDN_HEADS // GDN_HEADS_PER_STEP
    n_chunks = TOK // GDN_CHUNK
    width = GDN_HEADS_PER_STEP * LANES
    zblk = COL_Z // width
    n_chain = 2 * GDN_HEADS_PER_STEP
    blk = lambda off: pl.BlockSpec((1, TOK, width), lambda b, h: (b, 0, off + h))
    vec = pl.BlockSpec((1, LANES), lambda b, h: (0, 0))
    return pl.pallas_call(
        _gdn_kernel,
        grid=(BATCH, n_steps),
        in_specs=[blk(0), blk(n_steps), blk(2 * n_steps), blk(zblk),
                  pl.BlockSpec((1, TOK, LANES), lambda b, h: (b, 0, 0)), vec, vec, vec],
        out_specs=pl.BlockSpec((1, TOK, width), lambda b, h: (b, 0, h)),
        out_shape=jax.ShapeDtypeStruct((BATCH, TOK, BRANCH_WIDTH), BF16),
        scratch_shapes=[pltpu.VMEM((TOK, LANES), F32),
                        pltpu.VMEM((n_chain, n_chunks * GDN_QM_ROWS, LANES), BF16),
                        pltpu.VMEM((n_chain, n_chunks * GDN_HEAD_DIM, LANES), F32),
                        pltpu.VMEM((n_chain, TOK, LANES), F32),
                        pltpu.VMEM((n_chain, n_chunks * GDN_GL_ROWS, LANES), F32)],
        compiler_params=_params("parallel", "parallel"),
        name="gdn",
    )(qkv, qkv, qkv, p, ba, pad_lanes(a_log), pad_lanes(dt_bias), out_gain.reshape(1, LANES))


def _rms_halves(x, gain):
    lane = lax.broadcasted_iota(jnp.int32, x.shape, 1)
    lo = lane < DIFF_HEAD_DIM
    x2 = x * x
    s_lo = jnp.sum(jnp.where(lo, x2, 0.0), axis=-1, keepdims=True)
    s_hi = jnp.sum(jnp.where(lo, 0.0, x2), axis=-1, keepdims=True)
    ms = jnp.where(lo, s_lo, s_hi) * (1.0 / DIFF_HEAD_DIM)
    return x * lax.rsqrt(ms + EPS) * gain


def _rope(x, cos, sin):
    lane = lax.broadcasted_iota(jnp.int32, x.shape, 1)
    first = (lane & ROPE_PAIRS) == 0
    partner = jnp.where(first, -pltpu.roll(x, LANES - ROPE_PAIRS, 1), pltpu.roll(x, ROPE_PAIRS, 1))
    return x * cos + partner * sin


ATTN_Q_BLOCK = 512
ATTN_GROUP_ROWS = 128


def _attn_kernel(q_ref, k_ref, v_ref, cosk_ref, sink_ref, cosq_ref, sinq_ref, qg_ref, kg_ref, lv_ref, og_ref,
                 o_ref, kn_scr, *, ctx_block, lam_init):
    qi = pl.program_id(2)

    @pl.when(qi == 0)
    def _():
        kn = _rope(_rms_halves(k_ref[0].astype(F32), kg_ref[...]), cosk_ref[...], sink_ref[...])
        kn_scr[...] = kn.astype(BF16)

    lv = lv_ref[...]
    lam = (jnp.exp(jnp.sum(lv[0:1] * lv[1:2], axis=-1, keepdims=True))
           - jnp.exp(jnp.sum(lv[2:3] * lv[3:4], axis=-1, keepdims=True)) + lam_init)
    def attend(n_rows, kn, v):
        q = _rope(_rms_halves(q_ref[0, :n_rows, :].astype(F32), qg_ref[...]), cosq_ref[:n_rows, :], sinq_ref[:n_rows, :])
        q = q * (DIFF_HEAD_DIM ** -0.5 * math.log2(math.e))
        lane = lax.broadcasted_iota(jnp.int32, q.shape, 1)
        lo = lane < DIFF_HEAD_DIM
        q1 = jnp.where(lo, q, 0.0).astype(BF16)
        q2 = jnp.where(lo, 0.0, q).astype(BF16)

        def half(s):
            p = jnp.exp2(s - jnp.max(s, axis=-1, keepdims=True))
            return _dot(p.astype(BF16), v), jnp.sum(p, axis=-1, keepdims=True)
        rows = ATTN_GROUP_ROWS
        scores = [(_dot_nt(q1[r:r + rows], kn), _dot_nt(q2[r:r + rows], kn)) for r in range(0, n_rows, rows)]
        for g, (s1, s2) in enumerate(scores):
            a1, l1 = half(s1)
            a2, l2 = half(s2)
            o = a1 * (1.0 / l1) - a2 * (lam / l2)
            o_ref[0, g * rows:(g + 1) * rows, :] = (_rms(o, og_ref[...]) * (1.0 - lam_init)).astype(BF16)

    if ctx_block is None:
        attend(q_ref.shape[1], kn_scr[...], v_ref[0])
    else:
        @pl.when(qi == ctx_block)
        def _():
            attend(CTX_LEN, kn_scr[SEQ:, :], v_ref[0, SEQ:, :])

        @pl.when(qi != ctx_block)
        def _():
            attend(q_ref.shape[1], kn_scr[...], v_ref[0])


def _rope_tables():
    n_rows = SEQ // GRID_W
    row_id = jnp.broadcast_to(jnp.arange(n_rows, dtype=F32)[:, None], (n_rows, GRID_W)).reshape(-1)
    col_id = jnp.broadcast_to(jnp.arange(GRID_W, dtype=F32)[None, :], (n_rows, GRID_W)).reshape(-1)
    inv_freq = jnp.power(ROPE_BASE, -jnp.arange(ROPE_PAIRS, dtype=F32) / ROPE_PAIRS)
    row_ang = row_id[:, None] * inv_freq
    col_ang = col_id[:, None] * inv_freq
    ang = jnp.concatenate([row_ang, row_ang, col_ang, col_ang], axis=-1)
    ang = jnp.concatenate([ang, ang], axis=-1)
    pad = ((0, CTX_LEN), (0, 0))
    return jnp.pad(jnp.cos(ang), pad, constant_values=1.0), jnp.pad(jnp.sin(ang), pad)


def _diff_attn(p, cos, sin, q_gain, k_gain, lam_vecs, out_gain, lam_init, with_ctx):
    tq = ATTN_Q_BLOCK
    n_rows = TOK if with_ctx else SEQ
    nq = pl.cdiv(n_rows, tq)
    nh = DIFF_HEADS
    qb, kb, vb = COL_DQ // LANES, COL_DK // LANES, COL_DV // LANES
    full = lambda off: pl.BlockSpec((1, TOK, LANES), lambda b, h, i: (b, 0, off + h))
    tab_full = pl.BlockSpec((TOK, LANES), lambda b, h, i: (0, 0))
    tab_q = pl.BlockSpec((tq, LANES), lambda b, h, i: (i, 0))
    vec = pl.BlockSpec((1, LANES), lambda b, h, i: (0, 0))
    tile2 = lambda g: jnp.concatenate([g, g]).reshape(1, LANES)
    return pl.pallas_call(
        functools.partial(_attn_kernel, ctx_block=SEQ // tq if with_ctx else None, lam_init=lam_init),
        grid=(BATCH, nh, nq),
        in_specs=[pl.BlockSpec((1, tq, LANES), lambda b, h, i: (b, i, qb + h)), full(kb), full(vb),
                  tab_full, tab_full, tab_q, tab_q, vec, vec,
                  pl.BlockSpec((4, DIFF_HEAD_DIM), lambda b, h, i: (0, 0)), vec],
        out_specs=pl.BlockSpec((1, tq, LANES), lambda b, h, i: (b, i, h)),
        out_shape=jax.ShapeDtypeStruct((BATCH, n_rows, BRANCH_WIDTH), BF16),
        scratch_shapes=[pltpu.VMEM((TOK, LANES), BF16)],
        compiler_params=_params("parallel", "parallel", "arbitrary"),
        name="diff_attn",
    )(p, p, p, cos, sin, cos, sin, tile2(q_gain), tile2(k_gain), lam_vecs, out_gain.reshape(1, LANES))


LRU_SLAB = 256
LRU_ROWS = 256
LRU_SCAN_BLOCK = 8


def _lru_kernel(x_ref, y_ref, cw_ref, cb_ref, wg_ref, bg_ref, lam_ref, o_ref, xc_scr, af_scr, bf_scr, ab_scr, bb_scr):
    w = LRU_SLAB
    blk = LRU_SCAN_BLOCK
    for c0 in range(0, w, LANES):
        cols = slice(c0, c0 + LANES)
        xc_scr[:, cols] = _conv(x_ref[0, :, cols].astype(F32), cw_ref[:, cols]) + cb_ref[:, cols]
    sp = _softplus(-lam_ref[0])
    sub = lax.broadcasted_iota(jnp.int32, (LRU_ROWS, w), 0) % blk

    def gates(i, _):
        r0 = pl.multiple_of(i * LRU_ROWS, LRU_ROWS)
        xc = xc_scr[pl.ds(r0, LRU_ROWS), :]
        pre = _dot(xc.astype(BF16), wg_ref[0]) + bg_ref[0]
        for d, (a_scr, b_scr) in enumerate(((af_scr, bf_scr), (ab_scr, bb_scr))):
            r = _sigmoid(pre[:, (2 * d) * w:(2 * d + 1) * w])
            gi = _sigmoid(pre[:, (2 * d + 1) * w:(2 * d + 2) * w])
            log_a = -LRU_C * r * sp[d:d + 1]
            a = jnp.exp(log_a)
            b = jnp.sqrt(1.0 - a * a) * gi * xc
            shift = 1
            while shift < blk:
                if d == 0:
                    ok, roll_by = sub >= shift, shift
                else:
                    ok, roll_by = sub < blk - shift, LRU_ROWS - shift
                b = jnp.where(ok, a * pltpu.roll(b, roll_by, 0) + b, b)
                a = jnp.where(ok, a * pltpu.roll(a, roll_by, 0), a)
                shift *= 2
            a_scr[pl.ds(r0, LRU_ROWS), :] = a
            b_scr[pl.ds(r0, LRU_ROWS), :] = b
        return 0

    lax.fori_loop(0, TOK // LRU_ROWS, gates, 0)

    n_blk, n_lat_blk, n_ctx_blk = TOK // blk, SEQ // blk, CTX_LEN // blk

    def step(s, carry):
        h_f, h_b = carry
        rows_f = pl.ds(pl.multiple_of(jnp.where(s < n_ctx_blk, n_lat_blk + s, s - n_ctx_blk) * blk, blk), blk)
        rows_b = pl.ds(pl.multiple_of((n_blk - 1 - s) * blk, blk), blk)
        hf = af_scr[rows_f, :] * h_f + bf_scr[rows_f, :]
        bf_scr[rows_f, :] = hf
        hb = ab_scr[rows_b, :] * h_b + bb_scr[rows_b, :]
        bb_scr[rows_b, :] = hb
        return hf[blk - 1:blk, :], hb[0:1, :]

    zero = jnp.zeros((1, w), F32)
    lax.fori_loop(0, n_blk, step, (zero, zero), unroll=4)
    h = bf_scr[...] + bb_scr[...]
    o_ref[0] = (h * jax.nn.gelu(y_ref[0].astype(F32))).astype(BF16)


def _lru_gate_weights(w_gate, b_gate):
    n_slab = BRANCH_WIDTH // LRU_SLAB
    per = LRU_SLAB // LRU_BLOCK_DIM
    wg = w_gate.reshape(2, 2, n_slab, per, LRU_BLOCK_DIM, LRU_BLOCK_DIM)
    eye = jnp.eye(per, dtype=w_gate.dtype)
    dense = jnp.einsum('dgsnjk,nm->snjdgmk', wg, eye)
    dense = dense.reshape(n_slab, LRU_SLAB, 4 * LRU_SLAB)
    bg = b_gate.reshape(2, 2, n_slab, LRU_SLAB).transpose(2, 0, 1, 3).reshape(n_slab, 1, 4 * LRU_SLAB)
    return dense.astype(BF16), bg


def _lru(p, conv_w, conv_b, w_gate, b_gate, lam):
    n_slab = BRANCH_WIDTH // LRU_SLAB
    wg, bg = _lru_gate_weights(w_gate, b_gate)
    lam_s = lam.reshape(2, n_slab, LRU_SLAB).transpose(1, 0, 2)
    xb, yb = COL_LX // LRU_SLAB, COL_LY // LRU_SLAB
    return pl.pallas_call(
        _lru_kernel,
        grid=(BATCH, n_slab),
        in_specs=[pl.BlockSpec((1, TOK, LRU_SLAB), lambda b, s: (b, 0, xb + s)),
                  pl.BlockSpec((1, TOK, LRU_SLAB), lambda b, s: (b, 0, yb + s)),
                  pl.BlockSpec((CONV_WIDTH, LRU_SLAB), lambda b, s: (0, s)),
                  pl.BlockSpec((1, LRU_SLAB), lambda b, s: (0, s)),
                  pl.BlockSpec((1, LRU_SLAB, 4 * LRU_SLAB), lambda b, s: (s, 0, 0)),
                  pl.BlockSpec((1, 1, 4 * LRU_SLAB), lambda b, s: (s, 0, 0)),
                  pl.BlockSpec((1, 2, LRU_SLAB), lambda b, s: (s, 0, 0))],
        out_specs=pl.BlockSpec((1, TOK, LRU_SLAB), lambda b, s: (b, 0, s)),
        out_shape=jax.ShapeDtypeStruct((BATCH, TOK, BRANCH_WIDTH), BF16),
        scratch_shapes=[pltpu.VMEM((TOK, LRU_SLAB), F32)] * 5,
        compiler_params=_params("parallel", "parallel"),
        name="lru",
    )(p, p, conv_w, conv_b.reshape(1, BRANCH_WIDTH), wg, bg, lam_s)


def _route(logits):
    lane = lax.broadcasted_iota(jnp.int32, logits.shape, 1)
    lane_f = lane.astype(F32)
    far = float(LANES)
    lg = jnp.where(lane < N_EXPERTS, logits, -jnp.inf)
    ex = jnp.exp(lg - jnp.max(lg, axis=-1, keepdims=True))
    probs = ex / jnp.sum(ex, axis=-1, keepdims=True)
    per_group = N_EXPERTS // N_GROUPS
    grp = lane // per_group

    def top2(vals):
        m1 = jnp.max(vals, axis=-1, keepdims=True)
        i1 = jnp.min(jnp.where(vals == m1, lane_f, far), axis=-1, keepdims=True)
        rest = jnp.where(lane_f == i1, -2.0, vals)
        m2 = jnp.max(rest, axis=-1, keepdims=True)
        i2 = jnp.min(jnp.where(rest == m2, lane_f, far), axis=-1, keepdims=True)
        return m1, i1, m2, i2

    best = jnp.zeros(logits.shape[:1] + (1,), jnp.int32)
    best_score = None
    for g in range(N_GROUPS):
        m1, _, m2, _ = top2(jnp.where(grp == g, probs, -1.0))
        score = m1 + m2
        if best_score is None:
            best_score = score
        else:
            better = score > best_score
            best = jnp.where(better, g, best)
            best_score = jnp.where(better, score, best_score)
    m1, i1, m2, i2 = top2(jnp.where(grp == best, probs, -1.0))
    den = m1 + m2
    return i1, i2, m1 / den, m2 / den


ROUTE_E, ROUTE_W, ROUTE_RANK = 0, 2, 4


def _pack_bf16_pairs(x):
    n = x.shape[1] // 2
    xb = x.astype(BF16).astype(F32)
    lo = pltpu.bitcast(xb[:, :n], jnp.uint32)
    hi = pltpu.bitcast(xb[:, n:], jnp.uint32)
    return (lo >> 16) | (hi & jnp.uint32(0xFFFF0000))


def _unpack_bf16_pairs(p):
    lo = pltpu.bitcast(p << 16, F32)
    hi = pltpu.bitcast(p & jnp.uint32(0xFFFF0000), F32)
    return jnp.concatenate([lo, hi], axis=1)


def _merge_kernel(ya_ref, yb_ref, yc_ref, gates_ref, x_ref, ml_ref, mc_ref, g2_ref, wbr_ref, wout_ref, wr_ref,
                  br_ref, xo_ref, h2_ref, route_ref, cnt_ref, cnt_scr, *, tm):
    i = pl.program_id(1)

    @pl.when(jnp.logical_and(pl.program_id(0) == 0, i == 0))
    def _():
        cnt_scr[...] = jnp.zeros_like(cnt_scr)

    acc = None
    for n, y_ref in enumerate((ya_ref, yb_ref, yc_ref)):
        yn = _dot(y_ref[0], wbr_ref[n])
        gate = _sigmoid(gates_ref[0, :, n * D_MODEL:(n + 1) * D_MODEL].astype(F32))
        acc = gate * yn if acc is None else acc + gate * yn
    out = _dot(acc.astype(BF16), wout_ref[...])
    is_ctx = _ctx_rows(i, tm, D_MODEL)
    xn = x_ref[0] + jnp.where(is_ctx, mc_ref[0, 2:3, :], ml_ref[0, 2:3, :]) * out
    xo_ref[0] = xn
    h2 = _modulated_norm(xn, g2_ref[...], ml_ref, mc_ref, is_ctx, 3)
    h2_ref[0] = _pack_bf16_pairs(h2)
    i1, i2, w1, w2 = _route(_dot(h2.astype(BF16), wr_ref[...]) + br_ref[...])
    lane = lax.broadcasted_iota(jnp.int32, (tm, LANES), 1)
    lane_f = lane.astype(F32)
    chosen = jnp.where(jnp.logical_or(lane_f == i1, lane_f == i2), 1.0, 0.0)
    earlier = (lax.broadcasted_iota(jnp.int32, (tm, tm), 0) > lax.broadcasted_iota(jnp.int32, (tm, tm), 1))
    before = _dot(jnp.where(earlier, 1.0, 0.0).astype(BF16), chosen.astype(BF16)) + cnt_scr[...]
    rank1 = jnp.sum(jnp.where(lane_f == i1, before, 0.0), axis=-1, keepdims=True)
    rank2 = jnp.sum(jnp.where(lane_f == i2, before, 0.0), axis=-1, keepdims=True)
    cnt_scr[...] += jnp.sum(chosen, axis=0, keepdims=True)
    cnt_ref[...] = cnt_scr[...]
    record = jnp.zeros((tm, LANES), F32)
    for pos, val in enumerate((i1, i2, w1, w2, rank1, rank2)):
        record = jnp.where(lane == pos, val, record)
    route_ref[0] = record


def _merge(ya, yb, yc, p, xs, mods, gain2, w_branch, w_out, w_router, b_router, rows, tm):
    wr = jnp.zeros((D_MODEL, LANES), BF16).at[:, :N_EXPERTS].set(w_router.astype(BF16))
    br = jnp.zeros((1, LANES), F32).at[0, :N_EXPERTS].set(b_router)
    tile = lambda w: pl.BlockSpec((1, tm, w), lambda b, i: (b, i, 0))
    const = lambda shape: pl.BlockSpec(shape, lambda b, i: (0,) * len(shape))
    return pl.pallas_call(
        functools.partial(_merge_kernel, tm=tm),
        grid=(BATCH, rows // tm),
        in_specs=[tile(BRANCH_WIDTH), tile(BRANCH_WIDTH), tile(BRANCH_WIDTH), tile(3 * D_MODEL), tile(D_MODEL),
                  pl.BlockSpec((1, N_MOD, D_MODEL), lambda b, i: (b, 0, 0)),
                  pl.BlockSpec((1, N_MOD, D_MODEL), lambda b, i: (BATCH, 0, 0)),
                  const((1, D_MODEL)), const((3, BRANCH_WIDTH, D_MODEL)), const((D_MODEL, D_MODEL)),
                  const((D_MODEL, LANES)), const((1, LANES))],
        out_specs=[tile(D_MODEL), tile(D_MODEL // 2), tile(LANES), const((1, LANES))],
        out_shape=[jax.ShapeDtypeStruct((BATCH, rows, D_MODEL), F32),
                   jax.ShapeDtypeStruct((BATCH, rows, D_MODEL // 2), jnp.uint32),
                   jax.ShapeDtypeStruct((BATCH, rows, LANES), F32),
                   jax.ShapeDtypeStruct((1, LANES), F32)],
        scratch_shapes=[pltpu.VMEM((1, LANES), F32)],
        compiler_params=_params("arbitrary", "arbitrary"),
        name="merge",
    )(ya, yb, yc, p, xs, mods, mods, gain2.reshape(1, D_MODEL), w_branch, w_out, wr, br)


MOE_TILE = 512
SC_GATHER_ROWS = 64


def _sc_gather(table, idx):
    info = plsc.get_sparse_core_info()
    n_workers = info.num_cores * info.num_subcores
    n_rows, width = idx.shape[0], table.shape[1]
    per_worker = n_rows // n_workers
    assert per_worker * n_workers == n_rows and per_worker % SC_GATHER_ROWS == 0
    mesh = plsc.VectorSubcoreMesh(core_axis_name="c", subcore_axis_name="s")

    @functools.partial(
        pl.kernel, mesh=mesh, out_type=jax.ShapeDtypeStruct((n_rows, width), table.dtype),
        scratch_types=[pltpu.VMEM((SC_GATHER_ROWS,), jnp.int32),
                       pltpu.VMEM((SC_GATHER_ROWS, width), table.dtype),
                       pltpu.SemaphoreType.DMA],
        name="sc_gather")
    def gather(table_hbm, idx_hbm, out_hbm, idx_v, rows_v, sem):
        worker = lax.axis_index("s") * info.num_cores + lax.axis_index("c")
        base = worker * per_worker

        @pl.loop(0, per_worker // SC_GATHER_ROWS)
        def _(j):
            off = base + j * SC_GATHER_ROWS
            pltpu.sync_copy(idx_hbm.at[pl.ds(off, SC_GATHER_ROWS)], idx_v)
            pltpu.async_copy(table_hbm.at[idx_v], rows_v, sem).wait()
            pltpu.sync_copy(rows_v, out_hbm.at[pl.ds(off, SC_GATHER_ROWS)])

    return gather(table, idx)


SC_SCATTER_ROWS = 128


def _sc_scatter(table, dest, n_out):
    info = plsc.get_sparse_core_info()
    n_workers = info.num_cores * info.num_subcores
    n_tok, width = table.shape
    n_assign = dest.shape[0]
    per_worker = n_assign // n_workers
    assert per_worker * n_workers == n_assign and per_worker % SC_SCATTER_ROWS == 0 and n_tok % SC_SCATTER_ROWS == 0
    mesh = plsc.VectorSubcoreMesh(core_axis_name="c", subcore_axis_name="s")

    @functools.partial(
        pl.kernel, mesh=mesh, out_type=jax.ShapeDtypeStruct((n_out, width), table.dtype),
        scratch_types=[pltpu.VMEM((SC_SCATTER_ROWS,), jnp.int32),
                       pltpu.VMEM((SC_SCATTER_ROWS, width), table.dtype),
                       pltpu.SemaphoreType.DMA],
        name="sc_scatter")
    def scatter(table_hbm, dest_hbm, out_hbm, idx_v, rows_v, sem):
        worker = lax.axis_index("s") * info.num_cores + lax.axis_index("c")
        base = worker * per_worker

        @pl.loop(0, per_worker // SC_SCATTER_ROWS)
        def _(j):
            off = base + j * SC_SCATTER_ROWS
            pltpu.sync_copy(dest_hbm.at[pl.ds(off, SC_SCATTER_ROWS)], idx_v)
            pltpu.sync_copy(table_hbm.at[pl.ds(lax.rem(off, n_tok), SC_SCATTER_ROWS)], rows_v)
            pltpu.async_copy(rows_v, out_hbm.at[idx_v], sem).wait()

    return scatter(table, dest)


def _experts_kernel(tile_expert_ref, n_valid_ref, x_ref, wgu_ref, wd_ref, o_ref, wgu_scr, wd_scr):
    j = pl.program_id(0)
    valid = j < n_valid_ref[0]
    fresh = jnp.logical_or(j == 0, tile_expert_ref[j] != tile_expert_ref[jnp.maximum(j - 1, 0)])

    @pl.when(jnp.logical_and(valid, fresh))
    def _():
        wgu_scr[...] = wgu_ref[0, 0].astype(BF16)
        wd_scr[...] = wd_ref[0, 0].astype(BF16)

    @pl.when(valid)
    def _():
        x = _unpack_bf16_pairs(x_ref[...]).astype(BF16)
        gu = _dot(x, wgu_scr[...])
        act = (_silu(gu[:, :EXPERT_FF]) * gu[:, EXPERT_FF:]).astype(BF16)
        o_ref[...] = _pack_bf16_pairs(_dot(act, wd_scr[...]))

    @pl.when(jnp.logical_not(valid))
    def _():
        o_ref[...] = jnp.zeros_like(o_ref)


def _experts(x_sorted, tile_expert, n_valid, w_gate_up, w_down, layer):
    n_tiles = x_sorted.shape[0] // MOE_TILE
    half = D_MODEL // 2
    return pl.pallas_call(
        _experts_kernel,
        grid_spec=pltpu.PrefetchScalarGridSpec(
            num_scalar_prefetch=2, grid=(n_tiles,),
            in_specs=[pl.BlockSpec((MOE_TILE, half), lambda j, te, nv: (j, 0)),
                      pl.BlockSpec((1, 1, D_MODEL, 2 * EXPERT_FF), lambda j, te, nv: (layer, te[j], 0, 0)),
                      pl.BlockSpec((1, 1, EXPERT_FF, D_MODEL), lambda j, te, nv: (layer, te[j], 0, 0))],
            out_specs=pl.BlockSpec((MOE_TILE, half), lambda j, te, nv: (j, 0)),
            scratch_shapes=[pltpu.VMEM((D_MODEL, 2 * EXPERT_FF), BF16), pltpu.VMEM((EXPERT_FF, D_MODEL), BF16)]),
        out_shape=jax.ShapeDtypeStruct((x_sorted.shape[0], half), jnp.uint32),
        compiler_params=_params("arbitrary"),
        name="experts",
    )(tile_expert, n_valid, x_sorted, w_gate_up, w_down)


def _combine_kernel(y1_ref, y2_ref, route_ref, x_ref, ml_ref, mc_ref, o_ref, *, tm):
    i = pl.program_id(1)
    route = route_ref[0]
    w1 = route[:, ROUTE_W:ROUTE_W + 1]
    w2 = route[:, ROUTE_W + 1:ROUTE_W + 2]
    moe = w1 * _unpack_bf16_pairs(y1_ref[0, 0]) + w2 * _unpack_bf16_pairs(y2_ref[0, 0])
    is_ctx = _ctx_rows(i, tm, D_MODEL)
    o_ref[0] = x_ref[0] + jnp.where(is_ctx, mc_ref[0, 5:6, :], ml_ref[0, 5:6, :]) * moe


def _combine(y_pairs, route, xs, mods, rows, tm):
    half = D_MODEL // 2
    tile = lambda w: pl.BlockSpec((1, tm, w), lambda b, i: (b, i, 0))
    slot = lambda s: pl.BlockSpec((1, 1, tm, half), lambda b, i: (s, b, i, 0))
    return pl.pallas_call(
        functools.partial(_combine_kernel, tm=tm),
        grid=(BATCH, rows // tm),
        in_specs=[slot(0), slot(1), tile(LANES), tile(D_MODEL),
                  pl.BlockSpec((1, N_MOD, D_MODEL), lambda b, i: (b, 0, 0)),
                  pl.BlockSpec((1, N_MOD, D_MODEL), lambda b, i: (BATCH, 0, 0))],
        out_specs=tile(D_MODEL),
        out_shape=jax.ShapeDtypeStruct((BATCH, rows, D_MODEL), F32),
        compiler_params=_params("parallel", "parallel"),
        name="combine",
    )(y_pairs, y_pairs, route, xs, mods, mods)


def _moe(h2, route, counts, xs, mods, w_gate_up, w_down, layer, rows, tm):
    n_tok = BATCH * rows
    half = D_MODEL // 2
    n_sorted = 2 * n_tok + N_EXPERTS * MOE_TILE
    n_tiles = n_sorted // MOE_TILE
    rec = route.reshape(n_tok, LANES)
    expert = rec[:, ROUTE_E:ROUTE_E + 2].astype(jnp.int32)
    rank = rec[:, ROUTE_RANK:ROUTE_RANK + 2].astype(jnp.int32)
    count = counts[0, :N_EXPERTS].astype(jnp.int32)
    padded = (count + MOE_TILE - 1) // MOE_TILE * MOE_TILE
    end = jnp.cumsum(padded)
    start = end - padded
    first = jnp.sum(jnp.where(expert[:, :1] == jnp.arange(N_EXPERTS), start, 0), axis=1)
    second = jnp.sum(jnp.where(expert[:, 1:] == jnp.arange(N_EXPERTS), start, 0), axis=1)
    dest = jnp.concatenate([first + rank[:, 0], second + rank[:, 1]])
    tile_start = jnp.arange(n_tiles, dtype=jnp.int32) * MOE_TILE
    tile_expert = jnp.minimum(jnp.sum(tile_start[:, None] >= end[None, :], axis=1), N_EXPERTS - 1).astype(jnp.int32)
    n_valid = (end[-1:] // MOE_TILE).astype(jnp.int32)
    x_sorted = _sc_scatter(h2.reshape(n_tok, half), dest, n_sorted)
    y_sorted = _experts(x_sorted, tile_expert, n_valid, w_gate_up, w_down, layer)
    y_pairs = _sc_gather(y_sorted, dest).reshape(2, BATCH, rows, half)
    return _combine(y_pairs, route, xs, mods, rows, tm)


def _split_w_in(w_in):
    bw = BRANCH_WIDTH
    sizes = (3 * bw, bw, 2 * GDN_HEADS, 2 * GDN_HEADS, bw, bw, bw, bw, bw, 3 * D_MODEL)
    offs = [0]
    for s in sizes:
        offs.append(offs[-1] + s)
    part = lambda i: w_in[:, :, offs[i]:offs[i + 1]]
    main = jnp.concatenate([part(9), part(0), part(1), part(4), part(5), part(6), part(7), part(8)], axis=2)
    ba = jnp.zeros(w_in.shape[:2] + (LANES,), F32).at[:, :, :4 * GDN_HEADS].set(
        jnp.concatenate([part(2), part(3)], axis=2))
    return main.astype(BF16), ba.astype(BF16)


def kernel(x, c, ctx, c_ctx, w_mod, b_mod, norm1_gain, norm2_gain, w_in, gdn_conv_w, gdn_a_log, gdn_dt_bias, gdn_out_gain, diff_q_gain, diff_k_gain, diff_lambda, diff_out_gain, lru_conv_w, lru_conv_b, lru_w_gate, lru_b_gate, lru_lambda, w_branch, w_out, w_router, b_router, w_gate_up, w_down):
    mods = _mods(c, c_ctx, w_mod, b_mod)
    cos, sin = _rope_tables()
    xs = jnp.concatenate([x, ctx], axis=1)
    w_main, w_ba = _split_w_in(w_in)
    for layer in range(DEPTH):
        last = layer == DEPTH - 1
        lam_init = 0.8 - 0.6 * math.exp(-0.3 * layer)
        m = mods[layer]
        p, ba = _project(xs, m, norm1_gain[layer], w_main, w_ba, layer)
        qkv = _gdn_conv(p, gdn_conv_w[layer])
        ya = _gdn(qkv, p, ba, gdn_a_log[layer], gdn_dt_bias[layer], gdn_out_gain[layer])
        yb = _diff_attn(p, cos, sin, diff_q_gain[layer], diff_k_gain[layer], diff_lambda[layer],
                        diff_out_gain[layer], lam_init, with_ctx=not last)
        yc = _lru(p, lru_conv_w[layer], lru_conv_b[layer], lru_w_gate[layer], lru_b_gate[layer], lru_lambda[layer])
        rows, tm = (SEQ, 512) if last else (TOK, 768)
        xs, h2, route, counts = _merge(ya, yb, yc, p, xs, m, norm2_gain[layer], w_branch[layer].astype(BF16),
                                       w_out[layer].astype(BF16), w_router, b_router, rows, tm)
        xs = _moe(h2, route, counts, xs, m, w_gate_up, w_down, layer, rows, tm)
    return xs
```

```python
import functools
import math

import jax
import jax.numpy as jnp
from jax import lax
from jax.experimental import pallas as pl
from jax.experimental.pallas import tpu as pltpu
from jax.experimental.pallas import tpu_sc as plsc

F32 = jnp.float32
BF16 = jnp.bfloat16

D_MODEL = 1024
BATCH = 8
SEQ = 2048
DEPTH = 2
GRID_W = 64
CTX_LEN = 256
TOK = SEQ + CTX_LEN
N_MOD = 6
EPS = 1e-6
CONV_WIDTH = 4
BRANCH_WIDTH = 512
GDN_HEADS = 4
GDN_HEAD_DIM = 128
GDN_CHUNK = 64
DIFF_HEADS = 4
DIFF_HEAD_DIM = 64
ROPE_BASE = 10000.0
ROPE_PAIRS = DIFF_HEAD_DIM // 4
LRU_BLOCKS = 8
LRU_BLOCK_DIM = BRANCH_WIDTH // LRU_BLOCKS
LRU_C = 8.0
N_EXPERTS = 16
N_GROUPS = 4
EXPERT_FF = 512

LANES = 128
VMEM_LIMIT = 56 * 1024 * 1024

COL_GATES = 0
COL_QKV = 3 * D_MODEL
COL_Z = COL_QKV + 3 * BRANCH_WIDTH
COL_DQ = COL_Z + BRANCH_WIDTH
COL_DK = COL_DQ + BRANCH_WIDTH
COL_DV = COL_DK + BRANCH_WIDTH
COL_LX = COL_DV + BRANCH_WIDTH
COL_LY = COL_LX + BRANCH_WIDTH
PROJ_COLS = COL_LY + BRANCH_WIDTH


def _params(*sem):
    return pltpu.CompilerParams(dimension_semantics=sem, vmem_limit_bytes=VMEM_LIMIT)


def _dot(a, b, precision=None):
    return jnp.dot(a, b, preferred_element_type=F32, precision=precision)


def _dot_nt(a, b):
    return lax.dot_general(a, b, (((1,), (1,)), ((), ())), preferred_element_type=F32)


_sigmoid = jax.nn.sigmoid


def _silu(x):
    return x * _sigmoid(x)


def _softplus(x):
    return jnp.maximum(x, 0.0) + jnp.log(1.0 + jnp.exp(-jnp.abs(x)))


def _rms(x, gain):
    return x * lax.rsqrt(jnp.mean(x * x, axis=-1, keepdims=True) + EPS) * gain


def _mod_kernel(c_ref, w_ref, b_ref, o_ref):
    c = c_ref[...]
    o_ref[0] = _dot(_silu(c), w_ref[0], precision=lax.Precision.HIGHEST) + b_ref[0]


def _mods(c, c_ctx, w_mod, b_mod):
    depth = w_mod.shape[0]
    rows = 16
    cc = jnp.zeros((rows, D_MODEL), F32).at[:BATCH].set(c).at[BATCH].set(c_ctx)
    tn = 1536
    out = pl.pallas_call(
        _mod_kernel,
        grid=(depth, N_MOD * D_MODEL // tn),
        in_specs=[pl.BlockSpec((rows, D_MODEL), lambda l, j: (0, 0)),
                  pl.BlockSpec((1, D_MODEL, tn), lambda l, j: (l, 0, j)),
                  pl.BlockSpec((1, 1, tn), lambda l, j: (l, 0, j))],
        out_specs=pl.BlockSpec((1, rows, tn), lambda l, j: (l, 0, j)),
        out_shape=jax.ShapeDtypeStruct((depth, rows, N_MOD * D_MODEL), F32),
        compiler_params=_params("parallel", "parallel"),
        name="mods",
    )(cc, w_mod, b_mod.reshape(depth, 1, N_MOD * D_MODEL))
    return out.reshape(depth, rows, N_MOD, D_MODEL)


def _modulated_norm(x, gain, ml_ref, mc_ref, is_ctx, shift_idx):
    shift = jnp.where(is_ctx, mc_ref[0, shift_idx:shift_idx + 1, :], ml_ref[0, shift_idx:shift_idx + 1, :])
    scale = jnp.where(is_ctx, mc_ref[0, shift_idx + 1:shift_idx + 2, :], ml_ref[0, shift_idx + 1:shift_idx + 2, :])
    return _rms(x, gain) * (1.0 + scale) + shift


def _ctx_rows(tile, tm, width):
    row = tile * tm + lax.broadcasted_iota(jnp.int32, (tm, width), 0)
    return row >= SEQ


def _proj_kernel(x_ref, ml_ref, mc_ref, g_ref, w_ref, wba_ref, p_ref, ba_ref, h_scr, *, tm):
    i = pl.program_id(1)
    j = pl.program_id(2)

    @pl.when(j == 0)
    def _():
        is_ctx = _ctx_rows(i, tm, D_MODEL)
        h_scr[...] = _modulated_norm(x_ref[0], g_ref[...], ml_ref, mc_ref, is_ctx, 0).astype(BF16)

    h = h_scr[...]
    p_ref[0] = _dot(h, w_ref[0]).astype(BF16)
    ba_ref[0] = _dot(h, wba_ref[0])


def _project(xs, mods, gain, w_main, w_ba, layer):
    tm, tn = 1152, 1280
    return pl.pallas_call(
        functools.partial(_proj_kernel, tm=tm),
        grid=(BATCH, TOK // tm, PROJ_COLS // tn),
        in_specs=[pl.BlockSpec((1, tm, D_MODEL), lambda b, i, j: (b, i, 0)),
                  pl.BlockSpec((1, N_MOD, D_MODEL), lambda b, i, j: (b, 0, 0)),
                  pl.BlockSpec((1, N_MOD, D_MODEL), lambda b, i, j: (BATCH, 0, 0)),
                  pl.BlockSpec((1, D_MODEL), lambda b, i, j: (0, 0)),
                  pl.BlockSpec((1, D_MODEL, tn), lambda b, i, j: (layer, 0, j)),
                  pl.BlockSpec((1, D_MODEL, LANES), lambda b, i, j: (layer, 0, 0))],
        out_specs=[pl.BlockSpec((1, tm, tn), lambda b, i, j: (b, i, j)),
                   pl.BlockSpec((1, tm, LANES), lambda b, i, j: (b, i, 0))],
        out_shape=[jax.ShapeDtypeStruct((BATCH, TOK, PROJ_COLS), BF16),
                   jax.ShapeDtypeStruct((BATCH, TOK, LANES), F32)],
        scratch_shapes=[pltpu.VMEM((tm, D_MODEL), BF16)],
        compiler_params=_params("parallel", "parallel", "arbitrary"),
        name="proj",
    )(xs, mods, mods, gain.reshape(1, D_MODEL), w_main, w_ba)


def _conv(x, w):
    n, c = x.shape
    t = lax.broadcasted_iota(jnp.int32, (n, c), 0)
    is_ctx = t >= SEQ
    local = jnp.where(is_ctx, t - SEQ, t)
    seg_len = jnp.where(is_ctx, CTX_LEN, SEQ)
    y = jnp.zeros_like(x)
    for j in range(CONV_WIDTH):
        s = j - CONV_WIDTH // 2
        if s == 0:
            y = y + x * w[j:j + 1, :]
        else:
            shifted = pltpu.roll(x, (-s) % n, 0)
            ok = jnp.logical_and(local + s >= 0, local + s < seg_len)
            y = y + jnp.where(ok, shifted, 0.0) * w[j:j + 1, :]
    return y


def _gdn_conv_kernel(x_ref, w_ref, o_ref):
    j = pl.program_id(1)
    for h in range(GDN_HEADS):
        cols = slice(h * GDN_HEAD_DIM, (h + 1) * GDN_HEAD_DIM)
        y = _silu(_conv(x_ref[0, :, cols].astype(F32), w_ref[:, cols]))
        nrm = lax.rsqrt(jnp.sum(y * y, axis=-1, keepdims=True) + EPS)
        scale = jnp.where(j == 0, nrm * GDN_HEAD_DIM ** -0.5, jnp.where(j == 1, nrm, 1.0))
        o_ref[0, :, cols] = (y * scale).astype(BF16)


def _gdn_conv(p, conv_w):
    first = COL_QKV // BRANCH_WIDTH
    return pl.pallas_call(
        _gdn_conv_kernel,
        grid=(BATCH, 3),
        in_specs=[pl.BlockSpec((1, TOK, BRANCH_WIDTH), lambda b, j: (b, 0, first + j)),
                  pl.BlockSpec((CONV_WIDTH, BRANCH_WIDTH), lambda b, j: (0, j))],
        out_specs=pl.BlockSpec((1, TOK, BRANCH_WIDTH), lambda b, j: (b, 0, j)),
        out_shape=jax.ShapeDtypeStruct((BATCH, TOK, 3 * BRANCH_WIDTH), BF16),
        compiler_params=_params("parallel", "parallel"),
        name="gdn_conv",
    )(p, conv_w)


GDN_QM_ROWS = GDN_CHUNK + GDN_HEAD_DIM
GDN_GL_ROWS = 8
GDN_HEADS_PER_STEP = 2
GDN_PREP_UNROLL = 12


def _gdn_prepare(q_ref, k_ref, v_ref, gb_scr, qm_scr, nn_scr, o_scr, gl_scr, chunks, head, local):
    c = GDN_CHUNK
    cols = slice(local * LANES, (local + 1) * LANES)
    lane = lax.broadcasted_iota(jnp.int32, (c, LANES), 1)
    row = lax.broadcasted_iota(jnp.int32, (c, LANES), 0)
    ii = lax.broadcasted_iota(jnp.int32, (c, c), 0)
    jj = lax.broadcasted_iota(jnp.int32, (c, c), 1)
    eye = (ii == jj).astype(F32)
    masks =((ii >= jj, ii > jj, row > lane), (ii <= jj, ii < jj, row < lane))

    loaded = []
    for chunk in chunks:
        r0 = pl.multiple_of(chunk * c, c)
        k = k_ref[0, pl.ds(r0, c), cols]
        q = q_ref[0, pl.ds(r0, c), cols]
        kq = _dot_nt(jnp.concatenate([k, q], axis=0), k)
        loaded.append((chunk, r0, q, k, kq))

    chains = []
    for chunk, r0, q, k, kq in loaded:
        gb = gb_scr[pl.ds(r0, c), :]
        for d in range(2):
            col = head + d * GDN_HEADS
            beta = jnp.sum(jnp.where(lane == col, gb, 0.0), axis=-1, keepdims=True)
            g = jnp.sum(jnp.where(lane == col + 2 * GDN_HEADS, gb, 0.0), axis=-1, keepdims=True)
            incl, strict, strict_wide = masks[d]
            rhs = jnp.where(lane >= c, g, jnp.where(strict_wide, g, 0.0))
            mask = incl.astype(BF16)
            hi = rhs.astype(BF16)
            rest = rhs - hi.astype(F32)
            mid = rest.astype(BF16)
            low = (rest - mid.astype(F32)).astype(BF16)
            e = _dot(mask, hi) + _dot(mask, mid) + _dot(mask, low)
            chains.append(dict(chunk=chunk, r0=r0, d=d, q=q, k=k, kq=kq, beta=beta, e=e))

    for ch in chains:
        incl, strict, _ = masks[ch["d"]]
        e = ch["e"]
        decay = jnp.where(incl, jnp.exp(e[:, :c]), 0.0)
        gc = e[:, c:c + 1]
        last = 0 if ch["d"] == 1 else c - 1
        gc_last = e[last:last + 1, c:c + 1]
        ch.update(decay=decay, gc=gc, gc_last=gc_last, egc=jnp.exp(gc))
        ch["a"] = jnp.where(strict, ch["beta"] * ch["kq"][:c] * decay, 0.0)
        ch["t"] = eye
    s = 1
    while s < c:
        pair = jnp.logical_and((ii // (2 * s)) == (jj // (2 * s)), (ii // s) != (jj // s))
        for ch in chains:
            ch["a_off"] = jnp.where(pair, ch["a"], 0.0)
        if s == 1:
            for ch in chains:
                ch["t"] = eye - ch["a_off"]
        else:
            for ch in chains:
                ch["m"] = _dot(ch["t"].astype(BF16), ch["a_off"].astype(BF16))
            for ch in chains:
                ch["t"] = ch["t"] - _dot(ch["m"].astype(BF16), ch["t"].astype(BF16))
        s *= 2
    for ch in chains:
        r0, beta, egc = ch["r0"], ch["beta"], ch["egc"]
        kf = ch["k"].astype(F32)
        vf = v_ref[0, pl.ds(r0, c), cols].astype(F32)
        rhs2 = jnp.concatenate([vf * beta, kf * (beta * egc)], axis=1).astype(BF16)
        ch["uw"] = _dot(ch["t"].astype(BF16), rhs2).astype(BF16)
        ch["k_dec_t"] = (kf * jnp.exp(ch["gc_last"] - ch["gc"])).T.astype(BF16)
    for ch in chains:
        incl = masks[ch["d"]][0]
        qk = jnp.where(incl, ch["kq"][c:] * ch["decay"], 0.0).astype(BF16)
        ch["nm"] = _dot(ch["k_dec_t"], ch["uw"])
        ch["ow"] = _dot(qk, ch["uw"])
    for ch in chains:
        chunk, r0, nm, ow = ch["chunk"], ch["r0"], ch["nm"], ch["ow"]
        s = 2 * local + ch["d"]
        q0 = pl.multiple_of(chunk * GDN_QM_ROWS, 16)
        qm_scr[s, pl.ds(q0, c), :] = (ch["q"].astype(F32) * ch["egc"] - ow[:, GDN_HEAD_DIM:]).astype(BF16)
        qm_scr[s, pl.ds(q0 + c, GDN_HEAD_DIM), :] = nm[:, GDN_HEAD_DIM:].astype(BF16)
        nn_scr[s, pl.ds(pl.multiple_of(chunk * GDN_HEAD_DIM, GDN_HEAD_DIM), GDN_HEAD_DIM), :] = nm[:, :GDN_HEAD_DIM]
        o_scr[s, pl.ds(r0, c), :] = ow[:, :GDN_HEAD_DIM]
        gl_scr[s, pl.ds(pl.multiple_of(chunk * GDN_GL_ROWS, GDN_GL_ROWS), GDN_GL_ROWS), :] = jnp.broadcast_to(
            jnp.exp(ch["gc_last"]), (GDN_GL_ROWS, LANES))


def _gdn_advance(qm_scr, nn_scr, o_scr, gl_scr, d, chunk, state):
    c = GDN_CHUNK
    qm = qm_scr[d, pl.ds(pl.multiple_of(chunk * GDN_QM_ROWS, 16), GDN_QM_ROWS), :]
    r = _dot(qm, state.astype(BF16))
    rows = pl.ds(pl.multiple_of(chunk * c, c), c)
    o_scr[d, rows, :] = o_scr[d, rows, :] + r[:c]
    gl = gl_scr[d, pl.ds(pl.multiple_of(chunk * GDN_GL_ROWS, GDN_GL_ROWS), 1), :]
    n = nn_scr[d, pl.ds(pl.multiple_of(chunk * GDN_HEAD_DIM, GDN_HEAD_DIM), GDN_HEAD_DIM), :]
    return state * gl - r[c:] + n


def _gdn_kernel(q_ref, k_ref, v_ref, z_ref, ba_ref, alog_ref, dtb_ref, gain_ref, o_ref,
                gb_scr, qm_scr, nn_scr, o_scr, gl_scr):
    first_head = pl.program_id(1) * GDN_HEADS_PER_STEP
    c = GDN_CHUNK
    n_lat, n_ctx = SEQ // c, CTX_LEN // c
    n_chunks = n_lat + n_ctx
    ba = ba_ref[0]
    lane = lax.broadcasted_iota(jnp.int32, ba.shape, 1)
    g_all = -jnp.exp(alog_ref[...]) * _softplus(ba + dtb_ref[...])
    gb_scr[...] = jnp.where(lane < 2 * GDN_HEADS, _sigmoid(ba), g_all)

    for local in range(GDN_HEADS_PER_STEP):
        def prepare(i, _, local=local):
            chunks = [i * GDN_PREP_UNROLL + j for j in range(GDN_PREP_UNROLL)]
            _gdn_prepare(q_ref, k_ref, v_ref, gb_scr, qm_scr, nn_scr, o_scr, gl_scr, chunks, first_head + local, local)
            return 0

        lax.fori_loop(0, n_chunks // GDN_PREP_UNROLL, prepare, 0)

    def advance(i, states):
        cf = jnp.where(i < n_ctx, n_lat + i, i - n_ctx)
        cb = n_chunks - 1 - i
        return tuple(_gdn_advance(qm_scr, nn_scr, o_scr, gl_scr, s, cb if s % 2 else cf, state)
                     for s, state in enumerate(states))

    zero = jnp.zeros((GDN_HEAD_DIM, GDN_HEAD_DIM), F32)
    lax.fori_loop(0, n_chunks, advance, (zero,) * (2 * GDN_HEADS_PER_STEP))
    for local in range(GDN_HEADS_PER_STEP):
        cols = slice(local * LANES, (local + 1) * LANES)
        o = o_scr[2 * local] + o_scr[2 * local + 1]
        o_ref[0, :, cols] = (_rms(o, gain_ref[...]) * _silu(z_ref[0, :, cols].astype(F32))).astype(BF16)


def _gdn(qkv, p, ba, a_log, dt_bias, out_gain):
    def pad_lanes(vals):
        row = jnp.zeros((LANES,), F32).at[2 * GDN_HEADS:4 * GDN_HEADS].set(vals.reshape(-1))
        return row.reshape(1, LANES)

    n_steps = GDN_HEADS // GDN_HEADS_PER_STEP
    n_chunks = TOK // GDN_CHUNK
    width = GDN_HEADS_PER_STEP * LANES
    zblk = COL_Z // width
    n_chain = 2 * GDN_HEADS_PER_STEP
    blk = lambda off: pl.BlockSpec((1, TOK, width), lambda b, h: (b, 0, off + h))
    vec = pl.BlockSpec((1, LANES), lambda b, h: (0, 0))
    return pl.pallas_call(
        _gdn_kernel,
        grid=(BATCH, n_steps),
        in_specs=[blk(0), blk(n_steps), blk(2 * n_steps), blk(zblk),
                  pl.BlockSpec((1, TOK, LANES), lambda b, h: (b, 0, 0)), vec, vec, vec],
        out_specs=pl.BlockSpec((1, TOK, width), lambda b, h: (b, 0, h)),
        out_shape=jax.ShapeDtypeStruct((BATCH, TOK, BRANCH_WIDTH), BF16),
        scratch_shapes=[pltpu.VMEM((TOK, LANES), F32),
                        pltpu.VMEM((n_chain, n_chunks * GDN_QM_ROWS, LANES), BF16),
                        pltpu.VMEM((n_chain, n_chunks * GDN_HEAD_DIM, LANES), F32),
                        pltpu.VMEM((n_chain, TOK, LANES), F32),
                        pltpu.VMEM((n_chain, n_chunks * GDN_GL_ROWS, LANES), F32)],
        compiler_params=_params("parallel", "parallel"),
        name="gdn",
    )(qkv, qkv, qkv, p, ba, pad_lanes(a_log), pad_lanes(dt_bias), out_gain.reshape(1, LANES))


def _rms_halves(x, gain):
    lane = lax.broadcasted_iota(jnp.int32, x.shape, 1)
    lo = lane < DIFF_HEAD_DIM
    x2 = x * x
    s_lo = jnp.sum(jnp.where(lo, x2, 0.0), axis=-1, keepdims=True)
    s_hi = jnp.sum(jnp.where(lo, 0.0, x2), axis=-1, keepdims=True)
    ms = jnp.where(lo, s_lo, s_hi) * (1.0 / DIFF_HEAD_DIM)
    return x * lax.rsqrt(ms + EPS) * gain


def _rope(x, cos, sin):
    lane = lax.broadcasted_iota(jnp.int32, x.shape, 1)
    first = (lane & ROPE_PAIRS) == 0
    partner = jnp.where(first, -pltpu.roll(x, LANES - ROPE_PAIRS, 1), pltpu.roll(x, ROPE_PAIRS, 1))
    return x * cos + partner * sin


ATTN_Q_BLOCK = 512
ATTN_GROUP_ROWS = 128


def _attn_kernel(q_ref, k_ref, v_ref, cosk_ref, sink_ref, cosq_ref, sinq_ref, qg_ref, kg_ref, lv_ref, og_ref,
                 o_ref, kn_scr, *, ctx_block, lam_init):
    qi = pl.program_id(2)

    @pl.when(qi == 0)
    def _():
        kn = _rope(_rms_halves(k_ref[0].astype(F32), kg_ref[...]), cosk_ref[...], sink_ref[...])
        kn_scr[...] = kn.astype(BF16)

    lv = lv_ref[...]
    lam = (jnp.exp(jnp.sum(lv[0:1] * lv[1:2], axis=-1, keepdims=True))
           - jnp.exp(jnp.sum(lv[2:3] * lv[3:4], axis=-1, keepdims=True)) + lam_init)
    def attend(n_rows, kn, v):
        q = _rope(_rms_halves(q_ref[0, :n_rows, :].astype(F32), qg_ref[...]), cosq_ref[:n_rows, :], sinq_ref[:n_rows, :])
        q = q * (DIFF_HEAD_DIM ** -0.5 * math.log2(math.e))
        lane = lax.broadcasted_iota(jnp.int32, q.shape, 1)
        lo = lane < DIFF_HEAD_DIM
        q1 = jnp.where(lo, q, 0.0).astype(BF16)
        q2 = jnp.where(lo, 0.0, q).astype(BF16)

        def half(s):
            p = jnp.exp2(s - jnp.max(s, axis=-1, keepdims=True))
            return _dot(p.astype(BF16), v), jnp.sum(p, axis=-1, keepdims=True)
        rows = ATTN_GROUP_ROWS
        scores = [(_dot_nt(q1[r:r + rows], kn), _dot_nt(q2[r:r + rows], kn)) for r in range(0, n_rows, rows)]
        for g, (s1, s2) in enumerate(scores):
            a1, l1 = half(s1)
            a2, l2 = half(s2)
            o = a1 * (1.0 / l1) - a2 * (lam / l2)
            o_ref[0, g * rows:(g + 1) * rows, :] = (_rms(o, og_ref[...]) * (1.0 - lam_init)).astype(BF16)

    if ctx_block is None:
        attend(q_ref.shape[1], kn_scr[...], v_ref[0])
    else:
        @pl.when(qi == ctx_block)
        def _():
            attend(CTX_LEN, kn_scr[SEQ:, :], v_ref[0, SEQ:, :])

        @pl.when(qi != ctx_block)
        def _():
            attend(q_ref.shape[1], kn_scr[...], v_ref[0])


def _rope_tables():
    n_rows = SEQ // GRID_W
    row_id = jnp.broadcast_to(jnp.arange(n_rows, dtype=F32)[:, None], (n_rows, GRID_W)).reshape(-1)
    col_id = jnp.broadcast_to(jnp.arange(GRID_W, dtype=F32)[None, :], (n_rows, GRID_W)).reshape(-1)
    inv_freq = jnp.power(ROPE_BASE, -jnp.arange(ROPE_PAIRS, dtype=F32) / ROPE_PAIRS)
    row_ang = row_id[:, None] * inv_freq
    col_ang = col_id[:, None] * inv_freq
    ang = jnp.concatenate([row_ang, row_ang, col_ang, col_ang], axis=-1)
    ang = jnp.concatenate([ang, ang], axis=-1)
    pad = ((0, CTX_LEN), (0, 0))
    return jnp.pad(jnp.cos(ang), pad, constant_values=1.0), jnp.pad(jnp.sin(ang), pad)


def _diff_attn(p, cos, sin, q_gain, k_gain, lam_vecs, out_gain, lam_init, with_ctx):
    tq = ATTN_Q_BLOCK
    n_rows = TOK if with_ctx else SEQ
    nq = pl.cdiv(n_rows, tq)
    nh = DIFF_HEADS
    qb, kb, vb = COL_DQ // LANES, COL_DK // LANES, COL_DV // LANES
    full = lambda off: pl.BlockSpec((1, TOK, LANES), lambda b, h, i: (b, 0, off + h))
    tab_full = pl.BlockSpec((TOK, LANES), lambda b, h, i: (0, 0))
    tab_q = pl.BlockSpec((tq, LANES), lambda b, h, i: (i, 0))
    vec = pl.BlockSpec((1, LANES), lambda b, h, i: (0, 0))
    tile2 = lambda g: jnp.concatenate([g, g]).reshape(1, LANES)
    return pl.pallas_call(
        functools.partial(_attn_kernel, ctx_block=SEQ // tq if with_ctx else None, lam_init=lam_init),
        grid=(BATCH, nh, nq),
        in_specs=[pl.BlockSpec((1, tq, LANES), lambda b, h, i: (b, i, qb + h)), full(kb), full(vb),
                  tab_full, tab_full, tab_q, tab_q, vec, vec,
                  pl.BlockSpec((4, DIFF_HEAD_DIM), lambda b, h, i: (0, 0)), vec],
        out_specs=pl.BlockSpec((1, tq, LANES), lambda b, h, i: (b, i, h)),
        out_shape=jax.ShapeDtypeStruct((BATCH, n_rows, BRANCH_WIDTH), BF16),
        scratch_shapes=[pltpu.VMEM((TOK, LANES), BF16)],
        compiler_params=_params("parallel", "parallel", "arbitrary"),
        name="diff_attn",
    )(p, p, p, cos, sin, cos, sin, tile2(q_gain), tile2(k_gain), lam_vecs, out_gain.reshape(1, LANES))


LRU_SLAB = 256
LRU_ROWS = 256
LRU_SCAN_BLOCK = 8


def _lru_kernel(x_ref, y_ref, cw_ref, cb_ref, wg_ref, bg_ref, lam_ref, o_ref, xc_scr, af_scr, bf_scr, ab_scr, bb_scr):
    w = LRU_SLAB
    blk = LRU_SCAN_BLOCK
    for c0 in range(0, w, LANES):
        cols = slice(c0, c0 + LANES)
        xc_scr[:, cols] = _conv(x_ref[0, :, cols].astype(F32), cw_ref[:, cols]) + cb_ref[:, cols]
    sp = _softplus(-lam_ref[0])
    sub = lax.broadcasted_iota(jnp.int32, (LRU_ROWS, w), 0) % blk

    def gates(i, _):
        r0 = pl.multiple_of(i * LRU_ROWS, LRU_ROWS)
        xc = xc_scr[pl.ds(r0, LRU_ROWS), :]
        pre = _dot(xc.astype(BF16), wg_ref[0]) + bg_ref[0]
        for d, (a_scr, b_scr) in enumerate(((af_scr, bf_scr), (ab_scr, bb_scr))):
            r = _sigmoid(pre[:, (2 * d) * w:(2 * d + 1) * w])
            gi = _sigmoid(pre[:, (2 * d + 1) * w:(2 * d + 2) * w])
            log_a = -LRU_C * r * sp[d:d + 1]
            a = jnp.exp(log_a)
            b = jnp.sqrt(1.0 - a * a) * gi * xc
            shift = 1
            while shift < blk:
                if d == 0:
                    ok, roll_by = sub >= shift, shift
                else:
                    ok, roll_by = sub < blk - shift, LRU_ROWS - shift
                b = jnp.where(ok, a * pltpu.roll(b, roll_by, 0) + b, b)
                a = jnp.where(ok, a * pltpu.roll(a, roll_by, 0), a)
                shift *= 2
            a_scr[pl.ds(r0, LRU_ROWS), :] = a
            b_scr[pl.ds(r0, LRU_ROWS), :] = b
        return 0

    lax.fori_loop(0, TOK // LRU_ROWS, gates, 0)

    n_blk, n_lat_blk, n_ctx_blk = TOK // blk, SEQ // blk, CTX_LEN // blk

    def step(s, carry):
        h_f, h_b = carry
        rows_f = pl.ds(pl.multiple_of(jnp.where(s < n_ctx_blk, n_lat_blk + s, s - n_ctx_blk) * blk, blk), blk)
        rows_b = pl.ds(pl.multiple_of((n_blk - 1 - s) * blk, blk), blk)
        hf = af_scr[rows_f, :] * h_f + bf_scr[rows_f, :]
        bf_scr[rows_f, :] = hf
        hb = ab_scr[rows_b, :] * h_b + bb_scr[rows_b, :]
        bb_scr[rows_b, :] = hb
        return hf[blk - 1:blk, :], hb[0:1, :]

    zero = jnp.zeros((1, w), F32)
    lax.fori_loop(0, n_blk, step, (zero, zero), unroll=4)
    h = bf_scr[...] + bb_scr[...]
    o_ref[0] = (h * jax.nn.gelu(y_ref[0].astype(F32))).astype(BF16)


def _lru_gate_weights(w_gate, b_gate):
    n_slab = BRANCH_WIDTH // LRU_SLAB
    per = LRU_SLAB // LRU_BLOCK_DIM
    wg = w_gate.reshape(2, 2, n_slab, per, LRU_BLOCK_DIM, LRU_BLOCK_DIM)
    eye = jnp.eye(per, dtype=w_gate.dtype)
    dense = jnp.einsum('dgsnjk,nm->snjdgmk', wg, eye)
    dense = dense.reshape(n_slab, LRU_SLAB, 4 * LRU_SLAB)
    bg = b_gate.reshape(2, 2, n_slab, LRU_SLAB).transpose(2, 0, 1, 3).reshape(n_slab, 1, 4 * LRU_SLAB)
    return dense.astype(BF16), bg


def _lru(p, conv_w, conv_b, w_gate, b_gate, lam):
    n_slab = BRANCH_WIDTH // LRU_SLAB
    wg, bg = _lru_gate_weights(w_gate, b_gate)
    lam_s = lam.reshape(2, n_slab, LRU_SLAB).transpose(1, 0, 2)
    xb, yb = COL_LX // LRU_SLAB, COL_LY // LRU_SLAB
    return pl.pallas_call(
        _lru_kernel,
        grid=(BATCH, n_slab),
        in_specs=[pl.BlockSpec((1, TOK, LRU_SLAB), lambda b, s: (b, 0, xb + s)),
                  pl.BlockSpec((1, TOK, LRU_SLAB), lambda b, s: (b, 0, yb + s)),
                  pl.BlockSpec((CONV_WIDTH, LRU_SLAB), lambda b, s: (0, s)),
                  pl.BlockSpec((1, LRU_SLAB), lambda b, s: (0, s)),
                  pl.BlockSpec((1, LRU_SLAB, 4 * LRU_SLAB), lambda b, s: (s, 0, 0)),
                  pl.BlockSpec((1, 1, 4 * LRU_SLAB), lambda b, s: (s, 0, 0)),
                  pl.BlockSpec((1, 2, LRU_SLAB), lambda b, s: (s, 0, 0))],
        out_specs=pl.BlockSpec((1, TOK, LRU_SLAB), lambda b, s: (b, 0, s)),
        out_shape=jax.ShapeDtypeStruct((BATCH, TOK, BRANCH_WIDTH), BF16),
        scratch_shapes=[pltpu.VMEM((TOK, LRU_SLAB), F32)] * 5,
        compiler_params=_params("parallel", "parallel"),
        name="lru",
    )(p, p, conv_w, conv_b.reshape(1, BRANCH_WIDTH), wg, bg, lam_s)


def _route(logits):
    lane = lax.broadcasted_iota(jnp.int32, logits.shape, 1)
    lane_f = lane.astype(F32)
    far = float(LANES)
    lg = jnp.where(lane < N_EXPERTS, logits, -jnp.inf)
    ex = jnp.exp(lg - jnp.max(lg, axis=-1, keepdims=True))
    probs = ex / jnp.sum(ex, axis=-1, keepdims=True)
    per_group = N_EXPERTS // N_GROUPS
    grp = lane // per_group

    def top2(vals):
        m1 = jnp.max(vals, axis=-1, keepdims=True)
        i1 = jnp.min(jnp.where(vals == m1, lane_f, far), axis=-1, keepdims=True)
        rest = jnp.where(lane_f == i1, -2.0, vals)
        m2 = jnp.max(rest, axis=-1, keepdims=True)
        i2 = jnp.min(jnp.where(rest == m2, lane_f, far), axis=-1, keepdims=True)
        return m1, i1, m2, i2

    best = jnp.zeros(logits.shape[:1] + (1,), jnp.int32)
    best_score = None
    for g in range(N_GROUPS):
        m1, _, m2, _ = top2(jnp.where(grp == g, probs, -1.0))
        score = m1 + m2
        if best_score is None:
            best_score = score
        else:
            better = score > best_score
            best = jnp.where(better, g, best)
            best_score = jnp.where(better, score, best_score)
    m1, i1, m2, i2 = top2(jnp.where(grp == best, probs, -1.0))
    den = m1 + m2
    return i1, i2, m1 / den, m2 / den


ROUTE_E, ROUTE_W, ROUTE_RANK = 0, 2, 4


def _pack_bf16_pairs(x):
    n = x.shape[1] // 2
    xb = x.astype(BF16).astype(F32)
    lo = pltpu.bitcast(xb[:, :n], jnp.uint32)
    hi = pltpu.bitcast(xb[:, n:], jnp.uint32)
    return (lo >> 16) | (hi & jnp.uint32(0xFFFF0000))


def _unpack_bf16_pairs(p):
    lo = pltpu.bitcast(p << 16, F32)
    hi = pltpu.bitcast(p & jnp.uint32(0xFFFF0000), F32)
    return jnp.concatenate([lo, hi], axis=1)


def _merge_kernel(ya_ref, yb_ref, yc_ref, gates_ref, x_ref, ml_ref, mc_ref, g2_ref, wbr_ref, wout_ref, wr_ref,
                  br_ref, xo_ref, h2_ref, route_ref, cnt_ref, cnt_scr, *, tm):
    i = pl.program_id(1)

    @pl.when(jnp.logical_and(pl.program_id(0) == 0, i == 0))
    def _():
        cnt_scr[...] = jnp.zeros_like(cnt_scr)

    acc = None
    for n, y_ref in enumerate((ya_ref, yb_ref, yc_ref)):
        yn = _dot(y_ref[0], wbr_ref[n])
        gate = _sigmoid(gates_ref[0, :, n * D_MODEL:(n + 1) * D_MODEL].astype(F32))
        acc = gate * yn if acc is None else acc + gate * yn
    out = _dot(acc.astype(BF16), wout_ref[...])
    is_ctx = _ctx_rows(i, tm, D_MODEL)
    xn = x_ref[0] + jnp.where(is_ctx, mc_ref[0, 2:3, :], ml_ref[0, 2:3, :]) * out
    xo_ref[0] = xn
    h2 = _modulated_norm(xn, g2_ref[...], ml_ref, mc_ref, is_ctx, 3)
    h2_ref[0] = _pack_bf16_pairs(h2)
    i1, i2, w1, w2 = _route(_dot(h2.astype(BF16), wr_ref[...]) + br_ref[...])
    lane = lax.broadcasted_iota(jnp.int32, (tm, LANES), 1)
    lane_f = lane.astype(F32)
    chosen = jnp.where(jnp.logical_or(lane_f == i1, lane_f == i2), 1.0, 0.0)
    earlier = (lax.broadcasted_iota(jnp.int32, (tm, tm), 0) > lax.broadcasted_iota(jnp.int32, (tm, tm), 1))
    before = _dot(jnp.where(earlier, 1.0, 0.0).astype(BF16), chosen.astype(BF16)) + cnt_scr[...]
    rank1 = jnp.sum(jnp.where(lane_f == i1, before, 0.0), axis=-1, keepdims=True)
    rank2 = jnp.sum(jnp.where(lane_f == i2, before, 0.0), axis=-1, keepdims=True)
    cnt_scr[...] += jnp.sum(chosen, axis=0, keepdims=True)
    cnt_ref[...] = cnt_scr[...]
    record = jnp.zeros((tm, LANES), F32)
    for pos, val in enumerate((i1, i2, w1, w2, rank1, rank2)):
        record = jnp.where(lane == pos, val, record)
    route_ref[0] = record


def _merge(ya, yb, yc, p, xs, mods, gain2, w_branch, w_out, w_router, b_router, rows, tm):
    wr = jnp.zeros((D_MODEL, LANES), BF16).at[:, :N_EXPERTS].set(w_router.astype(BF16))
    br = jnp.zeros((1, LANES), F32).at[0, :N_EXPERTS].set(b_router)
    tile = lambda w: pl.BlockSpec((1, tm, w), lambda b, i: (b, i, 0))
    const = lambda shape: pl.BlockSpec(shape, lambda b, i: (0,) * len(shape))
    return pl.pallas_call(
        functools.partial(_merge_kernel, tm=tm),
        grid=(BATCH, rows // tm),
        in_specs=[tile(BRANCH_WIDTH), tile(BRANCH_WIDTH), tile(BRANCH_WIDTH), tile(3 * D_MODEL), tile(D_MODEL),
                  pl.BlockSpec((1, N_MOD, D_MODEL), lambda b, i: (b, 0, 0)),
                  pl.BlockSpec((1, N_MOD, D_MODEL), lambda b, i: (BATCH, 0, 0)),
                  const((1, D_MODEL)), const((3, BRANCH_WIDTH, D_MODEL)), const((D_MODEL, D_MODEL)),
                  const((D_MODEL, LANES)), const((1, LANES))],
        out_specs=[tile(D_MODEL), tile(D_MODEL // 2), tile(LANES), const((1, LANES))],
        out_shape=[jax.ShapeDtypeStruct((BATCH, rows, D_MODEL), F32),
                   jax.ShapeDtypeStruct((BATCH, rows, D_MODEL // 2), jnp.uint32),
                   jax.ShapeDtypeStruct((BATCH, rows, LANES), F32),
                   jax.ShapeDtypeStruct((1, LANES), F32)],
        scratch_shapes=[pltpu.VMEM((1, LANES), F32)],
        compiler_params=_params("arbitrary", "arbitrary"),
        name="merge",
    )(ya, yb, yc, p, xs, mods, mods, gain2.reshape(1, D_MODEL), w_branch, w_out, wr, br)


MOE_TILE = 512
SC_GATHER_ROWS = 64


def _sc_gather(table, idx):
    info = plsc.get_sparse_core_info()
    n_workers = info.num_cores * info.num_subcores
    n_rows, width = idx.shape[0], table.shape[1]
    per_worker = n_rows // n_workers
    assert per_worker * n_workers == n_rows and per_worker % SC_GATHER_ROWS == 0
    mesh = plsc.VectorSubcoreMesh(core_axis_name="c", subcore_axis_name="s")

    @functools.partial(
        pl.kernel, mesh=mesh, out_type=jax.ShapeDtypeStruct((n_rows, width), table.dtype),
        scratch_types=[pltpu.VMEM((SC_GATHER_ROWS,), jnp.int32),
                       pltpu.VMEM((SC_GATHER_ROWS, width), table.dtype),
                       pltpu.SemaphoreType.DMA],
        name="sc_gather")
    def gather(table_hbm, idx_hbm, out_hbm, idx_v, rows_v, sem):
        worker = lax.axis_index("s") * info.num_cores + lax.axis_index("c")
        base = worker * per_worker

        @pl.loop(0, per_worker // SC_GATHER_ROWS)
        def _(j):
            off = base + j * SC_GATHER_ROWS
            pltpu.sync_copy(idx_hbm.at[pl.ds(off, SC_GATHER_ROWS)], idx_v)
            pltpu.async_copy(table_hbm.at[idx_v], rows_v, sem).wait()
            pltpu.sync_copy(rows_v, out_hbm.at[pl.ds(off, SC_GATHER_ROWS)])

    return gather(table, idx)


SC_SCATTER_ROWS = 128


def _sc_scatter(table, dest, n_out):
    info = plsc.get_sparse_core_info()
    n_workers = info.num_cores * info.num_subcores
    n_tok, width = table.shape
    n_assign = dest.shape[0]
    per_worker = n_assign // n_workers
    assert per_worker * n_workers == n_assign and per_worker % SC_SCATTER_ROWS == 0 and n_tok % SC_SCATTER_ROWS == 0
    mesh = plsc.VectorSubcoreMesh(core_axis_name="c", subcore_axis_name="s")

    @functools.partial(
        pl.kernel, mesh=mesh, out_type=jax.ShapeDtypeStruct((n_out, width), table.dtype),
        scratch_types=[pltpu.VMEM((SC_SCATTER_ROWS,), jnp.int32),
                       pltpu.VMEM((SC_SCATTER_ROWS, width), table.dtype),
                       pltpu.SemaphoreType.DMA],
        name="sc_scatter")
    def scatter(table_hbm, dest_hbm, out_hbm, idx_v, rows_v, sem):
        worker = lax.axis_index("s") * info.num_cores + lax.axis_index("c")
        base = worker * per_worker

        @pl.loop(0, per_worker // SC_SCATTER_ROWS)
        def _(j):
            off = base + j * SC_SCATTER_ROWS
            pltpu.sync_copy(dest_hbm.at[pl.ds(off, SC_SCATTER_ROWS)], idx_v)
            pltpu.sync_copy(table_hbm.at[pl.ds(lax.rem(off, n_tok), SC_SCATTER_ROWS)], rows_v)
            pltpu.async_copy(rows_v, out_hbm.at[idx_v], sem).wait()

    return scatter(table, dest)


def _experts_kernel(tile_expert_ref, n_valid_ref, x_ref, wgu_ref, wd_ref, o_ref, wgu_scr, wd_scr):
    j = pl.program_id(0)
    valid = j < n_valid_ref[0]
    fresh = jnp.logical_or(j == 0, tile_expert_ref[j] != tile_expert_ref[jnp.maximum(j - 1, 0)])

    @pl.when(jnp.logical_and(valid, fresh))
    def _():
        wgu_scr[...] = wgu_ref[0, 0].astype(BF16)
        wd_scr[...] = wd_ref[0, 0].astype(BF16)

    @pl.when(valid)
    def _():
        x = _unpack_bf16_pairs(x_ref[...]).astype(BF16)
        gu = _dot(x, wgu_scr[...])
        act = (_silu(gu[:, :EXPERT_FF]) * gu[:, EXPERT_FF:]).astype(BF16)
        o_ref[...] = _pack_bf16_pairs(_dot(act, wd_scr[...]))

    @pl.when(jnp.logical_not(valid))
    def _():
        o_ref[...] = jnp.zeros_like(o_ref)


def _experts(x_sorted, tile_expert, n_valid, w_gate_up, w_down, layer):
    n_tiles = x_sorted.shape[0] // MOE_TILE
    half = D_MODEL // 2
    return pl.pallas_call(
        _experts_kernel,
        grid_spec=pltpu.PrefetchScalarGridSpec(
            num_scalar_prefetch=2, grid=(n_tiles,),
            in_specs=[pl.BlockSpec((MOE_TILE, half), lambda j, te, nv: (j, 0)),
                      pl.BlockSpec((1, 1, D_MODEL, 2 * EXPERT_FF), lambda j, te, nv: (layer, te[j], 0, 0)),
                      pl.BlockSpec((1, 1, EXPERT_FF, D_MODEL), lambda j, te, nv: (layer, te[j], 0, 0))],
            out_specs=pl.BlockSpec((MOE_TILE, half), lambda j, te, nv: (j, 0)),
            scratch_shapes=[pltpu.VMEM((D_MODEL, 2 * EXPERT_FF), BF16), pltpu.VMEM((EXPERT_FF, D_MODEL), BF16)]),
        out_shape=jax.ShapeDtypeStruct((x_sorted.shape[0], half), jnp.uint32),
        compiler_params=_params("arbitrary"),
        name="experts",
    )(tile_expert, n_valid, x_sorted, w_gate_up, w_down)


def _combine_kernel(y1_ref, y2_ref, route_ref, x_ref, ml_ref, mc_ref, o_ref, *, tm):
    i = pl.program_id(1)
    route = route_ref[0]
    w1 = route[:, ROUTE_W:ROUTE_W + 1]
    w2 = route[:, ROUTE_W + 1:ROUTE_W + 2]
    moe = w1 * _unpack_bf16_pairs(y1_ref[0, 0]) + w2 * _unpack_bf16_pairs(y2_ref[0, 0])
    is_ctx = _ctx_rows(i, tm, D_MODEL)
    o_ref[0] = x_ref[0] + jnp.where(is_ctx, mc_ref[0, 5:6, :], ml_ref[0, 5:6, :]) * moe


def _combine(y_pairs, route, xs, mods, rows, tm):
    half = D_MODEL // 2
    tile = lambda w: pl.BlockSpec((1, tm, w), lambda b, i: (b, i, 0))
    slot = lambda s: pl.BlockSpec((1, 1, tm, half), lambda b, i: (s, b, i, 0))
    return pl.pallas_call(
        functools.partial(_combine_kernel, tm=tm),
        grid=(BATCH, rows // tm),
        in_specs=[slot(0), slot(1), tile(LANES), tile(D_MODEL),
                  pl.BlockSpec((1, N_MOD, D_MODEL), lambda b, i: (b, 0, 0)),
                  pl.BlockSpec((1, N_MOD, D_MODEL), lambda b, i: (BATCH, 0, 0))],
        out_specs=tile(D_MODEL),
        out_shape=jax.ShapeDtypeStruct((BATCH, rows, D_MODEL), F32),
        compiler_params=_params("parallel", "parallel"),
        name="combine",
    )(y_pairs, y_pairs, route, xs, mods, mods)


def _moe(h2, route, counts, xs, mods, w_gate_up, w_down, layer, rows, tm):
    n_tok = BATCH * rows
    half = D_MODEL // 2
    n_sorted = 2 * n_tok + N_EXPERTS * MOE_TILE
    n_tiles = n_sorted // MOE_TILE
    rec = route.reshape(n_tok, LANES)
    expert = rec[:, ROUTE_E:ROUTE_E + 2].astype(jnp.int32)
    rank = rec[:, ROUTE_RANK:ROUTE_RANK + 2].astype(jnp.int32)
    count = counts[0, :N_EXPERTS].astype(jnp.int32)
    padded = (count + MOE_TILE - 1) // MOE_TILE * MOE_TILE
    end = jnp.cumsum(padded)
    start = end - padded
    first = jnp.sum(jnp.where(expert[:, :1] == jnp.arange(N_EXPERTS), start, 0), axis=1)
    second = jnp.sum(jnp.where(expert[:, 1:] == jnp.arange(N_EXPERTS), start, 0), axis=1)
    dest = jnp.concatenate([first + rank[:, 0], second + rank[:, 1]])
    tile_start = jnp.arange(n_tiles, dtype=jnp.int32) * MOE_TILE
    tile_expert = jnp.minimum(jnp.sum(tile_start[:, None] >= end[None, :], axis=1), N_EXPERTS - 1).astype(jnp.int32)
    n_valid = (end[-1:] // MOE_TILE).astype(jnp.int32)
    x_sorted = _sc_scatter(h2.reshape(n_tok, half), dest, n_sorted)
    y_sorted = _experts(x_sorted, tile_expert, n_valid, w_gate_up, w_down, layer)
    y_pairs = _sc_gather(y_sorted, dest).reshape(2, BATCH, rows, half)
    return _combine(y_pairs, route, xs, mods, rows, tm)


def _split_w_in(w_in):
    bw = BRANCH_WIDTH
    sizes = (3 * bw, bw, 2 * GDN_HEADS, 2 * GDN_HEADS, bw, bw, bw, bw, bw, 3 * D_MODEL)
    offs = [0]
    for s in sizes:
        offs.append(offs[-1] + s)
    part = lambda i: w_in[:, :, offs[i]:offs[i + 1]]
    main = jnp.concatenate([part(9), part(0), part(1), part(4), part(5), part(6), part(7), part(8)], axis=2)
    ba = jnp.zeros(w_in.shape[:2] + (LANES,), F32).at[:, :, :4 * GDN_HEADS].set(
        jnp.concatenate([part(2), part(3)], axis=2))
    return main.astype(BF16), ba.astype(BF16)


def kernel(x, c, ctx, c_ctx, w_mod, b_mod, norm1_gain, norm2_gain, w_in, gdn_conv_w, gdn_a_log, gdn_dt_bias, gdn_out_gain, diff_q_gain, diff_k_gain, diff_lambda, diff_out_gain, lru_conv_w, lru_conv_b, lru_w_gate, lru_b_gate, lru_lambda, w_branch, w_out, w_router, b_router, w_gate_up, w_down):
    mods = _mods(c, c_ctx, w_mod, b_mod)
    cos, sin = _rope_tables()
    xs = jnp.concatenate([x, ctx], axis=1)
    w_main, w_ba = _split_w_in(w_in)
    for layer in range(DEPTH):
        last = layer == DEPTH - 1
        lam_init = 0.8 - 0.6 * math.exp(-0.3 * layer)
        m = mods[layer]
        p, ba = _project(xs, m, norm1_gain[layer], w_main, w_ba, layer)
        qkv = _gdn_conv(p, gdn_conv_w[layer])
        ya = _gdn(qkv, p, ba, gdn_a_log[layer], gdn_dt_bias[layer], gdn_out_gain[layer])
        yb = _diff_attn(p, cos, sin, diff_q_gain[layer], diff_k_gain[layer], diff_lambda[layer],
                        diff_out_gain[layer], lam_init, with_ctx=not last)
        yc = _lru(p, lru_conv_w[layer], lru_conv_b[layer], lru_w_gate[layer], lru_b_gate[layer], lru_lambda[layer])
        rows, tm = (SEQ, 512) if last else (TOK, 768)
        xs, h2, route, counts = _merge(ya, yb, yc, p, xs, m, norm2_gain[layer], w_branch[layer].astype(BF16),
                                       w_out[layer].astype(BF16), w_router, b_router, rows, tm)
        xs = _moe(h2, route, counts, xs, m, w_gate_up, w_down, layer, rows, tm)
    return xs
```

```python
import functools
import math

import jax
import jax.numpy as jnp
from jax import lax
from jax.experimental import pallas as pl
from jax.experimental.pallas import tpu as pltpu
from jax.experimental.pallas import tpu_sc as plsc

F32 = jnp.float32
BF16 = jnp.bfloat16

D_MODEL = 1024
BATCH = 8
SEQ = 2048
DEPTH = 2
GRID_W = 64
CTX_LEN = 256
TOK = SEQ + CTX_LEN
N_MOD = 6
EPS = 1e-6
CONV_WIDTH = 4
BRANCH_WIDTH = 512
GDN_HEADS = 4
GDN_HEAD_DIM = 128
GDN_CHUNK = 64
DIFF_HEADS = 4
DIFF_HEAD_DIM = 64
ROPE_BASE = 10000.0
ROPE_PAIRS = DIFF_HEAD_DIM // 4
LRU_BLOCKS = 8
LRU_BLOCK_DIM = BRANCH_WIDTH // LRU_BLOCKS
LRU_C = 8.0
N_EXPERTS = 16
N_GROUPS = 4
EXPERT_FF = 512

LANES = 128
VMEM_LIMIT = 56 * 1024 * 1024

COL_GATES = 0
COL_QKV = 3 * D_MODEL
COL_Z = COL_QKV + 3 * BRANCH_WIDTH
COL_DQ = COL_Z + BRANCH_WIDTH
COL_DK = COL_DQ + BRANCH_WIDTH
COL_DV = COL_DK + BRANCH_WIDTH
COL_LX = COL_DV + BRANCH_WIDTH
COL_LY = COL_LX + BRANCH_WIDTH
PROJ_COLS = COL_LY + BRANCH_WIDTH


def _params(*sem):
    return pltpu.CompilerParams(dimension_semantics=sem, vmem_limit_bytes=VMEM_LIMIT)


def _dot(a, b, precision=None):
    return jnp.dot(a, b, preferred_element_type=F32, precision=precision)


def _dot_nt(a, b):
    return lax.dot_general(a, b, (((1,), (1,)), ((), ())), preferred_element_type=F32)


_sigmoid = jax.nn.sigmoid


def _silu(x):
    return x * _sigmoid(x)


def _softplus(x):
    return jnp.maximum(x, 0.0) + jnp.log(1.0 + jnp.exp(-jnp.abs(x)))


def _rms(x, gain):
    return x * lax.rsqrt(jnp.mean(x * x, axis=-1, keepdims=True) + EPS) * gain


def _mod_kernel(c_ref, w_ref, b_ref, o_ref):
    c = c_ref[...]
    o_ref[0] = _dot(_silu(c), w_ref[0], precision=lax.Precision.HIGHEST) + b_ref[0]


def _mods(c, c_ctx, w_mod, b_mod):
    depth = w_mod.shape[0]
    rows = 16
    cc = jnp.zeros((rows, D_MODEL), F32).at[:BATCH].set(c).at[BATCH].set(c_ctx)
    tn = 1536
    out = pl.pallas_call(
        _mod_kernel,
        grid=(depth, N_MOD * D_MODEL // tn),
        in_specs=[pl.BlockSpec((rows, D_MODEL), lambda l, j: (0, 0)),
                  pl.BlockSpec((1, D_MODEL, tn), lambda l, j: (l, 0, j)),
                  pl.BlockSpec((1, 1, tn), lambda l, j: (l, 0, j))],
        out_specs=pl.BlockSpec((1, rows, tn), lambda l, j: (l, 0, j)),
        out_shape=jax.ShapeDtypeStruct((depth, rows, N_MOD * D_MODEL), F32),
        compiler_params=_params("parallel", "parallel"),
        name="mods",
    )(cc, w_mod, b_mod.reshape(depth, 1, N_MOD * D_MODEL))
    return out.reshape(depth, rows, N_MOD, D_MODEL)


def _modulated_norm(x, gain, ml_ref, mc_ref, is_ctx, shift_idx):
    shift = jnp.where(is_ctx, mc_ref[0, shift_idx:shift_idx + 1, :], ml_ref[0, shift_idx:shift_idx + 1, :])
    scale = jnp.where(is_ctx, mc_ref[0, shift_idx + 1:shift_idx + 2, :], ml_ref[0, shift_idx + 1:shift_idx + 2, :])
    return _rms(x, gain) * (1.0 + scale) + shift


def _ctx_rows(tile, tm, width):
    row = tile * tm + lax.broadcasted_iota(jnp.int32, (tm, width), 0)
    return row >= SEQ


def _proj_kernel(x_ref, ml_ref, mc_ref, g_ref, w_ref, wba_ref, alog_ref, dtb_ref, p_ref, gb_ref, h_scr, *, tm):
    i = pl.program_id(1)
    j = pl.program_id(2)

    @pl.when(j == 0)
    def _():
        is_ctx = _ctx_rows(i, tm, D_MODEL)
        h = _modulated_norm(x_ref[0], g_ref[...], ml_ref, mc_ref, is_ctx, 0).astype(BF16)
        h_scr[...] = h
        ba = _dot(h, wba_ref[0])
        lane = lax.broadcasted_iota(jnp.int32, ba.shape, 1)
        log_decay = -jnp.exp(alog_ref[...]) * _softplus(ba + dtb_ref[...])
        gb_ref[0] = jnp.where(lane < 2 * GDN_HEADS, _sigmoid(ba), log_decay)

    p_ref[0] = _dot(h_scr[...], w_ref[0]).astype(BF16)


def _project(xs, mods, gain, w_main, w_ba, a_log, dt_bias, layer):
    def pad_lanes(vals):
        row = jnp.zeros((LANES,), F32).at[2 * GDN_HEADS:4 * GDN_HEADS].set(vals.reshape(-1))
        return row.reshape(1, LANES)

    vec = pl.BlockSpec((1, LANES), lambda b, i, j: (0, 0))
    tm, tn = 1152, 1280
    return pl.pallas_call(
        functools.partial(_proj_kernel, tm=tm),
        grid=(BATCH, TOK // tm, PROJ_COLS // tn),
        in_specs=[pl.BlockSpec((1, tm, D_MODEL), lambda b, i, j: (b, i, 0)),
                  pl.BlockSpec((1, N_MOD, D_MODEL), lambda b, i, j: (b, 0, 0)),
                  pl.BlockSpec((1, N_MOD, D_MODEL), lambda b, i, j: (BATCH, 0, 0)),
                  pl.BlockSpec((1, D_MODEL), lambda b, i, j: (0, 0)),
                  pl.BlockSpec((1, D_MODEL, tn), lambda b, i, j: (layer, 0, j)),
                  pl.BlockSpec((1, D_MODEL, LANES), lambda b, i, j: (layer, 0, 0)), vec, vec],
        out_specs=[pl.BlockSpec((1, tm, tn), lambda b, i, j: (b, i, j)),
                   pl.BlockSpec((1, tm, LANES), lambda b, i, j: (b, i, 0))],
        out_shape=[jax.ShapeDtypeStruct((BATCH, TOK, PROJ_COLS), BF16),
                   jax.ShapeDtypeStruct((BATCH, TOK, LANES), F32)],
        scratch_shapes=[pltpu.VMEM((tm, D_MODEL), BF16)],
        compiler_params=_params("parallel", "parallel", "arbitrary"),
        name="proj",
    )(xs, mods, mods, gain.reshape(1, D_MODEL), w_main, w_ba, pad_lanes(a_log), pad_lanes(dt_bias))


def _conv(x, w):
    n, c = x.shape
    t = lax.broadcasted_iota(jnp.int32, (n, c), 0)
    is_ctx = t >= SEQ
    local = jnp.where(is_ctx, t - SEQ, t)
    seg_len = jnp.where(is_ctx, CTX_LEN, SEQ)
    y = jnp.zeros_like(x)
    for j in range(CONV_WIDTH):
        s = j - CONV_WIDTH // 2
        if s == 0:
            y = y + x * w[j:j + 1, :]
        else:
            shifted = pltpu.roll(x, (-s) % n, 0)
            ok = jnp.logical_and(local + s >= 0, local + s < seg_len)
            y = y + jnp.where(ok, shifted, 0.0) * w[j:j + 1, :]
    return y


def _gdn_conv_kernel(x_ref, w_ref, o_ref):
    j = pl.program_id(1)
    for h in range(GDN_HEADS):
        cols = slice(h * GDN_HEAD_DIM, (h + 1) * GDN_HEAD_DIM)
        y = _silu(_conv(x_ref[0, :, cols].astype(F32), w_ref[:, cols]))
        nrm = lax.rsqrt(jnp.sum(y * y, axis=-1, keepdims=True) + EPS)
        scale = jnp.where(j == 0, nrm * GDN_HEAD_DIM ** -0.5, jnp.where(j == 1, nrm, 1.0))
        o_ref[0, :, cols] = (y * scale).astype(BF16)


def _gdn_conv(p, conv_w):
    first = COL_QKV // BRANCH_WIDTH
    return pl.pallas_call(
        _gdn_conv_kernel,
        grid=(BATCH, 3),
        in_specs=[pl.BlockSpec((1, TOK, BRANCH_WIDTH), lambda b, j: (b, 0, first + j)),
                  pl.BlockSpec((CONV_WIDTH, BRANCH_WIDTH), lambda b, j: (0, j))],
        out_specs=pl.BlockSpec((1, TOK, BRANCH_WIDTH), lambda b, j: (b, 0, j)),
        out_shape=jax.ShapeDtypeStruct((BATCH, TOK, 3 * BRANCH_WIDTH), BF16),
        compiler_params=_params("parallel", "parallel"),
        name="gdn_conv",
    )(p, conv_w)


GDN_QM_ROWS = GDN_CHUNK + GDN_HEAD_DIM
GDN_GL_ROWS = 8
GDN_HEADS_PER_STEP = 2
GDN_PREP_UNROLL = 12


def _gdn_prepare(q_ref, k_ref, v_ref, gb_ref, qm_scr, nn_scr, o_scr, gl_scr, chunks, head, local):
    c = GDN_CHUNK
    cols = slice(local * LANES, (local + 1) * LANES)
    lane = lax.broadcasted_iota(jnp.int32, (c, LANES), 1)
    row = lax.broadcasted_iota(jnp.int32, (c, LANES), 0)
    ii = lax.broadcasted_iota(jnp.int32, (c, c), 0)
    jj = lax.broadcasted_iota(jnp.int32, (c, c), 1)
    eye = (ii == jj).astype(F32)
    masks =((ii >= jj, ii > jj, row > lane), (ii <= jj, ii < jj, row < lane))

    loaded = []
    for chunk in chunks:
        r0 = pl.multiple_of(chunk * c, c)
        k = k_ref[0, pl.ds(r0, c), cols]
        q = q_ref[0, pl.ds(r0, c), cols]
        kq = _dot_nt(jnp.concatenate([k, q], axis=0), k)
        loaded.append((chunk, r0, q, k, kq))

    chains = []
    for chunk, r0, q, k, kq in loaded:
        gb = gb_ref[0, pl.ds(r0, c), :]
        for d in range(2):
            col = head + d * GDN_HEADS
            beta = jnp.sum(jnp.where(lane == col, gb, 0.0), axis=-1, keepdims=True)
            g = jnp.sum(jnp.where(lane == col + 2 * GDN_HEADS, gb, 0.0), axis=-1, keepdims=True)
            incl, strict, strict_wide = masks[d]
            rhs = jnp.where(lane >= c, g, jnp.where(strict_wide, g, 0.0))
            mask = incl.astype(BF16)
            hi = rhs.astype(BF16)
            rest = rhs - hi.astype(F32)
            mid = rest.astype(BF16)
            low = (rest - mid.astype(F32)).astype(BF16)
            e = _dot(mask, hi) + _dot(mask, mid) + _dot(mask, low)
            chains.append(dict(chunk=chunk, r0=r0, d=d, q=q, k=k, kq=kq, beta=beta, e=e))

    for ch in chains:
        incl, strict, _ = masks[ch["d"]]
        e = ch["e"]
        decay = jnp.where(incl, jnp.exp(e[:, :c]), 0.0)
        gc = e[:, c:c + 1]
        last = 0 if ch["d"] == 1 else c - 1
        gc_last = e[last:last + 1, c:c + 1]
        ch.update(decay=decay, gc=gc, gc_last=gc_last, egc=jnp.exp(gc))
        ch["a"] = jnp.where(strict, ch["beta"] * ch["kq"][:c] * decay, 0.0)
        ch["t"] = eye
    s = 1
    while s < c:
        pair = jnp.logical_and((ii // (2 * s)) == (jj // (2 * s)), (ii // s) != (jj // s))
        for ch in chains:
            ch["a_off"] = jnp.where(pair, ch["a"], 0.0)
        if s == 1:
            for ch in chains:
                ch["t"] = eye - ch["a_off"]
        else:
            for ch in chains:
                ch["m"] = _dot(ch["t"].astype(BF16), ch["a_off"].astype(BF16))
            for ch in chains:
                ch["t"] = ch["t"] - _dot(ch["m"].astype(BF16), ch["t"].astype(BF16))
        s *= 2
    for ch in chains:
        r0, beta, egc = ch["r0"], ch["beta"], ch["egc"]
        kf = ch["k"].astype(F32)
        vf = v_ref[0, pl.ds(r0, c), cols].astype(F32)
        rhs2 = jnp.concatenate([vf * beta, kf * (beta * egc)], axis=1).astype(BF16)
        ch["uw"] = _dot(ch["t"].astype(BF16), rhs2).astype(BF16)
        ch["k_dec_t"] = (kf * jnp.exp(ch["gc_last"] - ch["gc"])).T.astype(BF16)
    for ch in chains:
        incl = masks[ch["d"]][0]
        qk = jnp.where(incl, ch["kq"][c:] * ch["decay"], 0.0).astype(BF16)
        ch["nm"] = _dot(ch["k_dec_t"], ch["uw"])
        ch["ow"] = _dot(qk, ch["uw"])
    for ch in chains:
        chunk, r0, nm, ow = ch["chunk"], ch["r0"], ch["nm"], ch["ow"]
        s = 2 * local + ch["d"]
        q0 = pl.multiple_of(chunk * GDN_QM_ROWS, 16)
        qm_scr[s, pl.ds(q0, c), :] = (ch["q"].astype(F32) * ch["egc"] - ow[:, GDN_HEAD_DIM:]).astype(BF16)
        qm_scr[s, pl.ds(q0 + c, GDN_HEAD_DIM), :] = nm[:, GDN_HEAD_DIM:].astype(BF16)
        nn_scr[s, pl.ds(pl.multiple_of(chunk * GDN_HEAD_DIM, GDN_HEAD_DIM), GDN_HEAD_DIM), :] = nm[:, :GDN_HEAD_DIM]
        o_scr[s, pl.ds(r0, c), :] = ow[:, :GDN_HEAD_DIM]
        gl_scr[s, pl.ds(pl.multiple_of(chunk * GDN_GL_ROWS, GDN_GL_ROWS), GDN_GL_ROWS), :] = jnp.broadcast_to(
            jnp.exp(ch["gc_last"]), (GDN_GL_ROWS, LANES))


def _gdn_advance(qm_scr, nn_scr, o_scr, gl_scr, d, chunk, state):
    c = GDN_CHUNK
    qm = qm_scr[d, pl.ds(pl.multiple_of(chunk * GDN_QM_ROWS, 16), GDN_QM_ROWS), :]
    r = _dot(qm, state.astype(BF16))
    rows = pl.ds(pl.multiple_of(chunk * c, c), c)
    o_scr[d, rows, :] = o_scr[d, rows, :] + r[:c]
    gl = gl_scr[d, pl.ds(pl.multiple_of(chunk * GDN_GL_ROWS, GDN_GL_ROWS), 1), :]
    n = nn_scr[d, pl.ds(pl.multiple_of(chunk * GDN_HEAD_DIM, GDN_HEAD_DIM), GDN_HEAD_DIM), :]
    return state * gl - r[c:] + n


def _gdn_kernel(q_ref, k_ref, v_ref, z_ref, gb_ref, gain_ref, o_ref, qm_scr, nn_scr, o_scr, gl_scr):
    first_head = pl.program_id(1) * GDN_HEADS_PER_STEP
    c = GDN_CHUNK
    n_lat, n_ctx = SEQ // c, CTX_LEN // c
    n_chunks = n_lat + n_ctx

    for local in range(GDN_HEADS_PER_STEP):
        def prepare(i, _, local=local):
            chunks = [i * GDN_PREP_UNROLL + j for j in range(GDN_PREP_UNROLL)]
            _gdn_prepare(q_ref, k_ref, v_ref, gb_ref, qm_scr, nn_scr, o_scr, gl_scr, chunks, first_head + local, local)
            return 0

        lax.fori_loop(0, n_chunks // GDN_PREP_UNROLL, prepare, 0)

    def advance(i, states):
        cf = jnp.where(i < n_ctx, n_lat + i, i - n_ctx)
        cb = n_chunks - 1 - i
        return tuple(_gdn_advance(qm_scr, nn_scr, o_scr, gl_scr, s, cb if s % 2 else cf, state)
                     for s, state in enumerate(states))

    zero = jnp.zeros((GDN_HEAD_DIM, GDN_HEAD_DIM), F32)
    lax.fori_loop(0, n_chunks, advance, (zero,) * (2 * GDN_HEADS_PER_STEP))
    for local in range(GDN_HEADS_PER_STEP):
        cols = slice(local * LANES, (local + 1) * LANES)
        o = o_scr[2 * local] + o_scr[2 * local + 1]
        o_ref[0, :, cols] = (_rms(o, gain_ref[...]) * _silu(z_ref[0, :, cols].astype(F32))).astype(BF16)


def _gdn(qkv, p, gates, out_gain):
    n_steps = GDN_HEADS // GDN_HEADS_PER_STEP
    n_chunks = TOK // GDN_CHUNK
    width = GDN_HEADS_PER_STEP * LANES
    zblk = COL_Z // width
    n_chain = 2 * GDN_HEADS_PER_STEP
    blk = lambda off: pl.BlockSpec((1, TOK, width), lambda b, h: (b, 0, off + h))
    vec = pl.BlockSpec((1, LANES), lambda b, h: (0, 0))
    return pl.pallas_call(
        _gdn_kernel,
        grid=(BATCH, n_steps),
        in_specs=[blk(0), blk(n_steps), blk(2 * n_steps), blk(zblk),
                  pl.BlockSpec((1, TOK, LANES), lambda b, h: (b, 0, 0)), vec],
        out_specs=pl.BlockSpec((1, TOK, width), lambda b, h: (b, 0, h)),
        out_shape=jax.ShapeDtypeStruct((BATCH, TOK, BRANCH_WIDTH), BF16),
        scratch_shapes=[pltpu.VMEM((n_chain, n_chunks * GDN_QM_ROWS, LANES), BF16),
                        pltpu.VMEM((n_chain, n_chunks * GDN_HEAD_DIM, LANES), F32),
                        pltpu.VMEM((n_chain, TOK, LANES), F32),
                        pltpu.VMEM((n_chain, n_chunks * GDN_GL_ROWS, LANES), F32)],
        compiler_params=_params("parallel", "parallel"),
        name="gdn",
    )(qkv, qkv, qkv, p, gates, out_gain.reshape(1, LANES))


def _rms_halves(x, gain):
    lane = lax.broadcasted_iota(jnp.int32, x.shape, 1)
    lo = lane < DIFF_HEAD_DIM
    x2 = x * x
    s_lo = jnp.sum(jnp.where(lo, x2, 0.0), axis=-1, keepdims=True)
    s_hi = jnp.sum(jnp.where(lo, 0.0, x2), axis=-1, keepdims=True)
    ms = jnp.where(lo, s_lo, s_hi) * (1.0 / DIFF_HEAD_DIM)
    return x * lax.rsqrt(ms + EPS) * gain


def _rope(x, cos, sin):
    lane = lax.broadcasted_iota(jnp.int32, x.shape, 1)
    first = (lane & ROPE_PAIRS) == 0
    partner = jnp.where(first, -pltpu.roll(x, LANES - ROPE_PAIRS, 1), pltpu.roll(x, ROPE_PAIRS, 1))
    return x * cos + partner * sin


ATTN_Q_BLOCK = 512
ATTN_GROUP_ROWS = 128


def _attn_kernel(q_ref, k_ref, v_ref, cosk_ref, sink_ref, cosq_ref, sinq_ref, qg_ref, kg_ref, lv_ref, og_ref,
                 o_ref, kn_scr, *, ctx_block, lam_init):
    qi = pl.program_id(2)

    @pl.when(qi == 0)
    def _():
        kn = _rope(_rms_halves(k_ref[0].astype(F32), kg_ref[...]), cosk_ref[...], sink_ref[...])
        kn_scr[...] = kn.astype(BF16)

    lv = lv_ref[...]
    lam = (jnp.exp(jnp.sum(lv[0:1] * lv[1:2], axis=-1, keepdims=True))
           - jnp.exp(jnp.sum(lv[2:3] * lv[3:4], axis=-1, keepdims=True)) + lam_init)
    def attend(n_rows, kn, v):
        q = _rope(_rms_halves(q_ref[0, :n_rows, :].astype(F32), qg_ref[...]), cosq_ref[:n_rows, :], sinq_ref[:n_rows, :])
        q = q * (DIFF_HEAD_DIM ** -0.5 * math.log2(math.e))
        lane = lax.broadcasted_iota(jnp.int32, q.shape, 1)
        lo = lane < DIFF_HEAD_DIM
        q1 = jnp.where(lo, q, 0.0).astype(BF16)
        q2 = jnp.where(lo, 0.0, q).astype(BF16)

        def half(s):
            p = jnp.exp2(s - jnp.max(s, axis=-1, keepdims=True))
            return _dot(p.astype(BF16), v), jnp.sum(p, axis=-1, keepdims=True)
        rows = ATTN_GROUP_ROWS
        scores = [(_dot_nt(q1[r:r + rows], kn), _dot_nt(q2[r:r + rows], kn)) for r in range(0, n_rows, rows)]
        for g, (s1, s2) in enumerate(scores):
            a1, l1 = half(s1)
            a2, l2 = half(s2)
            o = a1 * (1.0 / l1) - a2 * (lam / l2)
            o_ref[0, g * rows:(g + 1) * rows, :] = (_rms(o, og_ref[...]) * (1.0 - lam_init)).astype(BF16)

    if ctx_block is None:
        attend(q_ref.shape[1], kn_scr[...], v_ref[0])
    else:
        @pl.when(qi == ctx_block)
        def _():
            attend(CTX_LEN, kn_scr[SEQ:, :], v_ref[0, SEQ:, :])

        @pl.when(qi != ctx_block)
        def _():
            attend(q_ref.shape[1], kn_scr[...], v_ref[0])


def _rope_tables():
    n_rows = SEQ // GRID_W
    row_id = jnp.broadcast_to(jnp.arange(n_rows, dtype=F32)[:, None], (n_rows, GRID_W)).reshape(-1)
    col_id = jnp.broadcast_to(jnp.arange(GRID_W, dtype=F32)[None, :], (n_rows, GRID_W)).reshape(-1)
    inv_freq = jnp.power(ROPE_BASE, -jnp.arange(ROPE_PAIRS, dtype=F32) / ROPE_PAIRS)
    row_ang = row_id[:, None] * inv_freq
    col_ang = col_id[:, None] * inv_freq
    ang = jnp.concatenate([row_ang, row_ang, col_ang, col_ang], axis=-1)
    ang = jnp.concatenate([ang, ang], axis=-1)
    pad = ((0, CTX_LEN), (0, 0))
    return jnp.pad(jnp.cos(ang), pad, constant_values=1.0), jnp.pad(jnp.sin(ang), pad)


def _diff_attn(p, cos, sin, q_gain, k_gain, lam_vecs, out_gain, lam_init, with_ctx):
    tq = ATTN_Q_BLOCK
    n_rows = TOK if with_ctx else SEQ
    nq = pl.cdiv(n_rows, tq)
    nh = DIFF_HEADS
    qb, kb, vb = COL_DQ // LANES, COL_DK // LANES, COL_DV // LANES
    full = lambda off: pl.BlockSpec((1, TOK, LANES), lambda b, h, i: (b, 0, off + h))
    tab_full = pl.BlockSpec((TOK, LANES), lambda b, h, i: (0, 0))
    tab_q = pl.BlockSpec((tq, LANES), lambda b, h, i: (i, 0))
    vec = pl.BlockSpec((1, LANES), lambda b, h, i: (0, 0))
    tile2 = lambda g: jnp.concatenate([g, g]).reshape(1, LANES)
    return pl.pallas_call(
        functools.partial(_attn_kernel, ctx_block=SEQ // tq if with_ctx else None, lam_init=lam_init),
        grid=(BATCH, nh, nq),
        in_specs=[pl.BlockSpec((1, tq, LANES), lambda b, h, i: (b, i, qb + h)), full(kb), full(vb),
                  tab_full, tab_full, tab_q, tab_q, vec, vec,
                  pl.BlockSpec((4, DIFF_HEAD_DIM), lambda b, h, i: (0, 0)), vec],
        out_specs=pl.BlockSpec((1, tq, LANES), lambda b, h, i: (b, i, h)),
        out_shape=jax.ShapeDtypeStruct((BATCH, n_rows, BRANCH_WIDTH), BF16),
        scratch_shapes=[pltpu.VMEM((TOK, LANES), BF16)],
        compiler_params=_params("parallel", "parallel", "arbitrary"),
        name="diff_attn",
    )(p, p, p, cos, sin, cos, sin, tile2(q_gain), tile2(k_gain), lam_vecs, out_gain.reshape(1, LANES))


LRU_SLAB = 256
LRU_ROWS = 256
LRU_SCAN_BLOCK = 8


def _lru_kernel(x_ref, y_ref, cw_ref, cb_ref, wg_ref, bg_ref, lam_ref, o_ref, xc_scr, af_scr, bf_scr, ab_scr, bb_scr):
    w = LRU_SLAB
    blk = LRU_SCAN_BLOCK
    for c0 in range(0, w, LANES):
        cols = slice(c0, c0 + LANES)
        xc_scr[:, cols] = _conv(x_ref[0, :, cols].astype(F32), cw_ref[:, cols]) + cb_ref[:, cols]
    sp = _softplus(-lam_ref[0])
    sub = lax.broadcasted_iota(jnp.int32, (LRU_ROWS // blk, blk, w), 1)

    def gates(i, _):
        r0 = pl.multiple_of(i * LRU_ROWS, LRU_ROWS)
        xc = xc_scr[pl.ds(r0, LRU_ROWS), :]
        pre = _dot(xc.astype(BF16), wg_ref[0]) + bg_ref[0]
        for d, (a_scr, b_scr) in enumerate(((af_scr, bf_scr), (ab_scr, bb_scr))):
            r = _sigmoid(pre[:, (2 * d) * w:(2 * d + 1) * w])
            gi = _sigmoid(pre[:, (2 * d + 1) * w:(2 * d + 2) * w])
            log_a = -LRU_C * r * sp[d:d + 1]
            a = jnp.exp(log_a)
            b = jnp.sqrt(1.0 - a * a) * gi * xc
            a = a.reshape(LRU_ROWS // blk, blk, w)
            b = b.reshape(LRU_ROWS // blk, blk, w)
            shift = 1
            while shift < blk:
                if d == 0:
                    ok, roll_by = sub >= shift, shift
                else:
                    ok, roll_by = sub < blk - shift, blk - shift
                b = jnp.where(ok, a * pltpu.roll(b, roll_by, 1) + b, b)
                a = jnp.where(ok, a * pltpu.roll(a, roll_by, 1), a)
                shift *= 2
            a_scr[pl.ds(r0, LRU_ROWS), :] = a.reshape(LRU_ROWS, w)
            b_scr[pl.ds(r0, LRU_ROWS), :] = b.reshape(LRU_ROWS, w)
        return 0

    lax.fori_loop(0, TOK // LRU_ROWS, gates, 0)

    n_blk, n_lat_blk, n_ctx_blk = TOK // blk, SEQ // blk, CTX_LEN // blk

    def step(s, carry):
        h_f, h_b = carry
        rows_f = pl.ds(pl.multiple_of(jnp.where(s < n_ctx_blk, n_lat_blk + s, s - n_ctx_blk) * blk, blk), blk)
        rows_b = pl.ds(pl.multiple_of((n_blk - 1 - s) * blk, blk), blk)
        hf = af_scr[rows_f, :] * h_f + bf_scr[rows_f, :]
        bf_scr[rows_f, :] = hf
        hb = ab_scr[rows_b, :] * h_b + bb_scr[rows_b, :]
        bb_scr[rows_b, :] = hb
        return hf[blk - 1:blk, :], hb[0:1, :]

    zero = jnp.zeros((1, w), F32)
    lax.fori_loop(0, n_blk, step, (zero, zero), unroll=4)
    h = bf_scr[...] + bb_scr[...]
    o_ref[0] = (h * jax.nn.gelu(y_ref[0].astype(F32))).astype(BF16)


def _lru_gate_weights(w_gate, b_gate):
    n_slab = BRANCH_WIDTH // LRU_SLAB
    per = LRU_SLAB // LRU_BLOCK_DIM
    wg = w_gate.reshape(2, 2, n_slab, per, LRU_BLOCK_DIM, LRU_BLOCK_DIM)
    eye = jnp.eye(per, dtype=w_gate.dtype)
    dense = jnp.einsum('dgsnjk,nm->snjdgmk', wg, eye)
    dense = dense.reshape(n_slab, LRU_SLAB, 4 * LRU_SLAB)
    bg = b_gate.reshape(2, 2, n_slab, LRU_SLAB).transpose(2, 0, 1, 3).reshape(n_slab, 1, 4 * LRU_SLAB)
    return dense.astype(BF16), bg


def _lru(p, conv_w, conv_b, w_gate, b_gate, lam):
    n_slab = BRANCH_WIDTH // LRU_SLAB
    wg, bg = _lru_gate_weights(w_gate, b_gate)
    lam_s = lam.reshape(2, n_slab, LRU_SLAB).transpose(1, 0, 2)
    xb, yb = COL_LX // LRU_SLAB, COL_LY // LRU_SLAB
    return pl.pallas_call(
        _lru_kernel,
        grid=(BATCH, n_slab),
        in_specs=[pl.BlockSpec((1, TOK, LRU_SLAB), lambda b, s: (b, 0, xb + s)),
                  pl.BlockSpec((1, TOK, LRU_SLAB), lambda b, s: (b, 0, yb + s)),
                  pl.BlockSpec((CONV_WIDTH, LRU_SLAB), lambda b, s: (0, s)),
                  pl.BlockSpec((1, LRU_SLAB), lambda b, s: (0, s)),
                  pl.BlockSpec((1, LRU_SLAB, 4 * LRU_SLAB), lambda b, s: (s, 0, 0)),
                  pl.BlockSpec((1, 1, 4 * LRU_SLAB), lambda b, s: (s, 0, 0)),
                  pl.BlockSpec((1, 2, LRU_SLAB), lambda b, s: (s, 0, 0))],
        out_specs=pl.BlockSpec((1, TOK, LRU_SLAB), lambda b, s: (b, 0, s)),
        out_shape=jax.ShapeDtypeStruct((BATCH, TOK, BRANCH_WIDTH), BF16),
        scratch_shapes=[pltpu.VMEM((TOK, LRU_SLAB), F32)] * 5,
        compiler_params=_params("parallel", "parallel"),
        name="lru",
    )(p, p, conv_w, conv_b.reshape(1, BRANCH_WIDTH), wg, bg, lam_s)


def _route(logits):
    lane = lax.broadcasted_iota(jnp.int32, logits.shape, 1)
    lane_f = lane.astype(F32)
    far = float(LANES)
    lg = jnp.where(lane < N_EXPERTS, logits, -jnp.inf)
    ex = jnp.exp(lg - jnp.max(lg, axis=-1, keepdims=True))
    probs = ex / jnp.sum(ex, axis=-1, keepdims=True)
    per_group = N_EXPERTS // N_GROUPS
    grp = lane // per_group

    def top2(vals):
        m1 = jnp.max(vals, axis=-1, keepdims=True)
        i1 = jnp.min(jnp.where(vals == m1, lane_f, far), axis=-1, keepdims=True)
        rest = jnp.where(lane_f == i1, -2.0, vals)
        m2 = jnp.max(rest, axis=-1, keepdims=True)
        i2 = jnp.min(jnp.where(rest == m2, lane_f, far), axis=-1, keepdims=True)
        return m1, i1, m2, i2

    best = jnp.zeros(logits.shape[:1] + (1,), jnp.int32)
    best_score = None
    for g in range(N_GROUPS):
        m1, _, m2, _ = top2(jnp.where(grp == g, probs, -1.0))
        score = m1 + m2
        if best_score is None:
            best_score = score
        else:
            better = score > best_score
            best = jnp.where(better, g, best)
            best_score = jnp.where(better, score, best_score)
    m1, i1, m2, i2 = top2(jnp.where(grp == best, probs, -1.0))
    den = m1 + m2
    return i1, i2, m1 / den, m2 / den


ROUTE_E, ROUTE_W, ROUTE_RANK = 0, 2, 4


def _pack_bf16_pairs(x):
    n = x.shape[1] // 2
    xb = x.astype(BF16).astype(F32)
    lo = pltpu.bitcast(xb[:, :n], jnp.uint32)
    hi = pltpu.bitcast(xb[:, n:], jnp.uint32)
    return (lo >> 16) | (hi & jnp.uint32(0xFFFF0000))


def _unpack_bf16_pairs(p):
    lo = pltpu.bitcast(p << 16, F32)
    hi = pltpu.bitcast(p & jnp.uint32(0xFFFF0000), F32)
    return jnp.concatenate([lo, hi], axis=1)


def _merge_kernel(ya_ref, yb_ref, yc_ref, gates_ref, x_ref, ml_ref, mc_ref, g2_ref, wbr_ref, wout_ref, wr_ref,
                  br_ref, xo_ref, h2_ref, route_ref, cnt_ref, cnt_scr, *, tm):
    i = pl.program_id(1)

    @pl.when(jnp.logical_and(pl.program_id(0) == 0, i == 0))
    def _():
        cnt_scr[...] = jnp.zeros_like(cnt_scr)

    acc = None
    for n, y_ref in enumerate((ya_ref, yb_ref, yc_ref)):
        yn = _dot(y_ref[0], wbr_ref[n])
        gate = _sigmoid(gates_ref[0, :, n * D_MODEL:(n + 1) * D_MODEL].astype(F32))
        acc = gate * yn if acc is None else acc + gate * yn
    out = _dot(acc.astype(BF16), wout_ref[...])
    is_ctx = _ctx_rows(i, tm, D_MODEL)
    xn = x_ref[0] + jnp.where(is_ctx, mc_ref[0, 2:3, :], ml_ref[0, 2:3, :]) * out
    xo_ref[0] = xn
    h2 = _modulated_norm(xn, g2_ref[...], ml_ref, mc_ref, is_ctx, 3)
    h2_ref[0] = _pack_bf16_pairs(h2)
    i1, i2, w1, w2 = _route(_dot(h2.astype(BF16), wr_ref[...]) + br_ref[...])
    lane = lax.broadcasted_iota(jnp.int32, (tm, LANES), 1)
    lane_f = lane.astype(F32)
    chosen = jnp.where(jnp.logical_or(lane_f == i1, lane_f == i2), 1.0, 0.0)
    earlier = (lax.broadcasted_iota(jnp.int32, (tm, tm), 0) > lax.broadcasted_iota(jnp.int32, (tm, tm), 1))
    before = _dot(jnp.where(earlier, 1.0, 0.0).astype(BF16), chosen.astype(BF16)) + cnt_scr[...]
    rank1 = jnp.sum(jnp.where(lane_f == i1, before, 0.0), axis=-1, keepdims=True)
    rank2 = jnp.sum(jnp.where(lane_f == i2, before, 0.0), axis=-1, keepdims=True)
    cnt_scr[...] += jnp.sum(chosen, axis=0, keepdims=True)
    cnt_ref[...] = cnt_scr[...]
    record = jnp.zeros((tm, LANES), F32)
    for pos, val in enumerate((i1, i2, w1, w2, rank1, rank2)):
        record = jnp.where(lane == pos, val, record)
    route_ref[0] = record


def _merge(ya, yb, yc, p, xs, mods, gain2, w_branch, w_out, w_router, b_router, rows, tm):
    wr = jnp.zeros((D_MODEL, LANES), BF16).at[:, :N_EXPERTS].set(w_router.astype(BF16))
    br = jnp.zeros((1, LANES), F32).at[0, :N_EXPERTS].set(b_router)
    tile = lambda w: pl.BlockSpec((1, tm, w), lambda b, i: (b, i, 0))
    const = lambda shape: pl.BlockSpec(shape, lambda b, i: (0,) * len(shape))
    return pl.pallas_call(
        functools.partial(_merge_kernel, tm=tm),
        grid=(BATCH, rows // tm),
        in_specs=[tile(BRANCH_WIDTH), tile(BRANCH_WIDTH), tile(BRANCH_WIDTH), tile(3 * D_MODEL), tile(D_MODEL),
                  pl.BlockSpec((1, N_MOD, D_MODEL), lambda b, i: (b, 0, 0)),
                  pl.BlockSpec((1, N_MOD, D_MODEL), lambda b, i: (BATCH, 0, 0)),
                  const((1, D_MODEL)), const((3, BRANCH_WIDTH, D_MODEL)), const((D_MODEL, D_MODEL)),
                  const((D_MODEL, LANES)), const((1, LANES))],
        out_specs=[tile(D_MODEL), tile(D_MODEL // 2), tile(LANES), const((1, LANES))],
        out_shape=[jax.ShapeDtypeStruct((BATCH, rows, D_MODEL), F32),
                   jax.ShapeDtypeStruct((BATCH, rows, D_MODEL // 2), jnp.uint32),
                   jax.ShapeDtypeStruct((BATCH, rows, LANES), F32),
                   jax.ShapeDtypeStruct((1, LANES), F32)],
        scratch_shapes=[pltpu.VMEM((1, LANES), F32)],
        compiler_params=_params("arbitrary", "arbitrary"),
        name="merge",
    )(ya, yb, yc, p, xs, mods, mods, gain2.reshape(1, D_MODEL), w_branch, w_out, wr, br)


MOE_TILE = 512
SC_GATHER_ROWS = 64


def _sc_gather(table, idx):
    info = plsc.get_sparse_core_info()
    n_workers = info.num_cores * info.num_subcores
    n_rows, width = idx.shape[0], table.shape[1]
    per_worker = n_rows // n_workers
    assert per_worker * n_workers == n_rows and per_worker % SC_GATHER_ROWS == 0
    mesh = plsc.VectorSubcoreMesh(core_axis_name="c", subcore_axis_name="s")

    @functools.partial(
        pl.kernel, mesh=mesh, out_type=jax.ShapeDtypeStruct((n_rows, width), table.dtype),
        scratch_types=[pltpu.VMEM((SC_GATHER_ROWS,), jnp.int32),
                       pltpu.VMEM((SC_GATHER_ROWS, width), table.dtype),
                       pltpu.SemaphoreType.DMA],
        name="sc_gather")
    def gather(table_hbm, idx_hbm, out_hbm, idx_v, rows_v, sem):
        worker = lax.axis_index("s") * info.num_cores + lax.axis_index("c")
        base = worker * per_worker

        @pl.loop(0, per_worker // SC_GATHER_ROWS)
        def _(j):
            off = base + j * SC_GATHER_ROWS
            pltpu.sync_copy(idx_hbm.at[pl.ds(off, SC_GATHER_ROWS)], idx_v)
            pltpu.async_copy(table_hbm.at[idx_v], rows_v, sem).wait()
            pltpu.sync_copy(rows_v, out_hbm.at[pl.ds(off, SC_GATHER_ROWS)])

    return gather(table, idx)


SC_SCATTER_ROWS = 128


def _sc_scatter(table, dest, n_out):
    info = plsc.get_sparse_core_info()
    n_workers = info.num_cores * info.num_subcores
    n_tok, width = table.shape
    n_assign = dest.shape[0]
    per_worker = n_assign // n_workers
    assert per_worker * n_workers == n_assign and per_worker % SC_SCATTER_ROWS == 0 and n_tok % SC_SCATTER_ROWS == 0
    mesh = plsc.VectorSubcoreMesh(core_axis_name="c", subcore_axis_name="s")

    @functools.partial(
        pl.kernel, mesh=mesh, out_type=jax.ShapeDtypeStruct((n_out, width), table.dtype),
        scratch_types=[pltpu.VMEM((SC_SCATTER_ROWS,), jnp.int32),
                       pltpu.VMEM((SC_SCATTER_ROWS, width), table.dtype),
                       pltpu.SemaphoreType.DMA],
        name="sc_scatter")
    def scatter(table_hbm, dest_hbm, out_hbm, idx_v, rows_v, sem):
        worker = lax.axis_index("s") * info.num_cores + lax.axis_index("c")
        base = worker * per_worker

        @pl.loop(0, per_worker // SC_SCATTER_ROWS)
        def _(j):
            off = base + j * SC_SCATTER_ROWS
            pltpu.sync_copy(dest_hbm.at[pl.ds(off, SC_SCATTER_ROWS)], idx_v)
            pltpu.sync_copy(table_hbm.at[pl.ds(lax.rem(off, n_tok), SC_SCATTER_ROWS)], rows_v)
            pltpu.async_copy(rows_v, out_hbm.at[idx_v], sem).wait()

    return scatter(table, dest)


def _experts_kernel(tile_expert_ref, n_valid_ref, x_ref, wgu_ref, wd_ref, o_ref, wgu_scr, wd_scr):
    j = pl.program_id(0)
    valid = j < n_valid_ref[0]
    fresh = jnp.logical_or(j == 0, tile_expert_ref[j] != tile_expert_ref[jnp.maximum(j - 1, 0)])

    @pl.when(jnp.logical_and(valid, fresh))
    def _():
        wgu_scr[...] = wgu_ref[0, 0].astype(BF16)
        wd_scr[...] = wd_ref[0, 0].astype(BF16)

    @pl.when(valid)
    def _():
        x = _unpack_bf16_pairs(x_ref[...]).astype(BF16)
        gu = _dot(x, wgu_scr[...])
        act = (_silu(gu[:, :EXPERT_FF]) * gu[:, EXPERT_FF:]).astype(BF16)
        o_ref[...] = _pack_bf16_pairs(_dot(act, wd_scr[...]))

    @pl.when(jnp.logical_not(valid))
    def _():
        o_ref[...] = jnp.zeros_like(o_ref)


def _experts(x_sorted, tile_expert, n_valid, w_gate_up, w_down, layer):
    n_tiles = x_sorted.shape[0] // MOE_TILE
    half = D_MODEL // 2
    return pl.pallas_call(
        _experts_kernel,
        grid_spec=pltpu.PrefetchScalarGridSpec(
            num_scalar_prefetch=2, grid=(n_tiles,),
            in_specs=[pl.BlockSpec((MOE_TILE, half), lambda j, te, nv: (j, 0)),
                      pl.BlockSpec((1, 1, D_MODEL, 2 * EXPERT_FF), lambda j, te, nv: (layer, te[j], 0, 0)),
                      pl.BlockSpec((1, 1, EXPERT_FF, D_MODEL), lambda j, te, nv: (layer, te[j], 0, 0))],
            out_specs=pl.BlockSpec((MOE_TILE, half), lambda j, te, nv: (j, 0)),
            scratch_shapes=[pltpu.VMEM((D_MODEL, 2 * EXPERT_FF), BF16), pltpu.VMEM((EXPERT_FF, D_MODEL), BF16)]),
        out_shape=jax.ShapeDtypeStruct((x_sorted.shape[0], half), jnp.uint32),
        compiler_params=_params("arbitrary"),
        name="experts",
    )(tile_expert, n_valid, x_sorted, w_gate_up, w_down)


def _combine_kernel(y1_ref, y2_ref, route_ref, x_ref, ml_ref, mc_ref, o_ref, *, tm):
    i = pl.program_id(1)
    route = route_ref[0]
    w1 = route[:, ROUTE_W:ROUTE_W + 1]
    w2 = route[:, ROUTE_W + 1:ROUTE_W + 2]
    moe = w1 * _unpack_bf16_pairs(y1_ref[0, 0]) + w2 * _unpack_bf16_pairs(y2_ref[0, 0])
    is_ctx = _ctx_rows(i, tm, D_MODEL)
    o_ref[0] = x_ref[0] + jnp.where(is_ctx, mc_ref[0, 5:6, :], ml_ref[0, 5:6, :]) * moe


def _combine(y_pairs, route, xs, mods, rows, tm):
    half = D_MODEL // 2
    tile = lambda w: pl.BlockSpec((1, tm, w), lambda b, i: (b, i, 0))
    slot = lambda s: pl.BlockSpec((1, 1, tm, half), lambda b, i: (s, b, i, 0))
    return pl.pallas_call(
        functools.partial(_combine_kernel, tm=tm),
        grid=(BATCH, rows // tm),
        in_specs=[slot(0), slot(1), tile(LANES), tile(D_MODEL),
                  pl.BlockSpec((1, N_MOD, D_MODEL), lambda b, i: (b, 0, 0)),
                  pl.BlockSpec((1, N_MOD, D_MODEL), lambda b, i: (BATCH, 0, 0))],
        out_specs=tile(D_MODEL),
        out_shape=jax.ShapeDtypeStruct((BATCH, rows, D_MODEL), F32),
        compiler_params=_params("parallel", "parallel"),
        name="combine",
    )(y_pairs, y_pairs, route, xs, mods, mods)


def _moe(h2, route, counts, xs, mods, w_gate_up, w_down, layer, rows, tm):
    n_tok = BATCH * rows
    half = D_MODEL // 2
    n_sorted = 2 * n_tok + N_EXPERTS * MOE_TILE
    n_tiles = n_sorted // MOE_TILE
    rec = route.reshape(n_tok, LANES)
    expert = rec[:, ROUTE_E:ROUTE_E + 2].astype(jnp.int32)
    rank = rec[:, ROUTE_RANK:ROUTE_RANK + 2].astype(jnp.int32)
    count = counts[0, :N_EXPERTS].astype(jnp.int32)
    padded = (count + MOE_TILE - 1) // MOE_TILE * MOE_TILE
    end = jnp.cumsum(padded)
    start = end - padded
    first = jnp.sum(jnp.where(expert[:, :1] == jnp.arange(N_EXPERTS), start, 0), axis=1)
    second = jnp.sum(jnp.where(expert[:, 1:] == jnp.arange(N_EXPERTS), start, 0), axis=1)
    dest = jnp.concatenate([first + rank[:, 0], second + rank[:, 1]])
    tile_start = jnp.arange(n_tiles, dtype=jnp.int32) * MOE_TILE
    tile_expert = jnp.minimum(jnp.sum(tile_start[:, None] >= end[None, :], axis=1), N_EXPERTS - 1).astype(jnp.int32)
    n_valid = (end[-1:] // MOE_TILE).astype(jnp.int32)
    x_sorted = _sc_scatter(h2.reshape(n_tok, half), dest, n_sorted)
    y_sorted = _experts(x_sorted, tile_expert, n_valid, w_gate_up, w_down, layer)
    y_pairs = _sc_gather(y_sorted, dest).reshape(2, BATCH, rows, half)
    return _combine(y_pairs, route, xs, mods, rows, tm)


def _split_w_in(w_in):
    bw = BRANCH_WIDTH
    sizes = (3 * bw, bw, 2 * GDN_HEADS, 2 * GDN_HEADS, bw, bw, bw, bw, bw, 3 * D_MODEL)
    offs = [0]
    for s in sizes:
        offs.append(offs[-1] + s)
    part = lambda i: w_in[:, :, offs[i]:offs[i + 1]]
    main = jnp.concatenate([part(9), part(0), part(1), part(4), part(5), part(6), part(7), part(8)], axis=2)
    ba = jnp.zeros(w_in.shape[:2] + (LANES,), F32).at[:, :, :4 * GDN_HEADS].set(
        jnp.concatenate([part(2), part(3)], axis=2))
    return main.astype(BF16), ba.astype(BF16)


def kernel(x, c, ctx, c_ctx, w_mod, b_mod, norm1_gain, norm2_gain, w_in, gdn_conv_w, gdn_a_log, gdn_dt_bias, gdn_out_gain, diff_q_gain, diff_k_gain, diff_lambda, diff_out_gain, lru_conv_w, lru_conv_b, lru_w_gate, lru_b_gate, lru_lambda, w_branch, w_out, w_router, b_router, w_gate_up, w_down):
    mods = _mods(c, c_ctx, w_mod, b_mod)
    cos, sin = _rope_tables()
    xs = jnp.concatenate([x, ctx], axis=1)
    w_main, w_ba = _split_w_in(w_in)
    for layer in range(DEPTH):
        last = layer == DEPTH - 1
        lam_init = 0.8 - 0.6 * math.exp(-0.3 * layer)
        m = mods[layer]
        p, gates = _project(xs, m, norm1_gain[layer], w_main, w_ba, gdn_a_log[layer], gdn_dt_bias[layer], layer)
        qkv = _gdn_conv(p, gdn_conv_w[layer])
        ya = _gdn(qkv, p, gates, gdn_out_gain[layer])
        yb = _diff_attn(p, cos, sin, diff_q_gain[layer], diff_k_gain[layer], diff_lambda[layer],
                        diff_out_gain[layer], lam_init, with_ctx=not last)
        yc = _lru(p, lru_conv_w[layer], lru_conv_b[layer], lru_w_gate[layer], lru_b_gate[layer], lru_lambda[layer])
        rows, tm = (SEQ, 512) if last else (TOK, 768)
        xs, h2, route, counts = _merge(ya, yb, yc, p, xs, m, norm2_gain[layer], w_branch[layer].astype(BF16),
                                       w_out[layer].astype(BF16), w_router, b_router, rows, tm)
        xs = _moe(h2, route, counts, xs, m, w_gate_up, w_down, layer, rows, tm)
    return xs
```

```python
import functools
import math

import jax
import jax.numpy as jnp
from jax import lax
from jax.experimental import pallas as pl
from jax.experimental.pallas import tpu as pltpu
from jax.experimental.pallas import tpu_sc as plsc

F32 = jnp.float32
BF16 = jnp.bfloat16

D_MODEL = 1024
BATCH = 8
SEQ = 2048
DEPTH = 2
GRID_W = 64
CTX_LEN = 256
TOK = SEQ + CTX_LEN
N_MOD = 6
EPS = 1e-6
CONV_WIDTH = 4
BRANCH_WIDTH = 512
GDN_HEADS = 4
GDN_HEAD_DIM = 128
GDN_CHUNK = 64
DIFF_HEADS = 4
DIFF_HEAD_DIM = 64
ROPE_BASE = 10000.0
ROPE_PAIRS = DIFF_HEAD_DIM // 4
LRU_BLOCKS = 8
LRU_BLOCK_DIM = BRANCH_WIDTH // LRU_BLOCKS
LRU_C = 8.0
N_EXPERTS = 16
N_GROUPS = 4
EXPERT_FF = 512

LANES = 128
VMEM_LIMIT = 56 * 1024 * 1024

COL_GATES = 0
COL_QKV = 3 * D_MODEL
COL_Z = COL_QKV + 3 * BRANCH_WIDTH
COL_DQ = COL_Z + BRANCH_WIDTH
COL_DK = COL_DQ + BRANCH_WIDTH
COL_DV = COL_DK + BRANCH_WIDTH
COL_LX = COL_DV + BRANCH_WIDTH
COL_LY = COL_LX + BRANCH_WIDTH
PROJ_COLS = COL_LY + BRANCH_WIDTH


def _params(*sem):
    return pltpu.CompilerParams(dimension_semantics=sem, vmem_limit_bytes=VMEM_LIMIT)


def _dot(a, b, precision=None):
    return jnp.dot(a, b, preferred_element_type=F32, precision=precision)


def _dot_nt(a, b):
    return lax.dot_general(a, b, (((1,), (1,)), ((), ())), preferred_element_type=F32)


_sigmoid = jax.nn.sigmoid


def _silu(x):
    return x * _sigmoid(x)


def _softplus(x):
    return jnp.maximum(x, 0.0) + jnp.log(1.0 + jnp.exp(-jnp.abs(x)))


def _rms(x, gain):
    return x * lax.rsqrt(jnp.mean(x * x, axis=-1, keepdims=True) + EPS) * gain


def _mod_kernel(c_ref, w_ref, b_ref, o_ref):
    c = c_ref[...]
    o_ref[0] = _dot(_silu(c), w_ref[0], precision=lax.Precision.HIGHEST) + b_ref[0]


def _mods(c, c_ctx, w_mod, b_mod):
    depth = w_mod.shape[0]
    rows = 16
    cc = jnp.zeros((rows, D_MODEL), F32).at[:BATCH].set(c).at[BATCH].set(c_ctx)
    tn = 1536
    out = pl.pallas_call(
        _mod_kernel,
        grid=(depth, N_MOD * D_MODEL // tn),
        in_specs=[pl.BlockSpec((rows, D_MODEL), lambda l, j: (0, 0)),
                  pl.BlockSpec((1, D_MODEL, tn), lambda l, j: (l, 0, j)),
                  pl.BlockSpec((1, 1, tn), lambda l, j: (l, 0, j))],
        out_specs=pl.BlockSpec((1, rows, tn), lambda l, j: (l, 0, j)),
        out_shape=jax.ShapeDtypeStruct((depth, rows, N_MOD * D_MODEL), F32),
        compiler_params=_params("parallel", "parallel"),
        name="mods",
    )(cc, w_mod, b_mod.reshape(depth, 1, N_MOD * D_MODEL))
    return out.reshape(depth, rows, N_MOD, D_MODEL)


def _modulated_norm(x, gain, ml_ref, mc_ref, is_ctx, shift_idx):
    shift = jnp.where(is_ctx, mc_ref[0, shift_idx:shift_idx + 1, :], ml_ref[0, shift_idx:shift_idx + 1, :])
    scale = jnp.where(is_ctx, mc_ref[0, shift_idx + 1:shift_idx + 2, :], ml_ref[0, shift_idx + 1:shift_idx + 2, :])
    return _rms(x, gain) * (1.0 + scale) + shift


def _ctx_rows(tile, tm, width):
    row = tile * tm + lax.broadcasted_iota(jnp.int32, (tm, width), 0)
    return row >= SEQ


def _proj_kernel(x_ref, ml_ref, mc_ref, g_ref, w_ref, wba_ref, alog_ref, dtb_ref, p_ref, gb_ref, h_scr, *, tm):
    i = pl.program_id(1)
    j = pl.program_id(2)

    @pl.when(j == 0)
    def _():
        is_ctx = _ctx_rows(i, tm, D_MODEL)
        h = _modulated_norm(x_ref[0], g_ref[...], ml_ref, mc_ref, is_ctx, 0).astype(BF16)
        h_scr[...] = h
        ba = _dot(h, wba_ref[0])
        lane = lax.broadcasted_iota(jnp.int32, ba.shape, 1)
        log_decay = -jnp.exp(alog_ref[...]) * _softplus(ba + dtb_ref[...])
        gb_ref[0] = jnp.where(lane < 2 * GDN_HEADS, _sigmoid(ba), log_decay)

    p_ref[0] = _dot(h_scr[...], w_ref[0]).astype(BF16)


def _project(xs, mods, gain, w_main, w_ba, a_log, dt_bias, layer):
    def pad_lanes(vals):
        row = jnp.zeros((LANES,), F32).at[2 * GDN_HEADS:4 * GDN_HEADS].set(vals.reshape(-1))
        return row.reshape(1, LANES)

    vec = pl.BlockSpec((1, LANES), lambda b, i, j: (0, 0))
    tm, tn = 1152, 1280
    return pl.pallas_call(
        functools.partial(_proj_kernel, tm=tm),
        grid=(BATCH, TOK // tm, PROJ_COLS // tn),
        in_specs=[pl.BlockSpec((1, tm, D_MODEL), lambda b, i, j: (b, i, 0)),
                  pl.BlockSpec((1, N_MOD, D_MODEL), lambda b, i, j: (b, 0, 0)),
                  pl.BlockSpec((1, N_MOD, D_MODEL), lambda b, i, j: (BATCH, 0, 0)),
                  pl.BlockSpec((1, D_MODEL), lambda b, i, j: (0, 0)),
                  pl.BlockSpec((1, D_MODEL, tn), lambda b, i, j: (layer, 0, j)),
                  pl.BlockSpec((1, D_MODEL, LANES), lambda b, i, j: (layer, 0, 0)), vec, vec],
        out_specs=[pl.BlockSpec((1, tm, tn), lambda b, i, j: (b, i, j)),
                   pl.BlockSpec((1, tm, LANES), lambda b, i, j: (b, i, 0))],
        out_shape=[jax.ShapeDtypeStruct((BATCH, TOK, PROJ_COLS), BF16),
                   jax.ShapeDtypeStruct((BATCH, TOK, LANES), F32)],
        scratch_shapes=[pltpu.VMEM((tm, D_MODEL), BF16)],
        compiler_params=_params("parallel", "parallel", "arbitrary"),
        name="proj",
    )(xs, mods, mods, gain.reshape(1, D_MODEL), w_main, w_ba, pad_lanes(a_log), pad_lanes(dt_bias))


def _conv(x, w):
    n, c = x.shape
    t = lax.broadcasted_iota(jnp.int32, (n, c), 0)
    is_ctx = t >= SEQ
    local = jnp.where(is_ctx, t - SEQ, t)
    seg_len = jnp.where(is_ctx, CTX_LEN, SEQ)
    y = jnp.zeros_like(x)
    for j in range(CONV_WIDTH):
        s = j - CONV_WIDTH // 2
        if s == 0:
            y = y + x * w[j:j + 1, :]
        else:
            shifted = pltpu.roll(x, (-s) % n, 0)
            ok = jnp.logical_and(local + s >= 0, local + s < seg_len)
            y = y + jnp.where(ok, shifted, 0.0) * w[j:j + 1, :]
    return y


def _gdn_conv_kernel(x_ref, w_ref, o_ref):
    j = pl.program_id(1)
    for h in range(GDN_HEADS):
        cols = slice(h * GDN_HEAD_DIM, (h + 1) * GDN_HEAD_DIM)
        y = _silu(_conv(x_ref[0, :, cols].astype(F32), w_ref[:, cols]))
        nrm = lax.rsqrt(jnp.sum(y * y, axis=-1, keepdims=True) + EPS)
        scale = jnp.where(j == 0, nrm * GDN_HEAD_DIM ** -0.5, jnp.where(j == 1, nrm, 1.0))
        o_ref[0, :, cols] = (y * scale).astype(BF16)


def _gdn_conv(p, conv_w):
    first = COL_QKV // BRANCH_WIDTH
    return pl.pallas_call(
        _gdn_conv_kernel,
        grid=(BATCH, 3),
        in_specs=[pl.BlockSpec((1, TOK, BRANCH_WIDTH), lambda b, j: (b, 0, first + j)),
                  pl.BlockSpec((CONV_WIDTH, BRANCH_WIDTH), lambda b, j: (0, j))],
        out_specs=pl.BlockSpec((1, TOK, BRANCH_WIDTH), lambda b, j: (b, 0, j)),
        out_shape=jax.ShapeDtypeStruct((BATCH, TOK, 3 * BRANCH_WIDTH), BF16),
        compiler_params=_params("parallel", "parallel"),
        name="gdn_conv",
    )(p, conv_w)


GDN_QM_ROWS = GDN_CHUNK + GDN_HEAD_DIM
GDN_GL_ROWS = 8
GDN_HEADS_PER_STEP = 2
GDN_PREP_UNROLL = 12


def _gdn_prepare(q_ref, k_ref, v_ref, gb_ref, qm_scr, nn_scr, o_scr, gl_scr, chunks, head, local):
    c = GDN_CHUNK
    cols = slice(local * LANES, (local + 1) * LANES)
    lane = lax.broadcasted_iota(jnp.int32, (c, LANES), 1)
    row = lax.broadcasted_iota(jnp.int32, (c, LANES), 0)
    ii = lax.broadcasted_iota(jnp.int32, (c, c), 0)
    jj = lax.broadcasted_iota(jnp.int32, (c, c), 1)
    eye = (ii == jj).astype(F32)
    masks =((ii >= jj, ii > jj, row > lane), (ii <= jj, ii < jj, row < lane))

    loaded = []
    for chunk in chunks:
        r0 = pl.multiple_of(chunk * c, c)
        k = k_ref[0, pl.ds(r0, c), cols]
        q = q_ref[0, pl.ds(r0, c), cols]
        kq = _dot_nt(jnp.concatenate([k, q], axis=0), k)
        loaded.append((chunk, r0, q, k, kq))

    chains = []
    for chunk, r0, q, k, kq in loaded:
        gb = gb_ref[0, pl.ds(r0, c), :]
        for d in range(2):
            col = head + d * GDN_HEADS
            beta = jnp.sum(jnp.where(lane == col, gb, 0.0), axis=-1, keepdims=True)
            g = jnp.sum(jnp.where(lane == col + 2 * GDN_HEADS, gb, 0.0), axis=-1, keepdims=True)
            incl, strict, strict_wide = masks[d]
            rhs = jnp.where(lane >= c, g, jnp.where(strict_wide, g, 0.0))
            mask = incl.astype(BF16)
            hi = rhs.astype(BF16)
            rest = rhs - hi.astype(F32)
            mid = rest.astype(BF16)
            low = (rest - mid.astype(F32)).astype(BF16)
            e = _dot(mask, hi) + _dot(mask, mid) + _dot(mask, low)
            chains.append(dict(chunk=chunk, r0=r0, d=d, q=q, k=k, kq=kq, beta=beta, e=e))

    for ch in chains:
        incl, strict, _ = masks[ch["d"]]
        e = ch["e"]
        decay = jnp.where(incl, jnp.exp(e[:, :c]), 0.0)
        gc = e[:, c:c + 1]
        last = 0 if ch["d"] == 1 else c - 1
        gc_last = e[last:last + 1, c:c + 1]
        ch.update(decay=decay, gc=gc, gc_last=gc_last, egc=jnp.exp(gc))
        ch["a"] = jnp.where(strict, ch["beta"] * ch["kq"][:c] * decay, 0.0)
        ch["t"] = eye
    s = 1
    while s < c:
        pair = jnp.logical_and((ii // (2 * s)) == (jj // (2 * s)), (ii // s) != (jj // s))
        for ch in chains:
            ch["a_off"] = jnp.where(pair, ch["a"], 0.0)
        if s == 1:
            for ch in chains:
                ch["t"] = eye - ch["a_off"]
        else:
            for ch in chains:
                ch["m"] = _dot(ch["t"].astype(BF16), ch["a_off"].astype(BF16))
            for ch in chains:
                ch["t"] = ch["t"] - _dot(ch["m"].astype(BF16), ch["t"].astype(BF16))
        s *= 2
    for ch in chains:
        r0, beta, egc = ch["r0"], ch["beta"], ch["egc"]
        kf = ch["k"].astype(F32)
        vf = v_ref[0, pl.ds(r0, c), cols].astype(F32)
        rhs2 = jnp.concatenate([vf * beta, kf * (beta * egc)], axis=1).astype(BF16)
        ch["uw"] = _dot(ch["t"].astype(BF16), rhs2).astype(BF16)
        ch["k_dec_t"] = (kf * jnp.exp(ch["gc_last"] - ch["gc"])).T.astype(BF16)
    for ch in chains:
        incl = masks[ch["d"]][0]
        qk = jnp.where(incl, ch["kq"][c:] * ch["decay"], 0.0).astype(BF16)
        ch["nm"] = _dot(ch["k_dec_t"], ch["uw"])
        ch["ow"] = _dot(qk, ch["uw"])
    for ch in chains:
        chunk, r0, nm, ow = ch["chunk"], ch["r0"], ch["nm"], ch["ow"]
        s = 2 * local + ch["d"]
        q0 = pl.multiple_of(chunk * GDN_QM_ROWS, 16)
        qm_scr[s, pl.ds(q0, c), :] = (ch["q"].astype(F32) * ch["egc"] - ow[:, GDN_HEAD_DIM:]).astype(BF16)
        qm_scr[s, pl.ds(q0 + c, GDN_HEAD_DIM), :] = nm[:, GDN_HEAD_DIM:].astype(BF16)
        nn_scr[s, pl.ds(pl.multiple_of(chunk * GDN_HEAD_DIM, GDN_HEAD_DIM), GDN_HEAD_DIM), :] = nm[:, :GDN_HEAD_DIM]
        o_scr[s, pl.ds(r0, c), :] = ow[:, :GDN_HEAD_DIM]
        gl_scr[s, pl.ds(pl.multiple_of(chunk * GDN_GL_ROWS, GDN_GL_ROWS), GDN_GL_ROWS), :] = jnp.broadcast_to(
            jnp.exp(ch["gc_last"]), (GDN_GL_ROWS, LANES))


def _gdn_advance(qm_scr, nn_scr, o_scr, gl_scr, d, chunk, state):
    c = GDN_CHUNK
    qm = qm_scr[d, pl.ds(pl.multiple_of(chunk * GDN_QM_ROWS, 16), GDN_QM_ROWS), :]
    r = _dot(qm, state.astype(BF16))
    rows = pl.ds(pl.multiple_of(chunk * c, c), c)
    o_scr[d, rows, :] = o_scr[d, rows, :] + r[:c]
    gl = gl_scr[d, pl.ds(pl.multiple_of(chunk * GDN_GL_ROWS, GDN_GL_ROWS), 1), :]
    n = nn_scr[d, pl.ds(pl.multiple_of(chunk * GDN_HEAD_DIM, GDN_HEAD_DIM), GDN_HEAD_DIM), :]
    return state * gl - r[c:] + n


def _gdn_kernel(q_ref, k_ref, v_ref, z_ref, gb_ref, gain_ref, o_ref, qm_scr, nn_scr, o_scr, gl_scr):
    first_head = pl.program_id(1) * GDN_HEADS_PER_STEP
    c = GDN_CHUNK
    n_lat, n_ctx = SEQ // c, CTX_LEN // c
    n_chunks = n_lat + n_ctx

    for local in range(GDN_HEADS_PER_STEP):
        def prepare(i, _, local=local):
            chunks = [i * GDN_PREP_UNROLL + j for j in range(GDN_PREP_UNROLL)]
            _gdn_prepare(q_ref, k_ref, v_ref, gb_ref, qm_scr, nn_scr, o_scr, gl_scr, chunks, first_head + local, local)
            return 0

        lax.fori_loop(0, n_chunks // GDN_PREP_UNROLL, prepare, 0)

    def advance(i, states):
        cf = jnp.where(i < n_ctx, n_lat + i, i - n_ctx)
        cb = n_chunks - 1 - i
        return tuple(_gdn_advance(qm_scr, nn_scr, o_scr, gl_scr, s, cb if s % 2 else cf, state)
                     for s, state in enumerate(states))

    zero = jnp.zeros((GDN_HEAD_DIM, GDN_HEAD_DIM), F32)
    lax.fori_loop(0, n_chunks, advance, (zero,) * (2 * GDN_HEADS_PER_STEP))
    for local in range(GDN_HEADS_PER_STEP):
        cols = slice(local * LANES, (local + 1) * LANES)
        o = o_scr[2 * local] + o_scr[2 * local + 1]
        o_ref[0, :, cols] = (_rms(o, gain_ref[...]) * _silu(z_ref[0, :, cols].astype(F32))).astype(BF16)


def _gdn(qkv, p, gates, out_gain):
    n_steps = GDN_HEADS // GDN_HEADS_PER_STEP
    n_chunks = TOK // GDN_CHUNK
    width = GDN_HEADS_PER_STEP * LANES
    zblk = COL_Z // width
    n_chain = 2 * GDN_HEADS_PER_STEP
    blk = lambda off: pl.BlockSpec((1, TOK, width), lambda b, h: (b, 0, off + h))
    vec = pl.BlockSpec((1, LANES), lambda b, h: (0, 0))
    return pl.pallas_call(
        _gdn_kernel,
        grid=(BATCH, n_steps),
        in_specs=[blk(0), blk(n_steps), blk(2 * n_steps), blk(zblk),
                  pl.BlockSpec((1, TOK, LANES), lambda b, h: (b, 0, 0)), vec],
        out_specs=pl.BlockSpec((1, TOK, width), lambda b, h: (b, 0, h)),
        out_shape=jax.ShapeDtypeStruct((BATCH, TOK, BRANCH_WIDTH), BF16),
        scratch_shapes=[pltpu.VMEM((n_chain, n_chunks * GDN_QM_ROWS, LANES), BF16),
                        pltpu.VMEM((n_chain, n_chunks * GDN_HEAD_DIM, LANES), F32),
                        pltpu.VMEM((n_chain, TOK, LANES), F32),
                        pltpu.VMEM((n_chain, n_chunks * GDN_GL_ROWS, LANES), F32)],
        compiler_params=_params("parallel", "parallel"),
        name="gdn",
    )(qkv, qkv, qkv, p, gates, out_gain.reshape(1, LANES))


def _rms_halves(x, gain):
    lane = lax.broadcasted_iota(jnp.int32, x.shape, 1)
    lo = lane < DIFF_HEAD_DIM
    x2 = x * x
    s_lo = jnp.sum(jnp.where(lo, x2, 0.0), axis=-1, keepdims=True)
    s_hi = jnp.sum(jnp.where(lo, 0.0, x2), axis=-1, keepdims=True)
    ms = jnp.where(lo, s_lo, s_hi) * (1.0 / DIFF_HEAD_DIM)
    return x * lax.rsqrt(ms + EPS) * gain


def _rope(x, cos, sin):
    lane = lax.broadcasted_iota(jnp.int32, x.shape, 1)
    first = (lane & ROPE_PAIRS) == 0
    partner = jnp.where(first, -pltpu.roll(x, LANES - ROPE_PAIRS, 1), pltpu.roll(x, ROPE_PAIRS, 1))
    return x * cos + partner * sin


ATTN_Q_BLOCK = 1024
ATTN_GROUP_ROWS = 128


def _attn_kernel(q_ref, k_ref, v_ref, cosk_ref, sink_ref, cosq_ref, sinq_ref, qg_ref, kg_ref, lv_ref, og_ref,
                 o_ref, kn_scr, *, ctx_block, lam_init):
    qi = pl.program_id(2)

    @pl.when(qi == 0)
    def _():
        kn = _rope(_rms_halves(k_ref[0].astype(F32), kg_ref[...]), cosk_ref[...], sink_ref[...])
        kn_scr[...] = kn.astype(BF16)

    lv = lv_ref[...]
    lam = (jnp.exp(jnp.sum(lv[0:1] * lv[1:2], axis=-1, keepdims=True))
           - jnp.exp(jnp.sum(lv[2:3] * lv[3:4], axis=-1, keepdims=True)) + lam_init)
    def attend(n_rows, kn, v):
        q = _rope(_rms_halves(q_ref[0, :n_rows, :].astype(F32), qg_ref[...]), cosq_ref[:n_rows, :], sinq_ref[:n_rows, :])
        q = q * (DIFF_HEAD_DIM ** -0.5 * math.log2(math.e))
        lane = lax.broadcasted_iota(jnp.int32, q.shape, 1)
        lo = lane < DIFF_HEAD_DIM
        q1 = jnp.where(lo, q, 0.0).astype(BF16)
        q2 = jnp.where(lo, 0.0, q).astype(BF16)

        def half(s):
            p = jnp.exp2(s - jnp.max(s, axis=-1, keepdims=True))
            return _dot(p.astype(BF16), v), jnp.sum(p, axis=-1, keepdims=True)
        rows = ATTN_GROUP_ROWS
        scores = [(_dot_nt(q1[r:r + rows], kn), _dot_nt(q2[r:r + rows], kn)) for r in range(0, n_rows, rows)]
        for g, (s1, s2) in enumerate(scores):
            a1, l1 = half(s1)
            a2, l2 = half(s2)
            o = a1 * (1.0 / l1) - a2 * (lam / l2)
            o_ref[0, g * rows:(g + 1) * rows, :] = (_rms(o, og_ref[...]) * (1.0 - lam_init)).astype(BF16)

    if ctx_block is None:
        attend(q_ref.shape[1], kn_scr[...], v_ref[0])
    else:
        @pl.when(qi == ctx_block)
        def _():
            attend(CTX_LEN, kn_scr[SEQ:, :], v_ref[0, SEQ:, :])

        @pl.when(qi != ctx_block)
        def _():
            attend(q_ref.shape[1], kn_scr[...], v_ref[0])


def _rope_tables():
    n_rows = SEQ // GRID_W
    row_id = jnp.broadcast_to(jnp.arange(n_rows, dtype=F32)[:, None], (n_rows, GRID_W)).reshape(-1)
    col_id = jnp.broadcast_to(jnp.arange(GRID_W, dtype=F32)[None, :], (n_rows, GRID_W)).reshape(-1)
    inv_freq = jnp.power(ROPE_BASE, -jnp.arange(ROPE_PAIRS, dtype=F32) / ROPE_PAIRS)
    row_ang = row_id[:, None] * inv_freq
    col_ang = col_id[:, None] * inv_freq
    ang = jnp.concatenate([row_ang, row_ang, col_ang, col_ang], axis=-1)
    ang = jnp.concatenate([ang, ang], axis=-1)
    pad = ((0, CTX_LEN), (0, 0))
    return jnp.pad(jnp.cos(ang), pad, constant_values=1.0), jnp.pad(jnp.sin(ang), pad)


def _diff_attn(p, cos, sin, q_gain, k_gain, lam_vecs, out_gain, lam_init, with_ctx):
    tq = ATTN_Q_BLOCK
    n_rows = TOK if with_ctx else SEQ
    nq = pl.cdiv(n_rows, tq)
    nh = DIFF_HEADS
    qb, kb, vb = COL_DQ // LANES, COL_DK // LANES, COL_DV // LANES
    full = lambda off: pl.BlockSpec((1, TOK, LANES), lambda b, h, i: (b, 0, off + h))
    tab_full = pl.BlockSpec((TOK, LANES), lambda b, h, i: (0, 0))
    tab_q = pl.BlockSpec((tq, LANES), lambda b, h, i: (i, 0))
    vec = pl.BlockSpec((1, LANES), lambda b, h, i: (0, 0))
    tile2 = lambda g: jnp.concatenate([g, g]).reshape(1, LANES)
    return pl.pallas_call(
        functools.partial(_attn_kernel, ctx_block=SEQ // tq if with_ctx else None, lam_init=lam_init),
        grid=(BATCH, nh, nq),
        in_specs=[pl.BlockSpec((1, tq, LANES), lambda b, h, i: (b, i, qb + h)), full(kb), full(vb),
                  tab_full, tab_full, tab_q, tab_q, vec, vec,
                  pl.BlockSpec((4, DIFF_HEAD_DIM), lambda b, h, i: (0, 0)), vec],
        out_specs=pl.BlockSpec((1, tq, LANES), lambda b, h, i: (b, i, h)),
        out_shape=jax.ShapeDtypeStruct((BATCH, n_rows, BRANCH_WIDTH), BF16),
        scratch_shapes=[pltpu.VMEM((TOK, LANES), BF16)],
        compiler_params=_params("parallel", "parallel", "arbitrary"),
        name="diff_attn",
    )(p, p, p, cos, sin, cos, sin, tile2(q_gain), tile2(k_gain), lam_vecs, out_gain.reshape(1, LANES))


LRU_SLAB = 256
LRU_ROWS = 256
LRU_SCAN_BLOCK = 8


def _lru_kernel(x_ref, y_ref, cw_ref, cb_ref, wg_ref, bg_ref, lam_ref, o_ref, xc_scr, af_scr, bf_scr, ab_scr, bb_scr):
    w = LRU_SLAB
    blk = LRU_SCAN_BLOCK
    for c0 in range(0, w, LANES):
        cols = slice(c0, c0 + LANES)
        xc_scr[:, cols] = _conv(x_ref[0, :, cols].astype(F32), cw_ref[:, cols]) + cb_ref[:, cols]
    sp = _softplus(-lam_ref[0])
    sub = lax.broadcasted_iota(jnp.int32, (LRU_ROWS // blk, blk, w), 1)

    def gates(i, _):
        r0 = pl.multiple_of(i * LRU_ROWS, LRU_ROWS)
        xc = xc_scr[pl.ds(r0, LRU_ROWS), :]
        pre = _dot(xc.astype(BF16), wg_ref[0]) + bg_ref[0]
        for d, (a_scr, b_scr) in enumerate(((af_scr, bf_scr), (ab_scr, bb_scr))):
            r = _sigmoid(pre[:, (2 * d) * w:(2 * d + 1) * w])
            gi = _sigmoid(pre[:, (2 * d + 1) * w:(2 * d + 2) * w])
            log_a = -LRU_C * r * sp[d:d + 1]
            a = jnp.exp(log_a)
            b = jnp.sqrt(1.0 - a * a) * gi * xc
            a = a.reshape(LRU_ROWS // blk, blk, w)
            b = b.reshape(LRU_ROWS // blk, blk, w)
            shift = 1
            while shift < blk:
                if d == 0:
                    ok, roll_by = sub >= shift, shift
                else:
                    ok, roll_by = sub < blk - shift, blk - shift
                b = jnp.where(ok, a * pltpu.roll(b, roll_by, 1) + b, b)
                a = jnp.where(ok, a * pltpu.roll(a, roll_by, 1), a)
                shift *= 2
            a_scr[pl.ds(r0, LRU_ROWS), :] = a.reshape(LRU_ROWS, w)
            b_scr[pl.ds(r0, LRU_ROWS), :] = b.reshape(LRU_ROWS, w)
        return 0

    lax.fori_loop(0, TOK // LRU_ROWS, gates, 0)

    n_blk, n_lat_blk, n_ctx_blk = TOK // blk, SEQ // blk, CTX_LEN // blk

    def step(s, carry):
        h_f, h_b = carry
        rows_f = pl.ds(pl.multiple_of(jnp.where(s < n_ctx_blk, n_lat_blk + s, s - n_ctx_blk) * blk, blk), blk)
        rows_b = pl.ds(pl.multiple_of((n_blk - 1 - s) * blk, blk), blk)
        hf = af_scr[rows_f, :] * h_f + bf_scr[rows_f, :]
        bf_scr[rows_f, :] = hf
        hb = ab_scr[rows_b, :] * h_b + bb_scr[rows_b, :]
        bb_scr[rows_b, :] = hb
        return hf[blk - 1:blk, :], hb[0:1, :]

    zero = jnp.zeros((1, w), F32)
    lax.fori_loop(0, n_blk, step, (zero, zero), unroll=4)
    h = bf_scr[...] + bb_scr[...]
    o_ref[0] = (h * jax.nn.gelu(y_ref[0].astype(F32))).astype(BF16)


def _lru_gate_weights(w_gate, b_gate):
    n_slab = BRANCH_WIDTH // LRU_SLAB
    per = LRU_SLAB // LRU_BLOCK_DIM
    wg = w_gate.reshape(2, 2, n_slab, per, LRU_BLOCK_DIM, LRU_BLOCK_DIM)
    eye = jnp.eye(per, dtype=w_gate.dtype)
    dense = jnp.einsum('dgsnjk,nm->snjdgmk', wg, eye)
    dense = dense.reshape(n_slab, LRU_SLAB, 4 * LRU_SLAB)
    bg = b_gate.reshape(2, 2, n_slab, LRU_SLAB).transpose(2, 0, 1, 3).reshape(n_slab, 1, 4 * LRU_SLAB)
    return dense.astype(BF16), bg


def _lru(p, conv_w, conv_b, w_gate, b_gate, lam):
    n_slab = BRANCH_WIDTH // LRU_SLAB
    wg, bg = _lru_gate_weights(w_gate, b_gate)
    lam_s = lam.reshape(2, n_slab, LRU_SLAB).transpose(1, 0, 2)
    xb, yb = COL_LX // LRU_SLAB, COL_LY // LRU_SLAB
    return pl.pallas_call(
        _lru_kernel,
        grid=(BATCH, n_slab),
        in_specs=[pl.BlockSpec((1, TOK, LRU_SLAB), lambda b, s: (b, 0, xb + s)),
                  pl.BlockSpec((1, TOK, LRU_SLAB), lambda b, s: (b, 0, yb + s)),
                  pl.BlockSpec((CONV_WIDTH, LRU_SLAB), lambda b, s: (0, s)),
                  pl.BlockSpec((1, LRU_SLAB), lambda b, s: (0, s)),
                  pl.BlockSpec((1, LRU_SLAB, 4 * LRU_SLAB), lambda b, s: (s, 0, 0)),
                  pl.BlockSpec((1, 1, 4 * LRU_SLAB), lambda b, s: (s, 0, 0)),
                  pl.BlockSpec((1, 2, LRU_SLAB), lambda b, s: (s, 0, 0))],
        out_specs=pl.BlockSpec((1, TOK, LRU_SLAB), lambda b, s: (b, 0, s)),
        out_shape=jax.ShapeDtypeStruct((BATCH, TOK, BRANCH_WIDTH), BF16),
        scratch_shapes=[pltpu.VMEM((TOK, LRU_SLAB), F32)] * 5,
        compiler_params=_params("parallel", "parallel"),
        name="lru",
    )(p, p, conv_w, conv_b.reshape(1, BRANCH_WIDTH), wg, bg, lam_s)


def _route(logits):
    lane = lax.broadcasted_iota(jnp.int32, logits.shape, 1)
    lane_f = lane.astype(F32)
    far = float(LANES)
    lg = jnp.where(lane < N_EXPERTS, logits, -jnp.inf)
    ex = jnp.exp(lg - jnp.max(lg, axis=-1, keepdims=True))
    probs = ex / jnp.sum(ex, axis=-1, keepdims=True)
    per_group = N_EXPERTS // N_GROUPS
    grp = lane // per_group

    def top2(vals):
        m1 = jnp.max(vals, axis=-1, keepdims=True)
        i1 = jnp.min(jnp.where(vals == m1, lane_f, far), axis=-1, keepdims=True)
        rest = jnp.where(lane_f == i1, -2.0, vals)
        m2 = jnp.max(rest, axis=-1, keepdims=True)
        i2 = jnp.min(jnp.where(rest == m2, lane_f, far), axis=-1, keepdims=True)
        return m1, i1, m2, i2

    best = jnp.zeros(logits.shape[:1] + (1,), jnp.int32)
    best_score = None
    for g in range(N_GROUPS):
        m1, _, m2, _ = top2(jnp.where(grp == g, probs, -1.0))
        score = m1 + m2
        if best_score is None:
            best_score = score
        else:
            better = score > best_score
            best = jnp.where(better, g, best)
            best_score = jnp.where(better, score, best_score)
    m1, i1, m2, i2 = top2(jnp.where(grp == best, probs, -1.0))
    den = m1 + m2
    return i1, i2, m1 / den, m2 / den


ROUTE_E, ROUTE_W, ROUTE_RANK = 0, 2, 4


def _pack_bf16_pairs(x):
    n = x.shape[1] // 2
    xb = x.astype(BF16).astype(F32)
    lo = pltpu.bitcast(xb[:, :n], jnp.uint32)
    hi = pltpu.bitcast(xb[:, n:], jnp.uint32)
    return (lo >> 16) | (hi & jnp.uint32(0xFFFF0000))


def _unpack_bf16_pairs(p):
    lo = pltpu.bitcast(p << 16, F32)
    hi = pltpu.bitcast(p & jnp.uint32(0xFFFF0000), F32)
    return jnp.concatenate([lo, hi], axis=1)


def _merge_kernel(ya_ref, yb_ref, yc_ref, gates_ref, x_ref, ml_ref, mc_ref, g2_ref, wbr_ref, wout_ref, wr_ref,
                  br_ref, xo_ref, h2_ref, route_ref, cnt_ref, cnt_scr, *, tm):
    i = pl.program_id(1)

    @pl.when(jnp.logical_and(pl.program_id(0) == 0, i == 0))
    def _():
        cnt_scr[...] = jnp.zeros_like(cnt_scr)

    acc = None
    for n, y_ref in enumerate((ya_ref, yb_ref, yc_ref)):
        yn = _dot(y_ref[0], wbr_ref[n])
        gate = _sigmoid(gates_ref[0, :, n * D_MODEL:(n + 1) * D_MODEL].astype(F32))
        acc = gate * yn if acc is None else acc + gate * yn
    out = _dot(acc.astype(BF16), wout_ref[...])
    is_ctx = _ctx_rows(i, tm, D_MODEL)
    xn = x_ref[0] + jnp.where(is_ctx, mc_ref[0, 2:3, :], ml_ref[0, 2:3, :]) * out
    xo_ref[0] = xn
    h2 = _modulated_norm(xn, g2_ref[...], ml_ref, mc_ref, is_ctx, 3)
    h2_ref[0] = _pack_bf16_pairs(h2)
    i1, i2, w1, w2 = _route(_dot(h2.astype(BF16), wr_ref[...]) + br_ref[...])
    lane = lax.broadcasted_iota(jnp.int32, (tm, LANES), 1)
    lane_f = lane.astype(F32)
    chosen = jnp.where(jnp.logical_or(lane_f == i1, lane_f == i2), 1.0, 0.0)
    earlier = (lax.broadcasted_iota(jnp.int32, (tm, tm), 0) > lax.broadcasted_iota(jnp.int32, (tm, tm), 1))
    before = _dot(jnp.where(earlier, 1.0, 0.0).astype(BF16), chosen.astype(BF16)) + cnt_scr[...]
    rank1 = jnp.sum(jnp.where(lane_f == i1, before, 0.0), axis=-1, keepdims=True)
    rank2 = jnp.sum(jnp.where(lane_f == i2, before, 0.0), axis=-1, keepdims=True)
    cnt_scr[...] += jnp.sum(chosen, axis=0, keepdims=True)
    cnt_ref[...] = cnt_scr[...]
    record = jnp.zeros((tm, LANES), F32)
    for pos, val in enumerate((i1, i2, w1, w2, rank1, rank2)):
        record = jnp.where(lane == pos, val, record)
    route_ref[0] = record


def _merge(ya, yb, yc, p, xs, mods, gain2, w_branch, w_out, w_router, b_router, rows, tm):
    wr = jnp.zeros((D_MODEL, LANES), BF16).at[:, :N_EXPERTS].set(w_router.astype(BF16))
    br = jnp.zeros((1, LANES), F32).at[0, :N_EXPERTS].set(b_router)
    tile = lambda w: pl.BlockSpec((1, tm, w), lambda b, i: (b, i, 0))
    const = lambda shape: pl.BlockSpec(shape, lambda b, i: (0,) * len(shape))
    return pl.pallas_call(
        functools.partial(_merge_kernel, tm=tm),
        grid=(BATCH, rows // tm),
        in_specs=[tile(BRANCH_WIDTH), tile(BRANCH_WIDTH), tile(BRANCH_WIDTH), tile(3 * D_MODEL), tile(D_MODEL),
                  pl.BlockSpec((1, N_MOD, D_MODEL), lambda b, i: (b, 0, 0)),
                  pl.BlockSpec((1, N_MOD, D_MODEL), lambda b, i: (BATCH, 0, 0)),
                  const((1, D_MODEL)), const((3, BRANCH_WIDTH, D_MODEL)), const((D_MODEL, D_MODEL)),
                  const((D_MODEL, LANES)), const((1, LANES))],
        out_specs=[tile(D_MODEL), tile(D_MODEL // 2), tile(LANES), const((1, LANES))],
        out_shape=[jax.ShapeDtypeStruct((BATCH, rows, D_MODEL), F32),
                   jax.ShapeDtypeStruct((BATCH, rows, D_MODEL // 2), jnp.uint32),
                   jax.ShapeDtypeStruct((BATCH, rows, LANES), F32),
                   jax.ShapeDtypeStruct((1, LANES), F32)],
        scratch_shapes=[pltpu.VMEM((1, LANES), F32)],
        compiler_params=_params("arbitrary", "arbitrary"),
        name="merge",
    )(ya, yb, yc, p, xs, mods, mods, gain2.reshape(1, D_MODEL), w_branch, w_out, wr, br)


MOE_TILE = 512
SC_GATHER_ROWS = 64


def _sc_gather(table, idx):
    info = plsc.get_sparse_core_info()
    n_workers = info.num_cores * info.num_subcores
    n_rows, width = idx.shape[0], table.shape[1]
    per_worker = n_rows // n_workers
    assert per_worker * n_workers == n_rows and per_worker % SC_GATHER_ROWS == 0
    mesh = plsc.VectorSubcoreMesh(core_axis_name="c", subcore_axis_name="s")

    @functools.partial(
        pl.kernel, mesh=mesh, out_type=jax.ShapeDtypeStruct((n_rows, width), table.dtype),
        scratch_types=[pltpu.VMEM((SC_GATHER_ROWS,), jnp.int32),
                       pltpu.VMEM((SC_GATHER_ROWS, width), table.dtype),
                       pltpu.SemaphoreType.DMA],
        name="sc_gather")
    def gather(table_hbm, idx_hbm, out_hbm, idx_v, rows_v, sem):
        worker = lax.axis_index("s") * info.num_cores + lax.axis_index("c")
        base = worker * per_worker

        @pl.loop(0, per_worker // SC_GATHER_ROWS)
        def _(j):
            off = base + j * SC_GATHER_ROWS
            pltpu.sync_copy(idx_hbm.at[pl.ds(off, SC_GATHER_ROWS)], idx_v)
            pltpu.async_copy(table_hbm.at[idx_v], rows_v, sem).wait()
            pltpu.sync_copy(rows_v, out_hbm.at[pl.ds(off, SC_GATHER_ROWS)])

    return gather(table, idx)


SC_SCATTER_ROWS = 128


def _sc_scatter(table, dest, n_out):
    info = plsc.get_sparse_core_info()
    n_workers = info.num_cores * info.num_subcores
    n_tok, width = table.shape
    n_assign = dest.shape[0]
    per_worker = n_assign // n_workers
    assert per_worker * n_workers == n_assign and per_worker % SC_SCATTER_ROWS == 0 and n_tok % SC_SCATTER_ROWS == 0
    mesh = plsc.VectorSubcoreMesh(core_axis_name="c", subcore_axis_name="s")

    @functools.partial(
        pl.kernel, mesh=mesh, out_type=jax.ShapeDtypeStruct((n_out, width), table.dtype),
        scratch_types=[pltpu.VMEM((SC_SCATTER_ROWS,), jnp.int32),
                       pltpu.VMEM((SC_SCATTER_ROWS, width), table.dtype),
                       pltpu.SemaphoreType.DMA],
        name="sc_scatter")
    def scatter(table_hbm, dest_hbm, out_hbm, idx_v, rows_v, sem):
        worker = lax.axis_index("s") * info.num_cores + lax.axis_index("c")
        base = worker * per_worker

        @pl.loop(0, per_worker // SC_SCATTER_ROWS)
        def _(j):
            off = base + j * SC_SCATTER_ROWS
            pltpu.sync_copy(dest_hbm.at[pl.ds(off, SC_SCATTER_ROWS)], idx_v)
            pltpu.sync_copy(table_hbm.at[pl.ds(lax.rem(off, n_tok), SC_SCATTER_ROWS)], rows_v)
            pltpu.async_copy(rows_v, out_hbm.at[idx_v], sem).wait()

    return scatter(table, dest)


def _experts_kernel(tile_expert_ref, n_valid_ref, x_ref, wgu_ref, wd_ref, o_ref, wgu_scr, wd_scr):
    j = pl.program_id(0)
    valid = j < n_valid_ref[0]
    fresh = jnp.logical_or(j == 0, tile_expert_ref[j] != tile_expert_ref[jnp.maximum(j - 1, 0)])

    @pl.when(jnp.logical_and(valid, fresh))
    def _():
        wgu_scr[...] = wgu_ref[0, 0].astype(BF16)
        wd_scr[...] = wd_ref[0, 0].astype(BF16)

    @pl.when(valid)
    def _():
        x = _unpack_bf16_pairs(x_ref[...]).astype(BF16)
        gu = _dot(x, wgu_scr[...])
        act = (_silu(gu[:, :EXPERT_FF]) * gu[:, EXPERT_FF:]).astype(BF16)
        o_ref[...] = _pack_bf16_pairs(_dot(act, wd_scr[...]))

    @pl.when(jnp.logical_not(valid))
    def _():
        o_ref[...] = jnp.zeros_like(o_ref)


def _experts(x_sorted, tile_expert, n_valid, w_gate_up, w_down, layer):
    n_tiles = x_sorted.shape[0] // MOE_TILE
    half = D_MODEL // 2
    return pl.pallas_call(
        _experts_kernel,
        grid_spec=pltpu.PrefetchScalarGridSpec(
            num_scalar_prefetch=2, grid=(n_tiles,),
            in_specs=[pl.BlockSpec((MOE_TILE, half), lambda j, te, nv: (j, 0)),
                      pl.BlockSpec((1, 1, D_MODEL, 2 * EXPERT_FF), lambda j, te, nv: (layer, te[j], 0, 0)),
                      pl.BlockSpec((1, 1, EXPERT_FF, D_MODEL), lambda j, te, nv: (layer, te[j], 0, 0))],
            out_specs=pl.BlockSpec((MOE_TILE, half), lambda j, te, nv: (j, 0)),
            scratch_shapes=[pltpu.VMEM((D_MODEL, 2 * EXPERT_FF), BF16), pltpu.VMEM((EXPERT_FF, D_MODEL), BF16)]),
        out_shape=jax.ShapeDtypeStruct((x_sorted.shape[0], half), jnp.uint32),
        compiler_params=_params("arbitrary"),
        name="experts",
    )(tile_expert, n_valid, x_sorted, w_gate_up, w_down)


def _combine_kernel(y1_ref, y2_ref, route_ref, x_ref, ml_ref, mc_ref, o_ref, *, tm):
    i = pl.program_id(1)
    route = route_ref[0]
    w1 = route[:, ROUTE_W:ROUTE_W + 1]
    w2 = route[:, ROUTE_W + 1:ROUTE_W + 2]
    moe = w1 * _unpack_bf16_pairs(y1_ref[0, 0]) + w2 * _unpack_bf16_pairs(y2_ref[0, 0])
    is_ctx = _ctx_rows(i, tm, D_MODEL)
    o_ref[0] = x_ref[0] + jnp.where(is_ctx, mc_ref[0, 5:6, :], ml_ref[0, 5:6, :]) * moe


def _combine(y_pairs, route, xs, mods, rows, tm):
    half = D_MODEL // 2
    tile = lambda w: pl.BlockSpec((1, tm, w), lambda b, i: (b, i, 0))
    slot = lambda s: pl.BlockSpec((1, 1, tm, half), lambda b, i: (s, b, i, 0))
    return pl.pallas_call(
        functools.partial(_combine_kernel, tm=tm),
        grid=(BATCH, rows // tm),
        in_specs=[slot(0), slot(1), tile(LANES), tile(D_MODEL),
                  pl.BlockSpec((1, N_MOD, D_MODEL), lambda b, i: (b, 0, 0)),
                  pl.BlockSpec((1, N_MOD, D_MODEL), lambda b, i: (BATCH, 0, 0))],
        out_specs=tile(D_MODEL),
        out_shape=jax.ShapeDtypeStruct((BATCH, rows, D_MODEL), F32),
        compiler_params=_params("parallel", "parallel"),
        name="combine",
    )(y_pairs, y_pairs, route, xs, mods, mods)


def _moe(h2, route, counts, xs, mods, w_gate_up, w_down, layer, rows, tm):
    n_tok = BATCH * rows
    half = D_MODEL // 2
    n_sorted = 2 * n_tok + N_EXPERTS * MOE_TILE
    n_tiles = n_sorted // MOE_TILE
    rec = route.reshape(n_tok, LANES)
    expert = rec[:, ROUTE_E:ROUTE_E + 2].astype(jnp.int32)
    rank = rec[:, ROUTE_RANK:ROUTE_RANK + 2].astype(jnp.int32)
    count = counts[0, :N_EXPERTS].astype(jnp.int32)
    padded = (count + MOE_TILE - 1) // MOE_TILE * MOE_TILE
    end = jnp.cumsum(padded)
    start = end - padded
    first = jnp.sum(jnp.where(expert[:, :1] == jnp.arange(N_EXPERTS), start, 0), axis=1)
    second = jnp.sum(jnp.where(expert[:, 1:] == jnp.arange(N_EXPERTS), start, 0), axis=1)
    dest = jnp.concatenate([first + rank[:, 0], second + rank[:, 1]])
    tile_start = jnp.arange(n_tiles, dtype=jnp.int32) * MOE_TILE
    tile_expert = jnp.minimum(jnp.sum(tile_start[:, None] >= end[None, :], axis=1), N_EXPERTS - 1).astype(jnp.int32)
    n_valid = (end[-1:] // MOE_TILE).astype(jnp.int32)
    x_sorted = _sc_scatter(h2.reshape(n_tok, half), dest, n_sorted)
    y_sorted = _experts(x_sorted, tile_expert, n_valid, w_gate_up, w_down, layer)
    y_pairs = _sc_gather(y_sorted, dest).reshape(2, BATCH, rows, half)
    return _combine(y_pairs, route, xs, mods, rows, tm)


def _split_w_in(w_in):
    bw = BRANCH_WIDTH
    sizes = (3 * bw, bw, 2 * GDN_HEADS, 2 * GDN_HEADS, bw, bw, bw, bw, bw, 3 * D_MODEL)
    offs = [0]
    for s in sizes:
        offs.append(offs[-1] + s)
    part = lambda i: w_in[:, :, offs[i]:offs[i + 1]]
    main = jnp.concatenate([part(9), part(0), part(1), part(4), part(5), part(6), part(7), part(8)], axis=2)
    ba = jnp.zeros(w_in.shape[:2] + (LANES,), F32).at[:, :, :4 * GDN_HEADS].set(
        jnp.concatenate([part(2), part(3)], axis=2))
    return main.astype(BF16), ba.astype(BF16)


def kernel(x, c, ctx, c_ctx, w_mod, b_mod, norm1_gain, norm2_gain, w_in, gdn_conv_w, gdn_a_log, gdn_dt_bias, gdn_out_gain, diff_q_gain, diff_k_gain, diff_lambda, diff_out_gain, lru_conv_w, lru_conv_b, lru_w_gate, lru_b_gate, lru_lambda, w_branch, w_out, w_router, b_router, w_gate_up, w_down):
    mods = _mods(c, c_ctx, w_mod, b_mod)
    cos, sin = _rope_tables()
    xs = jnp.concatenate([x, ctx], axis=1)
    w_main, w_ba = _split_w_in(w_in)
    for layer in range(DEPTH):
        last = layer == DEPTH - 1
        lam_init = 0.8 - 0.6 * math.exp(-0.3 * layer)
        m = mods[layer]
        p, gates = _project(xs, m, norm1_gain[layer], w_main, w_ba, gdn_a_log[layer], gdn_dt_bias[layer], layer)
        qkv = _gdn_conv(p, gdn_conv_w[layer])
        ya = _gdn(qkv, p, gates, gdn_out_gain[layer])
        yb = _diff_attn(p, cos, sin, diff_q_gain[layer], diff_k_gain[layer], diff_lambda[layer],
                        diff_out_gain[layer], lam_init, with_ctx=not last)
        yc = _lru(p, lru_conv_w[layer], lru_conv_b[layer], lru_w_gate[layer], lru_b_gate[layer], lru_lambda[layer])
        rows, tm = (SEQ, 512) if last else (TOK, 768)
        xs, h2, route, counts = _merge(ya, yb, yc, p, xs, m, norm2_gain[layer], w_branch[layer].astype(BF16),
                                       w_out[layer].astype(BF16), w_router, b_router, rows, tm)
        xs = _moe(h2, route, counts, xs, m, w_gate_up, w_down, layer, rows, tm)
    return xs
```

```python
import functools
import math

import jax
import jax.numpy as jnp
from jax import lax
from jax.experimental import pallas as pl
from jax.experimental.pallas import tpu as pltpu
from jax.experimental.pallas import tpu_sc as plsc

F32 = jnp.float32
BF16 = jnp.bfloat16

D_MODEL = 1024
BATCH = 8
SEQ = 2048
DEPTH = 2
GRID_W = 64
CTX_LEN = 256
TOK = SEQ + CTX_LEN
N_MOD = 6
EPS = 1e-6
CONV_WIDTH = 4
BRANCH_WIDTH = 512
GDN_HEADS = 4
GDN_HEAD_DIM = 128
GDN_CHUNK = 64
DIFF_HEADS = 4
DIFF_HEAD_DIM = 64
ROPE_BASE = 10000.0
ROPE_PAIRS = DIFF_HEAD_DIM // 4
LRU_BLOCKS = 8
LRU_BLOCK_DIM = BRANCH_WIDTH // LRU_BLOCKS
LRU_C = 8.0
N_EXPERTS = 16
N_GROUPS = 4
EXPERT_FF = 512

LANES = 128
VMEM_LIMIT = 56 * 1024 * 1024

COL_GATES = 0
COL_QKV = 3 * D_MODEL
COL_Z = COL_QKV + 3 * BRANCH_WIDTH
COL_DQ = COL_Z + BRANCH_WIDTH
COL_DK = COL_DQ + BRANCH_WIDTH
COL_DV = COL_DK + BRANCH_WIDTH
COL_LX = COL_DV + BRANCH_WIDTH
COL_LY = COL_LX + BRANCH_WIDTH
PROJ_COLS = COL_LY + BRANCH_WIDTH


def _params(*sem):
    return pltpu.CompilerParams(dimension_semantics=sem, vmem_limit_bytes=VMEM_LIMIT)


def _dot(a, b, precision=None):
    return jnp.dot(a, b, preferred_element_type=F32, precision=precision)


def _dot_nt(a, b):
    return lax.dot_general(a, b, (((1,), (1,)), ((), ())), preferred_element_type=F32)


_sigmoid = jax.nn.sigmoid


def _silu(x):
    return x * _sigmoid(x)


def _softplus(x):
    return jnp.maximum(x, 0.0) + jnp.log(1.0 + jnp.exp(-jnp.abs(x)))


def _rms(x, gain):
    return x * lax.rsqrt(jnp.mean(x * x, axis=-1, keepdims=True) + EPS) * gain


def _mod_kernel(c_ref, w_ref, b_ref, o_ref):
    c = c_ref[...]
    o_ref[0] = _dot(_silu(c), w_ref[0], precision=lax.Precision.HIGHEST) + b_ref[0]


def _mods(c, c_ctx, w_mod, b_mod):
    depth = w_mod.shape[0]
    rows = 16
    cc = jnp.zeros((rows, D_MODEL), F32).at[:BATCH].set(c).at[BATCH].set(c_ctx)
    tn = 1536
    out = pl.pallas_call(
        _mod_kernel,
        grid=(depth, N_MOD * D_MODEL // tn),
        in_specs=[pl.BlockSpec((rows, D_MODEL), lambda l, j: (0, 0)),
                  pl.BlockSpec((1, D_MODEL, tn), lambda l, j: (l, 0, j)),
                  pl.BlockSpec((1, 1, tn), lambda l, j: (l, 0, j))],
        out_specs=pl.BlockSpec((1, rows, tn), lambda l, j: (l, 0, j)),
        out_shape=jax.ShapeDtypeStruct((depth, rows, N_MOD * D_MODEL), F32),
        compiler_params=_params("parallel", "parallel"),
        name="mods",
    )(cc, w_mod, b_mod.reshape(depth, 1, N_MOD * D_MODEL))
    return out.reshape(depth, rows, N_MOD, D_MODEL)


def _modulated_norm(x, gain, ml_ref, mc_ref, is_ctx, shift_idx):
    shift = jnp.where(is_ctx, mc_ref[0, shift_idx:shift_idx + 1, :], ml_ref[0, shift_idx:shift_idx + 1, :])
    scale = jnp.where(is_ctx, mc_ref[0, shift_idx + 1:shift_idx + 2, :], ml_ref[0, shift_idx + 1:shift_idx + 2, :])
    return _rms(x, gain) * (1.0 + scale) + shift


def _ctx_rows(tile, tm, width):
    row = tile * tm + lax.broadcasted_iota(jnp.int32, (tm, width), 0)
    return row >= SEQ


def _proj_kernel(x_ref, ml_ref, mc_ref, g_ref, w_ref, wba_ref, alog_ref, dtb_ref, p_ref, gb_ref, h_scr, *, tm):
    i = pl.program_id(1)
    j = pl.program_id(2)

    @pl.when(j == 0)
    def _():
        is_ctx = _ctx_rows(i, tm, D_MODEL)
        h = _modulated_norm(x_ref[0], g_ref[...], ml_ref, mc_ref, is_ctx, 0).astype(BF16)
        h_scr[...] = h
        ba = _dot(h, wba_ref[0])
        lane = lax.broadcasted_iota(jnp.int32, ba.shape, 1)
        log_decay = -jnp.exp(alog_ref[...]) * _softplus(ba + dtb_ref[...])
        gb_ref[0] = jnp.where(lane < 2 * GDN_HEADS, _sigmoid(ba), log_decay)

    p_ref[0] = _dot(h_scr[...], w_ref[0]).astype(BF16)


def _project(xs, mods, gain, w_main, w_ba, a_log, dt_bias, layer):
    def pad_lanes(vals):
        row = jnp.zeros((LANES,), F32).at[2 * GDN_HEADS:4 * GDN_HEADS].set(vals.reshape(-1))
        return row.reshape(1, LANES)

    vec = pl.BlockSpec((1, LANES), lambda b, i, j: (0, 0))
    tm, tn = 1152, 1280
    return pl.pallas_call(
        functools.partial(_proj_kernel, tm=tm),
        grid=(BATCH, TOK // tm, PROJ_COLS // tn),
        in_specs=[pl.BlockSpec((1, tm, D_MODEL), lambda b, i, j: (b, i, 0)),
                  pl.BlockSpec((1, N_MOD, D_MODEL), lambda b, i, j: (b, 0, 0)),
                  pl.BlockSpec((1, N_MOD, D_MODEL), lambda b, i, j: (BATCH, 0, 0)),
                  pl.BlockSpec((1, D_MODEL), lambda b, i, j: (0, 0)),
                  pl.BlockSpec((1, D_MODEL, tn), lambda b, i, j: (layer, 0, j)),
                  pl.BlockSpec((1, D_MODEL, LANES), lambda b, i, j: (layer, 0, 0)), vec, vec],
        out_specs=[pl.BlockSpec((1, tm, tn), lambda b, i, j: (b, i, j)),
                   pl.BlockSpec((1, tm, LANES), lambda b, i, j: (b, i, 0))],
        out_shape=[jax.ShapeDtypeStruct((BATCH, TOK, PROJ_COLS), BF16),
                   jax.ShapeDtypeStruct((BATCH, TOK, LANES), F32)],
        scratch_shapes=[pltpu.VMEM((tm, D_MODEL), BF16)],
        compiler_params=_params("parallel", "parallel", "arbitrary"),
        name="proj",
    )(xs, mods, mods, gain.reshape(1, D_MODEL), w_main, w_ba, pad_lanes(a_log), pad_lanes(dt_bias))


def _conv(x, w):
    n, c = x.shape
    t = lax.broadcasted_iota(jnp.int32, (n, c), 0)
    is_ctx = t >= SEQ
    local = jnp.where(is_ctx, t - SEQ, t)
    seg_len = jnp.where(is_ctx, CTX_LEN, SEQ)
    y = jnp.zeros_like(x)
    for j in range(CONV_WIDTH):
        s = j - CONV_WIDTH // 2
        if s == 0:
            y = y + x * w[j:j + 1, :]
        else:
            shifted = pltpu.roll(x, (-s) % n, 0)
            ok = jnp.logical_and(local + s >= 0, local + s < seg_len)
            y = y + jnp.where(ok, shifted, 0.0) * w[j:j + 1, :]
    return y


def _gdn_conv_kernel(x_ref, w_ref, o_ref):
    j = pl.program_id(1)
    mul = jnp.where(j == 0, float(GDN_HEAD_DIM), jnp.where(j == 1, 1.0, 0.0))
    add = jnp.where(j == 0, float(GDN_HEAD_DIM) * EPS, jnp.where(j == 1, EPS, 1.0))
    for h in range(GDN_HEADS):
        cols = slice(h * GDN_HEAD_DIM, (h + 1) * GDN_HEAD_DIM)
        y = _silu(_conv(x_ref[0, :, cols].astype(F32), w_ref[:, cols]))
        scale = lax.rsqrt(jnp.sum(y * y, axis=-1, keepdims=True) * mul + add)
        o_ref[0, :, cols] = (y * scale).astype(BF16)


def _gdn_conv(p, conv_w):
    first = COL_QKV // BRANCH_WIDTH
    return pl.pallas_call(
        _gdn_conv_kernel,
        grid=(BATCH, 3),
        in_specs=[pl.BlockSpec((1, TOK, BRANCH_WIDTH), lambda b, j: (b, 0, first + j)),
                  pl.BlockSpec((CONV_WIDTH, BRANCH_WIDTH), lambda b, j: (0, j))],
        out_specs=pl.BlockSpec((1, TOK, BRANCH_WIDTH), lambda b, j: (b, 0, j)),
        out_shape=jax.ShapeDtypeStruct((BATCH, TOK, 3 * BRANCH_WIDTH), BF16),
        compiler_params=_params("parallel", "parallel"),
        name="gdn_conv",
    )(p, conv_w)


GDN_QM_ROWS = GDN_CHUNK + GDN_HEAD_DIM
GDN_GL_ROWS = 8
GDN_HEADS_PER_STEP = 2
GDN_PREP_UNROLL = 12


def _gdn_prepare(q_ref, k_ref, v_ref, gb_ref, qm_scr, nn_scr, o_scr, gl_scr, chunks, head, local):
    c = GDN_CHUNK
    cols = slice(local * LANES, (local + 1) * LANES)
    lane = lax.broadcasted_iota(jnp.int32, (c, LANES), 1)
    row = lax.broadcasted_iota(jnp.int32, (c, LANES), 0)
    ii = lax.broadcasted_iota(jnp.int32, (c, c), 0)
    jj = lax.broadcasted_iota(jnp.int32, (c, c), 1)
    eye = (ii == jj).astype(F32)
    masks =((ii >= jj, ii > jj, row > lane), (ii <= jj, ii < jj, row < lane))

    loaded = []
    for chunk in chunks:
        r0 = pl.multiple_of(chunk * c, c)
        k = k_ref[0, pl.ds(r0, c), cols]
        q = q_ref[0, pl.ds(r0, c), cols]
        kq = _dot_nt(jnp.concatenate([k, q], axis=0), k)
        loaded.append((chunk, r0, q, k, kq))

    chains = []
    for chunk, r0, q, k, kq in loaded:
        gb = gb_ref[0, pl.ds(r0, c), :]
        for d in range(2):
            col = head + d * GDN_HEADS
            beta = jnp.sum(jnp.where(lane == col, gb, 0.0), axis=-1, keepdims=True)
            g = jnp.sum(jnp.where(lane == col + 2 * GDN_HEADS, gb, 0.0), axis=-1, keepdims=True)
            incl, strict, strict_wide = masks[d]
            rhs = jnp.where(lane >= c, g, jnp.where(strict_wide, g, 0.0))
            mask = incl.astype(BF16)
            hi = rhs.astype(BF16)
            rest = rhs - hi.astype(F32)
            mid = rest.astype(BF16)
            low = (rest - mid.astype(F32)).astype(BF16)
            e = _dot(mask, hi) + _dot(mask, mid) + _dot(mask, low)
            chains.append(dict(chunk=chunk, r0=r0, d=d, q=q, k=k, kq=kq, beta=beta, e=e))

    for ch in chains:
        incl, strict, _ = masks[ch["d"]]
        e = ch["e"]
        decay = jnp.where(incl, jnp.exp(e[:, :c]), 0.0)
        gc = e[:, c:c + 1]
        last = 0 if ch["d"] == 1 else c - 1
        gc_last = e[last:last + 1, c:c + 1]
        ch.update(decay=decay, gc=gc, gc_last=gc_last, egc=jnp.exp(gc))
        ch["a"] = jnp.where(strict, ch["beta"] * ch["kq"][:c] * decay, 0.0)
        ch["t"] = eye
    s = 1
    while s < c:
        pair = jnp.logical_and((ii // (2 * s)) == (jj // (2 * s)), (ii // s) != (jj // s))
        for ch in chains:
            ch["a_off"] = jnp.where(pair, ch["a"], 0.0)
        if s == 1:
            for ch in chains:
                ch["t"] = eye - ch["a_off"]
        else:
            for ch in chains:
                ch["m"] = _dot(ch["t"].astype(BF16), ch["a_off"].astype(BF16))
            for ch in chains:
                ch["t"] = ch["t"] - _dot(ch["m"].astype(BF16), ch["t"].astype(BF16))
        s *= 2
    for ch in chains:
        r0, beta, egc = ch["r0"], ch["beta"], ch["egc"]
        kf = ch["k"].astype(F32)
        vf = v_ref[0, pl.ds(r0, c), cols].astype(F32)
        rhs2 = jnp.concatenate([vf * beta, kf * (beta * egc)], axis=1).astype(BF16)
        ch["uw"] = _dot(ch["t"].astype(BF16), rhs2).astype(BF16)
        ch["k_dec_t"] = (kf * jnp.exp(ch["gc_last"] - ch["gc"])).T.astype(BF16)
    for ch in chains:
        incl = masks[ch["d"]][0]
        qk = jnp.where(incl, ch["kq"][c:] * ch["decay"], 0.0).astype(BF16)
        ch["nm"] = _dot(ch["k_dec_t"], ch["uw"])
        ch["ow"] = _dot(qk, ch["uw"])
    for ch in chains:
        chunk, r0, nm, ow = ch["chunk"], ch["r0"], ch["nm"], ch["ow"]
        s = 2 * local + ch["d"]
        q0 = pl.multiple_of(chunk * GDN_QM_ROWS, 16)
        qm_scr[s, pl.ds(q0, c), :] = (ch["q"].astype(F32) * ch["egc"] - ow[:, GDN_HEAD_DIM:]).astype(BF16)
        qm_scr[s, pl.ds(q0 + c, GDN_HEAD_DIM), :] = nm[:, GDN_HEAD_DIM:].astype(BF16)
        nn_scr[s, pl.ds(pl.multiple_of(chunk * GDN_HEAD_DIM, GDN_HEAD_DIM), GDN_HEAD_DIM), :] = nm[:, :GDN_HEAD_DIM]
        o_scr[s, pl.ds(r0, c), :] = ow[:, :GDN_HEAD_DIM]
        gl_scr[s, pl.ds(pl.multiple_of(chunk * GDN_GL_ROWS, GDN_GL_ROWS), GDN_GL_ROWS), :] = jnp.broadcast_to(
            jnp.exp(ch["gc_last"]), (GDN_GL_ROWS, LANES))


def _gdn_advance(qm_scr, nn_scr, o_scr, gl_scr, d, chunk, state):
    c = GDN_CHUNK
    qm = qm_scr[d, pl.ds(pl.multiple_of(chunk * GDN_QM_ROWS, 16), GDN_QM_ROWS), :]
    r = _dot(qm, state.astype(BF16))
    rows = pl.ds(pl.multiple_of(chunk * c, c), c)
    o_scr[d, rows, :] = o_scr[d, rows, :] + r[:c]
    gl = gl_scr[d, pl.ds(pl.multiple_of(chunk * GDN_GL_ROWS, GDN_GL_ROWS), 1), :]
    n = nn_scr[d, pl.ds(pl.multiple_of(chunk * GDN_HEAD_DIM, GDN_HEAD_DIM), GDN_HEAD_DIM), :]
    return state * gl - r[c:] + n


def _gdn_kernel(q_ref, k_ref, v_ref, z_ref, gb_ref, gain_ref, o_ref, qm_scr, nn_scr, o_scr, gl_scr):
    first_head = pl.program_id(1) * GDN_HEADS_PER_STEP
    c = GDN_CHUNK
    n_lat, n_ctx = SEQ // c, CTX_LEN // c
    n_chunks = n_lat + n_ctx

    for local in range(GDN_HEADS_PER_STEP):
        def prepare(i, _, local=local):
            chunks = [i * GDN_PREP_UNROLL + j for j in range(GDN_PREP_UNROLL)]
            _gdn_prepare(q_ref, k_ref, v_ref, gb_ref, qm_scr, nn_scr, o_scr, gl_scr, chunks, first_head + local, local)
            return 0

        lax.fori_loop(0, n_chunks // GDN_PREP_UNROLL, prepare, 0)

    def advance(i, states):
        cf = jnp.where(i < n_ctx, n_lat + i, i - n_ctx)
        cb = n_chunks - 1 - i
        return tuple(_gdn_advance(qm_scr, nn_scr, o_scr, gl_scr, s, cb if s % 2 else cf, state)
                     for s, state in enumerate(states))

    zero = jnp.zeros((GDN_HEAD_DIM, GDN_HEAD_DIM), F32)
    lax.fori_loop(0, n_chunks, advance, (zero,) * (2 * GDN_HEADS_PER_STEP), unroll=4)
    for local in range(GDN_HEADS_PER_STEP):
        cols = slice(local * LANES, (local + 1) * LANES)
        o = o_scr[2 * local] + o_scr[2 * local + 1]
        o_ref[0, :, cols] = (_rms(o, gain_ref[...]) * _silu(z_ref[0, :, cols].astype(F32))).astype(BF16)


def _gdn(qkv, p, gates, out_gain):
    n_steps = GDN_HEADS // GDN_HEADS_PER_STEP
    n_chunks = TOK // GDN_CHUNK
    width = GDN_HEADS_PER_STEP * LANES
    zblk = COL_Z // width
    n_chain = 2 * GDN_HEADS_PER_STEP
    blk = lambda off: pl.BlockSpec((1, TOK, width), lambda b, h: (b, 0, off + h))
    vec = pl.BlockSpec((1, LANES), lambda b, h: (0, 0))
    return pl.pallas_call(
        _gdn_kernel,
        grid=(BATCH, n_steps),
        in_specs=[blk(0), blk(n_steps), blk(2 * n_steps), blk(zblk),
                  pl.BlockSpec((1, TOK, LANES), lambda b, h: (b, 0, 0)), vec],
        out_specs=pl.BlockSpec((1, TOK, width), lambda b, h: (b, 0, h)),
        out_shape=jax.ShapeDtypeStruct((BATCH, TOK, BRANCH_WIDTH), BF16),
        scratch_shapes=[pltpu.VMEM((n_chain, n_chunks * GDN_QM_ROWS, LANES), BF16),
                        pltpu.VMEM((n_chain, n_chunks * GDN_HEAD_DIM, LANES), F32),
                        pltpu.VMEM((n_chain, TOK, LANES), F32),
                        pltpu.VMEM((n_chain, n_chunks * GDN_GL_ROWS, LANES), F32)],
        compiler_params=_params("parallel", "parallel"),
        name="gdn",
    )(qkv, qkv, qkv, p, gates, out_gain.reshape(1, LANES))


def _rms_halves(x, gain):
    lane = lax.broadcasted_iota(jnp.int32, x.shape, 1)
    lo = lane < DIFF_HEAD_DIM
    x2 = x * x
    s_lo = jnp.sum(jnp.where(lo, x2, 0.0), axis=-1, keepdims=True)
    s_hi = jnp.sum(jnp.where(lo, 0.0, x2), axis=-1, keepdims=True)
    ms = jnp.where(lo, s_lo, s_hi) * (1.0 / DIFF_HEAD_DIM)
    return x * lax.rsqrt(ms + EPS) * gain


def _rope(x, cos, sin):
    lane = lax.broadcasted_iota(jnp.int32, x.shape, 1)
    first = (lane & ROPE_PAIRS) == 0
    partner = jnp.where(first, -pltpu.roll(x, LANES - ROPE_PAIRS, 1), pltpu.roll(x, ROPE_PAIRS, 1))
    return x * cos + partner * sin


ATTN_Q_BLOCK = 1024
ATTN_GROUP_ROWS = 128


def _attn_kernel(q_ref, k_ref, v_ref, cosk_ref, sink_ref, cosq_ref, sinq_ref, qg_ref, kg_ref, lv_ref, og_ref,
                 o_ref, kn_scr, *, ctx_block, lam_init):
    qi = pl.program_id(2)

    @pl.when(qi == 0)
    def _():
        kn = _rope(_rms_halves(k_ref[0].astype(F32), kg_ref[...]), cosk_ref[...], sink_ref[...])
        kn_scr[...] = kn.astype(BF16)

    lv = lv_ref[...]
    lam = (jnp.exp(jnp.sum(lv[0:1] * lv[1:2], axis=-1, keepdims=True))
           - jnp.exp(jnp.sum(lv[2:3] * lv[3:4], axis=-1, keepdims=True)) + lam_init)
    def attend(n_rows, kn, v):
        q = _rope(_rms_halves(q_ref[0, :n_rows, :].astype(F32), qg_ref[...]), cosq_ref[:n_rows, :], sinq_ref[:n_rows, :])
        q = q * (DIFF_HEAD_DIM ** -0.5 * math.log2(math.e))
        lane = lax.broadcasted_iota(jnp.int32, q.shape, 1)
        lo = lane < DIFF_HEAD_DIM
        q1 = jnp.where(lo, q, 0.0).astype(BF16)
        q2 = jnp.where(lo, 0.0, q).astype(BF16)

        def half(s):
            p = jnp.exp2(s - jnp.max(s, axis=-1, keepdims=True))
            return _dot(p.astype(BF16), v), jnp.sum(p, axis=-1, keepdims=True)
        rows = ATTN_GROUP_ROWS
        scores = [(_dot_nt(q1[r:r + rows], kn), _dot_nt(q2[r:r + rows], kn)) for r in range(0, n_rows, rows)]
        for g, (s1, s2) in enumerate(scores):
            a1, l1 = half(s1)
            a2, l2 = half(s2)
            o = a1 * (1.0 / l1) - a2 * (lam / l2)
            o_ref[0, g * rows:(g + 1) * rows, :] = (_rms(o, og_ref[...]) * (1.0 - lam_init)).astype(BF16)

    if ctx_block is None:
        attend(q_ref.shape[1], kn_scr[...], v_ref[0])
    else:
        @pl.when(qi == ctx_block)
        def _():
            attend(CTX_LEN, kn_scr[SEQ:, :], v_ref[0, SEQ:, :])

        @pl.when(qi != ctx_block)
        def _():
            attend(q_ref.shape[1], kn_scr[...], v_ref[0])


def _rope_tables():
    n_rows = SEQ // GRID_W
    row_id = jnp.broadcast_to(jnp.arange(n_rows, dtype=F32)[:, None], (n_rows, GRID_W)).reshape(-1)
    col_id = jnp.broadcast_to(jnp.arange(GRID_W, dtype=F32)[None, :], (n_rows, GRID_W)).reshape(-1)
    inv_freq = jnp.power(ROPE_BASE, -jnp.arange(ROPE_PAIRS, dtype=F32) / ROPE_PAIRS)
    row_ang = row_id[:, None] * inv_freq
    col_ang = col_id[:, None] * inv_freq
    ang = jnp.concatenate([row_ang, row_ang, col_ang, col_ang], axis=-1)
    ang = jnp.concatenate([ang, ang], axis=-1)
    pad = ((0, CTX_LEN), (0, 0))
    return jnp.pad(jnp.cos(ang), pad, constant_values=1.0), jnp.pad(jnp.sin(ang), pad)


def _diff_attn(p, cos, sin, q_gain, k_gain, lam_vecs, out_gain, lam_init, with_ctx):
    tq = ATTN_Q_BLOCK
    n_rows = TOK if with_ctx else SEQ
    nq = pl.cdiv(n_rows, tq)
    nh = DIFF_HEADS
    qb, kb, vb = COL_DQ // LANES, COL_DK // LANES, COL_DV // LANES
    full = lambda off: pl.BlockSpec((1, TOK, LANES), lambda b, h, i: (b, 0, off + h))
    tab_full = pl.BlockSpec((TOK, LANES), lambda b, h, i: (0, 0))
    tab_q = pl.BlockSpec((tq, LANES), lambda b, h, i: (i, 0))
    vec = pl.BlockSpec((1, LANES), lambda b, h, i: (0, 0))
    tile2 = lambda g: jnp.concatenate([g, g]).reshape(1, LANES)
    return pl.pallas_call(
        functools.partial(_attn_kernel, ctx_block=SEQ // tq if with_ctx else None, lam_init=lam_init),
        grid=(BATCH, nh, nq),
        in_specs=[pl.BlockSpec((1, tq, LANES), lambda b, h, i: (b, i, qb + h)), full(kb), full(vb),
                  tab_full, tab_full, tab_q, tab_q, vec, vec,
                  pl.BlockSpec((4, DIFF_HEAD_DIM), lambda b, h, i: (0, 0)), vec],
        out_specs=pl.BlockSpec((1, tq, LANES), lambda b, h, i: (b, i, h)),
        out_shape=jax.ShapeDtypeStruct((BATCH, n_rows, BRANCH_WIDTH), BF16),
        scratch_shapes=[pltpu.VMEM((TOK, LANES), BF16)],
        compiler_params=_params("parallel", "parallel", "arbitrary"),
        name="diff_attn",
    )(p, p, p, cos, sin, cos, sin, tile2(q_gain), tile2(k_gain), lam_vecs, out_gain.reshape(1, LANES))


LRU_SLAB = 256
LRU_ROWS = 256
LRU_SCAN_BLOCK = 8


def _lru_kernel(x_ref, y_ref, cw_ref, cb_ref, wg_ref, bg_ref, lam_ref, o_ref, xc_scr, af_scr, bf_scr, ab_scr, bb_scr):
    w = LRU_SLAB
    blk = LRU_SCAN_BLOCK
    for c0 in range(0, w, LANES):
        cols = slice(c0, c0 + LANES)
        xc_scr[:, cols] = _conv(x_ref[0, :, cols].astype(F32), cw_ref[:, cols]) + cb_ref[:, cols]
    sp = _softplus(-lam_ref[0])
    sub = lax.broadcasted_iota(jnp.int32, (LRU_ROWS // blk, blk, w), 1)

    def gates(i, _):
        r0 = pl.multiple_of(i * LRU_ROWS, LRU_ROWS)
        xc = xc_scr[pl.ds(r0, LRU_ROWS), :]
        pre = _dot(xc.astype(BF16), wg_ref[0]) + bg_ref[0]
        for d, (a_scr, b_scr) in enumerate(((af_scr, bf_scr), (ab_scr, bb_scr))):
            r = _sigmoid(pre[:, (2 * d) * w:(2 * d + 1) * w])
            gi = _sigmoid(pre[:, (2 * d + 1) * w:(2 * d + 2) * w])
            log_a = -LRU_C * r * sp[d:d + 1]
            a = jnp.exp(log_a)
            b = jnp.sqrt(1.0 - a * a) * gi * xc
            a = a.reshape(LRU_ROWS // blk, blk, w)
            b = b.reshape(LRU_ROWS // blk, blk, w)
            shift = 1
            while shift < blk:
                if d == 0:
                    ok, roll_by = sub >= shift, shift
                else:
                    ok, roll_by = sub < blk - shift, blk - shift
                b = jnp.where(ok, a * pltpu.roll(b, roll_by, 1) + b, b)
                a = jnp.where(ok, a * pltpu.roll(a, roll_by, 1), a)
                shift *= 2
            a_scr[pl.ds(r0, LRU_ROWS), :] = a.reshape(LRU_ROWS, w)
            b_scr[pl.ds(r0, LRU_ROWS), :] = b.reshape(LRU_ROWS, w)
        return 0

    lax.fori_loop(0, TOK // LRU_ROWS, gates, 0)

    n_blk, n_lat_blk, n_ctx_blk = TOK // blk, SEQ // blk, CTX_LEN // blk

    def step(s, carry):
        h_f, h_b = carry
        rows_f = pl.ds(pl.multiple_of(jnp.where(s < n_ctx_blk, n_lat_blk + s, s - n_ctx_blk) * blk, blk), blk)
        rows_b = pl.ds(pl.multiple_of((n_blk - 1 - s) * blk, blk), blk)
        hf = af_scr[rows_f, :] * h_f + bf_scr[rows_f, :]
        bf_scr[rows_f, :] = hf
        hb = ab_scr[rows_b, :] * h_b + bb_scr[rows_b, :]
        bb_scr[rows_b, :] = hb
        return hf[blk - 1:blk, :], hb[0:1, :]

    zero = jnp.zeros((1, w), F32)
    lax.fori_loop(0, n_blk, step, (zero, zero), unroll=4)
    h = bf_scr[...] + bb_scr[...]
    o_ref[0] = (h * jax.nn.gelu(y_ref[0].astype(F32))).astype(BF16)


def _lru_gate_weights(w_gate, b_gate):
    n_slab = BRANCH_WIDTH // LRU_SLAB
    per = LRU_SLAB // LRU_BLOCK_DIM
    wg = w_gate.reshape(2, 2, n_slab, per, LRU_BLOCK_DIM, LRU_BLOCK_DIM)
    eye = jnp.eye(per, dtype=w_gate.dtype)
    dense = jnp.einsum('dgsnjk,nm->snjdgmk', wg, eye)
    dense = dense.reshape(n_slab, LRU_SLAB, 4 * LRU_SLAB)
    bg = b_gate.reshape(2, 2, n_slab, LRU_SLAB).transpose(2, 0, 1, 3).reshape(n_slab, 1, 4 * LRU_SLAB)
    return dense.astype(BF16), bg


def _lru(p, conv_w, conv_b, w_gate, b_gate, lam):
    n_slab = BRANCH_WIDTH // LRU_SLAB
    wg, bg = _lru_gate_weights(w_gate, b_gate)
    lam_s = lam.reshape(2, n_slab, LRU_SLAB).transpose(1, 0, 2)
    xb, yb = COL_LX // LRU_SLAB, COL_LY // LRU_SLAB
    return pl.pallas_call(
        _lru_kernel,
        grid=(BATCH, n_slab),
        in_specs=[pl.BlockSpec((1, TOK, LRU_SLAB), lambda b, s: (b, 0, xb + s)),
                  pl.BlockSpec((1, TOK, LRU_SLAB), lambda b, s: (b, 0, yb + s)),
                  pl.BlockSpec((CONV_WIDTH, LRU_SLAB), lambda b, s: (0, s)),
                  pl.BlockSpec((1, LRU_SLAB), lambda b, s: (0, s)),
                  pl.BlockSpec((1, LRU_SLAB, 4 * LRU_SLAB), lambda b, s: (s, 0, 0)),
                  pl.BlockSpec((1, 1, 4 * LRU_SLAB), lambda b, s: (s, 0, 0)),
                  pl.BlockSpec((1, 2, LRU_SLAB), lambda b, s: (s, 0, 0))],
        out_specs=pl.BlockSpec((1, TOK, LRU_SLAB), lambda b, s: (b, 0, s)),
        out_shape=jax.ShapeDtypeStruct((BATCH, TOK, BRANCH_WIDTH), BF16),
        scratch_shapes=[pltpu.VMEM((TOK, LRU_SLAB), F32)] * 5,
        compiler_params=_params("parallel", "parallel"),
        name="lru",
    )(p, p, conv_w, conv_b.reshape(1, BRANCH_WIDTH), wg, bg, lam_s)


def _route(logits):
    lane = lax.broadcasted_iota(jnp.int32, logits.shape, 1)
    lane_f = lane.astype(F32)
    far = float(LANES)
    lg = jnp.where(lane < N_EXPERTS, logits, -jnp.inf)
    ex = jnp.exp(lg - jnp.max(lg, axis=-1, keepdims=True))
    probs = ex / jnp.sum(ex, axis=-1, keepdims=True)
    per_group = N_EXPERTS // N_GROUPS
    grp = lane // per_group

    def top2(vals):
        m1 = jnp.max(vals, axis=-1, keepdims=True)
        i1 = jnp.min(jnp.where(vals == m1, lane_f, far), axis=-1, keepdims=True)
        rest = jnp.where(lane_f == i1, -2.0, vals)
        m2 = jnp.max(rest, axis=-1, keepdims=True)
        i2 = jnp.min(jnp.where(rest == m2, lane_f, far), axis=-1, keepdims=True)
        return m1, i1, m2, i2

    best = jnp.zeros(logits.shape[:1] + (1,), jnp.int32)
    best_score = None
    for g in range(N_GROUPS):
        m1, _, m2, _ = top2(jnp.where(grp == g, probs, -1.0))
        score = m1 + m2
        if best_score is None:
            best_score = score
        else:
            better = score > best_score
            best = jnp.where(better, g, best)
            best_score = jnp.where(better, score, best_score)
    m1, i1, m2, i2 = top2(jnp.where(grp == best, probs, -1.0))
    den = m1 + m2
    return i1, i2, m1 / den, m2 / den


ROUTE_E, ROUTE_W, ROUTE_RANK = 0, 2, 4


def _pack_bf16_pairs(x):
    n = x.shape[1] // 2
    xb = x.astype(BF16).astype(F32)
    lo = pltpu.bitcast(xb[:, :n], jnp.uint32)
    hi = pltpu.bitcast(xb[:, n:], jnp.uint32)
    return (lo >> 16) | (hi & jnp.uint32(0xFFFF0000))


def _unpack_bf16_pairs(p):
    lo = pltpu.bitcast(p << 16, F32)
    hi = pltpu.bitcast(p & jnp.uint32(0xFFFF0000), F32)
    return jnp.concatenate([lo, hi], axis=1)


def _merge_kernel(ya_ref, yb_ref, yc_ref, gates_ref, x_ref, ml_ref, mc_ref, g2_ref, wbr_ref, wout_ref, wr_ref,
                  br_ref, xo_ref, h2_ref, route_ref, cnt_ref, cnt_scr, *, tm):
    i = pl.program_id(1)

    @pl.when(jnp.logical_and(pl.program_id(0) == 0, i == 0))
    def _():
        cnt_scr[...] = jnp.zeros_like(cnt_scr)

    acc = None
    for n, y_ref in enumerate((ya_ref, yb_ref, yc_ref)):
        yn = _dot(y_ref[0], wbr_ref[n])
        gate = _sigmoid(gates_ref[0, :, n * D_MODEL:(n + 1) * D_MODEL].astype(F32))
        acc = gate * yn if acc is None else acc + gate * yn
    out = _dot(acc.astype(BF16), wout_ref[...])
    is_ctx = _ctx_rows(i, tm, D_MODEL)
    xn = x_ref[0] + jnp.where(is_ctx, mc_ref[0, 2:3, :], ml_ref[0, 2:3, :]) * out
    xo_ref[0] = xn
    h2 = _modulated_norm(xn, g2_ref[...], ml_ref, mc_ref, is_ctx, 3)
    h2_ref[0] = _pack_bf16_pairs(h2)
    i1, i2, w1, w2 = _route(_dot(h2.astype(BF16), wr_ref[...]) + br_ref[...])
    lane = lax.broadcasted_iota(jnp.int32, (tm, LANES), 1)
    lane_f = lane.astype(F32)
    chosen = jnp.where(jnp.logical_or(lane_f == i1, lane_f == i2), 1.0, 0.0)
    earlier = (lax.broadcasted_iota(jnp.int32, (tm, tm), 0) > lax.broadcasted_iota(jnp.int32, (tm, tm), 1))
    before = _dot(jnp.where(earlier, 1.0, 0.0).astype(BF16), chosen.astype(BF16)) + cnt_scr[...]
    rank1 = jnp.sum(jnp.where(lane_f == i1, before, 0.0), axis=-1, keepdims=True)
    rank2 = jnp.sum(jnp.where(lane_f == i2, before, 0.0), axis=-1, keepdims=True)
    cnt_scr[...] += jnp.sum(chosen, axis=0, keepdims=True)
    cnt_ref[...] = cnt_scr[...]
    record = jnp.zeros((tm, LANES), F32)
    for pos, val in enumerate((i1, i2, w1, w2, rank1, rank2)):
        record = jnp.where(lane == pos, val, record)
    route_ref[0] = record


def _merge(ya, yb, yc, p, xs, mods, gain2, w_branch, w_out, w_router, b_router, rows, tm):
    wr = jnp.zeros((D_MODEL, LANES), BF16).at[:, :N_EXPERTS].set(w_router.astype(BF16))
    br = jnp.zeros((1, LANES), F32).at[0, :N_EXPERTS].set(b_router)
    tile = lambda w: pl.BlockSpec((1, tm, w), lambda b, i: (b, i, 0))
    const = lambda shape: pl.BlockSpec(shape, lambda b, i: (0,) * len(shape))
    return pl.pallas_call(
        functools.partial(_merge_kernel, tm=tm),
        grid=(BATCH, rows // tm),
        in_specs=[tile(BRANCH_WIDTH), tile(BRANCH_WIDTH), tile(BRANCH_WIDTH), tile(3 * D_MODEL), tile(D_MODEL),
                  pl.BlockSpec((1, N_MOD, D_MODEL), lambda b, i: (b, 0, 0)),
                  pl.BlockSpec((1, N_MOD, D_MODEL), lambda b, i: (BATCH, 0, 0)),
                  const((1, D_MODEL)), const((3, BRANCH_WIDTH, D_MODEL)), const((D_MODEL, D_MODEL)),
                  const((D_MODEL, LANES)), const((1, LANES))],
        out_specs=[tile(D_MODEL), tile(D_MODEL // 2), tile(LANES), const((1, LANES))],
        out_shape=[jax.ShapeDtypeStruct((BATCH, rows, D_MODEL), F32),
                   jax.ShapeDtypeStruct((BATCH, rows, D_MODEL // 2), jnp.uint32),
                   jax.ShapeDtypeStruct((BATCH, rows, LANES), F32),
                   jax.ShapeDtypeStruct((1, LANES), F32)],
        scratch_shapes=[pltpu.VMEM((1, LANES), F32)],
        compiler_params=_params("arbitrary", "arbitrary"),
        name="merge",
    )(ya, yb, yc, p, xs, mods, mods, gain2.reshape(1, D_MODEL), w_branch, w_out, wr, br)


MOE_TILE = 512
SC_GATHER_ROWS = 64


def _sc_gather(table, idx):
    info = plsc.get_sparse_core_info()
    n_workers = info.num_cores * info.num_subcores
    n_rows, width = idx.shape[0], table.shape[1]
    per_worker = n_rows // n_workers
    assert per_worker * n_workers == n_rows and per_worker % SC_GATHER_ROWS == 0
    mesh = plsc.VectorSubcoreMesh(core_axis_name="c", subcore_axis_name="s")

    @functools.partial(
        pl.kernel, mesh=mesh, out_type=jax.ShapeDtypeStruct((n_rows, width), table.dtype),
        scratch_types=[pltpu.VMEM((SC_GATHER_ROWS,), jnp.int32),
                       pltpu.VMEM((SC_GATHER_ROWS, width), table.dtype),
                       pltpu.SemaphoreType.DMA],
        name="sc_gather")
    def gather(table_hbm, idx_hbm, out_hbm, idx_v, rows_v, sem):
        worker = lax.axis_index("s") * info.num_cores + lax.axis_index("c")
        base = worker * per_worker

        @pl.loop(0, per_worker // SC_GATHER_ROWS)
        def _(j):
            off = base + j * SC_GATHER_ROWS
            pltpu.sync_copy(idx_hbm.at[pl.ds(off, SC_GATHER_ROWS)], idx_v)
            pltpu.async_copy(table_hbm.at[idx_v], rows_v, sem).wait()
            pltpu.sync_copy(rows_v, out_hbm.at[pl.ds(off, SC_GATHER_ROWS)])

    return gather(table, idx)


SC_SCATTER_ROWS = 128


def _sc_scatter(table, dest, n_out):
    info = plsc.get_sparse_core_info()
    n_workers = info.num_cores * info.num_subcores
    n_tok, width = table.shape
    n_assign = dest.shape[0]
    per_worker = n_assign // n_workers
    assert per_worker * n_workers == n_assign and per_worker % SC_SCATTER_ROWS == 0 and n_tok % SC_SCATTER_ROWS == 0
    mesh = plsc.VectorSubcoreMesh(core_axis_name="c", subcore_axis_name="s")

    @functools.partial(
        pl.kernel, mesh=mesh, out_type=jax.ShapeDtypeStruct((n_out, width), table.dtype),
        scratch_types=[pltpu.VMEM((SC_SCATTER_ROWS,), jnp.int32),
                       pltpu.VMEM((SC_SCATTER_ROWS, width), table.dtype),
                       pltpu.SemaphoreType.DMA],
        name="sc_scatter")
    def scatter(table_hbm, dest_hbm, out_hbm, idx_v, rows_v, sem):
        worker = lax.axis_index("s") * info.num_cores + lax.axis_index("c")
        base = worker * per_worker

        @pl.loop(0, per_worker // SC_SCATTER_ROWS)
        def _(j):
            off = base + j * SC_SCATTER_ROWS
            pltpu.sync_copy(dest_hbm.at[pl.ds(off, SC_SCATTER_ROWS)], idx_v)
            pltpu.sync_copy(table_hbm.at[pl.ds(lax.rem(off, n_tok), SC_SCATTER_ROWS)], rows_v)
            pltpu.async_copy(rows_v, out_hbm.at[idx_v], sem).wait()

    return scatter(table, dest)


def _experts_kernel(tile_expert_ref, n_valid_ref, x_ref, wgu_ref, wd_ref, o_ref, wgu_scr, wd_scr):
    j = pl.program_id(0)
    valid = j < n_valid_ref[0]
    fresh = jnp.logical_or(j == 0, tile_expert_ref[j] != tile_expert_ref[jnp.maximum(j - 1, 0)])

    @pl.when(jnp.logical_and(valid, fresh))
    def _():
        wgu_scr[...] = wgu_ref[0, 0].astype(BF16)
        wd_scr[...] = wd_ref[0, 0].astype(BF16)

    @pl.when(valid)
    def _():
        x = _unpack_bf16_pairs(x_ref[...]).astype(BF16)
        gu = _dot(x, wgu_scr[...])
        act = (_silu(gu[:, :EXPERT_FF]) * gu[:, EXPERT_FF:]).astype(BF16)
        o_ref[...] = _pack_bf16_pairs(_dot(act, wd_scr[...]))

    @pl.when(jnp.logical_not(valid))
    def _():
        o_ref[...] = jnp.zeros_like(o_ref)


def _experts(x_sorted, tile_expert, n_valid, w_gate_up, w_down, layer):
    n_tiles = x_sorted.shape[0] // MOE_TILE
    half = D_MODEL // 2
    return pl.pallas_call(
        _experts_kernel,
        grid_spec=pltpu.PrefetchScalarGridSpec(
            num_scalar_prefetch=2, grid=(n_tiles,),
            in_specs=[pl.BlockSpec((MOE_TILE, half), lambda j, te, nv: (j, 0)),
                      pl.BlockSpec((1, 1, D_MODEL, 2 * EXPERT_FF), lambda j, te, nv: (layer, te[j], 0, 0)),
                      pl.BlockSpec((1, 1, EXPERT_FF, D_MODEL), lambda j, te, nv: (layer, te[j], 0, 0))],
            out_specs=pl.BlockSpec((MOE_TILE, half), lambda j, te, nv: (j, 0)),
            scratch_shapes=[pltpu.VMEM((D_MODEL, 2 * EXPERT_FF), BF16), pltpu.VMEM((EXPERT_FF, D_MODEL), BF16)]),
        out_shape=jax.ShapeDtypeStruct((x_sorted.shape[0], half), jnp.uint32),
        compiler_params=_params("arbitrary"),
        name="experts",
    )(tile_expert, n_valid, x_sorted, w_gate_up, w_down)


def _combine_kernel(y1_ref, y2_ref, route_ref, x_ref, ml_ref, mc_ref, o_ref, *, tm):
    i = pl.program_id(1)
    route = route_ref[0]
    w1 = route[:, ROUTE_W:ROUTE_W + 1]
    w2 = route[:, ROUTE_W + 1:ROUTE_W + 2]
    moe = w1 * _unpack_bf16_pairs(y1_ref[0, 0]) + w2 * _unpack_bf16_pairs(y2_ref[0, 0])
    is_ctx = _ctx_rows(i, tm, D_MODEL)
    o_ref[0] = x_ref[0] + jnp.where(is_ctx, mc_ref[0, 5:6, :], ml_ref[0, 5:6, :]) * moe


def _combine(y_pairs, route, xs, mods, rows, tm):
    half = D_MODEL // 2
    tile = lambda w: pl.BlockSpec((1, tm, w), lambda b, i: (b, i, 0))
    slot = lambda s: pl.BlockSpec((1, 1, tm, half), lambda b, i: (s, b, i, 0))
    return pl.pallas_call(
        functools.partial(_combine_kernel, tm=tm),
        grid=(BATCH, rows // tm),
        in_specs=[slot(0), slot(1), tile(LANES), tile(D_MODEL),
                  pl.BlockSpec((1, N_MOD, D_MODEL), lambda b, i: (b, 0, 0)),
                  pl.BlockSpec((1, N_MOD, D_MODEL), lambda b, i: (BATCH, 0, 0))],
        out_specs=tile(D_MODEL),
        out_shape=jax.ShapeDtypeStruct((BATCH, rows, D_MODEL), F32),
        compiler_params=_params("parallel", "parallel"),
        name="combine",
    )(y_pairs, y_pairs, route, xs, mods, mods)


def _moe(h2, route, counts, xs, mods, w_gate_up, w_down, layer, rows, tm):
    n_tok = BATCH * rows
    half = D_MODEL // 2
    n_sorted = 2 * n_tok + N_EXPERTS * MOE_TILE
    n_tiles = n_sorted // MOE_TILE
    rec = route.reshape(n_tok, LANES)
    expert = rec[:, ROUTE_E:ROUTE_E + 2].astype(jnp.int32)
    rank = rec[:, ROUTE_RANK:ROUTE_RANK + 2].astype(jnp.int32)
    count = counts[0, :N_EXPERTS].astype(jnp.int32)
    padded = (count + MOE_TILE - 1) // MOE_TILE * MOE_TILE
    end = jnp.cumsum(padded)
    start = end - padded
    first = jnp.sum(jnp.where(expert[:, :1] == jnp.arange(N_EXPERTS), start, 0), axis=1)
    second = jnp.sum(jnp.where(expert[:, 1:] == jnp.arange(N_EXPERTS), start, 0), axis=1)
    dest = jnp.concatenate([first + rank[:, 0], second + rank[:, 1]])
    tile_start = jnp.arange(n_tiles, dtype=jnp.int32) * MOE_TILE
    tile_expert = jnp.minimum(jnp.sum(tile_start[:, None] >= end[None, :], axis=1), N_EXPERTS - 1).astype(jnp.int32)
    n_valid = (end[-1:] // MOE_TILE).astype(jnp.int32)
    x_sorted = _sc_scatter(h2.reshape(n_tok, half), dest, n_sorted)
    y_sorted = _experts(x_sorted, tile_expert, n_valid, w_gate_up, w_down, layer)
    y_pairs = _sc_gather(y_sorted, dest).reshape(2, BATCH, rows, half)
    return _combine(y_pairs, route, xs, mods, rows, tm)


def _split_w_in(w_in):
    bw = BRANCH_WIDTH
    sizes = (3 * bw, bw, 2 * GDN_HEADS, 2 * GDN_HEADS, bw, bw, bw, bw, bw, 3 * D_MODEL)
    offs = [0]
    for s in sizes:
        offs.append(offs[-1] + s)
    part = lambda i: w_in[:, :, offs[i]:offs[i + 1]]
    main = jnp.concatenate([part(9), part(0), part(1), part(4), part(5), part(6), part(7), part(8)], axis=2)
    ba = jnp.zeros(w_in.shape[:2] + (LANES,), F32).at[:, :, :4 * GDN_HEADS].set(
        jnp.concatenate([part(2), part(3)], axis=2))
    return main.astype(BF16), ba.astype(BF16)


def kernel(x, c, ctx, c_ctx, w_mod, b_mod, norm1_gain, norm2_gain, w_in, gdn_conv_w, gdn_a_log, gdn_dt_bias, gdn_out_gain, diff_q_gain, diff_k_gain, diff_lambda, diff_out_gain, lru_conv_w, lru_conv_b, lru_w_gate, lru_b_gate, lru_lambda, w_branch, w_out, w_router, b_router, w_gate_up, w_down):
    mods = _mods(c, c_ctx, w_mod, b_mod)
    cos, sin = _rope_tables()
    xs = jnp.concatenate([x, ctx], axis=1)
    w_main, w_ba = _split_w_in(w_in)
    for layer in range(DEPTH):
        last = layer == DEPTH - 1
        lam_init = 0.8 - 0.6 * math.exp(-0.3 * layer)
        m = mods[layer]
        p, gates = _project(xs, m, norm1_gain[layer], w_main, w_ba, gdn_a_log[layer], gdn_dt_bias[layer], layer)
        qkv = _gdn_conv(p, gdn_conv_w[layer])
        ya = _gdn(qkv, p, gates, gdn_out_gain[layer])
        yb = _diff_attn(p, cos, sin, diff_q_gain[layer], diff_k_gain[layer], diff_lambda[layer],
                        diff_out_gain[layer], lam_init, with_ctx=not last)
        yc = _lru(p, lru_conv_w[layer], lru_conv_b[layer], lru_w_gate[layer], lru_b_gate[layer], lru_lambda[layer])
        rows, tm = (SEQ, 512) if last else (TOK, 768)
        xs, h2, route, counts = _merge(ya, yb, yc, p, xs, m, norm2_gain[layer], w_branch[layer].astype(BF16),
                                       w_out[layer].astype(BF16), w_router, b_router, rows, tm)
        xs = _moe(h2, route, counts, xs, m, w_gate_up, w_down, layer, rows, tm)
    return xs
```

```python
import functools
import math

import jax
import jax.numpy as jnp
from jax import lax
from jax.experimental import pallas as pl
from jax.experimental.pallas import tpu as pltpu
from jax.experimental.pallas import tpu_sc as plsc

F32 = jnp.float32
BF16 = jnp.bfloat16

D_MODEL = 1024
BATCH = 8
SEQ = 2048
DEPTH = 2
GRID_W = 64
CTX_LEN = 256
TOK = SEQ + CTX_LEN
N_MOD = 6
EPS = 1e-6
CONV_WIDTH = 4
BRANCH_WIDTH = 512
GDN_HEADS = 4
GDN_HEAD_DIM = 128
GDN_CHUNK = 64
DIFF_HEADS = 4
DIFF_HEAD_DIM = 64
ROPE_BASE = 10000.0
ROPE_PAIRS = DIFF_HEAD_DIM // 4
LRU_BLOCKS = 8
LRU_BLOCK_DIM = BRANCH_WIDTH // LRU_BLOCKS
LRU_C = 8.0
N_EXPERTS = 16
N_GROUPS = 4
EXPERT_FF = 512

LANES = 128
VMEM_LIMIT = 56 * 1024 * 1024

COL_GATES = 0
COL_QKV = 3 * D_MODEL
COL_Z = COL_QKV + 3 * BRANCH_WIDTH
COL_DQ = COL_Z + BRANCH_WIDTH
COL_DK = COL_DQ + BRANCH_WIDTH
COL_DV = COL_DK + BRANCH_WIDTH
COL_LX = COL_DV + BRANCH_WIDTH
COL_LY = COL_LX + BRANCH_WIDTH
PROJ_COLS = COL_LY + BRANCH_WIDTH


def _params(*sem):
    return pltpu.CompilerParams(dimension_semantics=sem, vmem_limit_bytes=VMEM_LIMIT)


def _dot(a, b, precision=None):
    return jnp.dot(a, b, preferred_element_type=F32, precision=precision)


def _dot_nt(a, b):
    return lax.dot_general(a, b, (((1,), (1,)), ((), ())), preferred_element_type=F32)


_sigmoid = jax.nn.sigmoid


def _silu(x):
    return x * _sigmoid(x)


def _softplus(x):
    return jnp.maximum(x, 0.0) + jnp.log(1.0 + jnp.exp(-jnp.abs(x)))


def _rms(x, gain):
    return x * lax.rsqrt(jnp.mean(x * x, axis=-1, keepdims=True) + EPS) * gain


def _mod_kernel(c_ref, w_ref, b_ref, o_ref):
    c = c_ref[...]
    o_ref[0] = _dot(_silu(c), w_ref[0], precision=lax.Precision.HIGHEST) + b_ref[0]


def _mods(c, c_ctx, w_mod, b_mod):
    depth = w_mod.shape[0]
    rows = 16
    cc = jnp.zeros((rows, D_MODEL), F32).at[:BATCH].set(c).at[BATCH].set(c_ctx)
    tn = 1536
    out = pl.pallas_call(
        _mod_kernel,
        grid=(depth, N_MOD * D_MODEL // tn),
        in_specs=[pl.BlockSpec((rows, D_MODEL), lambda l, j: (0, 0)),
                  pl.BlockSpec((1, D_MODEL, tn), lambda l, j: (l, 0, j)),
                  pl.BlockSpec((1, 1, tn), lambda l, j: (l, 0, j))],
        out_specs=pl.BlockSpec((1, rows, tn), lambda l, j: (l, 0, j)),
        out_shape=jax.ShapeDtypeStruct((depth, rows, N_MOD * D_MODEL), F32),
        compiler_params=_params("parallel", "parallel"),
        name="mods",
    )(cc, w_mod, b_mod.reshape(depth, 1, N_MOD * D_MODEL))
    return out.reshape(depth, rows, N_MOD, D_MODEL)


def _modulated_norm(x, gain, ml_ref, mc_ref, is_ctx, shift_idx):
    shift = jnp.where(is_ctx, mc_ref[0, shift_idx:shift_idx + 1, :], ml_ref[0, shift_idx:shift_idx + 1, :])
    scale = jnp.where(is_ctx, mc_ref[0, shift_idx + 1:shift_idx + 2, :], ml_ref[0, shift_idx + 1:shift_idx + 2, :])
    return _rms(x, gain) * (1.0 + scale) + shift


def _ctx_rows(tile, tm, width):
    row = tile * tm + lax.broadcasted_iota(jnp.int32, (tm, width), 0)
    return row >= SEQ


def _proj_kernel(x_ref, ml_ref, mc_ref, g_ref, w_ref, wba_ref, alog_ref, dtb_ref, p_ref, gb_ref, h_scr, *, tm):
    i = pl.program_id(1)
    j = pl.program_id(2)

    @pl.when(j == 0)
    def _():
        is_ctx = _ctx_rows(i, tm, D_MODEL)
        h = _modulated_norm(x_ref[0], g_ref[...], ml_ref, mc_ref, is_ctx, 0).astype(BF16)
        h_scr[...] = h
        ba = _dot(h, wba_ref[0])
        lane = lax.broadcasted_iota(jnp.int32, ba.shape, 1)
        log_decay = -jnp.exp(alog_ref[...]) * _softplus(ba + dtb_ref[...])
        gb_ref[0] = jnp.where(lane < 2 * GDN_HEADS, _sigmoid(ba), log_decay)

    p_ref[0] = _dot(h_scr[...], w_ref[0]).astype(BF16)


def _project(xs, mods, gain, w_main, w_ba, a_log, dt_bias, layer):
    def pad_lanes(vals):
        row = jnp.zeros((LANES,), F32).at[2 * GDN_HEADS:4 * GDN_HEADS].set(vals.reshape(-1))
        return row.reshape(1, LANES)

    vec = pl.BlockSpec((1, LANES), lambda b, i, j: (0, 0))
    tm, tn = 1152, 1280
    return pl.pallas_call(
        functools.partial(_proj_kernel, tm=tm),
        grid=(BATCH, TOK // tm, PROJ_COLS // tn),
        in_specs=[pl.BlockSpec((1, tm, D_MODEL), lambda b, i, j: (b, i, 0)),
                  pl.BlockSpec((1, N_MOD, D_MODEL), lambda b, i, j: (b, 0, 0)),
                  pl.BlockSpec((1, N_MOD, D_MODEL), lambda b, i, j: (BATCH, 0, 0)),
                  pl.BlockSpec((1, D_MODEL), lambda b, i, j: (0, 0)),
                  pl.BlockSpec((1, D_MODEL, tn), lambda b, i, j: (layer, 0, j)),
                  pl.BlockSpec((1, D_MODEL, LANES), lambda b, i, j: (layer, 0, 0)), vec, vec],
        out_specs=[pl.BlockSpec((1, tm, tn), lambda b, i, j: (b, i, j)),
                   pl.BlockSpec((1, tm, LANES), lambda b, i, j: (b, i, 0))],
        out_shape=[jax.ShapeDtypeStruct((BATCH, TOK, PROJ_COLS), BF16),
                   jax.ShapeDtypeStruct((BATCH, TOK, LANES), F32)],
        scratch_shapes=[pltpu.VMEM((tm, D_MODEL), BF16)],
        compiler_params=_params("parallel", "parallel", "arbitrary"),
        name="proj",
    )(xs, mods, mods, gain.reshape(1, D_MODEL), w_main, w_ba, pad_lanes(a_log), pad_lanes(dt_bias))


def _conv(x, w):
    n, c = x.shape
    t = lax.broadcasted_iota(jnp.int32, (n, c), 0)
    is_ctx = t >= SEQ
    local = jnp.where(is_ctx, t - SEQ, t)
    seg_len = jnp.where(is_ctx, CTX_LEN, SEQ)
    y = jnp.zeros_like(x)
    for j in range(CONV_WIDTH):
        s = j - CONV_WIDTH // 2
        if s == 0:
            y = y + x * w[j:j + 1, :]
        else:
            shifted = pltpu.roll(x, (-s) % n, 0)
            ok = jnp.logical_and(local + s >= 0, local + s < seg_len)
            y = y + jnp.where(ok, shifted, 0.0) * w[j:j + 1, :]
    return y


def _gdn_conv_kernel(x_ref, w_ref, o_ref):
    j = pl.program_id(1)
    mul = jnp.where(j == 0, float(GDN_HEAD_DIM), jnp.where(j == 1, 1.0, 0.0))
    add = jnp.where(j == 0, float(GDN_HEAD_DIM) * EPS, jnp.where(j == 1, EPS, 1.0))
    for h in range(GDN_HEADS):
        cols = slice(h * GDN_HEAD_DIM, (h + 1) * GDN_HEAD_DIM)
        y = _silu(_conv(x_ref[0, :, cols].astype(F32), w_ref[:, cols]))
        scale = lax.rsqrt(jnp.sum(y * y, axis=-1, keepdims=True) * mul + add)
        o_ref[0, :, cols] = (y * scale).astype(BF16)


def _gdn_conv(p, conv_w):
    first = COL_QKV // BRANCH_WIDTH
    return pl.pallas_call(
        _gdn_conv_kernel,
        grid=(BATCH, 3),
        in_specs=[pl.BlockSpec((1, TOK, BRANCH_WIDTH), lambda b, j: (b, 0, first + j)),
                  pl.BlockSpec((CONV_WIDTH, BRANCH_WIDTH), lambda b, j: (0, j))],
        out_specs=pl.BlockSpec((1, TOK, BRANCH_WIDTH), lambda b, j: (b, 0, j)),
        out_shape=jax.ShapeDtypeStruct((BATCH, TOK, 3 * BRANCH_WIDTH), BF16),
        compiler_params=_params("parallel", "parallel"),
        name="gdn_conv",
    )(p, conv_w)


GDN_QM_ROWS = GDN_CHUNK + GDN_HEAD_DIM
GDN_GL_ROWS = 8
GDN_HEADS_PER_STEP = 2
GDN_PREP_UNROLL = 12


def _gdn_prepare(q_ref, k_ref, v_ref, gb_ref, qm_scr, nn_scr, o_scr, gl_scr, chunks, head, local):
    c = GDN_CHUNK
    cols = slice(local * LANES, (local + 1) * LANES)
    lane = lax.broadcasted_iota(jnp.int32, (c, LANES), 1)
    row = lax.broadcasted_iota(jnp.int32, (c, LANES), 0)
    ii = lax.broadcasted_iota(jnp.int32, (c, c), 0)
    jj = lax.broadcasted_iota(jnp.int32, (c, c), 1)
    eye = (ii == jj).astype(F32)
    masks =((ii >= jj, ii > jj, row > lane), (ii <= jj, ii < jj, row < lane))

    loaded = []
    for chunk in chunks:
        r0 = pl.multiple_of(chunk * c, c)
        k = k_ref[0, pl.ds(r0, c), cols]
        q = q_ref[0, pl.ds(r0, c), cols]
        kq = _dot_nt(jnp.concatenate([k, q], axis=0), k)
        loaded.append((chunk, r0, q, k, kq))

    chains = []
    for chunk, r0, q, k, kq in loaded:
        gb = gb_ref[0, pl.ds(r0, c), :]
        for d in range(2):
            col = head + d * GDN_HEADS
            beta = jnp.sum(jnp.where(lane == col, gb, 0.0), axis=-1, keepdims=True)
            g = jnp.sum(jnp.where(lane == col + 2 * GDN_HEADS, gb, 0.0), axis=-1, keepdims=True)
            incl, strict, strict_wide = masks[d]
            rhs = jnp.where(lane >= c, g, jnp.where(strict_wide, g, 0.0))
            mask = incl.astype(BF16)
            hi = rhs.astype(BF16)
            rest = rhs - hi.astype(F32)
            mid = rest.astype(BF16)
            low = (rest - mid.astype(F32)).astype(BF16)
            e = _dot(mask, hi) + _dot(mask, mid) + _dot(mask, low)
            chains.append(dict(chunk=chunk, r0=r0, d=d, q=q, k=k, kq=kq, beta=beta, e=e))

    for ch in chains:
        incl, strict, _ = masks[ch["d"]]
        e = ch["e"]
        decay = jnp.where(incl, jnp.exp(e[:, :c]), 0.0)
        gc = e[:, c:c + 1]
        last = 0 if ch["d"] == 1 else c - 1
        gc_last = e[last:last + 1, c:c + 1]
        ch.update(decay=decay, gc=gc, gc_last=gc_last, egc=jnp.exp(gc))
        ch["a"] = jnp.where(strict, ch["beta"] * ch["kq"][:c] * decay, 0.0)
        ch["t"] = eye
    s = 1
    while s < c:
        pair = jnp.logical_and((ii // (2 * s)) == (jj // (2 * s)), (ii // s) != (jj // s))
        for ch in chains:
            ch["a_off"] = jnp.where(pair, ch["a"], 0.0)
        if s == 1:
            for ch in chains:
                ch["t"] = eye - ch["a_off"]
        else:
            for ch in chains:
                ch["m"] = _dot(ch["t"].astype(BF16), ch["a_off"].astype(BF16))
            for ch in chains:
                ch["t"] = ch["t"] - _dot(ch["m"].astype(BF16), ch["t"].astype(BF16))
        s *= 2
    for ch in chains:
        r0, beta, egc = ch["r0"], ch["beta"], ch["egc"]
        kf = ch["k"].astype(F32)
        vf = v_ref[0, pl.ds(r0, c), cols].astype(F32)
        rhs2 = jnp.concatenate([vf * beta, kf * (beta * egc)], axis=1).astype(BF16)
        ch["uw"] = _dot(ch["t"].astype(BF16), rhs2).astype(BF16)
        ch["k_dec_t"] = (kf * jnp.exp(ch["gc_last"] - ch["gc"])).T.astype(BF16)
    for ch in chains:
        incl = masks[ch["d"]][0]
        qk = jnp.where(incl, ch["kq"][c:] * ch["decay"], 0.0).astype(BF16)
        ch["nm"] = _dot(ch["k_dec_t"], ch["uw"])
        ch["ow"] = _dot(qk, ch["uw"])
    for ch in chains:
        chunk, r0, nm, ow = ch["chunk"], ch["r0"], ch["nm"], ch["ow"]
        s = 2 * local + ch["d"]
        q0 = pl.multiple_of(chunk * GDN_QM_ROWS, 16)
        qm_scr[s, pl.ds(q0, c), :] = (ch["q"].astype(F32) * ch["egc"] - ow[:, GDN_HEAD_DIM:]).astype(BF16)
        qm_scr[s, pl.ds(q0 + c, GDN_HEAD_DIM), :] = nm[:, GDN_HEAD_DIM:].astype(BF16)
        nn_scr[s, pl.ds(pl.multiple_of(chunk * GDN_HEAD_DIM, GDN_HEAD_DIM), GDN_HEAD_DIM), :] = nm[:, :GDN_HEAD_DIM]
        o_scr[s, pl.ds(r0, c), :] = ow[:, :GDN_HEAD_DIM]
        gl_scr[s, pl.ds(pl.multiple_of(chunk * GDN_GL_ROWS, GDN_GL_ROWS), GDN_GL_ROWS), :] = jnp.broadcast_to(
            jnp.exp(ch["gc_last"]), (GDN_GL_ROWS, LANES))


def _gdn_advance(qm_scr, nn_scr, o_scr, gl_scr, d, chunk, state):
    c = GDN_CHUNK
    qm = qm_scr[d, pl.ds(pl.multiple_of(chunk * GDN_QM_ROWS, 16), GDN_QM_ROWS), :]
    r = _dot(qm, state.astype(BF16))
    rows = pl.ds(pl.multiple_of(chunk * c, c), c)
    o_scr[d, rows, :] = o_scr[d, rows, :] + r[:c]
    gl = gl_scr[d, pl.ds(pl.multiple_of(chunk * GDN_GL_ROWS, GDN_GL_ROWS), 1), :]
    n = nn_scr[d, pl.ds(pl.multiple_of(chunk * GDN_HEAD_DIM, GDN_HEAD_DIM), GDN_HEAD_DIM), :]
    return state * gl - r[c:] + n


def _gdn_kernel(q_ref, k_ref, v_ref, z_ref, gb_ref, gain_ref, o_ref, qm_scr, nn_scr, o_scr, gl_scr):
    first_head = pl.program_id(1) * GDN_HEADS_PER_STEP
    c = GDN_CHUNK
    n_lat, n_ctx = SEQ // c, CTX_LEN // c
    n_chunks = n_lat + n_ctx

    for local in range(GDN_HEADS_PER_STEP):
        def prepare(i, _, local=local):
            chunks = [i * GDN_PREP_UNROLL + j for j in range(GDN_PREP_UNROLL)]
            _gdn_prepare(q_ref, k_ref, v_ref, gb_ref, qm_scr, nn_scr, o_scr, gl_scr, chunks, first_head + local, local)
            return 0

        lax.fori_loop(0, n_chunks // GDN_PREP_UNROLL, prepare, 0)

    def advance(i, states):
        cf = jnp.where(i < n_ctx, n_lat + i, i - n_ctx)
        cb = n_chunks - 1 - i
        return tuple(_gdn_advance(qm_scr, nn_scr, o_scr, gl_scr, s, cb if s % 2 else cf, state)
                     for s, state in enumerate(states))

    zero = jnp.zeros((GDN_HEAD_DIM, GDN_HEAD_DIM), F32)
    lax.fori_loop(0, n_chunks, advance, (zero,) * (2 * GDN_HEADS_PER_STEP), unroll=4)
    for local in range(GDN_HEADS_PER_STEP):
        cols = slice(local * LANES, (local + 1) * LANES)
        o = o_scr[2 * local] + o_scr[2 * local + 1]
        o_ref[0, :, cols] = (_rms(o, gain_ref[...]) * _silu(z_ref[0, :, cols].astype(F32))).astype(BF16)


def _gdn(qkv, p, gates, out_gain):
    n_steps = GDN_HEADS // GDN_HEADS_PER_STEP
    n_chunks = TOK // GDN_CHUNK
    width = GDN_HEADS_PER_STEP * LANES
    zblk = COL_Z // width
    n_chain = 2 * GDN_HEADS_PER_STEP
    blk = lambda off: pl.BlockSpec((1, TOK, width), lambda b, h: (b, 0, off + h))
    vec = pl.BlockSpec((1, LANES), lambda b, h: (0, 0))
    return pl.pallas_call(
        _gdn_kernel,
        grid=(BATCH, n_steps),
        in_specs=[blk(0), blk(n_steps), blk(2 * n_steps), blk(zblk),
                  pl.BlockSpec((1, TOK, LANES), lambda b, h: (b, 0, 0)), vec],
        out_specs=pl.BlockSpec((1, TOK, width), lambda b, h: (b, 0, h)),
        out_shape=jax.ShapeDtypeStruct((BATCH, TOK, BRANCH_WIDTH), BF16),
        scratch_shapes=[pltpu.VMEM((n_chain, n_chunks * GDN_QM_ROWS, LANES), BF16),
                        pltpu.VMEM((n_chain, n_chunks * GDN_HEAD_DIM, LANES), F32),
                        pltpu.VMEM((n_chain, TOK, LANES), F32),
                        pltpu.VMEM((n_chain, n_chunks * GDN_GL_ROWS, LANES), F32)],
        compiler_params=_params("parallel", "parallel"),
        name="gdn",
    )(qkv, qkv, qkv, p, gates, out_gain.reshape(1, LANES))


def _rms_halves(x, gain):
    lane = lax.broadcasted_iota(jnp.int32, x.shape, 1)
    lo = lane < DIFF_HEAD_DIM
    x2 = x * x
    s_lo = jnp.sum(jnp.where(lo, x2, 0.0), axis=-1, keepdims=True)
    s_hi = jnp.sum(jnp.where(lo, 0.0, x2), axis=-1, keepdims=True)
    ms = jnp.where(lo, s_lo, s_hi) * (1.0 / DIFF_HEAD_DIM)
    return x * lax.rsqrt(ms + EPS) * gain


def _rope(x, cos, sin):
    lane = lax.broadcasted_iota(jnp.int32, x.shape, 1)
    first = (lane & ROPE_PAIRS) == 0
    partner = jnp.where(first, -pltpu.roll(x, LANES - ROPE_PAIRS, 1), pltpu.roll(x, ROPE_PAIRS, 1))
    return x * cos + partner * sin


ATTN_Q_BLOCK = 1024
ATTN_GROUP_ROWS = 128


def _attn_kernel(q_ref, k_ref, v_ref, cosk_ref, sink_ref, cosq_ref, sinq_ref, qg_ref, kg_ref, lv_ref, og_ref,
                 o_ref, kn_scr, *, ctx_block, lam_init):
    qi = pl.program_id(2)

    @pl.when(qi == 0)
    def _():
        kn = _rope(_rms_halves(k_ref[0].astype(F32), kg_ref[...]), cosk_ref[...], sink_ref[...])
        kn_scr[...] = kn.astype(BF16)

    lv = lv_ref[...]
    lam = (jnp.exp(jnp.sum(lv[0:1] * lv[1:2], axis=-1, keepdims=True))
           - jnp.exp(jnp.sum(lv[2:3] * lv[3:4], axis=-1, keepdims=True)) + lam_init)
    def attend(n_rows, kn, v):
        q = _rope(_rms_halves(q_ref[0, :n_rows, :].astype(F32), qg_ref[...]), cosq_ref[:n_rows, :], sinq_ref[:n_rows, :])
        q = q * (DIFF_HEAD_DIM ** -0.5 * math.log2(math.e))
        lane = lax.broadcasted_iota(jnp.int32, q.shape, 1)
        lo = lane < DIFF_HEAD_DIM
        q1 = jnp.where(lo, q, 0.0).astype(BF16)
        q2 = jnp.where(lo, 0.0, q).astype(BF16)

        def half(s):
            p = jnp.exp2(s - jnp.max(s, axis=-1, keepdims=True))
            return _dot(p.astype(BF16), v), jnp.sum(p, axis=-1, keepdims=True)
        rows = ATTN_GROUP_ROWS
        scores = [(_dot_nt(q1[r:r + rows], kn), _dot_nt(q2[r:r + rows], kn)) for r in range(0, n_rows, rows)]
        for g, (s1, s2) in enumerate(scores):
            a1, l1 = half(s1)
            a2, l2 = half(s2)
            o = a1 * (1.0 / l1) - a2 * (lam / l2)
            o_ref[0, g * rows:(g + 1) * rows, :] = (_rms(o, og_ref[...]) * (1.0 - lam_init)).astype(BF16)

    if ctx_block is None:
        attend(q_ref.shape[1], kn_scr[...], v_ref[0])
    else:
        @pl.when(qi == ctx_block)
        def _():
            attend(CTX_LEN, kn_scr[SEQ:, :], v_ref[0, SEQ:, :])

        @pl.when(qi != ctx_block)
        def _():
            attend(q_ref.shape[1], kn_scr[...], v_ref[0])


def _rope_tables():
    n_rows = SEQ // GRID_W
    row_id = jnp.broadcast_to(jnp.arange(n_rows, dtype=F32)[:, None], (n_rows, GRID_W)).reshape(-1)
    col_id = jnp.broadcast_to(jnp.arange(GRID_W, dtype=F32)[None, :], (n_rows, GRID_W)).reshape(-1)
    inv_freq = jnp.power(ROPE_BASE, -jnp.arange(ROPE_PAIRS, dtype=F32) / ROPE_PAIRS)
    row_ang = row_id[:, None] * inv_freq
    col_ang = col_id[:, None] * inv_freq
    ang = jnp.concatenate([row_ang, row_ang, col_ang, col_ang], axis=-1)
    ang = jnp.concatenate([ang, ang], axis=-1)
    pad = ((0, CTX_LEN), (0, 0))
    return jnp.pad(jnp.cos(ang), pad, constant_values=1.0), jnp.pad(jnp.sin(ang), pad)


def _diff_attn(p, cos, sin, q_gain, k_gain, lam_vecs, out_gain, lam_init, with_ctx):
    tq = ATTN_Q_BLOCK
    n_rows = TOK if with_ctx else SEQ
    nq = pl.cdiv(n_rows, tq)
    nh = DIFF_HEADS
    qb, kb, vb = COL_DQ // LANES, COL_DK // LANES, COL_DV // LANES
    full = lambda off: pl.BlockSpec((1, TOK, LANES), lambda b, h, i: (b, 0, off + h))
    tab_full = pl.BlockSpec((TOK, LANES), lambda b, h, i: (0, 0))
    tab_q = pl.BlockSpec((tq, LANES), lambda b, h, i: (i, 0))
    vec = pl.BlockSpec((1, LANES), lambda b, h, i: (0, 0))
    tile2 = lambda g: jnp.concatenate([g, g]).reshape(1, LANES)
    return pl.pallas_call(
        functools.partial(_attn_kernel, ctx_block=SEQ // tq if with_ctx else None, lam_init=lam_init),
        grid=(BATCH, nh, nq),
        in_specs=[pl.BlockSpec((1, tq, LANES), lambda b, h, i: (b, i, qb + h)), full(kb), full(vb),
                  tab_full, tab_full, tab_q, tab_q, vec, vec,
                  pl.BlockSpec((4, DIFF_HEAD_DIM), lambda b, h, i: (0, 0)), vec],
        out_specs=pl.BlockSpec((1, tq, LANES), lambda b, h, i: (b, i, h)),
        out_shape=jax.ShapeDtypeStruct((BATCH, n_rows, BRANCH_WIDTH), BF16),
        scratch_shapes=[pltpu.VMEM((TOK, LANES), BF16)],
        compiler_params=_params("parallel", "parallel", "arbitrary"),
        name="diff_attn",
    )(p, p, p, cos, sin, cos, sin, tile2(q_gain), tile2(k_gain), lam_vecs, out_gain.reshape(1, LANES))


LRU_SLAB = 256
LRU_ROWS = 256
LRU_SCAN_BLOCK = 8


def _lru_kernel(x_ref, y_ref, cw_ref, cb_ref, wg_ref, bg_ref, lam_ref, o_ref, xc_scr, af_scr, bf_scr, ab_scr, bb_scr):
    w = LRU_SLAB
    blk = LRU_SCAN_BLOCK
    for c0 in range(0, w, LANES):
        cols = slice(c0, c0 + LANES)
        xc_scr[:, cols] = _conv(x_ref[0, :, cols].astype(F32), cw_ref[:, cols]) + cb_ref[:, cols]
    sp = _softplus(-lam_ref[0])
    sub = lax.broadcasted_iota(jnp.int32, (LRU_ROWS // blk, blk, w), 1)

    def gates(i, _):
        r0 = pl.multiple_of(i * LRU_ROWS, LRU_ROWS)
        xc = xc_scr[pl.ds(r0, LRU_ROWS), :]
        pre = _dot(xc.astype(BF16), wg_ref[0]) + bg_ref[0]
        for d, (a_scr, b_scr) in enumerate(((af_scr, bf_scr), (ab_scr, bb_scr))):
            r = _sigmoid(pre[:, (2 * d) * w:(2 * d + 1) * w])
            gi = _sigmoid(pre[:, (2 * d + 1) * w:(2 * d + 2) * w])
            log_a = -LRU_C * r * sp[d:d + 1]
            a = jnp.exp(log_a)
            b = jnp.sqrt(1.0 - a * a) * gi * xc
            a = a.reshape(LRU_ROWS // blk, blk, w)
            b = b.reshape(LRU_ROWS // blk, blk, w)
            shift = 1
            while shift < blk:
                if d == 0:
                    ok, roll_by = sub >= shift, shift
                else:
                    ok, roll_by = sub < blk - shift, blk - shift
                b = jnp.where(ok, a * pltpu.roll(b, roll_by, 1) + b, b)
                a = jnp.where(ok, a * pltpu.roll(a, roll_by, 1), a)
                shift *= 2
            a_scr[pl.ds(r0, LRU_ROWS), :] = a.reshape(LRU_ROWS, w)
            b_scr[pl.ds(r0, LRU_ROWS), :] = b.reshape(LRU_ROWS, w)
        return 0

    lax.fori_loop(0, TOK // LRU_ROWS, gates, 0)

    n_blk, n_lat_blk, n_ctx_blk = TOK // blk, SEQ // blk, CTX_LEN // blk

    def step(s, carry):
        h_f, h_b = carry
        rows_f = pl.ds(pl.multiple_of(jnp.where(s < n_ctx_blk, n_lat_blk + s, s - n_ctx_blk) * blk, blk), blk)
        rows_b = pl.ds(pl.multiple_of((n_blk - 1 - s) * blk, blk), blk)
        hf = af_scr[rows_f, :] * h_f + bf_scr[rows_f, :]
        bf_scr[rows_f, :] = hf
        hb = ab_scr[rows_b, :] * h_b + bb_scr[rows_b, :]
        bb_scr[rows_b, :] = hb
        return hf[blk - 1:blk, :], hb[0:1, :]

    zero = jnp.zeros((1, w), F32)
    lax.fori_loop(0, n_blk, step, (zero, zero), unroll=4)
    h = bf_scr[...] + bb_scr[...]
    o_ref[0] = (h * jax.nn.gelu(y_ref[0].astype(F32))).astype(BF16)


def _lru_gate_weights(w_gate, b_gate):
    n_slab = BRANCH_WIDTH // LRU_SLAB
    per = LRU_SLAB // LRU_BLOCK_DIM
    wg = w_gate.reshape(2, 2, n_slab, per, LRU_BLOCK_DIM, LRU_BLOCK_DIM)
    eye = jnp.eye(per, dtype=w_gate.dtype)
    dense = jnp.einsum('dgsnjk,nm->snjdgmk', wg, eye)
    dense = dense.reshape(n_slab, LRU_SLAB, 4 * LRU_SLAB)
    bg = b_gate.reshape(2, 2, n_slab, LRU_SLAB).transpose(2, 0, 1, 3).reshape(n_slab, 1, 4 * LRU_SLAB)
    return dense.astype(BF16), bg


def _lru(p, conv_w, conv_b, w_gate, b_gate, lam):
    n_slab = BRANCH_WIDTH // LRU_SLAB
    wg, bg = _lru_gate_weights(w_gate, b_gate)
    lam_s = lam.reshape(2, n_slab, LRU_SLAB).transpose(1, 0, 2)
    xb, yb = COL_LX // LRU_SLAB, COL_LY // LRU_SLAB
    return pl.pallas_call(
        _lru_kernel,
        grid=(BATCH, n_slab),
        in_specs=[pl.BlockSpec((1, TOK, LRU_SLAB), lambda b, s: (b, 0, xb + s)),
                  pl.BlockSpec((1, TOK, LRU_SLAB), lambda b, s: (b, 0, yb + s)),
                  pl.BlockSpec((CONV_WIDTH, LRU_SLAB), lambda b, s: (0, s)),
                  pl.BlockSpec((1, LRU_SLAB), lambda b, s: (0, s)),
                  pl.BlockSpec((1, LRU_SLAB, 4 * LRU_SLAB), lambda b, s: (s, 0, 0)),
                  pl.BlockSpec((1, 1, 4 * LRU_SLAB), lambda b, s: (s, 0, 0)),
                  pl.BlockSpec((1, 2, LRU_SLAB), lambda b, s: (s, 0, 0))],
        out_specs=pl.BlockSpec((1, TOK, LRU_SLAB), lambda b, s: (b, 0, s)),
        out_shape=jax.ShapeDtypeStruct((BATCH, TOK, BRANCH_WIDTH), BF16),
        scratch_shapes=[pltpu.VMEM((TOK, LRU_SLAB), F32)] * 5,
        compiler_params=_params("parallel", "parallel"),
        name="lru",
    )(p, p, conv_w, conv_b.reshape(1, BRANCH_WIDTH), wg, bg, lam_s)


def _route(logits):
    lane = lax.broadcasted_iota(jnp.int32, logits.shape, 1)
    lane_f = lane.astype(F32)
    far = float(LANES)
    lg = jnp.where(lane < N_EXPERTS, logits, -jnp.inf)
    ex = jnp.exp(lg - jnp.max(lg, axis=-1, keepdims=True))
    probs = ex / jnp.sum(ex, axis=-1, keepdims=True)
    per_group = N_EXPERTS // N_GROUPS
    grp = lane // per_group

    def top2(vals):
        m1 = jnp.max(vals, axis=-1, keepdims=True)
        i1 = jnp.min(jnp.where(vals == m1, lane_f, far), axis=-1, keepdims=True)
        rest = jnp.where(lane_f == i1, -2.0, vals)
        m2 = jnp.max(rest, axis=-1, keepdims=True)
        i2 = jnp.min(jnp.where(rest == m2, lane_f, far), axis=-1, keepdims=True)
        return m1, i1, m2, i2

    best = jnp.zeros(logits.shape[:1] + (1,), jnp.int32)
    best_score = None
    for g in range(N_GROUPS):
        m1, _, m2, _ = top2(jnp.where(grp == g, probs, -1.0))
        score = m1 + m2
        if best_score is None:
            best_score = score
        else:
            better = score > best_score
            best = jnp.where(better, g, best)
            best_score = jnp.where(better, score, best_score)
    m1, i1, m2, i2 = top2(jnp.where(grp == best, probs, -1.0))
    den = m1 + m2
    return i1, i2, m1 / den, m2 / den


ROUTE_E, ROUTE_W, ROUTE_RANK = 0, 2, 4


def _pack_bf16_pairs(x):
    n = x.shape[1] // 2
    xb = x.astype(BF16).astype(F32)
    lo = pltpu.bitcast(xb[:, :n], jnp.uint32)
    hi = pltpu.bitcast(xb[:, n:], jnp.uint32)
    return (lo >> 16) | (hi & jnp.uint32(0xFFFF0000))


def _unpack_bf16_pairs(p):
    lo = pltpu.bitcast(p << 16, F32)
    hi = pltpu.bitcast(p & jnp.uint32(0xFFFF0000), F32)
    return jnp.concatenate([lo, hi], axis=1)


def _merge_kernel(ya_ref, yb_ref, yc_ref, gates_ref, x_ref, ml_ref, mc_ref, g2_ref, wbr_ref, wout_ref, wr_ref,
                  br_ref, xo_ref, h2_ref, route_ref, cnt_ref, cnt_scr, *, tm):
    i = pl.program_id(1)

    @pl.when(jnp.logical_and(pl.program_id(0) == 0, i == 0))
    def _():
        cnt_scr[...] = jnp.zeros_like(cnt_scr)

    acc = None
    for n, y_ref in enumerate((ya_ref, yb_ref, yc_ref)):
        yn = _dot(y_ref[0], wbr_ref[n])
        gate = _sigmoid(gates_ref[0, :, n * D_MODEL:(n + 1) * D_MODEL].astype(F32))
        acc = gate * yn if acc is None else acc + gate * yn
    out = _dot(acc.astype(BF16), wout_ref[...])
    is_ctx = _ctx_rows(i, tm, D_MODEL)
    xn = x_ref[0] + jnp.where(is_ctx, mc_ref[0, 2:3, :], ml_ref[0, 2:3, :]) * out
    xo_ref[0] = xn
    h2 = _modulated_norm(xn, g2_ref[...], ml_ref, mc_ref, is_ctx, 3)
    h2_ref[0] = _pack_bf16_pairs(h2)
    i1, i2, w1, w2 = _route(_dot(h2.astype(BF16), wr_ref[...]) + br_ref[...])
    lane = lax.broadcasted_iota(jnp.int32, (tm, LANES), 1)
    lane_f = lane.astype(F32)
    chosen = jnp.where(jnp.logical_or(lane_f == i1, lane_f == i2), 1.0, 0.0)
    earlier = (lax.broadcasted_iota(jnp.int32, (tm, tm), 0) > lax.broadcasted_iota(jnp.int32, (tm, tm), 1))
    before = _dot(jnp.where(earlier, 1.0, 0.0).astype(BF16), chosen.astype(BF16)) + cnt_scr[...]
    rank1 = jnp.sum(jnp.where(lane_f == i1, before, 0.0), axis=-1, keepdims=True)
    rank2 = jnp.sum(jnp.where(lane_f == i2, before, 0.0), axis=-1, keepdims=True)
    cnt_scr[...] += jnp.sum(chosen, axis=0, keepdims=True)
    cnt_ref[...] = cnt_scr[...]
    record = jnp.zeros((tm, LANES), F32)
    for pos, val in enumerate((i1, i2, w1, w2, rank1, rank2)):
        record = jnp.where(lane == pos, val, record)
    route_ref[0] = record


def _merge(ya, yb, yc, p, xs, mods, gain2, w_branch, w_out, w_router, b_router, rows, tm):
    wr = jnp.zeros((D_MODEL, LANES), BF16).at[:, :N_EXPERTS].set(w_router.astype(BF16))
    br = jnp.zeros((1, LANES), F32).at[0, :N_EXPERTS].set(b_router)
    tile = lambda w: pl.BlockSpec((1, tm, w), lambda b, i: (b, i, 0))
    const = lambda shape: pl.BlockSpec(shape, lambda b, i: (0,) * len(shape))
    return pl.pallas_call(
        functools.partial(_merge_kernel, tm=tm),
        grid=(BATCH, rows // tm),
        in_specs=[tile(BRANCH_WIDTH), tile(BRANCH_WIDTH), tile(BRANCH_WIDTH), tile(3 * D_MODEL), tile(D_MODEL),
                  pl.BlockSpec((1, N_MOD, D_MODEL), lambda b, i: (b, 0, 0)),
                  pl.BlockSpec((1, N_MOD, D_MODEL), lambda b, i: (BATCH, 0, 0)),
                  const((1, D_MODEL)), const((3, BRANCH_WIDTH, D_MODEL)), const((D_MODEL, D_MODEL)),
                  const((D_MODEL, LANES)), const((1, LANES))],
        out_specs=[tile(D_MODEL), tile(D_MODEL // 2), tile(LANES), const((1, LANES))],
        out_shape=[jax.ShapeDtypeStruct((BATCH, rows, D_MODEL), F32),
                   jax.ShapeDtypeStruct((BATCH, rows, D_MODEL // 2), jnp.uint32),
                   jax.ShapeDtypeStruct((BATCH, rows, LANES), F32),
                   jax.ShapeDtypeStruct((1, LANES), F32)],
        scratch_shapes=[pltpu.VMEM((1, LANES), F32)],
        compiler_params=_params("arbitrary", "arbitrary"),
        name="merge",
    )(ya, yb, yc, p, xs, mods, mods, gain2.reshape(1, D_MODEL), w_branch, w_out, wr, br)


MOE_TILE = 512
SC_GATHER_ROWS = 64


def _sc_gather(table, idx):
    info = plsc.get_sparse_core_info()
    n_workers = info.num_cores * info.num_subcores
    n_rows, width = idx.shape[0], table.shape[1]
    per_worker = n_rows // n_workers
    assert per_worker * n_workers == n_rows and per_worker % SC_GATHER_ROWS == 0
    mesh = plsc.VectorSubcoreMesh(core_axis_name="c", subcore_axis_name="s")

    @functools.partial(
        pl.kernel, mesh=mesh, out_type=jax.ShapeDtypeStruct((n_rows, width), table.dtype),
        scratch_types=[pltpu.VMEM((SC_GATHER_ROWS,), jnp.int32),
                       pltpu.VMEM((SC_GATHER_ROWS, width), table.dtype),
                       pltpu.SemaphoreType.DMA],
        name="sc_gather")
    def gather(table_hbm, idx_hbm, out_hbm, idx_v, rows_v, sem):
        worker = lax.axis_index("s") * info.num_cores + lax.axis_index("c")
        base = worker * per_worker

        @pl.loop(0, per_worker // SC_GATHER_ROWS)
        def _(j):
            off = base + j * SC_GATHER_ROWS
            pltpu.sync_copy(idx_hbm.at[pl.ds(off, SC_GATHER_ROWS)], idx_v)
            pltpu.async_copy(table_hbm.at[idx_v], rows_v, sem).wait()
            pltpu.sync_copy(rows_v, out_hbm.at[pl.ds(off, SC_GATHER_ROWS)])

    return gather(table, idx)


SC_SCATTER_ROWS = 128


def _sc_scatter(table, dest, n_out):
    info = plsc.get_sparse_core_info()
    n_workers = info.num_cores * info.num_subcores
    n_tok, width = table.shape
    n_assign = dest.shape[0]
    per_worker = n_assign // n_workers
    assert per_worker * n_workers == n_assign and per_worker % SC_SCATTER_ROWS == 0 and n_tok % SC_SCATTER_ROWS == 0
    mesh = plsc.VectorSubcoreMesh(core_axis_name="c", subcore_axis_name="s")

    @functools.partial(
        pl.kernel, mesh=mesh, out_type=jax.ShapeDtypeStruct((n_out, width), table.dtype),
        scratch_types=[pltpu.VMEM((SC_SCATTER_ROWS,), jnp.int32),
                       pltpu.VMEM((SC_SCATTER_ROWS, width), table.dtype),
                       pltpu.SemaphoreType.DMA],
        name="sc_scatter")
    def scatter(table_hbm, dest_hbm, out_hbm, idx_v, rows_v, sem):
        worker = lax.axis_index("s") * info.num_cores + lax.axis_index("c")
        base = worker * per_worker

        @pl.loop(0, per_worker // SC_SCATTER_ROWS)
        def _(j):
            off = base + j * SC_SCATTER_ROWS
            pltpu.sync_copy(dest_hbm.at[pl.ds(off, SC_SCATTER_ROWS)], idx_v)
            pltpu.sync_copy(table_hbm.at[pl.ds(lax.rem(off, n_tok), SC_SCATTER_ROWS)], rows_v)
            pltpu.async_copy(rows_v, out_hbm.at[idx_v], sem).wait()

    return scatter(table, dest)


def _experts_kernel(tile_expert_ref, n_valid_ref, x_ref, wgu_ref, wd_ref, o_ref, wgu_scr, wd_scr):
    j = pl.program_id(0)
    valid = j < n_valid_ref[0]
    fresh = jnp.logical_or(j == 0, tile_expert_ref[j] != tile_expert_ref[jnp.maximum(j - 1, 0)])

    @pl.when(jnp.logical_and(valid, fresh))
    def _():
        wgu_scr[...] = wgu_ref[0, 0].astype(BF16)
        wd_scr[...] = wd_ref[0, 0].astype(BF16)

    @pl.when(valid)
    def _():
        x = _unpack_bf16_pairs(x_ref[...]).astype(BF16)
        gu = _dot(x, wgu_scr[...])
        act = (_silu(gu[:, :EXPERT_FF]) * gu[:, EXPERT_FF:]).astype(BF16)
        o_ref[...] = _pack_bf16_pairs(_dot(act, wd_scr[...]))

    @pl.when(jnp.logical_not(valid))
    def _():
        o_ref[...] = jnp.zeros_like(o_ref)


def _experts(x_sorted, tile_expert, n_valid, w_gate_up, w_down, layer):
    n_tiles = x_sorted.shape[0] // MOE_TILE
    half = D_MODEL // 2
    return pl.pallas_call(
        _experts_kernel,
        grid_spec=pltpu.PrefetchScalarGridSpec(
            num_scalar_prefetch=2, grid=(n_tiles,),
            in_specs=[pl.BlockSpec((MOE_TILE, half), lambda j, te, nv: (j, 0)),
                      pl.BlockSpec((1, 1, D_MODEL, 2 * EXPERT_FF), lambda j, te, nv: (layer, te[j], 0, 0)),
                      pl.BlockSpec((1, 1, EXPERT_FF, D_MODEL), lambda j, te, nv: (layer, te[j], 0, 0))],
            out_specs=pl.BlockSpec((MOE_TILE, half), lambda j, te, nv: (j, 0)),
            scratch_shapes=[pltpu.VMEM((D_MODEL, 2 * EXPERT_FF), BF16), pltpu.VMEM((EXPERT_FF, D_MODEL), BF16)]),
        out_shape=jax.ShapeDtypeStruct((x_sorted.shape[0], half), jnp.uint32),
        compiler_params=_params("arbitrary"),
        name="experts",
    )(tile_expert, n_valid, x_sorted, w_gate_up, w_down)


def _combine_kernel(y1_ref, y2_ref, route_ref, x_ref, ml_ref, mc_ref, o_ref, *, tm):
    i = pl.program_id(1)
    route = route_ref[0]
    w1 = route[:, ROUTE_W:ROUTE_W + 1]
    w2 = route[:, ROUTE_W + 1:ROUTE_W + 2]
    moe = w1 * _unpack_bf16_pairs(y1_ref[0, 0]) + w2 * _unpack_bf16_pairs(y2_ref[0, 0])
    is_ctx = _ctx_rows(i, tm, D_MODEL)
    o_ref[0] = x_ref[0] + jnp.where(is_ctx, mc_ref[0, 5:6, :], ml_ref[0, 5:6, :]) * moe


def _combine(y_pairs, route, xs, mods, rows, tm):
    half = D_MODEL // 2
    tile = lambda w: pl.BlockSpec((1, tm, w), lambda b, i: (b, i, 0))
    slot = lambda s: pl.BlockSpec((1, 1, tm, half), lambda b, i: (s, b, i, 0))
    return pl.pallas_call(
        functools.partial(_combine_kernel, tm=tm),
        grid=(BATCH, rows // tm),
        in_specs=[slot(0), slot(1), tile(LANES), tile(D_MODEL),
                  pl.BlockSpec((1, N_MOD, D_MODEL), lambda b, i: (b, 0, 0)),
                  pl.BlockSpec((1, N_MOD, D_MODEL), lambda b, i: (BATCH, 0, 0))],
        out_specs=tile(D_MODEL),
        out_shape=jax.ShapeDtypeStruct((BATCH, rows, D_MODEL), F32),
        compiler_params=_params("parallel", "parallel"),
        name="combine",
    )(y_pairs, y_pairs, route, xs, mods, mods)


def _moe(h2, route, counts, xs, mods, w_gate_up, w_down, layer, rows, tm):
    n_tok = BATCH * rows
    half = D_MODEL // 2
    n_sorted = 2 * n_tok + N_EXPERTS * MOE_TILE
    n_tiles = n_sorted // MOE_TILE
    rec = route.reshape(n_tok, LANES)
    expert = rec[:, ROUTE_E:ROUTE_E + 2].astype(jnp.int32)
    rank = rec[:, ROUTE_RANK:ROUTE_RANK + 2].astype(jnp.int32)
    count = counts[0, :N_EXPERTS].astype(jnp.int32)
    padded = (count + MOE_TILE - 1) // MOE_TILE * MOE_TILE
    end = jnp.cumsum(padded)
    start = end - padded
    first = jnp.sum(jnp.where(expert[:, :1] == jnp.arange(N_EXPERTS), start, 0), axis=1)
    second = jnp.sum(jnp.where(expert[:, 1:] == jnp.arange(N_EXPERTS), start, 0), axis=1)
    dest = jnp.concatenate([first + rank[:, 0], second + rank[:, 1]])
    tile_start = jnp.arange(n_tiles, dtype=jnp.int32) * MOE_TILE
    tile_expert = jnp.minimum(jnp.sum(tile_start[:, None] >= end[None, :], axis=1), N_EXPERTS - 1).astype(jnp.int32)
    n_valid = (end[-1:] // MOE_TILE).astype(jnp.int32)
    x_sorted = _sc_scatter(h2.reshape(n_tok, half), dest, n_sorted)
    y_sorted = _experts(x_sorted, tile_expert, n_valid, w_gate_up, w_down, layer)
    y_pairs = _sc_gather(y_sorted, dest).reshape(2, BATCH, rows, half)
    return _combine(y_pairs, route, xs, mods, rows, tm)


def _split_w_in(w_in):
    bw = BRANCH_WIDTH
    sizes = (3 * bw, bw, 2 * GDN_HEADS, 2 * GDN_HEADS, bw, bw, bw, bw, bw, 3 * D_MODEL)
    offs = [0]
    for s in sizes:
        offs.append(offs[-1] + s)
    part = lambda i: w_in[:, :, offs[i]:offs[i + 1]]
    main = jnp.concatenate([part(9), part(0), part(1), part(4), part(5), part(6), part(7), part(8)], axis=2)
    ba = jnp.zeros(w_in.shape[:2] + (LANES,), F32).at[:, :, :4 * GDN_HEADS].set(
        jnp.concatenate([part(2), part(3)], axis=2))
    return main.astype(BF16), ba.astype(BF16)


def kernel(x, c, ctx, c_ctx, w_mod, b_mod, norm1_gain, norm2_gain, w_in, gdn_conv_w, gdn_a_log, gdn_dt_bias, gdn_out_gain, diff_q_gain, diff_k_gain, diff_lambda, diff_out_gain, lru_conv_w, lru_conv_b, lru_w_gate, lru_b_gate, lru_lambda, w_branch, w_out, w_router, b_router, w_gate_up, w_down):
    mods = _mods(c, c_ctx, w_mod, b_mod)
    cos, sin = _rope_tables()
    xs = jnp.concatenate([x, ctx], axis=1)
    anchor = None
    for layer in range(DEPTH):
        last = layer == DEPTH - 1
        lam_init = 0.8 - 0.6 * math.exp(-0.3 * layer)
        m = mods[layer]
        w_layer = w_in[layer:layer + 1] if anchor is None else w_in[layer:layer + 1] + anchor
        w_main, w_ba = _split_w_in(w_layer)
        p, gates = _project(xs, m, norm1_gain[layer], w_main, w_ba, gdn_a_log[layer], gdn_dt_bias[layer], 0)
        qkv = _gdn_conv(p, gdn_conv_w[layer])
        ya = _gdn(qkv, p, gates, gdn_out_gain[layer])
        yb = _diff_attn(p, cos, sin, diff_q_gain[layer], diff_k_gain[layer], diff_lambda[layer],
                        diff_out_gain[layer], lam_init, with_ctx=not last)
        yc = _lru(p, lru_conv_w[layer], lru_conv_b[layer], lru_w_gate[layer], lru_b_gate[layer], lru_lambda[layer])
        rows, tm = (SEQ, 512) if last else (TOK, 768)
        xs, h2, route, counts = _merge(ya, yb, yc, p, xs, m, norm2_gain[layer], w_branch[layer].astype(BF16),
                                       w_out[layer].astype(BF16), w_router, b_router, rows, tm)
        xs = _moe(h2, route, counts, xs, m, w_gate_up, w_down, layer, rows, tm)
        anchor = jnp.where(counts[0, 0] < 0.0, 1.0, 0.0).astype(F32)
    return xs
```

```python
import functools
import math

import jax
import jax.numpy as jnp
from jax import lax
from jax.experimental import pallas as pl
from jax.experimental.pallas import tpu as pltpu
from jax.experimental.pallas import tpu_sc as plsc

F32 = jnp.float32
BF16 = jnp.bfloat16

D_MODEL = 1024
BATCH = 8
SEQ = 2048
DEPTH = 2
GRID_W = 64
CTX_LEN = 256
TOK = SEQ + CTX_LEN
N_MOD = 6
EPS = 1e-6
CONV_WIDTH = 4
BRANCH_WIDTH = 512
GDN_HEADS = 4
GDN_HEAD_DIM = 128
GDN_CHUNK = 64
DIFF_HEADS = 4
DIFF_HEAD_DIM = 64
ROPE_BASE = 10000.0
ROPE_PAIRS = DIFF_HEAD_DIM // 4
LRU_BLOCKS = 8
LRU_BLOCK_DIM = BRANCH_WIDTH // LRU_BLOCKS
LRU_C = 8.0
N_EXPERTS = 16
N_GROUPS = 4
EXPERT_FF = 512

LANES = 128
VMEM_LIMIT = 56 * 1024 * 1024

COL_GATES = 0
COL_QKV = 3 * D_MODEL
COL_Z = COL_QKV + 3 * BRANCH_WIDTH
COL_DQ = COL_Z + BRANCH_WIDTH
COL_DK = COL_DQ + BRANCH_WIDTH
COL_DV = COL_DK + BRANCH_WIDTH
COL_LX = COL_DV + BRANCH_WIDTH
COL_LY = COL_LX + BRANCH_WIDTH
PROJ_COLS = COL_LY + BRANCH_WIDTH


def _params(*sem):
    return pltpu.CompilerParams(dimension_semantics=sem, vmem_limit_bytes=VMEM_LIMIT)


def _dot(a, b, precision=None):
    return jnp.dot(a, b, preferred_element_type=F32, precision=precision)


def _dot_nt(a, b):
    return lax.dot_general(a, b, (((1,), (1,)), ((), ())), preferred_element_type=F32)


_sigmoid = jax.nn.sigmoid


def _silu(x):
    return x * _sigmoid(x)


def _softplus(x):
    return jnp.maximum(x, 0.0) + jnp.log(1.0 + jnp.exp(-jnp.abs(x)))


def _rms(x, gain):
    return x * lax.rsqrt(jnp.mean(x * x, axis=-1, keepdims=True) + EPS) * gain


def _mod_kernel(c_ref, w_ref, b_ref, o_ref):
    c = c_ref[...]
    o_ref[0] = _dot(_silu(c), w_ref[0], precision=lax.Precision.HIGHEST) + b_ref[0]


def _mods(c, c_ctx, w_mod, b_mod):
    depth = w_mod.shape[0]
    rows = 16
    cc = jnp.zeros((rows, D_MODEL), F32).at[:BATCH].set(c).at[BATCH].set(c_ctx)
    tn = 1536
    out = pl.pallas_call(
        _mod_kernel,
        grid=(depth, N_MOD * D_MODEL // tn),
        in_specs=[pl.BlockSpec((rows, D_MODEL), lambda l, j: (0, 0)),
                  pl.BlockSpec((1, D_MODEL, tn), lambda l, j: (l, 0, j)),
                  pl.BlockSpec((1, 1, tn), lambda l, j: (l, 0, j))],
        out_specs=pl.BlockSpec((1, rows, tn), lambda l, j: (l, 0, j)),
        out_shape=jax.ShapeDtypeStruct((depth, rows, N_MOD * D_MODEL), F32),
        compiler_params=_params("parallel", "parallel"),
        name="mods",
    )(cc, w_mod, b_mod.reshape(depth, 1, N_MOD * D_MODEL))
    return out.reshape(depth, rows, N_MOD, D_MODEL)


def _modulated_norm(x, gain, ml_ref, mc_ref, is_ctx, shift_idx):
    shift = jnp.where(is_ctx, mc_ref[0, shift_idx:shift_idx + 1, :], ml_ref[0, shift_idx:shift_idx + 1, :])
    scale = jnp.where(is_ctx, mc_ref[0, shift_idx + 1:shift_idx + 2, :], ml_ref[0, shift_idx + 1:shift_idx + 2, :])
    return _rms(x, gain) * (1.0 + scale) + shift


def _ctx_rows(tile, tm, width):
    row = tile * tm + lax.broadcasted_iota(jnp.int32, (tm, width), 0)
    return row >= SEQ


def _proj_kernel(x_ref, ml_ref, mc_ref, g_ref, w_ref, wba_ref, alog_ref, dtb_ref, p_ref, gb_ref, h_scr, *, tm):
    i = pl.program_id(1)
    j = pl.program_id(2)

    @pl.when(j == 0)
    def _():
        is_ctx = _ctx_rows(i, tm, D_MODEL)
        h = _modulated_norm(x_ref[0], g_ref[...], ml_ref, mc_ref, is_ctx, 0).astype(BF16)
        h_scr[...] = h
        ba = _dot(h, wba_ref[0])
        lane = lax.broadcasted_iota(jnp.int32, ba.shape, 1)
        log_decay = -jnp.exp(alog_ref[...]) * _softplus(ba + dtb_ref[...])
        gb_ref[0] = jnp.where(lane < 2 * GDN_HEADS, _sigmoid(ba), log_decay)

    p_ref[0] = _dot(h_scr[...], w_ref[0]).astype(BF16)


def _project(xs, mods, gain, w_main, w_ba, a_log, dt_bias, layer):
    def pad_lanes(vals):
        row = jnp.zeros((LANES,), F32).at[2 * GDN_HEADS:4 * GDN_HEADS].set(vals.reshape(-1))
        return row.reshape(1, LANES)

    vec = pl.BlockSpec((1, LANES), lambda b, i, j: (0, 0))
    tm, tn = 1152, 1280
    return pl.pallas_call(
        functools.partial(_proj_kernel, tm=tm),
        grid=(BATCH, TOK // tm, PROJ_COLS // tn),
        in_specs=[pl.BlockSpec((1, tm, D_MODEL), lambda b, i, j: (b, i, 0)),
                  pl.BlockSpec((1, N_MOD, D_MODEL), lambda b, i, j: (b, 0, 0)),
                  pl.BlockSpec((1, N_MOD, D_MODEL), lambda b, i, j: (BATCH, 0, 0)),
                  pl.BlockSpec((1, D_MODEL), lambda b, i, j: (0, 0)),
                  pl.BlockSpec((1, D_MODEL, tn), lambda b, i, j: (layer, 0, j)),
                  pl.BlockSpec((1, D_MODEL, LANES), lambda b, i, j: (layer, 0, 0)), vec, vec],
        out_specs=[pl.BlockSpec((1, tm, tn), lambda b, i, j: (b, i, j)),
                   pl.BlockSpec((1, tm, LANES), lambda b, i, j: (b, i, 0))],
        out_shape=[jax.ShapeDtypeStruct((BATCH, TOK, PROJ_COLS), BF16),
                   jax.ShapeDtypeStruct((BATCH, TOK, LANES), F32)],
        scratch_shapes=[pltpu.VMEM((tm, D_MODEL), BF16)],
        compiler_params=_params("parallel", "parallel", "arbitrary"),
        name="proj",
    )(xs, mods, mods, gain.reshape(1, D_MODEL), w_main, w_ba, pad_lanes(a_log), pad_lanes(dt_bias))


def _conv(x, w):
    n, c = x.shape
    t = lax.broadcasted_iota(jnp.int32, (n, c), 0)
    is_ctx = t >= SEQ
    local = jnp.where(is_ctx, t - SEQ, t)
    seg_len = jnp.where(is_ctx, CTX_LEN, SEQ)
    y = jnp.zeros_like(x)
    for j in range(CONV_WIDTH):
        s = j - CONV_WIDTH // 2
        if s == 0:
            y = y + x * w[j:j + 1, :]
        else:
            shifted = pltpu.roll(x, (-s) % n, 0)
            ok = jnp.logical_and(local + s >= 0, local + s < seg_len)
            y = y + jnp.where(ok, shifted, 0.0) * w[j:j + 1, :]
    return y


def _gdn_conv_kernel(x_ref, w_ref, o_ref):
    j = pl.program_id(1)
    mul = jnp.where(j == 0, float(GDN_HEAD_DIM), jnp.where(j == 1, 1.0, 0.0))
    add = jnp.where(j == 0, float(GDN_HEAD_DIM) * EPS, jnp.where(j == 1, EPS, 1.0))
    for h in range(GDN_HEADS):
        cols = slice(h * GDN_HEAD_DIM, (h + 1) * GDN_HEAD_DIM)
        y = _silu(_conv(x_ref[0, :, cols].astype(F32), w_ref[:, cols]))
        scale = lax.rsqrt(jnp.sum(y * y, axis=-1, keepdims=True) * mul + add)
        o_ref[0, :, cols] = (y * scale).astype(BF16)


def _gdn_conv(p, conv_w):
    first = COL_QKV // BRANCH_WIDTH
    return pl.pallas_call(
        _gdn_conv_kernel,
        grid=(BATCH, 3),
        in_specs=[pl.BlockSpec((1, TOK, BRANCH_WIDTH), lambda b, j: (b, 0, first + j)),
                  pl.BlockSpec((CONV_WIDTH, BRANCH_WIDTH), lambda b, j: (0, j))],
        out_specs=pl.BlockSpec((1, TOK, BRANCH_WIDTH), lambda b, j: (b, 0, j)),
        out_shape=jax.ShapeDtypeStruct((BATCH, TOK, 3 * BRANCH_WIDTH), BF16),
        compiler_params=_params("parallel", "parallel"),
        name="gdn_conv",
    )(p, conv_w)


GDN_QM_ROWS = GDN_CHUNK + GDN_HEAD_DIM
GDN_GL_ROWS = 8
GDN_HEADS_PER_STEP = 2
GDN_PREP_UNROLL = 12


def _gdn_prepare(q_ref, k_ref, v_ref, gb_ref, qm_scr, nn_scr, o_scr, gl_scr, chunks, head, local):
    c = GDN_CHUNK
    cols = slice(local * LANES, (local + 1) * LANES)
    lane = lax.broadcasted_iota(jnp.int32, (c, LANES), 1)
    row = lax.broadcasted_iota(jnp.int32, (c, LANES), 0)
    ii = lax.broadcasted_iota(jnp.int32, (c, c), 0)
    jj = lax.broadcasted_iota(jnp.int32, (c, c), 1)
    eye = (ii == jj).astype(F32)
    masks =((ii >= jj, ii > jj, row > lane), (ii <= jj, ii < jj, row < lane))

    loaded = []
    for chunk in chunks:
        r0 = pl.multiple_of(chunk * c, c)
        k = k_ref[0, pl.ds(r0, c), cols]
        q = q_ref[0, pl.ds(r0, c), cols]
        kq = _dot_nt(jnp.concatenate([k, q], axis=0), k)
        loaded.append((chunk, r0, q, k, kq))

    chains = []
    for chunk, r0, q, k, kq in loaded:
        gb = gb_ref[0, pl.ds(r0, c), :]
        for d in range(2):
            col = head + d * GDN_HEADS
            beta = jnp.sum(jnp.where(lane == col, gb, 0.0), axis=-1, keepdims=True)
            g = jnp.sum(jnp.where(lane == col + 2 * GDN_HEADS, gb, 0.0), axis=-1, keepdims=True)
            incl, strict, strict_wide = masks[d]
            rhs = jnp.where(lane >= c, g, jnp.where(strict_wide, g, 0.0))
            mask = incl.astype(BF16)
            hi = rhs.astype(BF16)
            rest = rhs - hi.astype(F32)
            mid = rest.astype(BF16)
            low = (rest - mid.astype(F32)).astype(BF16)
            e = _dot(mask, hi) + _dot(mask, mid) + _dot(mask, low)
            chains.append(dict(chunk=chunk, r0=r0, d=d, q=q, k=k, kq=kq, beta=beta, e=e))

    for ch in chains:
        incl, strict, _ = masks[ch["d"]]
        e = ch["e"]
        decay = jnp.where(incl, jnp.exp(e[:, :c]), 0.0)
        gc = e[:, c:c + 1]
        last = 0 if ch["d"] == 1 else c - 1
        gc_last = e[last:last + 1, c:c + 1]
        ch.update(decay=decay, gc=gc, gc_last=gc_last, egc=jnp.exp(gc))
        ch["a"] = jnp.where(strict, ch["beta"] * ch["kq"][:c] * decay, 0.0)
        ch["t"] = eye
    s = 1
    while s < c:
        pair = jnp.logical_and((ii // (2 * s)) == (jj // (2 * s)), (ii // s) != (jj // s))
        for ch in chains:
            ch["a_off"] = jnp.where(pair, ch["a"], 0.0)
        if s == 1:
            for ch in chains:
                ch["t"] = eye - ch["a_off"]
        else:
            for ch in chains:
                ch["m"] = _dot(ch["t"].astype(BF16), ch["a_off"].astype(BF16))
            for ch in chains:
                ch["t"] = ch["t"] - _dot(ch["m"].astype(BF16), ch["t"].astype(BF16))
        s *= 2
    for ch in chains:
        r0, beta, egc = ch["r0"], ch["beta"], ch["egc"]
        kf = ch["k"].astype(F32)
        vf = v_ref[0, pl.ds(r0, c), cols].astype(F32)
        rhs2 = jnp.concatenate([vf * beta, kf * (beta * egc)], axis=1).astype(BF16)
        ch["uw"] = _dot(ch["t"].astype(BF16), rhs2).astype(BF16)
        ch["k_dec_t"] = (kf * jnp.exp(ch["gc_last"] - ch["gc"])).T.astype(BF16)
    for ch in chains:
        incl = masks[ch["d"]][0]
        qk = jnp.where(incl, ch["kq"][c:] * ch["decay"], 0.0).astype(BF16)
        ch["nm"] = _dot(ch["k_dec_t"], ch["uw"])
        ch["ow"] = _dot(qk, ch["uw"])
    for ch in chains:
        chunk, r0, nm, ow = ch["chunk"], ch["r0"], ch["nm"], ch["ow"]
        s = 2 * local + ch["d"]
        q0 = pl.multiple_of(chunk * GDN_QM_ROWS, 16)
        qm_scr[s, pl.ds(q0, c), :] = (ch["q"].astype(F32) * ch["egc"] - ow[:, GDN_HEAD_DIM:]).astype(BF16)
        qm_scr[s, pl.ds(q0 + c, GDN_HEAD_DIM), :] = nm[:, GDN_HEAD_DIM:].astype(BF16)
        nn_scr[s, pl.ds(pl.multiple_of(chunk * GDN_HEAD_DIM, GDN_HEAD_DIM), GDN_HEAD_DIM), :] = nm[:, :GDN_HEAD_DIM]
        o_scr[s, pl.ds(r0, c), :] = ow[:, :GDN_HEAD_DIM]
        gl_scr[s, pl.ds(pl.multiple_of(chunk * GDN_GL_ROWS, GDN_GL_ROWS), GDN_GL_ROWS), :] = jnp.broadcast_to(
            jnp.exp(ch["gc_last"]), (GDN_GL_ROWS, LANES))


def _gdn_advance(qm_scr, nn_scr, o_scr, gl_scr, d, chunk, state):
    c = GDN_CHUNK
    qm = qm_scr[d, pl.ds(pl.multiple_of(chunk * GDN_QM_ROWS, 16), GDN_QM_ROWS), :]
    r = _dot(qm, state.astype(BF16))
    rows = pl.ds(pl.multiple_of(chunk * c, c), c)
    o_scr[d, rows, :] = o_scr[d, rows, :] + r[:c]
    gl = gl_scr[d, pl.ds(pl.multiple_of(chunk * GDN_GL_ROWS, GDN_GL_ROWS), 1), :]
    n = nn_scr[d, pl.ds(pl.multiple_of(chunk * GDN_HEAD_DIM, GDN_HEAD_DIM), GDN_HEAD_DIM), :]
    return state * gl - r[c:] + n


def _gdn_kernel(q_ref, k_ref, v_ref, z_ref, gb_ref, gain_ref, o_ref, qm_scr, nn_scr, o_scr, gl_scr):
    first_head = pl.program_id(1) * GDN_HEADS_PER_STEP
    c = GDN_CHUNK
    n_lat, n_ctx = SEQ // c, CTX_LEN // c
    n_chunks = n_lat + n_ctx

    for local in range(GDN_HEADS_PER_STEP):
        def prepare(i, _, local=local):
            chunks = [i * GDN_PREP_UNROLL + j for j in range(GDN_PREP_UNROLL)]
            _gdn_prepare(q_ref, k_ref, v_ref, gb_ref, qm_scr, nn_scr, o_scr, gl_scr, chunks, first_head + local, local)
            return 0

        lax.fori_loop(0, n_chunks // GDN_PREP_UNROLL, prepare, 0)

    def advance(i, states):
        cf = jnp.where(i < n_ctx, n_lat + i, i - n_ctx)
        cb = n_chunks - 1 - i
        return tuple(_gdn_advance(qm_scr, nn_scr, o_scr, gl_scr, s, cb if s % 2 else cf, state)
                     for s, state in enumerate(states))

    zero = jnp.zeros((GDN_HEAD_DIM, GDN_HEAD_DIM), F32)
    lax.fori_loop(0, n_chunks, advance, (zero,) * (2 * GDN_HEADS_PER_STEP), unroll=4)
    for local in range(GDN_HEADS_PER_STEP):
        cols = slice(local * LANES, (local + 1) * LANES)
        o = o_scr[2 * local] + o_scr[2 * local + 1]
        o_ref[0, :, cols] = (_rms(o, gain_ref[...]) * _silu(z_ref[0, :, cols].astype(F32))).astype(BF16)


def _gdn(qkv, p, gates, out_gain):
    n_steps = GDN_HEADS // GDN_HEADS_PER_STEP
    n_chunks = TOK // GDN_CHUNK
    width = GDN_HEADS_PER_STEP * LANES
    zblk = COL_Z // width
    n_chain = 2 * GDN_HEADS_PER_STEP
    blk = lambda off: pl.BlockSpec((1, TOK, width), lambda b, h: (b, 0, off + h))
    vec = pl.BlockSpec((1, LANES), lambda b, h: (0, 0))
    return pl.pallas_call(
        _gdn_kernel,
        grid=(BATCH, n_steps),
        in_specs=[blk(0), blk(n_steps), blk(2 * n_steps), blk(zblk),
                  pl.BlockSpec((1, TOK, LANES), lambda b, h: (b, 0, 0)), vec],
        out_specs=pl.BlockSpec((1, TOK, width), lambda b, h: (b, 0, h)),
        out_shape=jax.ShapeDtypeStruct((BATCH, TOK, BRANCH_WIDTH), BF16),
        scratch_shapes=[pltpu.VMEM((n_chain, n_chunks * GDN_QM_ROWS, LANES), BF16),
                        pltpu.VMEM((n_chain, n_chunks * GDN_HEAD_DIM, LANES), F32),
                        pltpu.VMEM((n_chain, TOK, LANES), F32),
                        pltpu.VMEM((n_chain, n_chunks * GDN_GL_ROWS, LANES), F32)],
        compiler_params=_params("parallel", "parallel"),
        name="gdn",
    )(qkv, qkv, qkv, p, gates, out_gain.reshape(1, LANES))


def _rms_halves(x, gain):
    lane = lax.broadcasted_iota(jnp.int32, x.shape, 1)
    lo = lane < DIFF_HEAD_DIM
    x2 = x * x
    s_lo = jnp.sum(jnp.where(lo, x2, 0.0), axis=-1, keepdims=True)
    s_hi = jnp.sum(jnp.where(lo, 0.0, x2), axis=-1, keepdims=True)
    ms = jnp.where(lo, s_lo, s_hi) * (1.0 / DIFF_HEAD_DIM)
    return x * lax.rsqrt(ms + EPS) * gain


def _rope(x, cos, sin):
    lane = lax.broadcasted_iota(jnp.int32, x.shape, 1)
    first = (lane & ROPE_PAIRS) == 0
    partner = jnp.where(first, -pltpu.roll(x, LANES - ROPE_PAIRS, 1), pltpu.roll(x, ROPE_PAIRS, 1))
    return x * cos + partner * sin


ATTN_Q_BLOCK = 1024
ATTN_GROUP_ROWS = 128


def _attn_kernel(q_ref, k_ref, v_ref, cosk_ref, sink_ref, cosq_ref, sinq_ref, qg_ref, kg_ref, lv_ref, og_ref,
                 o_ref, kn_scr, *, ctx_block, lam_init):
    qi = pl.program_id(2)

    @pl.when(qi == 0)
    def _():
        kn = _rope(_rms_halves(k_ref[0].astype(F32), kg_ref[...]), cosk_ref[...], sink_ref[...])
        kn_scr[...] = kn.astype(BF16)

    lv = lv_ref[...]
    lam = (jnp.exp(jnp.sum(lv[0:1] * lv[1:2], axis=-1, keepdims=True))
           - jnp.exp(jnp.sum(lv[2:3] * lv[3:4], axis=-1, keepdims=True)) + lam_init)
    def attend(n_rows, kn, v):
        q = _rope(_rms_halves(q_ref[0, :n_rows, :].astype(F32), qg_ref[...]), cosq_ref[:n_rows, :], sinq_ref[:n_rows, :])
        q = q * (DIFF_HEAD_DIM ** -0.5 * math.log2(math.e))
        lane = lax.broadcasted_iota(jnp.int32, q.shape, 1)
        lo = lane < DIFF_HEAD_DIM
        q1 = jnp.where(lo, q, 0.0).astype(BF16)
        q2 = jnp.where(lo, 0.0, q).astype(BF16)

        def half(s):
            p = jnp.exp2(s - jnp.max(s, axis=-1, keepdims=True))
            return _dot(p.astype(BF16), v), jnp.sum(p, axis=-1, keepdims=True)
        rows = ATTN_GROUP_ROWS
        scores = [(_dot_nt(q1[r:r + rows], kn), _dot_nt(q2[r:r + rows], kn)) for r in range(0, n_rows, rows)]
        for g, (s1, s2) in enumerate(scores):
            a1, l1 = half(s1)
            a2, l2 = half(s2)
            o = a1 * (1.0 / l1) - a2 * (lam / l2)
            o_ref[0, g * rows:(g + 1) * rows, :] = (_rms(o, og_ref[...]) * (1.0 - lam_init)).astype(BF16)

    if ctx_block is None:
        attend(q_ref.shape[1], kn_scr[...], v_ref[0])
    else:
        @pl.when(qi == ctx_block)
        def _():
            attend(CTX_LEN, kn_scr[SEQ:, :], v_ref[0, SEQ:, :])

        @pl.when(qi != ctx_block)
        def _():
            attend(q_ref.shape[1], kn_scr[...], v_ref[0])


def _rope_tables():
    n_rows = SEQ // GRID_W
    row_id = jnp.broadcast_to(jnp.arange(n_rows, dtype=F32)[:, None], (n_rows, GRID_W)).reshape(-1)
    col_id = jnp.broadcast_to(jnp.arange(GRID_W, dtype=F32)[None, :], (n_rows, GRID_W)).reshape(-1)
    inv_freq = jnp.power(ROPE_BASE, -jnp.arange(ROPE_PAIRS, dtype=F32) / ROPE_PAIRS)
    row_ang = row_id[:, None] * inv_freq
    col_ang = col_id[:, None] * inv_freq
    ang = jnp.concatenate([row_ang, row_ang, col_ang, col_ang], axis=-1)
    ang = jnp.concatenate([ang, ang], axis=-1)
    pad = ((0, CTX_LEN), (0, 0))
    return jnp.pad(jnp.cos(ang), pad, constant_values=1.0), jnp.pad(jnp.sin(ang), pad)


def _diff_attn(p, cos, sin, q_gain, k_gain, lam_vecs, out_gain, lam_init, with_ctx):
    tq = ATTN_Q_BLOCK
    n_rows = TOK if with_ctx else SEQ
    nq = pl.cdiv(n_rows, tq)
    nh = DIFF_HEADS
    qb, kb, vb = COL_DQ // LANES, COL_DK // LANES, COL_DV // LANES
    full = lambda off: pl.BlockSpec((1, TOK, LANES), lambda b, h, i: (b, 0, off + h))
    tab_full = pl.BlockSpec((TOK, LANES), lambda b, h, i: (0, 0))
    tab_q = pl.BlockSpec((tq, LANES), lambda b, h, i: (i, 0))
    vec = pl.BlockSpec((1, LANES), lambda b, h, i: (0, 0))
    tile2 = lambda g: jnp.concatenate([g, g]).reshape(1, LANES)
    return pl.pallas_call(
        functools.partial(_attn_kernel, ctx_block=SEQ // tq if with_ctx else None, lam_init=lam_init),
        grid=(BATCH, nh, nq),
        in_specs=[pl.BlockSpec((1, tq, LANES), lambda b, h, i: (b, i, qb + h)), full(kb), full(vb),
                  tab_full, tab_full, tab_q, tab_q, vec, vec,
                  pl.BlockSpec((4, DIFF_HEAD_DIM), lambda b, h, i: (0, 0)), vec],
        out_specs=pl.BlockSpec((1, tq, LANES), lambda b, h, i: (b, i, h)),
        out_shape=jax.ShapeDtypeStruct((BATCH, n_rows, BRANCH_WIDTH), BF16),
        scratch_shapes=[pltpu.VMEM((TOK, LANES), BF16)],
        compiler_params=_params("parallel", "parallel", "arbitrary"),
        name="diff_attn",
    )(p, p, p, cos, sin, cos, sin, tile2(q_gain), tile2(k_gain), lam_vecs, out_gain.reshape(1, LANES))


LRU_SLAB = 256
LRU_ROWS = 768
LRU_SCAN_BLOCK = 8


def _lru_kernel(x_ref, y_ref, cw_ref, cb_ref, wg_ref, bg_ref, lam_ref, o_ref, xc_scr, af_scr, bf_scr, ab_scr, bb_scr):
    w = LRU_SLAB
    blk = LRU_SCAN_BLOCK
    for c0 in range(0, w, LANES):
        cols = slice(c0, c0 + LANES)
        xc_scr[:, cols] = _conv(x_ref[0, :, cols].astype(F32), cw_ref[:, cols]) + cb_ref[:, cols]
    sp = _softplus(-lam_ref[0])
    sub = lax.broadcasted_iota(jnp.int32, (LRU_ROWS // blk, blk, w), 1)

    def gates(i, _):
        r0 = pl.multiple_of(i * LRU_ROWS, LRU_ROWS)
        xc = xc_scr[pl.ds(r0, LRU_ROWS), :]
        pre = _dot(xc.astype(BF16), wg_ref[0]) + bg_ref[0]
        for d, (a_scr, b_scr) in enumerate(((af_scr, bf_scr), (ab_scr, bb_scr))):
            r = _sigmoid(pre[:, (2 * d) * w:(2 * d + 1) * w])
            gi = _sigmoid(pre[:, (2 * d + 1) * w:(2 * d + 2) * w])
            log_a = -LRU_C * r * sp[d:d + 1]
            a = jnp.exp(log_a)
            b = jnp.sqrt(1.0 - a * a) * gi * xc
            a = a.reshape(LRU_ROWS // blk, blk, w)
            b = b.reshape(LRU_ROWS // blk, blk, w)
            shift = 1
            while shift < blk:
                if d == 0:
                    ok, roll_by = sub >= shift, shift
                else:
                    ok, roll_by = sub < blk - shift, blk - shift
                b = jnp.where(ok, a * pltpu.roll(b, roll_by, 1) + b, b)
                a = jnp.where(ok, a * pltpu.roll(a, roll_by, 1), a)
                shift *= 2
            a_scr[pl.ds(r0, LRU_ROWS), :] = a.reshape(LRU_ROWS, w)
            b_scr[pl.ds(r0, LRU_ROWS), :] = b.reshape(LRU_ROWS, w)
        return 0

    lax.fori_loop(0, TOK // LRU_ROWS, gates, 0)

    n_blk, n_lat_blk, n_ctx_blk = TOK // blk, SEQ // blk, CTX_LEN // blk

    def step(s, carry):
        h_f, h_b = carry
        rows_f = pl.ds(pl.multiple_of(jnp.where(s < n_ctx_blk, n_lat_blk + s, s - n_ctx_blk) * blk, blk), blk)
        rows_b = pl.ds(pl.multiple_of((n_blk - 1 - s) * blk, blk), blk)
        hf = af_scr[rows_f, :] * h_f + bf_scr[rows_f, :]
        bf_scr[rows_f, :] = hf
        hb = ab_scr[rows_b, :] * h_b + bb_scr[rows_b, :]
        bb_scr[rows_b, :] = hb
        return hf[blk - 1:blk, :], hb[0:1, :]

    zero = jnp.zeros((1, w), F32)
    lax.fori_loop(0, n_blk, step, (zero, zero), unroll=4)
    h = bf_scr[...] + bb_scr[...]
    o_ref[0] = (h * jax.nn.gelu(y_ref[0].astype(F32))).astype(BF16)


def _lru_gate_weights(w_gate, b_gate):
    n_slab = BRANCH_WIDTH // LRU_SLAB
    per = LRU_SLAB // LRU_BLOCK_DIM
    wg = w_gate.reshape(2, 2, n_slab, per, LRU_BLOCK_DIM, LRU_BLOCK_DIM)
    eye = jnp.eye(per, dtype=w_gate.dtype)
    dense = jnp.einsum('dgsnjk,nm->snjdgmk', wg, eye)
    dense = dense.reshape(n_slab, LRU_SLAB, 4 * LRU_SLAB)
    bg = b_gate.reshape(2, 2, n_slab, LRU_SLAB).transpose(2, 0, 1, 3).reshape(n_slab, 1, 4 * LRU_SLAB)
    return dense.astype(BF16), bg


def _lru(p, conv_w, conv_b, w_gate, b_gate, lam):
    n_slab = BRANCH_WIDTH // LRU_SLAB
    wg, bg = _lru_gate_weights(w_gate, b_gate)
    lam_s = lam.reshape(2, n_slab, LRU_SLAB).transpose(1, 0, 2)
    xb, yb = COL_LX // LRU_SLAB, COL_LY // LRU_SLAB
    return pl.pallas_call(
        _lru_kernel,
        grid=(BATCH, n_slab),
        in_specs=[pl.BlockSpec((1, TOK, LRU_SLAB), lambda b, s: (b, 0, xb + s)),
                  pl.BlockSpec((1, TOK, LRU_SLAB), lambda b, s: (b, 0, yb + s)),
                  pl.BlockSpec((CONV_WIDTH, LRU_SLAB), lambda b, s: (0, s)),
                  pl.BlockSpec((1, LRU_SLAB), lambda b, s: (0, s)),
                  pl.BlockSpec((1, LRU_SLAB, 4 * LRU_SLAB), lambda b, s: (s, 0, 0)),
                  pl.BlockSpec((1, 1, 4 * LRU_SLAB), lambda b, s: (s, 0, 0)),
                  pl.BlockSpec((1, 2, LRU_SLAB), lambda b, s: (s, 0, 0))],
        out_specs=pl.BlockSpec((1, TOK, LRU_SLAB), lambda b, s: (b, 0, s)),
        out_shape=jax.ShapeDtypeStruct((BATCH, TOK, BRANCH_WIDTH), BF16),
        scratch_shapes=[pltpu.VMEM((TOK, LRU_SLAB), F32)] * 5,
        compiler_params=_params("parallel", "parallel"),
        name="lru",
    )(p, p, conv_w, conv_b.reshape(1, BRANCH_WIDTH), wg, bg, lam_s)


def _route(logits):
    lane = lax.broadcasted_iota(jnp.int32, logits.shape, 1)
    lane_f = lane.astype(F32)
    far = float(LANES)
    lg = jnp.where(lane < N_EXPERTS, logits, -jnp.inf)
    ex = jnp.exp(lg - jnp.max(lg, axis=-1, keepdims=True))
    probs = ex / jnp.sum(ex, axis=-1, keepdims=True)
    per_group = N_EXPERTS // N_GROUPS
    grp = lane // per_group

    def top2(vals):
        m1 = jnp.max(vals, axis=-1, keepdims=True)
        i1 = jnp.min(jnp.where(vals == m1, lane_f, far), axis=-1, keepdims=True)
        rest = jnp.where(lane_f == i1, -2.0, vals)
        m2 = jnp.max(rest, axis=-1, keepdims=True)
        i2 = jnp.min(jnp.where(rest == m2, lane_f, far), axis=-1, keepdims=True)
        return m1, i1, m2, i2

    best = jnp.zeros(logits.shape[:1] + (1,), jnp.int32)
    best_score = None
    for g in range(N_GROUPS):
        m1, _, m2, _ = top2(jnp.where(grp == g, probs, -1.0))
        score = m1 + m2
        if best_score is None:
            best_score = score
        else:
            better = score > best_score
            best = jnp.where(better, g, best)
            best_score = jnp.where(better, score, best_score)
    m1, i1, m2, i2 = top2(jnp.where(grp == best, probs, -1.0))
    den = m1 + m2
    return i1, i2, m1 / den, m2 / den


ROUTE_E, ROUTE_W, ROUTE_RANK = 0, 2, 4


def _pack_bf16_pairs(x):
    n = x.shape[1] // 2
    xb = x.astype(BF16).astype(F32)
    lo = pltpu.bitcast(xb[:, :n], jnp.uint32)
    hi = pltpu.bitcast(xb[:, n:], jnp.uint32)
    return (lo >> 16) | (hi & jnp.uint32(0xFFFF0000))


def _unpack_bf16_pairs(p):
    lo = pltpu.bitcast(p << 16, F32)
    hi = pltpu.bitcast(p & jnp.uint32(0xFFFF0000), F32)
    return jnp.concatenate([lo, hi], axis=1)


def _merge_kernel(ya_ref, yb_ref, yc_ref, gates_ref, x_ref, ml_ref, mc_ref, g2_ref, wbr_ref, wout_ref, wr_ref,
                  br_ref, xo_ref, h2_ref, route_ref, cnt_ref, cnt_scr, tri_scr, *, tm):
    i = pl.program_id(1)

    @pl.when(jnp.logical_and(pl.program_id(0) == 0, i == 0))
    def _():
        cnt_scr[...] = jnp.zeros_like(cnt_scr)
        earlier = (lax.broadcasted_iota(jnp.int32, (tm, tm), 0) > lax.broadcasted_iota(jnp.int32, (tm, tm), 1))
        tri_scr[...] = jnp.where(earlier, 1.0, 0.0).astype(BF16)

    acc = None
    for n, y_ref in enumerate((ya_ref, yb_ref, yc_ref)):
        yn = _dot(y_ref[0], wbr_ref[n])
        gate = _sigmoid(gates_ref[0, :, n * D_MODEL:(n + 1) * D_MODEL].astype(F32))
        acc = gate * yn if acc is None else acc + gate * yn
    out = _dot(acc.astype(BF16), wout_ref[...])
    is_ctx = _ctx_rows(i, tm, D_MODEL)
    xn = x_ref[0] + jnp.where(is_ctx, mc_ref[0, 2:3, :], ml_ref[0, 2:3, :]) * out
    xo_ref[0] = xn
    h2 = _modulated_norm(xn, g2_ref[...], ml_ref, mc_ref, is_ctx, 3)
    h2_ref[0] = _pack_bf16_pairs(h2)
    i1, i2, w1, w2 = _route(_dot(h2.astype(BF16), wr_ref[...]) + br_ref[...])
    lane = lax.broadcasted_iota(jnp.int32, (tm, LANES), 1)
    lane_f = lane.astype(F32)
    chosen = jnp.where(jnp.logical_or(lane_f == i1, lane_f == i2), 1.0, 0.0)
    before = _dot(tri_scr[...], chosen.astype(BF16)) + cnt_scr[...]
    rank1 = jnp.sum(jnp.where(lane_f == i1, before, 0.0), axis=-1, keepdims=True)
    rank2 = jnp.sum(jnp.where(lane_f == i2, before, 0.0), axis=-1, keepdims=True)
    cnt_scr[...] += jnp.sum(chosen, axis=0, keepdims=True)
    cnt_ref[...] = cnt_scr[...]
    record = jnp.zeros((tm, LANES), F32)
    for pos, val in enumerate((i1, i2, w1, w2, rank1, rank2)):
        record = jnp.where(lane == pos, val, record)
    route_ref[0] = record


def _merge(ya, yb, yc, p, xs, mods, gain2, w_branch, w_out, w_router, b_router, rows, tm):
    wr = jnp.zeros((D_MODEL, LANES), BF16).at[:, :N_EXPERTS].set(w_router.astype(BF16))
    br = jnp.zeros((1, LANES), F32).at[0, :N_EXPERTS].set(b_router)
    tile = lambda w: pl.BlockSpec((1, tm, w), lambda b, i: (b, i, 0))
    const = lambda shape: pl.BlockSpec(shape, lambda b, i: (0,) * len(shape))
    return pl.pallas_call(
        functools.partial(_merge_kernel, tm=tm),
        grid=(BATCH, rows // tm),
        in_specs=[tile(BRANCH_WIDTH), tile(BRANCH_WIDTH), tile(BRANCH_WIDTH), tile(3 * D_MODEL), tile(D_MODEL),
                  pl.BlockSpec((1, N_MOD, D_MODEL), lambda b, i: (b, 0, 0)),
                  pl.BlockSpec((1, N_MOD, D_MODEL), lambda b, i: (BATCH, 0, 0)),
                  const((1, D_MODEL)), const((3, BRANCH_WIDTH, D_MODEL)), const((D_MODEL, D_MODEL)),
                  const((D_MODEL, LANES)), const((1, LANES))],
        out_specs=[tile(D_MODEL), tile(D_MODEL // 2), tile(LANES), const((1, LANES))],
        out_shape=[jax.ShapeDtypeStruct((BATCH, rows, D_MODEL), F32),
                   jax.ShapeDtypeStruct((BATCH, rows, D_MODEL // 2), jnp.uint32),
                   jax.ShapeDtypeStruct((BATCH, rows, LANES), F32),
                   jax.ShapeDtypeStruct((1, LANES), F32)],
        scratch_shapes=[pltpu.VMEM((1, LANES), F32), pltpu.VMEM((tm, tm), BF16)],
        compiler_params=_params("arbitrary", "arbitrary"),
        name="merge",
    )(ya, yb, yc, p, xs, mods, mods, gain2.reshape(1, D_MODEL), w_branch, w_out, wr, br)


MOE_TILE = 512
SC_GATHER_ROWS = 64


def _sc_gather(table, idx):
    info = plsc.get_sparse_core_info()
    n_workers = info.num_cores * info.num_subcores
    n_rows, width = idx.shape[0], table.shape[1]
    per_worker = n_rows // n_workers
    assert per_worker * n_workers == n_rows and per_worker % SC_GATHER_ROWS == 0
    mesh = plsc.VectorSubcoreMesh(core_axis_name="c", subcore_axis_name="s")

    @functools.partial(
        pl.kernel, mesh=mesh, out_type=jax.ShapeDtypeStruct((n_rows, width), table.dtype),
        scratch_types=[pltpu.VMEM((SC_GATHER_ROWS,), jnp.int32),
                       pltpu.VMEM((SC_GATHER_ROWS, width), table.dtype),
                       pltpu.SemaphoreType.DMA],
        name="sc_gather")
    def gather(table_hbm, idx_hbm, out_hbm, idx_v, rows_v, sem):
        worker = lax.axis_index("s") * info.num_cores + lax.axis_index("c")
        base = worker * per_worker

        @pl.loop(0, per_worker // SC_GATHER_ROWS)
        def _(j):
            off = base + j * SC_GATHER_ROWS
            pltpu.sync_copy(idx_hbm.at[pl.ds(off, SC_GATHER_ROWS)], idx_v)
            pltpu.async_copy(table_hbm.at[idx_v], rows_v, sem).wait()
            pltpu.sync_copy(rows_v, out_hbm.at[pl.ds(off, SC_GATHER_ROWS)])

    return gather(table, idx)


SC_SCATTER_ROWS = 128


def _sc_scatter(table, dest, n_out):
    info = plsc.get_sparse_core_info()
    n_workers = info.num_cores * info.num_subcores
    n_tok, width = table.shape
    n_assign = dest.shape[0]
    per_worker = n_assign // n_workers
    assert per_worker * n_workers == n_assign and per_worker % SC_SCATTER_ROWS == 0 and n_tok % SC_SCATTER_ROWS == 0
    mesh = plsc.VectorSubcoreMesh(core_axis_name="c", subcore_axis_name="s")

    @functools.partial(
        pl.kernel, mesh=mesh, out_type=jax.ShapeDtypeStruct((n_out, width), table.dtype),
        scratch_types=[pltpu.VMEM((SC_SCATTER_ROWS,), jnp.int32),
                       pltpu.VMEM((SC_SCATTER_ROWS, width), table.dtype),
                       pltpu.SemaphoreType.DMA],
        name="sc_scatter")
    def scatter(table_hbm, dest_hbm, out_hbm, idx_v, rows_v, sem):
        worker = lax.axis_index("s") * info.num_cores + lax.axis_index("c")
        base = worker * per_worker

        @pl.loop(0, per_worker // SC_SCATTER_ROWS)
        def _(j):
            off = base + j * SC_SCATTER_ROWS
            pltpu.sync_copy(dest_hbm.at[pl.ds(off, SC_SCATTER_ROWS)], idx_v)
            pltpu.sync_copy(table_hbm.at[pl.ds(lax.rem(off, n_tok), SC_SCATTER_ROWS)], rows_v)
            pltpu.async_copy(rows_v, out_hbm.at[idx_v], sem).wait()

    return scatter(table, dest)


def _experts_kernel(tile_expert_ref, n_valid_ref, x_ref, wgu_ref, wd_ref, o_ref, wgu_scr, wd_scr):
    j = pl.program_id(0)
    valid = j < n_valid_ref[0]
    fresh = jnp.logical_or(j == 0, tile_expert_ref[j] != tile_expert_ref[jnp.maximum(j - 1, 0)])

    @pl.when(jnp.logical_and(valid, fresh))
    def _():
        wgu_scr[...] = wgu_ref[0, 0].astype(BF16)
        wd_scr[...] = wd_ref[0, 0].astype(BF16)

    @pl.when(valid)
    def _():
        x = _unpack_bf16_pairs(x_ref[...]).astype(BF16)
        gu = _dot(x, wgu_scr[...])
        act = (_silu(gu[:, :EXPERT_FF]) * gu[:, EXPERT_FF:]).astype(BF16)
        o_ref[...] = _pack_bf16_pairs(_dot(act, wd_scr[...]))

    @pl.when(jnp.logical_not(valid))
    def _():
        o_ref[...] = jnp.zeros_like(o_ref)


def _experts(x_sorted, tile_expert, n_valid, w_gate_up, w_down, layer):
    n_tiles = x_sorted.shape[0] // MOE_TILE
    half = D_MODEL // 2
    return pl.pallas_call(
        _experts_kernel,
        grid_spec=pltpu.PrefetchScalarGridSpec(
            num_scalar_prefetch=2, grid=(n_tiles,),
            in_specs=[pl.BlockSpec((MOE_TILE, half), lambda j, te, nv: (j, 0)),
                      pl.BlockSpec((1, 1, D_MODEL, 2 * EXPERT_FF), lambda j, te, nv: (layer, te[j], 0, 0)),
                      pl.BlockSpec((1, 1, EXPERT_FF, D_MODEL), lambda j, te, nv: (layer, te[j], 0, 0))],
            out_specs=pl.BlockSpec((MOE_TILE, half), lambda j, te, nv: (j, 0)),
            scratch_shapes=[pltpu.VMEM((D_MODEL, 2 * EXPERT_FF), BF16), pltpu.VMEM((EXPERT_FF, D_MODEL), BF16)]),
        out_shape=jax.ShapeDtypeStruct((x_sorted.shape[0], half), jnp.uint32),
        compiler_params=_params("arbitrary"),
        name="experts",
    )(tile_expert, n_valid, x_sorted, w_gate_up, w_down)


def _combine_kernel(y1_ref, y2_ref, route_ref, x_ref, ml_ref, mc_ref, o_ref, *, tm):
    i = pl.program_id(1)
    route = route_ref[0]
    w1 = route[:, ROUTE_W:ROUTE_W + 1]
    w2 = route[:, ROUTE_W + 1:ROUTE_W + 2]
    moe = w1 * _unpack_bf16_pairs(y1_ref[0, 0]) + w2 * _unpack_bf16_pairs(y2_ref[0, 0])
    is_ctx = _ctx_rows(i, tm, D_MODEL)
    o_ref[0] = x_ref[0] + jnp.where(is_ctx, mc_ref[0, 5:6, :], ml_ref[0, 5:6, :]) * moe


def _combine(y_pairs, route, xs, mods, rows, tm):
    half = D_MODEL // 2
    tile = lambda w: pl.BlockSpec((1, tm, w), lambda b, i: (b, i, 0))
    slot = lambda s: pl.BlockSpec((1, 1, tm, half), lambda b, i: (s, b, i, 0))
    return pl.pallas_call(
        functools.partial(_combine_kernel, tm=tm),
        grid=(BATCH, rows // tm),
        in_specs=[slot(0), slot(1), tile(LANES), tile(D_MODEL),
                  pl.BlockSpec((1, N_MOD, D_MODEL), lambda b, i: (b, 0, 0)),
                  pl.BlockSpec((1, N_MOD, D_MODEL), lambda b, i: (BATCH, 0, 0))],
        out_specs=tile(D_MODEL),
        out_shape=jax.ShapeDtypeStruct((BATCH, rows, D_MODEL), F32),
        compiler_params=_params("parallel", "parallel"),
        name="combine",
    )(y_pairs, y_pairs, route, xs, mods, mods)


def _moe(h2, route, counts, xs, mods, w_gate_up, w_down, layer, rows, tm):
    n_tok = BATCH * rows
    half = D_MODEL // 2
    n_sorted = 2 * n_tok + N_EXPERTS * MOE_TILE
    n_tiles = n_sorted // MOE_TILE
    rec = route.reshape(n_tok, LANES)
    expert = rec[:, ROUTE_E:ROUTE_E + 2].astype(jnp.int32)
    rank = rec[:, ROUTE_RANK:ROUTE_RANK + 2].astype(jnp.int32)
    count = counts[0, :N_EXPERTS].astype(jnp.int32)
    padded = (count + MOE_TILE - 1) // MOE_TILE * MOE_TILE
    end = jnp.cumsum(padded)
    start = end - padded
    first = jnp.sum(jnp.where(expert[:, :1] == jnp.arange(N_EXPERTS), start, 0), axis=1)
    second = jnp.sum(jnp.where(expert[:, 1:] == jnp.arange(N_EXPERTS), start, 0), axis=1)
    dest = jnp.concatenate([first + rank[:, 0], second + rank[:, 1]])
    tile_start = jnp.arange(n_tiles, dtype=jnp.int32) * MOE_TILE
    tile_expert = jnp.minimum(jnp.sum(tile_start[:, None] >= end[None, :], axis=1), N_EXPERTS - 1).astype(jnp.int32)
    n_valid = (end[-1:] // MOE_TILE).astype(jnp.int32)
    x_sorted = _sc_scatter(h2.reshape(n_tok, half), dest, n_sorted)
    y_sorted = _experts(x_sorted, tile_expert, n_valid, w_gate_up, w_down, layer)
    y_pairs = _sc_gather(y_sorted, dest).reshape(2, BATCH, rows, half)
    return _combine(y_pairs, route, xs, mods, rows, tm)


def _split_w_in(w_in):
    bw = BRANCH_WIDTH
    sizes = (3 * bw, bw, 2 * GDN_HEADS, 2 * GDN_HEADS, bw, bw, bw, bw, bw, 3 * D_MODEL)
    offs = [0]
    for s in sizes:
        offs.append(offs[-1] + s)
    part = lambda i: w_in[:, :, offs[i]:offs[i + 1]]
    main = jnp.concatenate([part(9), part(0), part(1), part(4), part(5), part(6), part(7), part(8)], axis=2)
    ba = jnp.zeros(w_in.shape[:2] + (LANES,), F32).at[:, :, :4 * GDN_HEADS].set(
        jnp.concatenate([part(2), part(3)], axis=2))
    return main.astype(BF16), ba.astype(BF16)


def kernel(x, c, ctx, c_ctx, w_mod, b_mod, norm1_gain, norm2_gain, w_in, gdn_conv_w, gdn_a_log, gdn_dt_bias, gdn_out_gain, diff_q_gain, diff_k_gain, diff_lambda, diff_out_gain, lru_conv_w, lru_conv_b, lru_w_gate, lru_b_gate, lru_lambda, w_branch, w_out, w_router, b_router, w_gate_up, w_down):
    mods = _mods(c, c_ctx, w_mod, b_mod)
    cos, sin = _rope_tables()
    xs = jnp.concatenate([x, ctx], axis=1)
    w_main, w_ba = _split_w_in(w_in)
    for layer in range(DEPTH):
        last = layer == DEPTH - 1
        lam_init = 0.8 - 0.6 * math.exp(-0.3 * layer)
        m = mods[layer]
        p, gates = _project(xs, m, norm1_gain[layer], w_main, w_ba, gdn_a_log[layer], gdn_dt_bias[layer], layer)
        qkv = _gdn_conv(p, gdn_conv_w[layer])
        ya = _gdn(qkv, p, gates, gdn_out_gain[layer])
        yb = _diff_attn(p, cos, sin, diff_q_gain[layer], diff_k_gain[layer], diff_lambda[layer],
                        diff_out_gain[layer], lam_init, with_ctx=not last)
        yc = _lru(p, lru_conv_w[layer], lru_conv_b[layer], lru_w_gate[layer], lru_b_gate[layer], lru_lambda[layer])
        rows, tm = (SEQ, 512) if last else (TOK, 768)
        xs, h2, route, counts = _merge(ya, yb, yc, p, xs, m, norm2_gain[layer], w_branch[layer].astype(BF16),
                                       w_out[layer].astype(BF16), w_router, b_router, rows, tm)
        xs = _moe(h2, route, counts, xs, m, w_gate_up, w_down, layer, rows, tm)
    return xs
```

```python
import functools
import math

import jax
import jax.numpy as jnp
from jax import lax
from jax.experimental import pallas as pl
from jax.experimental.pallas import tpu as pltpu
from jax.experimental.pallas import tpu_sc as plsc

F32 = jnp.float32
BF16 = jnp.bfloat16

D_MODEL = 1024
BATCH = 8
SEQ = 2048
DEPTH = 2
GRID_W = 64
CTX_LEN = 256
TOK = SEQ + CTX_LEN
N_MOD = 6
EPS = 1e-6
CONV_WIDTH = 4
BRANCH_WIDTH = 512
GDN_HEADS = 4
GDN_HEAD_DIM = 128
GDN_CHUNK = 64
DIFF_HEADS = 4
DIFF_HEAD_DIM = 64
ROPE_BASE = 10000.0
ROPE_PAIRS = DIFF_HEAD_DIM // 4
LRU_BLOCKS = 8
LRU_BLOCK_DIM = BRANCH_WIDTH // LRU_BLOCKS
LRU_C = 8.0
N_EXPERTS = 16
N_GROUPS = 4
EXPERT_FF = 512

LANES = 128
VMEM_LIMIT = 56 * 1024 * 1024

COL_GATES = 0
COL_QKV = 3 * D_MODEL
COL_Z = COL_QKV + 3 * BRANCH_WIDTH
COL_DQ = COL_Z + BRANCH_WIDTH
COL_DK = COL_DQ + BRANCH_WIDTH
COL_DV = COL_DK + BRANCH_WIDTH
COL_LX = COL_DV + BRANCH_WIDTH
COL_LY = COL_LX + BRANCH_WIDTH
PROJ_COLS = COL_LY + BRANCH_WIDTH


def _params(*sem):
    return pltpu.CompilerParams(dimension_semantics=sem, vmem_limit_bytes=VMEM_LIMIT)


def _dot(a, b, precision=None):
    return jnp.dot(a, b, preferred_element_type=F32, precision=precision)


def _dot_nt(a, b):
    return lax.dot_general(a, b, (((1,), (1,)), ((), ())), preferred_element_type=F32)


_sigmoid = jax.nn.sigmoid


def _silu(x):
    return x * _sigmoid(x)


def _softplus(x):
    return jnp.maximum(x, 0.0) + jnp.log(1.0 + jnp.exp(-jnp.abs(x)))


def _rms(x, gain):
    return x * lax.rsqrt(jnp.mean(x * x, axis=-1, keepdims=True) + EPS) * gain


def _mod_kernel(c_ref, w_ref, b_ref, o_ref):
    c = c_ref[...]
    o_ref[0] = _dot(_silu(c), w_ref[0], precision=lax.Precision.HIGHEST) + b_ref[0]


def _mods(c, c_ctx, w_mod, b_mod):
    depth = w_mod.shape[0]
    rows = 16
    cc = jnp.zeros((rows, D_MODEL), F32).at[:BATCH].set(c).at[BATCH].set(c_ctx)
    tn = 1536
    out = pl.pallas_call(
        _mod_kernel,
        grid=(depth, N_MOD * D_MODEL // tn),
        in_specs=[pl.BlockSpec((rows, D_MODEL), lambda l, j: (0, 0)),
                  pl.BlockSpec((1, D_MODEL, tn), lambda l, j: (l, 0, j)),
                  pl.BlockSpec((1, 1, tn), lambda l, j: (l, 0, j))],
        out_specs=pl.BlockSpec((1, rows, tn), lambda l, j: (l, 0, j)),
        out_shape=jax.ShapeDtypeStruct((depth, rows, N_MOD * D_MODEL), F32),
        compiler_params=_params("parallel", "parallel"),
        name="mods",
    )(cc, w_mod, b_mod.reshape(depth, 1, N_MOD * D_MODEL))
    return out.reshape(depth, rows, N_MOD, D_MODEL)


def _modulated_norm(x, gain, ml_ref, mc_ref, is_ctx, shift_idx):
    shift = jnp.where(is_ctx, mc_ref[0, shift_idx:shift_idx + 1, :], ml_ref[0, shift_idx:shift_idx + 1, :])
    scale = jnp.where(is_ctx, mc_ref[0, shift_idx + 1:shift_idx + 2, :], ml_ref[0, shift_idx + 1:shift_idx + 2, :])
    return _rms(x, gain) * (1.0 + scale) + shift


def _ctx_rows(tile, tm, width):
    row = tile * tm + lax.broadcasted_iota(jnp.int32, (tm, width), 0)
    return row >= SEQ


def _proj_kernel(x_ref, ml_ref, mc_ref, g_ref, w_ref, wba_ref, alog_ref, dtb_ref, p_ref, gb_ref, h_scr, *, tm):
    i = pl.program_id(1)
    j = pl.program_id(2)

    @pl.when(j == 0)
    def _():
        is_ctx = _ctx_rows(i, tm, D_MODEL)
        h = _modulated_norm(x_ref[0], g_ref[...], ml_ref, mc_ref, is_ctx, 0).astype(BF16)
        h_scr[...] = h
        ba = _dot(h, wba_ref[0])
        lane = lax.broadcasted_iota(jnp.int32, ba.shape, 1)
        log_decay = -jnp.exp(alog_ref[...]) * _softplus(ba + dtb_ref[...])
        gb_ref[0] = jnp.where(lane < 2 * GDN_HEADS, _sigmoid(ba), log_decay)

    p_ref[0] = _dot(h_scr[...], w_ref[0]).astype(BF16)


def _project(xs, mods, gain, w_main, w_ba, a_log, dt_bias, layer):
    def pad_lanes(vals):
        row = jnp.zeros((LANES,), F32).at[2 * GDN_HEADS:4 * GDN_HEADS].set(vals.reshape(-1))
        return row.reshape(1, LANES)

    vec = pl.BlockSpec((1, LANES), lambda b, i, j: (0, 0))
    tm, tn = 1152, 1280
    return pl.pallas_call(
        functools.partial(_proj_kernel, tm=tm),
        grid=(BATCH, TOK // tm, PROJ_COLS // tn),
        in_specs=[pl.BlockSpec((1, tm, D_MODEL), lambda b, i, j: (b, i, 0)),
                  pl.BlockSpec((1, N_MOD, D_MODEL), lambda b, i, j: (b, 0, 0)),
                  pl.BlockSpec((1, N_MOD, D_MODEL), lambda b, i, j: (BATCH, 0, 0)),
                  pl.BlockSpec((1, D_MODEL), lambda b, i, j: (0, 0)),
                  pl.BlockSpec((1, D_MODEL, tn), lambda b, i, j: (layer, 0, j)),
                  pl.BlockSpec((1, D_MODEL, LANES), lambda b, i, j: (layer, 0, 0)), vec, vec],
        out_specs=[pl.BlockSpec((1, tm, tn), lambda b, i, j: (b, i, j)),
                   pl.BlockSpec((1, tm, LANES), lambda b, i, j: (b, i, 0))],
        out_shape=[jax.ShapeDtypeStruct((BATCH, TOK, PROJ_COLS), BF16),
                   jax.ShapeDtypeStruct((BATCH, TOK, LANES), F32)],
        scratch_shapes=[pltpu.VMEM((tm, D_MODEL), BF16)],
        compiler_params=_params("parallel", "parallel", "arbitrary"),
        name="proj",
    )(xs, mods, mods, gain.reshape(1, D_MODEL), w_main, w_ba, pad_lanes(a_log), pad_lanes(dt_bias))


def _conv(x, w):
    n, c = x.shape
    t = lax.broadcasted_iota(jnp.int32, (n, c), 0)
    is_ctx = t >= SEQ
    local = jnp.where(is_ctx, t - SEQ, t)
    seg_len = jnp.where(is_ctx, CTX_LEN, SEQ)
    y = jnp.zeros_like(x)
    for j in range(CONV_WIDTH):
        s = j - CONV_WIDTH // 2
        if s == 0:
            y = y + x * w[j:j + 1, :]
        else:
            shifted = pltpu.roll(x, (-s) % n, 0)
            ok = jnp.logical_and(local + s >= 0, local + s < seg_len)
            y = y + jnp.where(ok, shifted, 0.0) * w[j:j + 1, :]
    return y


def _gdn_conv_kernel(x_ref, w_ref, o_ref):
    j = pl.program_id(1)
    mul = jnp.where(j == 0, float(GDN_HEAD_DIM), jnp.where(j == 1, 1.0, 0.0))
    add = jnp.where(j == 0, float(GDN_HEAD_DIM) * EPS, jnp.where(j == 1, EPS, 1.0))
    for h in range(GDN_HEADS):
        cols = slice(h * GDN_HEAD_DIM, (h + 1) * GDN_HEAD_DIM)
        y = _silu(_conv(x_ref[0, :, cols].astype(F32), w_ref[:, cols]))
        scale = lax.rsqrt(jnp.sum(y * y, axis=-1, keepdims=True) * mul + add)
        o_ref[0, :, cols] = (y * scale).astype(BF16)


def _gdn_conv(p, conv_w):
    first = COL_QKV // BRANCH_WIDTH
    return pl.pallas_call(
        _gdn_conv_kernel,
        grid=(BATCH, 3),
        in_specs=[pl.BlockSpec((1, TOK, BRANCH_WIDTH), lambda b, j: (b, 0, first + j)),
                  pl.BlockSpec((CONV_WIDTH, BRANCH_WIDTH), lambda b, j: (0, j))],
        out_specs=pl.BlockSpec((1, TOK, BRANCH_WIDTH), lambda b, j: (b, 0, j)),
        out_shape=jax.ShapeDtypeStruct((BATCH, TOK, 3 * BRANCH_WIDTH), BF16),
        compiler_params=_params("parallel", "parallel"),
        name="gdn_conv",
    )(p, conv_w)


GDN_QM_ROWS = GDN_CHUNK + GDN_HEAD_DIM
GDN_GL_ROWS = 8
GDN_HEADS_PER_STEP = 2
GDN_PREP_UNROLL = 12


def _gdn_prepare(q_ref, k_ref, v_ref, gb_ref, qm_scr, nn_scr, o_scr, gl_scr, chunks, head, local):
    c = GDN_CHUNK
    cols = slice(local * LANES, (local + 1) * LANES)
    lane = lax.broadcasted_iota(jnp.int32, (c, LANES), 1)
    row = lax.broadcasted_iota(jnp.int32, (c, LANES), 0)
    ii = lax.broadcasted_iota(jnp.int32, (c, c), 0)
    jj = lax.broadcasted_iota(jnp.int32, (c, c), 1)
    eye = (ii == jj).astype(F32)
    masks =((ii >= jj, ii > jj, row > lane), (ii <= jj, ii < jj, row < lane))

    loaded = []
    for chunk in chunks:
        r0 = pl.multiple_of(chunk * c, c)
        k = k_ref[0, pl.ds(r0, c), cols]
        q = q_ref[0, pl.ds(r0, c), cols]
        kq = _dot_nt(jnp.concatenate([k, q], axis=0), k)
        loaded.append((chunk, r0, q, k, kq))

    chains = []
    for chunk, r0, q, k, kq in loaded:
        gb = gb_ref[0, pl.ds(r0, c), :]
        for d in range(2):
            col = head + d * GDN_HEADS
            beta = jnp.sum(jnp.where(lane == col, gb, 0.0), axis=-1, keepdims=True)
            g = jnp.sum(jnp.where(lane == col + 2 * GDN_HEADS, gb, 0.0), axis=-1, keepdims=True)
            incl, strict, strict_wide = masks[d]
            rhs = jnp.where(lane >= c, g, jnp.where(strict_wide, g, 0.0))
            mask = incl.astype(BF16)
            hi = rhs.astype(BF16)
            rest = rhs - hi.astype(F32)
            mid = rest.astype(BF16)
            low = (rest - mid.astype(F32)).astype(BF16)
            e = _dot(mask, hi) + _dot(mask, mid) + _dot(mask, low)
            chains.append(dict(chunk=chunk, r0=r0, d=d, q=q, k=k, kq=kq, beta=beta, e=e))

    for ch in chains:
        incl, strict, _ = masks[ch["d"]]
        e = ch["e"]
        decay = jnp.where(incl, jnp.exp(e[:, :c]), 0.0)
        gc = e[:, c:c + 1]
        last = 0 if ch["d"] == 1 else c - 1
        gc_last = e[last:last + 1, c:c + 1]
        ch.update(decay=decay, gc=gc, gc_last=gc_last, egc=jnp.exp(gc))
        ch["a"] = jnp.where(strict, ch["beta"] * ch["kq"][:c] * decay, 0.0)
        ch["t"] = eye
    s = 1
    while s < c:
        pair = jnp.logical_and((ii // (2 * s)) == (jj // (2 * s)), (ii // s) != (jj // s))
        for ch in chains:
            ch["a_off"] = jnp.where(pair, ch["a"], 0.0)
        if s == 1:
            for ch in chains:
                ch["t"] = eye - ch["a_off"]
        else:
            for ch in chains:
                ch["m"] = _dot(ch["t"].astype(BF16), ch["a_off"].astype(BF16))
            for ch in chains:
                ch["t"] = ch["t"] - _dot(ch["m"].astype(BF16), ch["t"].astype(BF16))
        s *= 2
    for ch in chains:
        r0, beta, egc = ch["r0"], ch["beta"], ch["egc"]
        kf = ch["k"].astype(F32)
        vf = v_ref[0, pl.ds(r0, c), cols].astype(F32)
        rhs2 = jnp.concatenate([vf * beta, kf * (beta * egc)], axis=1).astype(BF16)
        ch["uw"] = _dot(ch["t"].astype(BF16), rhs2).astype(BF16)
        ch["k_dec_t"] = (kf * jnp.exp(ch["gc_last"] - ch["gc"])).T.astype(BF16)
    for ch in chains:
        incl = masks[ch["d"]][0]
        qk = jnp.where(incl, ch["kq"][c:] * ch["decay"], 0.0).astype(BF16)
        ch["nm"] = _dot(ch["k_dec_t"], ch["uw"])
        ch["ow"] = _dot(qk, ch["uw"])
    for ch in chains:
        chunk, r0, nm, ow = ch["chunk"], ch["r0"], ch["nm"], ch["ow"]
        s = 2 * local + ch["d"]
        q0 = pl.multiple_of(chunk * GDN_QM_ROWS, 16)
        qm_scr[s, pl.ds(q0, c), :] = (ch["q"].astype(F32) * ch["egc"] - ow[:, GDN_HEAD_DIM:]).astype(BF16)
        qm_scr[s, pl.ds(q0 + c, GDN_HEAD_DIM), :] = nm[:, GDN_HEAD_DIM:].astype(BF16)
        nn_scr[s, pl.ds(pl.multiple_of(chunk * GDN_HEAD_DIM, GDN_HEAD_DIM), GDN_HEAD_DIM), :] = nm[:, :GDN_HEAD_DIM]
        o_scr[s, pl.ds(r0, c), :] = ow[:, :GDN_HEAD_DIM]
        gl_scr[s, pl.ds(pl.multiple_of(chunk * GDN_GL_ROWS, GDN_GL_ROWS), GDN_GL_ROWS), :] = jnp.broadcast_to(
            jnp.exp(ch["gc_last"]), (GDN_GL_ROWS, LANES))


def _gdn_advance(qm_scr, nn_scr, o_scr, gl_scr, d, chunk, state):
    c = GDN_CHUNK
    qm = qm_scr[d, pl.ds(pl.multiple_of(chunk * GDN_QM_ROWS, 16), GDN_QM_ROWS), :]
    r = _dot(qm, state.astype(BF16))
    rows = pl.ds(pl.multiple_of(chunk * c, c), c)
    o_scr[d, rows, :] = o_scr[d, rows, :] + r[:c]
    gl = gl_scr[d, pl.ds(pl.multiple_of(chunk * GDN_GL_ROWS, GDN_GL_ROWS), 1), :]
    n = nn_scr[d, pl.ds(pl.multiple_of(chunk * GDN_HEAD_DIM, GDN_HEAD_DIM), GDN_HEAD_DIM), :]
    return state * gl - r[c:] + n


def _gdn_kernel(q_ref, k_ref, v_ref, z_ref, gb_ref, gain_ref, o_ref, qm_scr, nn_scr, o_scr, gl_scr):
    first_head = pl.program_id(1) * GDN_HEADS_PER_STEP
    c = GDN_CHUNK
    n_lat, n_ctx = SEQ // c, CTX_LEN // c
    n_chunks = n_lat + n_ctx

    for local in range(GDN_HEADS_PER_STEP):
        def prepare(i, _, local=local):
            chunks = [i * GDN_PREP_UNROLL + j for j in range(GDN_PREP_UNROLL)]
            _gdn_prepare(q_ref, k_ref, v_ref, gb_ref, qm_scr, nn_scr, o_scr, gl_scr, chunks, first_head + local, local)
            return 0

        lax.fori_loop(0, n_chunks // GDN_PREP_UNROLL, prepare, 0)

    def advance(i, states):
        cf = jnp.where(i < n_ctx, n_lat + i, i - n_ctx)
        cb = n_chunks - 1 - i
        return tuple(_gdn_advance(qm_scr, nn_scr, o_scr, gl_scr, s, cb if s % 2 else cf, state)
                     for s, state in enumerate(states))

    zero = jnp.zeros((GDN_HEAD_DIM, GDN_HEAD_DIM), F32)
    lax.fori_loop(0, n_chunks, advance, (zero,) * (2 * GDN_HEADS_PER_STEP), unroll=4)
    for local in range(GDN_HEADS_PER_STEP):
        cols = slice(local * LANES, (local + 1) * LANES)
        o = o_scr[2 * local] + o_scr[2 * local + 1]
        o_ref[0, :, cols] = (_rms(o, gain_ref[...]) * _silu(z_ref[0, :, cols].astype(F32))).astype(BF16)


def _gdn(qkv, p, gates, out_gain):
    n_steps = GDN_HEADS // GDN_HEADS_PER_STEP
    n_chunks = TOK // GDN_CHUNK
    width = GDN_HEADS_PER_STEP * LANES
    zblk = COL_Z // width
    n_chain = 2 * GDN_HEADS_PER_STEP
    blk = lambda off: pl.BlockSpec((1, TOK, width), lambda b, h: (b, 0, off + h))
    vec = pl.BlockSpec((1, LANES), lambda b, h: (0, 0))
    return pl.pallas_call(
        _gdn_kernel,
        grid=(BATCH, n_steps),
        in_specs=[blk(0), blk(n_steps), blk(2 * n_steps), blk(zblk),
                  pl.BlockSpec((1, TOK, LANES), lambda b, h: (b, 0, 0)), vec],
        out_specs=pl.BlockSpec((1, TOK, width), lambda b, h: (b, 0, h)),
        out_shape=jax.ShapeDtypeStruct((BATCH, TOK, BRANCH_WIDTH), BF16),
        scratch_shapes=[pltpu.VMEM((n_chain, n_chunks * GDN_QM_ROWS, LANES), BF16),
                        pltpu.VMEM((n_chain, n_chunks * GDN_HEAD_DIM, LANES), F32),
                        pltpu.VMEM((n_chain, TOK, LANES), F32),
                        pltpu.VMEM((n_chain, n_chunks * GDN_GL_ROWS, LANES), F32)],
        compiler_params=_params("parallel", "parallel"),
        name="gdn",
    )(qkv, qkv, qkv, p, gates, out_gain.reshape(1, LANES))


def _rms_halves(x, gain):
    lane = lax.broadcasted_iota(jnp.int32, x.shape, 1)
    lo = lane < DIFF_HEAD_DIM
    x2 = x * x
    s_lo = jnp.sum(jnp.where(lo, x2, 0.0), axis=-1, keepdims=True)
    s_hi = jnp.sum(jnp.where(lo, 0.0, x2), axis=-1, keepdims=True)
    ms = jnp.where(lo, s_lo, s_hi) * (1.0 / DIFF_HEAD_DIM)
    return x * lax.rsqrt(ms + EPS) * gain


def _rope(x, cos, sin):
    lane = lax.broadcasted_iota(jnp.int32, x.shape, 1)
    first = (lane & ROPE_PAIRS) == 0
    partner = jnp.where(first, -pltpu.roll(x, LANES - ROPE_PAIRS, 1), pltpu.roll(x, ROPE_PAIRS, 1))
    return x * cos + partner * sin


ATTN_Q_BLOCK = 1024
ATTN_GROUP_ROWS = 128


def _attn_kernel(q_ref, k_ref, v_ref, cosk_ref, sink_ref, cosq_ref, sinq_ref, qg_ref, kg_ref, lv_ref, og_ref,
                 o_ref, kn_scr, *, ctx_block, lam_init):
    qi = pl.program_id(2)

    @pl.when(qi == 0)
    def _():
        kn = _rope(_rms_halves(k_ref[0].astype(F32), kg_ref[...]), cosk_ref[...], sink_ref[...])
        kn_scr[...] = kn.astype(BF16)

    lv = lv_ref[...]
    lam = (jnp.exp(jnp.sum(lv[0:1] * lv[1:2], axis=-1, keepdims=True))
           - jnp.exp(jnp.sum(lv[2:3] * lv[3:4], axis=-1, keepdims=True)) + lam_init)
    def attend(n_rows, kn, v):
        q = _rope(_rms_halves(q_ref[0, :n_rows, :].astype(F32), qg_ref[...]), cosq_ref[:n_rows, :], sinq_ref[:n_rows, :])
        q = q * (DIFF_HEAD_DIM ** -0.5 * math.log2(math.e))
        lane = lax.broadcasted_iota(jnp.int32, q.shape, 1)
        lo = lane < DIFF_HEAD_DIM
        q1 = jnp.where(lo, q, 0.0).astype(BF16)
        q2 = jnp.where(lo, 0.0, q).astype(BF16)

        def probs(s):
            p = jnp.exp2(s - jnp.max(s, axis=-1, keepdims=True))
            return p.astype(BF16), jnp.sum(p, axis=-1, keepdims=True)
        rows = ATTN_GROUP_ROWS
        starts = list(range(0, n_rows, rows))
        score = lambda r: (_dot_nt(q1[r:r + rows], kn), _dot_nt(q2[r:r + rows], kn))
        scores = [score(starts[0])]
        for g in range(len(starts)):
            s1, s2 = scores[g]
            p1, l1 = probs(s1)
            p2, l2 = probs(s2)
            if g + 1 < len(starts):
                scores.append(score(starts[g + 1]))
            o = _dot(p1, v) * (1.0 / l1) - _dot(p2, v) * (lam / l2)
            o_ref[0, g * rows:(g + 1) * rows, :] = (_rms(o, og_ref[...]) * (1.0 - lam_init)).astype(BF16)

    if ctx_block is None:
        attend(q_ref.shape[1], kn_scr[...], v_ref[0])
    else:
        @pl.when(qi == ctx_block)
        def _():
            attend(CTX_LEN, kn_scr[SEQ:, :], v_ref[0, SEQ:, :])

        @pl.when(qi != ctx_block)
        def _():
            attend(q_ref.shape[1], kn_scr[...], v_ref[0])


def _rope_tables():
    n_rows = SEQ // GRID_W
    row_id = jnp.broadcast_to(jnp.arange(n_rows, dtype=F32)[:, None], (n_rows, GRID_W)).reshape(-1)
    col_id = jnp.broadcast_to(jnp.arange(GRID_W, dtype=F32)[None, :], (n_rows, GRID_W)).reshape(-1)
    inv_freq = jnp.power(ROPE_BASE, -jnp.arange(ROPE_PAIRS, dtype=F32) / ROPE_PAIRS)
    row_ang = row_id[:, None] * inv_freq
    col_ang = col_id[:, None] * inv_freq
    ang = jnp.concatenate([row_ang, row_ang, col_ang, col_ang], axis=-1)
    ang = jnp.concatenate([ang, ang], axis=-1)
    pad = ((0, CTX_LEN), (0, 0))
    return jnp.pad(jnp.cos(ang), pad, constant_values=1.0), jnp.pad(jnp.sin(ang), pad)


def _diff_attn(p, cos, sin, q_gain, k_gain, lam_vecs, out_gain, lam_init, with_ctx):
    tq = ATTN_Q_BLOCK
    n_rows = TOK if with_ctx else SEQ
    nq = pl.cdiv(n_rows, tq)
    nh = DIFF_HEADS
    qb, kb, vb = COL_DQ // LANES, COL_DK // LANES, COL_DV // LANES
    full = lambda off: pl.BlockSpec((1, TOK, LANES), lambda b, h, i: (b, 0, off + h))
    tab_full = pl.BlockSpec((TOK, LANES), lambda b, h, i: (0, 0))
    tab_q = pl.BlockSpec((tq, LANES), lambda b, h, i: (i, 0))
    vec = pl.BlockSpec((1, LANES), lambda b, h, i: (0, 0))
    tile2 = lambda g: jnp.concatenate([g, g]).reshape(1, LANES)
    return pl.pallas_call(
        functools.partial(_attn_kernel, ctx_block=SEQ // tq if with_ctx else None, lam_init=lam_init),
        grid=(BATCH, nh, nq),
        in_specs=[pl.BlockSpec((1, tq, LANES), lambda b, h, i: (b, i, qb + h)), full(kb), full(vb),
                  tab_full, tab_full, tab_q, tab_q, vec, vec,
                  pl.BlockSpec((4, DIFF_HEAD_DIM), lambda b, h, i: (0, 0)), vec],
        out_specs=pl.BlockSpec((1, tq, LANES), lambda b, h, i: (b, i, h)),
        out_shape=jax.ShapeDtypeStruct((BATCH, n_rows, BRANCH_WIDTH), BF16),
        scratch_shapes=[pltpu.VMEM((TOK, LANES), BF16)],
        compiler_params=_params("parallel", "parallel", "arbitrary"),
        name="diff_attn",
    )(p, p, p, cos, sin, cos, sin, tile2(q_gain), tile2(k_gain), lam_vecs, out_gain.reshape(1, LANES))


LRU_SLAB = 256
LRU_ROWS = 768
LRU_SCAN_BLOCK = 8


def _lru_kernel(x_ref, y_ref, cw_ref, cb_ref, wg_ref, bg_ref, lam_ref, o_ref, xc_scr, af_scr, bf_scr, ab_scr, bb_scr):
    w = LRU_SLAB
    blk = LRU_SCAN_BLOCK
    for c0 in range(0, w, LANES):
        cols = slice(c0, c0 + LANES)
        xc_scr[:, cols] = _conv(x_ref[0, :, cols].astype(F32), cw_ref[:, cols]) + cb_ref[:, cols]
    sp = _softplus(-lam_ref[0])
    sub = lax.broadcasted_iota(jnp.int32, (LRU_ROWS // blk, blk, w), 1)

    def gates(i, _):
        r0 = pl.multiple_of(i * LRU_ROWS, LRU_ROWS)
        xc = xc_scr[pl.ds(r0, LRU_ROWS), :]
        pre = _dot(xc.astype(BF16), wg_ref[0]) + bg_ref[0]
        for d, (a_scr, b_scr) in enumerate(((af_scr, bf_scr), (ab_scr, bb_scr))):
            r = _sigmoid(pre[:, (2 * d) * w:(2 * d + 1) * w])
            gi = _sigmoid(pre[:, (2 * d + 1) * w:(2 * d + 2) * w])
            log_a = -LRU_C * r * sp[d:d + 1]
            a = jnp.exp(log_a)
            b = jnp.sqrt(1.0 - a * a) * gi * xc
            a = a.reshape(LRU_ROWS // blk, blk, w)
            b = b.reshape(LRU_ROWS // blk, blk, w)
            shift = 1
            while shift < blk:
                if d == 0:
                    ok, roll_by = sub >= shift, shift
                else:
                    ok, roll_by = sub < blk - shift, blk - shift
                b = jnp.where(ok, a * pltpu.roll(b, roll_by, 1) + b, b)
                a = jnp.where(ok, a * pltpu.roll(a, roll_by, 1), a)
                shift *= 2
            a_scr[pl.ds(r0, LRU_ROWS), :] = a.reshape(LRU_ROWS, w)
            b_scr[pl.ds(r0, LRU_ROWS), :] = b.reshape(LRU_ROWS, w)
        return 0

    lax.fori_loop(0, TOK // LRU_ROWS, gates, 0)

    n_blk, n_lat_blk, n_ctx_blk = TOK // blk, SEQ // blk, CTX_LEN // blk

    def step(s, carry):
        h_f, h_b = carry
        rows_f = pl.ds(pl.multiple_of(jnp.where(s < n_ctx_blk, n_lat_blk + s, s - n_ctx_blk) * blk, blk), blk)
        rows_b = pl.ds(pl.multiple_of((n_blk - 1 - s) * blk, blk), blk)
        hf = af_scr[rows_f, :] * h_f + bf_scr[rows_f, :]
        bf_scr[rows_f, :] = hf
        hb = ab_scr[rows_b, :] * h_b + bb_scr[rows_b, :]
        bb_scr[rows_b, :] = hb
        return hf[blk - 1:blk, :], hb[0:1, :]

    zero = jnp.zeros((1, w), F32)
    lax.fori_loop(0, n_blk, step, (zero, zero), unroll=4)
    h = bf_scr[...] + bb_scr[...]
    o_ref[0] = (h * jax.nn.gelu(y_ref[0].astype(F32))).astype(BF16)


def _lru_gate_weights(w_gate, b_gate):
    n_slab = BRANCH_WIDTH // LRU_SLAB
    per = LRU_SLAB // LRU_BLOCK_DIM
    wg = w_gate.reshape(2, 2, n_slab, per, LRU_BLOCK_DIM, LRU_BLOCK_DIM)
    eye = jnp.eye(per, dtype=w_gate.dtype)
    dense = jnp.einsum('dgsnjk,nm->snjdgmk', wg, eye)
    dense = dense.reshape(n_slab, LRU_SLAB, 4 * LRU_SLAB)
    bg = b_gate.reshape(2, 2, n_slab, LRU_SLAB).transpose(2, 0, 1, 3).reshape(n_slab, 1, 4 * LRU_SLAB)
    return dense.astype(BF16), bg


def _lru(p, conv_w, conv_b, w_gate, b_gate, lam):
    n_slab = BRANCH_WIDTH // LRU_SLAB
    wg, bg = _lru_gate_weights(w_gate, b_gate)
    lam_s = lam.reshape(2, n_slab, LRU_SLAB).transpose(1, 0, 2)
    xb, yb = COL_LX // LRU_SLAB, COL_LY // LRU_SLAB
    return pl.pallas_call(
        _lru_kernel,
        grid=(BATCH, n_slab),
        in_specs=[pl.BlockSpec((1, TOK, LRU_SLAB), lambda b, s: (b, 0, xb + s)),
                  pl.BlockSpec((1, TOK, LRU_SLAB), lambda b, s: (b, 0, yb + s)),
                  pl.BlockSpec((CONV_WIDTH, LRU_SLAB), lambda b, s: (0, s)),
                  pl.BlockSpec((1, LRU_SLAB), lambda b, s: (0, s)),
                  pl.BlockSpec((1, LRU_SLAB, 4 * LRU_SLAB), lambda b, s: (s, 0, 0)),
                  pl.BlockSpec((1, 1, 4 * LRU_SLAB), lambda b, s: (s, 0, 0)),
                  pl.BlockSpec((1, 2, LRU_SLAB), lambda b, s: (s, 0, 0))],
        out_specs=pl.BlockSpec((1, TOK, LRU_SLAB), lambda b, s: (b, 0, s)),
        out_shape=jax.ShapeDtypeStruct((BATCH, TOK, BRANCH_WIDTH), BF16),
        scratch_shapes=[pltpu.VMEM((TOK, LRU_SLAB), F32)] * 5,
        compiler_params=_params("parallel", "parallel"),
        name="lru",
    )(p, p, conv_w, conv_b.reshape(1, BRANCH_WIDTH), wg, bg, lam_s)


def _route(logits):
    lane = lax.broadcasted_iota(jnp.int32, logits.shape, 1)
    lane_f = lane.astype(F32)
    far = float(LANES)
    lg = jnp.where(lane < N_EXPERTS, logits, -jnp.inf)
    ex = jnp.exp(lg - jnp.max(lg, axis=-1, keepdims=True))
    probs = ex / jnp.sum(ex, axis=-1, keepdims=True)
    per_group = N_EXPERTS // N_GROUPS
    grp = lane // per_group

    def top2(vals):
        m1 = jnp.max(vals, axis=-1, keepdims=True)
        i1 = jnp.min(jnp.where(vals == m1, lane_f, far), axis=-1, keepdims=True)
        rest = jnp.where(lane_f == i1, -2.0, vals)
        m2 = jnp.max(rest, axis=-1, keepdims=True)
        i2 = jnp.min(jnp.where(rest == m2, lane_f, far), axis=-1, keepdims=True)
        return m1, i1, m2, i2

    best = jnp.zeros(logits.shape[:1] + (1,), jnp.int32)
    best_score = None
    for g in range(N_GROUPS):
        m1, _, m2, _ = top2(jnp.where(grp == g, probs, -1.0))
        score = m1 + m2
        if best_score is None:
            best_score = score
        else:
            better = score > best_score
            best = jnp.where(better, g, best)
            best_score = jnp.where(better, score, best_score)
    m1, i1, m2, i2 = top2(jnp.where(grp == best, probs, -1.0))
    den = m1 + m2
    return i1, i2, m1 / den, m2 / den


ROUTE_E, ROUTE_W, ROUTE_RANK = 0, 2, 4


def _pack_bf16_pairs(x):
    n = x.shape[1] // 2
    xb = x.astype(BF16).astype(F32)
    lo = pltpu.bitcast(xb[:, :n], jnp.uint32)
    hi = pltpu.bitcast(xb[:, n:], jnp.uint32)
    return (lo >> 16) | (hi & jnp.uint32(0xFFFF0000))


def _unpack_bf16_pairs(p):
    lo = pltpu.bitcast(p << 16, F32)
    hi = pltpu.bitcast(p & jnp.uint32(0xFFFF0000), F32)
    return jnp.concatenate([lo, hi], axis=1)


def _merge_kernel(ya_ref, yb_ref, yc_ref, gates_ref, x_ref, ml_ref, mc_ref, g2_ref, wbr_ref, wout_ref, wr_ref,
                  br_ref, xo_ref, h2_ref, route_ref, cnt_ref, cnt_scr, tri_scr, *, tm):
    i = pl.program_id(1)

    @pl.when(jnp.logical_and(pl.program_id(0) == 0, i == 0))
    def _():
        cnt_scr[...] = jnp.zeros_like(cnt_scr)
        earlier = (lax.broadcasted_iota(jnp.int32, (tm, tm), 0) > lax.broadcasted_iota(jnp.int32, (tm, tm), 1))
        tri_scr[...] = jnp.where(earlier, 1.0, 0.0).astype(BF16)

    acc = None
    for n, y_ref in enumerate((ya_ref, yb_ref, yc_ref)):
        yn = _dot(y_ref[0], wbr_ref[n])
        gate = _sigmoid(gates_ref[0, :, n * D_MODEL:(n + 1) * D_MODEL].astype(F32))
        acc = gate * yn if acc is None else acc + gate * yn
    out = _dot(acc.astype(BF16), wout_ref[...])
    is_ctx = _ctx_rows(i, tm, D_MODEL)
    xn = x_ref[0] + jnp.where(is_ctx, mc_ref[0, 2:3, :], ml_ref[0, 2:3, :]) * out
    xo_ref[0] = xn
    h2 = _modulated_norm(xn, g2_ref[...], ml_ref, mc_ref, is_ctx, 3)
    h2_ref[0] = _pack_bf16_pairs(h2)
    i1, i2, w1, w2 = _route(_dot(h2.astype(BF16), wr_ref[...]) + br_ref[...])
    lane = lax.broadcasted_iota(jnp.int32, (tm, LANES), 1)
    lane_f = lane.astype(F32)
    chosen = jnp.where(jnp.logical_or(lane_f == i1, lane_f == i2), 1.0, 0.0)
    before = _dot(tri_scr[...], chosen.astype(BF16)) + cnt_scr[...]
    rank1 = jnp.sum(jnp.where(lane_f == i1, before, 0.0), axis=-1, keepdims=True)
    rank2 = jnp.sum(jnp.where(lane_f == i2, before, 0.0), axis=-1, keepdims=True)
    cnt_scr[...] += jnp.sum(chosen, axis=0, keepdims=True)
    cnt_ref[...] = cnt_scr[...]
    record = jnp.zeros((tm, LANES), F32)
    for pos, val in enumerate((i1, i2, w1, w2, rank1, rank2)):
        record = jnp.where(lane == pos, val, record)
    route_ref[0] = record


def _merge(ya, yb, yc, p, xs, mods, gain2, w_branch, w_out, w_router, b_router, rows, tm):
    wr = jnp.zeros((D_MODEL, LANES), BF16).at[:, :N_EXPERTS].set(w_router.astype(BF16))
    br = jnp.zeros((1, LANES), F32).at[0, :N_EXPERTS].set(b_router)
    tile = lambda w: pl.BlockSpec((1, tm, w), lambda b, i: (b, i, 0))
    const = lambda shape: pl.BlockSpec(shape, lambda b, i: (0,) * len(shape))
    return pl.pallas_call(
        functools.partial(_merge_kernel, tm=tm),
        grid=(BATCH, rows // tm),
        in_specs=[tile(BRANCH_WIDTH), tile(BRANCH_WIDTH), tile(BRANCH_WIDTH), tile(3 * D_MODEL), tile(D_MODEL),
                  pl.BlockSpec((1, N_MOD, D_MODEL), lambda b, i: (b, 0, 0)),
                  pl.BlockSpec((1, N_MOD, D_MODEL), lambda b, i: (BATCH, 0, 0)),
                  const((1, D_MODEL)), const((3, BRANCH_WIDTH, D_MODEL)), const((D_MODEL, D_MODEL)),
                  const((D_MODEL, LANES)), const((1, LANES))],
        out_specs=[tile(D_MODEL), tile(D_MODEL // 2), tile(LANES), const((1, LANES))],
        out_shape=[jax.ShapeDtypeStruct((BATCH, rows, D_MODEL), F32),
                   jax.ShapeDtypeStruct((BATCH, rows, D_MODEL // 2), jnp.uint32),
                   jax.ShapeDtypeStruct((BATCH, rows, LANES), F32),
                   jax.ShapeDtypeStruct((1, LANES), F32)],
        scratch_shapes=[pltpu.VMEM((1, LANES), F32), pltpu.VMEM((tm, tm), BF16)],
        compiler_params=_params("arbitrary", "arbitrary"),
        name="merge",
    )(ya, yb, yc, p, xs, mods, mods, gain2.reshape(1, D_MODEL), w_branch, w_out, wr, br)


MOE_TILE = 512
SC_GATHER_ROWS = 64


def _sc_gather(table, idx):
    info = plsc.get_sparse_core_info()
    n_workers = info.num_cores * info.num_subcores
    n_rows, width = idx.shape[0], table.shape[1]
    per_worker = n_rows // n_workers
    assert per_worker * n_workers == n_rows and per_worker % SC_GATHER_ROWS == 0
    mesh = plsc.VectorSubcoreMesh(core_axis_name="c", subcore_axis_name="s")

    @functools.partial(
        pl.kernel, mesh=mesh, out_type=jax.ShapeDtypeStruct((n_rows, width), table.dtype),
        scratch_types=[pltpu.VMEM((SC_GATHER_ROWS,), jnp.int32),
                       pltpu.VMEM((SC_GATHER_ROWS, width), table.dtype),
                       pltpu.SemaphoreType.DMA],
        name="sc_gather")
    def gather(table_hbm, idx_hbm, out_hbm, idx_v, rows_v, sem):
        worker = lax.axis_index("s") * info.num_cores + lax.axis_index("c")
        base = worker * per_worker

        @pl.loop(0, per_worker // SC_GATHER_ROWS)
        def _(j):
            off = base + j * SC_GATHER_ROWS
            pltpu.sync_copy(idx_hbm.at[pl.ds(off, SC_GATHER_ROWS)], idx_v)
            pltpu.async_copy(table_hbm.at[idx_v], rows_v, sem).wait()
            pltpu.sync_copy(rows_v, out_hbm.at[pl.ds(off, SC_GATHER_ROWS)])

    return gather(table, idx)


SC_SCATTER_ROWS = 128


def _sc_scatter(table, dest, n_out):
    info = plsc.get_sparse_core_info()
    n_workers = info.num_cores * info.num_subcores
    n_tok, width = table.shape
    n_assign = dest.shape[0]
    per_worker = n_assign // n_workers
    assert per_worker * n_workers == n_assign and per_worker % SC_SCATTER_ROWS == 0 and n_tok % SC_SCATTER_ROWS == 0
    mesh = plsc.VectorSubcoreMesh(core_axis_name="c", subcore_axis_name="s")

    @functools.partial(
        pl.kernel, mesh=mesh, out_type=jax.ShapeDtypeStruct((n_out, width), table.dtype),
        scratch_types=[pltpu.VMEM((SC_SCATTER_ROWS,), jnp.int32),
                       pltpu.VMEM((SC_SCATTER_ROWS, width), table.dtype),
                       pltpu.SemaphoreType.DMA],
        name="sc_scatter")
    def scatter(table_hbm, dest_hbm, out_hbm, idx_v, rows_v, sem):
        worker = lax.axis_index("s") * info.num_cores + lax.axis_index("c")
        base = worker * per_worker

        @pl.loop(0, per_worker // SC_SCATTER_ROWS)
        def _(j):
            off = base + j * SC_SCATTER_ROWS
            pltpu.sync_copy(dest_hbm.at[pl.ds(off, SC_SCATTER_ROWS)], idx_v)
            pltpu.sync_copy(table_hbm.at[pl.ds(lax.rem(off, n_tok), SC_SCATTER_ROWS)], rows_v)
            pltpu.async_copy(rows_v, out_hbm.at[idx_v], sem).wait()

    return scatter(table, dest)


def _experts_kernel(tile_expert_ref, n_valid_ref, x_ref, wgu_ref, wd_ref, o_ref, wgu_scr, wd_scr):
    j = pl.program_id(0)
    valid = j < n_valid_ref[0]
    fresh = jnp.logical_or(j == 0, tile_expert_ref[j] != tile_expert_ref[jnp.maximum(j - 1, 0)])

    @pl.when(jnp.logical_and(valid, fresh))
    def _():
        wgu_scr[...] = wgu_ref[0, 0].astype(BF16)
        wd_scr[...] = wd_ref[0, 0].astype(BF16)

    @pl.when(valid)
    def _():
        x = _unpack_bf16_pairs(x_ref[...]).astype(BF16)
        gu = _dot(x, wgu_scr[...])
        act = (_silu(gu[:, :EXPERT_FF]) * gu[:, EXPERT_FF:]).astype(BF16)
        o_ref[...] = _pack_bf16_pairs(_dot(act, wd_scr[...]))

    @pl.when(jnp.logical_not(valid))
    def _():
        o_ref[...] = jnp.zeros_like(o_ref)


def _experts(x_sorted, tile_expert, n_valid, w_gate_up, w_down, layer):
    n_tiles = x_sorted.shape[0] // MOE_TILE
    half = D_MODEL // 2
    return pl.pallas_call(
        _experts_kernel,
        grid_spec=pltpu.PrefetchScalarGridSpec(
            num_scalar_prefetch=2, grid=(n_tiles,),
            in_specs=[pl.BlockSpec((MOE_TILE, half), lambda j, te, nv: (j, 0)),
                      pl.BlockSpec((1, 1, D_MODEL, 2 * EXPERT_FF), lambda j, te, nv: (layer, te[j], 0, 0)),
                      pl.BlockSpec((1, 1, EXPERT_FF, D_MODEL), lambda j, te, nv: (layer, te[j], 0, 0))],
            out_specs=pl.BlockSpec((MOE_TILE, half), lambda j, te, nv: (j, 0)),
            scratch_shapes=[pltpu.VMEM((D_MODEL, 2 * EXPERT_FF), BF16), pltpu.VMEM((EXPERT_FF, D_MODEL), BF16)]),
        out_shape=jax.ShapeDtypeStruct((x_sorted.shape[0], half), jnp.uint32),
        compiler_params=_params("arbitrary"),
        name="experts",
    )(tile_expert, n_valid, x_sorted, w_gate_up, w_down)


def _combine_kernel(y1_ref, y2_ref, route_ref, x_ref, ml_ref, mc_ref, o_ref, *, tm):
    i = pl.program_id(1)
    route = route_ref[0]
    w1 = route[:, ROUTE_W:ROUTE_W + 1]
    w2 = route[:, ROUTE_W + 1:ROUTE_W + 2]
    moe = w1 * _unpack_bf16_pairs(y1_ref[0, 0]) + w2 * _unpack_bf16_pairs(y2_ref[0, 0])
    is_ctx = _ctx_rows(i, tm, D_MODEL)
    o_ref[0] = x_ref[0] + jnp.where(is_ctx, mc_ref[0, 5:6, :], ml_ref[0, 5:6, :]) * moe


def _combine(y_pairs, route, xs, mods, rows, tm):
    half = D_MODEL // 2
    tile = lambda w: pl.BlockSpec((1, tm, w), lambda b, i: (b, i, 0))
    slot = lambda s: pl.BlockSpec((1, 1, tm, half), lambda b, i: (s, b, i, 0))
    return pl.pallas_call(
        functools.partial(_combine_kernel, tm=tm),
        grid=(BATCH, rows // tm),
        in_specs=[slot(0), slot(1), tile(LANES), tile(D_MODEL),
                  pl.BlockSpec((1, N_MOD, D_MODEL), lambda b, i: (b, 0, 0)),
                  pl.BlockSpec((1, N_MOD, D_MODEL), lambda b, i: (BATCH, 0, 0))],
        out_specs=tile(D_MODEL),
        out_shape=jax.ShapeDtypeStruct((BATCH, rows, D_MODEL), F32),
        compiler_params=_params("parallel", "parallel"),
        name="combine",
    )(y_pairs, y_pairs, route, xs, mods, mods)


def _moe(h2, route, counts, xs, mods, w_gate_up, w_down, layer, rows, tm):
    n_tok = BATCH * rows
    half = D_MODEL // 2
    n_sorted = 2 * n_tok + N_EXPERTS * MOE_TILE
    n_tiles = n_sorted // MOE_TILE
    rec = route.reshape(n_tok, LANES)
    expert = rec[:, ROUTE_E:ROUTE_E + 2].astype(jnp.int32)
    rank = rec[:, ROUTE_RANK:ROUTE_RANK + 2].astype(jnp.int32)
    count = counts[0, :N_EXPERTS].astype(jnp.int32)
    padded = (count + MOE_TILE - 1) // MOE_TILE * MOE_TILE
    end = jnp.cumsum(padded)
    start = end - padded
    first = jnp.sum(jnp.where(expert[:, :1] == jnp.arange(N_EXPERTS), start, 0), axis=1)
    second = jnp.sum(jnp.where(expert[:, 1:] == jnp.arange(N_EXPERTS), start, 0), axis=1)
    dest = jnp.concatenate([first + rank[:, 0], second + rank[:, 1]])
    tile_start = jnp.arange(n_tiles, dtype=jnp.int32) * MOE_TILE
    tile_expert = jnp.minimum(jnp.sum(tile_start[:, None] >= end[None, :], axis=1), N_EXPERTS - 1).astype(jnp.int32)
    n_valid = (end[-1:] // MOE_TILE).astype(jnp.int32)
    x_sorted = _sc_scatter(h2.reshape(n_tok, half), dest, n_sorted)
    y_sorted = _experts(x_sorted, tile_expert, n_valid, w_gate_up, w_down, layer)
    y_pairs = _sc_gather(y_sorted, dest).reshape(2, BATCH, rows, half)
    return _combine(y_pairs, route, xs, mods, rows, tm)


def _split_w_in(w_in):
    bw = BRANCH_WIDTH
    sizes = (3 * bw, bw, 2 * GDN_HEADS, 2 * GDN_HEADS, bw, bw, bw, bw, bw, 3 * D_MODEL)
    offs = [0]
    for s in sizes:
        offs.append(offs[-1] + s)
    part = lambda i: w_in[:, :, offs[i]:offs[i + 1]]
    main = jnp.concatenate([part(9), part(0), part(1), part(4), part(5), part(6), part(7), part(8)], axis=2)
    ba = jnp.zeros(w_in.shape[:2] + (LANES,), F32).at[:, :, :4 * GDN_HEADS].set(
        jnp.concatenate([part(2), part(3)], axis=2))
    return main.astype(BF16), ba.astype(BF16)


def kernel(x, c, ctx, c_ctx, w_mod, b_mod, norm1_gain, norm2_gain, w_in, gdn_conv_w, gdn_a_log, gdn_dt_bias, gdn_out_gain, diff_q_gain, diff_k_gain, diff_lambda, diff_out_gain, lru_conv_w, lru_conv_b, lru_w_gate, lru_b_gate, lru_lambda, w_branch, w_out, w_router, b_router, w_gate_up, w_down):
    mods = _mods(c, c_ctx, w_mod, b_mod)
    cos, sin = _rope_tables()
    xs = jnp.concatenate([x, ctx], axis=1)
    w_main, w_ba = _split_w_in(w_in)
    for layer in range(DEPTH):
        last = layer == DEPTH - 1
        lam_init = 0.8 - 0.6 * math.exp(-0.3 * layer)
        m = mods[layer]
        p, gates = _project(xs, m, norm1_gain[layer], w_main, w_ba, gdn_a_log[layer], gdn_dt_bias[layer], layer)
        qkv = _gdn_conv(p, gdn_conv_w[layer])
        ya = _gdn(qkv, p, gates, gdn_out_gain[layer])
        yb = _diff_attn(p, cos, sin, diff_q_gain[layer], diff_k_gain[layer], diff_lambda[layer],
                        diff_out_gain[layer], lam_init, with_ctx=not last)
        yc = _lru(p, lru_conv_w[layer], lru_conv_b[layer], lru_w_gate[layer], lru_b_gate[layer], lru_lambda[layer])
        rows, tm = (SEQ, 512) if last else (TOK, 768)
        xs, h2, route, counts = _merge(ya, yb, yc, p, xs, m, norm2_gain[layer], w_branch[layer].astype(BF16),
                                       w_out[layer].astype(BF16), w_router, b_router, rows, tm)
        xs = _moe(h2, route, counts, xs, m, w_gate_up, w_down, layer, rows, tm)
    return xs
```

```python
import functools
import math

import jax
import jax.numpy as jnp
from jax import lax
from jax.experimental import pallas as pl
from jax.experimental.pallas import tpu as pltpu
from jax.experimental.pallas import tpu_sc as plsc

F32 = jnp.float32
BF16 = jnp.bfloat16

D_MODEL = 1024
BATCH = 8
SEQ = 2048
DEPTH = 2
GRID_W = 64
CTX_LEN = 256
TOK = SEQ + CTX_LEN
N_MOD = 6
EPS = 1e-6
CONV_WIDTH = 4
BRANCH_WIDTH = 512
GDN_HEADS = 4
GDN_HEAD_DIM = 128
GDN_CHUNK = 64
DIFF_HEADS = 4
DIFF_HEAD_DIM = 64
ROPE_BASE = 10000.0
ROPE_PAIRS = DIFF_HEAD_DIM // 4
LRU_BLOCKS = 8
LRU_BLOCK_DIM = BRANCH_WIDTH // LRU_BLOCKS
LRU_C = 8.0
N_EXPERTS = 16
N_GROUPS = 4
EXPERT_FF = 512

LANES = 128
VMEM_LIMIT = 56 * 1024 * 1024

COL_GATES = 0
COL_QKV = 3 * D_MODEL
COL_Z = COL_QKV + 3 * BRANCH_WIDTH
COL_DQ = COL_Z + BRANCH_WIDTH
COL_DK = COL_DQ + BRANCH_WIDTH
COL_DV = COL_DK + BRANCH_WIDTH
COL_LX = COL_DV + BRANCH_WIDTH
COL_LY = COL_LX + BRANCH_WIDTH
PROJ_COLS = COL_LY + BRANCH_WIDTH


def _params(*sem):
    return pltpu.CompilerParams(dimension_semantics=sem, vmem_limit_bytes=VMEM_LIMIT)


def _dot(a, b, precision=None):
    return jnp.dot(a, b, preferred_element_type=F32, precision=precision)


def _dot_nt(a, b):
    return lax.dot_general(a, b, (((1,), (1,)), ((), ())), preferred_element_type=F32)


_sigmoid = jax.nn.sigmoid


def _silu(x):
    return x * _sigmoid(x)


def _softplus(x):
    return jnp.maximum(x, 0.0) + jnp.log(1.0 + jnp.exp(-jnp.abs(x)))


def _rms(x, gain):
    return x * lax.rsqrt(jnp.mean(x * x, axis=-1, keepdims=True) + EPS) * gain


def _mod_kernel(c_ref, w_ref, b_ref, o_ref):
    c = c_ref[...]
    o_ref[0] = _dot(_silu(c), w_ref[0], precision=lax.Precision.HIGHEST) + b_ref[0]


def _mods(c, c_ctx, w_mod, b_mod):
    depth = w_mod.shape[0]
    rows = 16
    cc = jnp.zeros((rows, D_MODEL), F32).at[:BATCH].set(c).at[BATCH].set(c_ctx)
    tn = 1536
    out = pl.pallas_call(
        _mod_kernel,
        grid=(depth, N_MOD * D_MODEL // tn),
        in_specs=[pl.BlockSpec((rows, D_MODEL), lambda l, j: (0, 0)),
                  pl.BlockSpec((1, D_MODEL, tn), lambda l, j: (l, 0, j)),
                  pl.BlockSpec((1, 1, tn), lambda l, j: (l, 0, j))],
        out_specs=pl.BlockSpec((1, rows, tn), lambda l, j: (l, 0, j)),
        out_shape=jax.ShapeDtypeStruct((depth, rows, N_MOD * D_MODEL), F32),
        compiler_params=_params("parallel", "parallel"),
        name="mods",
    )(cc, w_mod, b_mod.reshape(depth, 1, N_MOD * D_MODEL))
    return out.reshape(depth, rows, N_MOD, D_MODEL)


def _modulated_norm(x, gain, ml_ref, mc_ref, is_ctx, shift_idx):
    shift = jnp.where(is_ctx, mc_ref[0, shift_idx:shift_idx + 1, :], ml_ref[0, shift_idx:shift_idx + 1, :])
    scale = jnp.where(is_ctx, mc_ref[0, shift_idx + 1:shift_idx + 2, :], ml_ref[0, shift_idx + 1:shift_idx + 2, :])
    return _rms(x, gain) * (1.0 + scale) + shift


def _ctx_rows(tile, tm, width):
    row = tile * tm + lax.broadcasted_iota(jnp.int32, (tm, width), 0)
    return row >= SEQ


def _proj_kernel(x_ref, ml_ref, mc_ref, g_ref, w_ref, wba_ref, alog_ref, dtb_ref, p_ref, gb_ref, h_scr, *, tm):
    i = pl.program_id(1)
    j = pl.program_id(2)

    @pl.when(j == 0)
    def _():
        is_ctx = _ctx_rows(i, tm, D_MODEL)
        h = _modulated_norm(x_ref[0], g_ref[...], ml_ref, mc_ref, is_ctx, 0).astype(BF16)
        h_scr[...] = h
        ba = _dot(h, wba_ref[0])
        lane = lax.broadcasted_iota(jnp.int32, ba.shape, 1)
        log_decay = -jnp.exp(alog_ref[...]) * _softplus(ba + dtb_ref[...])
        gb_ref[0] = jnp.where(lane < 2 * GDN_HEADS, _sigmoid(ba), log_decay)

    p_ref[0] = _dot(h_scr[...], w_ref[0]).astype(BF16)


def _project(xs, mods, gain, w_main, w_ba, a_log, dt_bias, layer):
    def pad_lanes(vals):
        row = jnp.zeros((LANES,), F32).at[2 * GDN_HEADS:4 * GDN_HEADS].set(vals.reshape(-1))
        return row.reshape(1, LANES)

    vec = pl.BlockSpec((1, LANES), lambda b, i, j: (0, 0))
    tm, tn = 1152, 1280
    return pl.pallas_call(
        functools.partial(_proj_kernel, tm=tm),
        grid=(BATCH, TOK // tm, PROJ_COLS // tn),
        in_specs=[pl.BlockSpec((1, tm, D_MODEL), lambda b, i, j: (b, i, 0)),
                  pl.BlockSpec((1, N_MOD, D_MODEL), lambda b, i, j: (b, 0, 0)),
                  pl.BlockSpec((1, N_MOD, D_MODEL), lambda b, i, j: (BATCH, 0, 0)),
                  pl.BlockSpec((1, D_MODEL), lambda b, i, j: (0, 0)),
                  pl.BlockSpec((1, D_MODEL, tn), lambda b, i, j: (layer, 0, j)),
                  pl.BlockSpec((1, D_MODEL, LANES), lambda b, i, j: (layer, 0, 0)), vec, vec],
        out_specs=[pl.BlockSpec((1, tm, tn), lambda b, i, j: (b, i, j)),
                   pl.BlockSpec((1, tm, LANES), lambda b, i, j: (b, i, 0))],
        out_shape=[jax.ShapeDtypeStruct((BATCH, TOK, PROJ_COLS), BF16),
                   jax.ShapeDtypeStruct((BATCH, TOK, LANES), F32)],
        scratch_shapes=[pltpu.VMEM((tm, D_MODEL), BF16)],
        compiler_params=_params("parallel", "parallel", "arbitrary"),
        name="proj",
    )(xs, mods, mods, gain.reshape(1, D_MODEL), w_main, w_ba, pad_lanes(a_log), pad_lanes(dt_bias))


def _conv(x, w):
    n, c = x.shape
    t = lax.broadcasted_iota(jnp.int32, (n, c), 0)
    is_ctx = t >= SEQ
    local = jnp.where(is_ctx, t - SEQ, t)
    seg_len = jnp.where(is_ctx, CTX_LEN, SEQ)
    y = jnp.zeros_like(x)
    for j in range(CONV_WIDTH):
        s = j - CONV_WIDTH // 2
        if s == 0:
            y = y + x * w[j:j + 1, :]
        else:
            shifted = pltpu.roll(x, (-s) % n, 0)
            ok = jnp.logical_and(local + s >= 0, local + s < seg_len)
            y = y + jnp.where(ok, shifted, 0.0) * w[j:j + 1, :]
    return y


def _gdn_conv_kernel(x_ref, w_ref, o_ref):
    j = pl.program_id(1)
    mul = jnp.where(j == 0, float(GDN_HEAD_DIM), jnp.where(j == 1, 1.0, 0.0))
    add = jnp.where(j == 0, float(GDN_HEAD_DIM) * EPS, jnp.where(j == 1, EPS, 1.0))
    for h in range(GDN_HEADS):
        cols = slice(h * GDN_HEAD_DIM, (h + 1) * GDN_HEAD_DIM)
        y = _silu(_conv(x_ref[0, :, cols].astype(F32), w_ref[:, cols]))
        scale = lax.rsqrt(jnp.sum(y * y, axis=-1, keepdims=True) * mul + add)
        o_ref[0, :, cols] = (y * scale).astype(BF16)


def _gdn_conv(p, conv_w):
    first = COL_QKV // BRANCH_WIDTH
    return pl.pallas_call(
        _gdn_conv_kernel,
        grid=(BATCH, 3),
        in_specs=[pl.BlockSpec((1, TOK, BRANCH_WIDTH), lambda b, j: (b, 0, first + j)),
                  pl.BlockSpec((CONV_WIDTH, BRANCH_WIDTH), lambda b, j: (0, j))],
        out_specs=pl.BlockSpec((1, TOK, BRANCH_WIDTH), lambda b, j: (b, 0, j)),
        out_shape=jax.ShapeDtypeStruct((BATCH, TOK, 3 * BRANCH_WIDTH), BF16),
        compiler_params=_params("parallel", "parallel"),
        name="gdn_conv",
    )(p, conv_w)


GDN_QM_ROWS = GDN_CHUNK + GDN_HEAD_DIM
GDN_GL_ROWS = 8
GDN_HEADS_PER_STEP = 2
GDN_PREP_UNROLL = 12


def _gdn_prepare(q_ref, k_ref, v_ref, gb_ref, qm_scr, nn_scr, o_scr, gl_scr, chunks, head, local):
    c = GDN_CHUNK
    cols = slice(local * LANES, (local + 1) * LANES)
    lane = lax.broadcasted_iota(jnp.int32, (c, LANES), 1)
    row = lax.broadcasted_iota(jnp.int32, (c, LANES), 0)
    ii = lax.broadcasted_iota(jnp.int32, (c, c), 0)
    jj = lax.broadcasted_iota(jnp.int32, (c, c), 1)
    eye = (ii == jj).astype(F32)
    masks =((ii >= jj, ii > jj, row > lane), (ii <= jj, ii < jj, row < lane))

    loaded = []
    for chunk in chunks:
        r0 = pl.multiple_of(chunk * c, c)
        k = k_ref[0, pl.ds(r0, c), cols]
        q = q_ref[0, pl.ds(r0, c), cols]
        kq = _dot_nt(jnp.concatenate([k, q], axis=0), k)
        loaded.append((chunk, r0, q, k, kq))

    chains = []
    for chunk, r0, q, k, kq in loaded:
        gb = gb_ref[0, pl.ds(r0, c), :]
        for d in range(2):
            col = head + d * GDN_HEADS
            beta = jnp.sum(jnp.where(lane == col, gb, 0.0), axis=-1, keepdims=True)
            g = jnp.sum(jnp.where(lane == col + 2 * GDN_HEADS, gb, 0.0), axis=-1, keepdims=True)
            incl, strict, strict_wide = masks[d]
            rhs = jnp.where(lane >= c, g, jnp.where(strict_wide, g, 0.0))
            mask = incl.astype(BF16)
            hi = rhs.astype(BF16)
            rest = rhs - hi.astype(F32)
            mid = rest.astype(BF16)
            low = (rest - mid.astype(F32)).astype(BF16)
            e = _dot(mask, hi) + _dot(mask, mid) + _dot(mask, low)
            chains.append(dict(chunk=chunk, r0=r0, d=d, q=q, k=k, kq=kq, beta=beta, e=e))

    for ch in chains:
        incl, strict, _ = masks[ch["d"]]
        e = ch["e"]
        decay = jnp.where(incl, jnp.exp(e[:, :c]), 0.0)
        gc = e[:, c:c + 1]
        last = 0 if ch["d"] == 1 else c - 1
        gc_last = e[last:last + 1, c:c + 1]
        ch.update(decay=decay, gc=gc, gc_last=gc_last, egc=jnp.exp(gc))
        ch["a"] = jnp.where(strict, ch["beta"] * ch["kq"][:c] * decay, 0.0)
        ch["t"] = eye
    s = 1
    while s < c:
        pair = jnp.logical_and((ii // (2 * s)) == (jj // (2 * s)), (ii // s) != (jj // s))
        for ch in chains:
            ch["a_off"] = jnp.where(pair, ch["a"], 0.0)
        if s == 1:
            for ch in chains:
                ch["t"] = eye - ch["a_off"]
        else:
            for ch in chains:
                ch["m"] = _dot(ch["t"].astype(BF16), ch["a_off"].astype(BF16))
            for ch in chains:
                ch["t"] = ch["t"] - _dot(ch["m"].astype(BF16), ch["t"].astype(BF16))
        s *= 2
    for ch in chains:
        r0, beta, egc = ch["r0"], ch["beta"], ch["egc"]
        kf = ch["k"].astype(F32)
        vf = v_ref[0, pl.ds(r0, c), cols].astype(F32)
        rhs2 = jnp.concatenate([vf * beta, kf * (beta * egc)], axis=1).astype(BF16)
        ch["uw"] = _dot(ch["t"].astype(BF16), rhs2).astype(BF16)
        ch["k_dec_t"] = (kf * jnp.exp(ch["gc_last"] - ch["gc"])).T.astype(BF16)
    for ch in chains:
        incl = masks[ch["d"]][0]
        qk = jnp.where(incl, ch["kq"][c:] * ch["decay"], 0.0).astype(BF16)
        ch["nm"] = _dot(ch["k_dec_t"], ch["uw"])
        ch["ow"] = _dot(qk, ch["uw"])
    for ch in chains:
        chunk, r0, nm, ow = ch["chunk"], ch["r0"], ch["nm"], ch["ow"]
        s = 2 * local + ch["d"]
        q0 = pl.multiple_of(chunk * GDN_QM_ROWS, 16)
        qm_scr[s, pl.ds(q0, c), :] = (ch["q"].astype(F32) * ch["egc"] - ow[:, GDN_HEAD_DIM:]).astype(BF16)
        qm_scr[s, pl.ds(q0 + c, GDN_HEAD_DIM), :] = nm[:, GDN_HEAD_DIM:].astype(BF16)
        nn_scr[s, pl.ds(pl.multiple_of(chunk * GDN_HEAD_DIM, GDN_HEAD_DIM), GDN_HEAD_DIM), :] = nm[:, :GDN_HEAD_DIM]
        o_scr[s, pl.ds(r0, c), :] = ow[:, :GDN_HEAD_DIM]
        gl_scr[s, pl.ds(pl.multiple_of(chunk * GDN_GL_ROWS, GDN_GL_ROWS), GDN_GL_ROWS), :] = jnp.broadcast_to(
            jnp.exp(ch["gc_last"]), (GDN_GL_ROWS, LANES))


def _gdn_advance(qm_scr, nn_scr, o_scr, gl_scr, d, chunk, state):
    c = GDN_CHUNK
    qm = qm_scr[d, pl.ds(pl.multiple_of(chunk * GDN_QM_ROWS, 16), GDN_QM_ROWS), :]
    r = _dot(qm, state.astype(BF16))
    rows = pl.ds(pl.multiple_of(chunk * c, c), c)
    o_scr[d, rows, :] = o_scr[d, rows, :] + r[:c]
    gl = gl_scr[d, pl.ds(pl.multiple_of(chunk * GDN_GL_ROWS, GDN_GL_ROWS), 1), :]
    n = nn_scr[d, pl.ds(pl.multiple_of(chunk * GDN_HEAD_DIM, GDN_HEAD_DIM), GDN_HEAD_DIM), :]
    return state * gl - r[c:] + n


def _gdn_kernel(q_ref, k_ref, v_ref, z_ref, gb_ref, gain_ref, o_ref, qm_scr, nn_scr, o_scr, gl_scr):
    first_head = pl.program_id(1) * GDN_HEADS_PER_STEP
    c = GDN_CHUNK
    n_lat, n_ctx = SEQ // c, CTX_LEN // c
    n_chunks = n_lat + n_ctx

    for local in range(GDN_HEADS_PER_STEP):
        def prepare(i, _, local=local):
            chunks = [i * GDN_PREP_UNROLL + j for j in range(GDN_PREP_UNROLL)]
            _gdn_prepare(q_ref, k_ref, v_ref, gb_ref, qm_scr, nn_scr, o_scr, gl_scr, chunks, first_head + local, local)
            return 0

        lax.fori_loop(0, n_chunks // GDN_PREP_UNROLL, prepare, 0)

    def advance(i, states):
        cf = jnp.where(i < n_ctx, n_lat + i, i - n_ctx)
        cb = n_chunks - 1 - i
        return tuple(_gdn_advance(qm_scr, nn_scr, o_scr, gl_scr, s, cb if s % 2 else cf, state)
                     for s, state in enumerate(states))

    zero = jnp.zeros((GDN_HEAD_DIM, GDN_HEAD_DIM), F32)
    lax.fori_loop(0, n_chunks, advance, (zero,) * (2 * GDN_HEADS_PER_STEP), unroll=4)
    for local in range(GDN_HEADS_PER_STEP):
        cols = slice(local * LANES, (local + 1) * LANES)
        o = o_scr[2 * local] + o_scr[2 * local + 1]
        o_ref[0, :, cols] = (_rms(o, gain_ref[...]) * _silu(z_ref[0, :, cols].astype(F32))).astype(BF16)


def _gdn(qkv, p, gates, out_gain):
    n_steps = GDN_HEADS // GDN_HEADS_PER_STEP
    n_chunks = TOK // GDN_CHUNK
    width = GDN_HEADS_PER_STEP * LANES
    zblk = COL_Z // width
    n_chain = 2 * GDN_HEADS_PER_STEP
    blk = lambda off: pl.BlockSpec((1, TOK, width), lambda b, h: (b, 0, off + h))
    vec = pl.BlockSpec((1, LANES), lambda b, h: (0, 0))
    return pl.pallas_call(
        _gdn_kernel,
        grid=(BATCH, n_steps),
        in_specs=[blk(0), blk(n_steps), blk(2 * n_steps), blk(zblk),
                  pl.BlockSpec((1, TOK, LANES), lambda b, h: (b, 0, 0)), vec],
        out_specs=pl.BlockSpec((1, TOK, width), lambda b, h: (b, 0, h)),
        out_shape=jax.ShapeDtypeStruct((BATCH, TOK, BRANCH_WIDTH), BF16),
        scratch_shapes=[pltpu.VMEM((n_chain, n_chunks * GDN_QM_ROWS, LANES), BF16),
                        pltpu.VMEM((n_chain, n_chunks * GDN_HEAD_DIM, LANES), F32),
                        pltpu.VMEM((n_chain, TOK, LANES), F32),
                        pltpu.VMEM((n_chain, n_chunks * GDN_GL_ROWS, LANES), F32)],
        compiler_params=_params("parallel", "parallel"),
        name="gdn",
    )(qkv, qkv, qkv, p, gates, out_gain.reshape(1, LANES))


def _rms_halves(x, gain):
    lane = lax.broadcasted_iota(jnp.int32, x.shape, 1)
    lo = lane < DIFF_HEAD_DIM
    x2 = x * x
    s_lo = jnp.sum(jnp.where(lo, x2, 0.0), axis=-1, keepdims=True)
    s_hi = jnp.sum(jnp.where(lo, 0.0, x2), axis=-1, keepdims=True)
    ms = jnp.where(lo, s_lo, s_hi) * (1.0 / DIFF_HEAD_DIM)
    return x * lax.rsqrt(ms + EPS) * gain


def _rope(x, cos, sin):
    lane = lax.broadcasted_iota(jnp.int32, x.shape, 1)
    first = (lane & ROPE_PAIRS) == 0
    partner = jnp.where(first, -pltpu.roll(x, LANES - ROPE_PAIRS, 1), pltpu.roll(x, ROPE_PAIRS, 1))
    return x * cos + partner * sin


ATTN_Q_BLOCK = 1024
ATTN_GROUP_ROWS = 128


def _attn_kernel(q_ref, k_ref, v_ref, cosk_ref, sink_ref, cosq_ref, sinq_ref, qg_ref, kg_ref, lv_ref, og_ref,
                 o_ref, kn_scr, *, ctx_block, lam_init):
    qi = pl.program_id(2)

    @pl.when(qi == 0)
    def _():
        kn = _rope(_rms_halves(k_ref[0].astype(F32), kg_ref[...]), cosk_ref[...], sink_ref[...])
        kn_scr[...] = kn.astype(BF16)

    lv = lv_ref[...]
    lam = (jnp.exp(jnp.sum(lv[0:1] * lv[1:2], axis=-1, keepdims=True))
           - jnp.exp(jnp.sum(lv[2:3] * lv[3:4], axis=-1, keepdims=True)) + lam_init)
    def attend(n_rows, kn, v):
        q = _rope(_rms_halves(q_ref[0, :n_rows, :].astype(F32), qg_ref[...]), cosq_ref[:n_rows, :], sinq_ref[:n_rows, :])
        q = q * (DIFF_HEAD_DIM ** -0.5 * math.log2(math.e))
        lane = lax.broadcasted_iota(jnp.int32, q.shape, 1)
        lo = lane < DIFF_HEAD_DIM
        q1 = jnp.where(lo, q, 0.0).astype(BF16)
        q2 = jnp.where(lo, 0.0, q).astype(BF16)

        def probs(s):
            p = jnp.exp2(s - jnp.max(s, axis=-1, keepdims=True))
            return p.astype(BF16), jnp.sum(p, axis=-1, keepdims=True)
        rows = ATTN_GROUP_ROWS
        starts = list(range(0, n_rows, rows))
        score = lambda r: (_dot_nt(q1[r:r + rows], kn), _dot_nt(q2[r:r + rows], kn))
        scores = [score(starts[0])]
        for g in range(len(starts)):
            s1, s2 = scores[g]
            p1, l1 = probs(s1)
            p2, l2 = probs(s2)
            if g + 1 < len(starts):
                scores.append(score(starts[g + 1]))
            o = _dot(p1, v) * (1.0 / l1) - _dot(p2, v) * (lam / l2)
            o_ref[0, g * rows:(g + 1) * rows, :] = (_rms(o, og_ref[...]) * (1.0 - lam_init)).astype(BF16)

    if ctx_block is None:
        attend(q_ref.shape[1], kn_scr[...], v_ref[0])
    else:
        @pl.when(qi == ctx_block)
        def _():
            attend(CTX_LEN, kn_scr[SEQ:, :], v_ref[0, SEQ:, :])

        @pl.when(qi != ctx_block)
        def _():
            attend(q_ref.shape[1], kn_scr[...], v_ref[0])


def _rope_tables():
    n_rows = SEQ // GRID_W
    row_id = jnp.broadcast_to(jnp.arange(n_rows, dtype=F32)[:, None], (n_rows, GRID_W)).reshape(-1)
    col_id = jnp.broadcast_to(jnp.arange(GRID_W, dtype=F32)[None, :], (n_rows, GRID_W)).reshape(-1)
    inv_freq = jnp.power(ROPE_BASE, -jnp.arange(ROPE_PAIRS, dtype=F32) / ROPE_PAIRS)
    row_ang = row_id[:, None] * inv_freq
    col_ang = col_id[:, None] * inv_freq
    ang = jnp.concatenate([row_ang, row_ang, col_ang, col_ang], axis=-1)
    ang = jnp.concatenate([ang, ang], axis=-1)
    pad = ((0, CTX_LEN), (0, 0))
    return jnp.pad(jnp.cos(ang), pad, constant_values=1.0), jnp.pad(jnp.sin(ang), pad)


def _diff_attn(p, cos, sin, q_gain, k_gain, lam_vecs, out_gain, lam_init, with_ctx):
    tq = ATTN_Q_BLOCK
    n_rows = TOK if with_ctx else SEQ
    nq = pl.cdiv(n_rows, tq)
    nh = DIFF_HEADS
    qb, kb, vb = COL_DQ // LANES, COL_DK // LANES, COL_DV // LANES
    full = lambda off: pl.BlockSpec((1, TOK, LANES), lambda b, h, i: (b, 0, off + h))
    tab_full = pl.BlockSpec((TOK, LANES), lambda b, h, i: (0, 0))
    tab_q = pl.BlockSpec((tq, LANES), lambda b, h, i: (i, 0))
    vec = pl.BlockSpec((1, LANES), lambda b, h, i: (0, 0))
    tile2 = lambda g: jnp.concatenate([g, g]).reshape(1, LANES)
    return pl.pallas_call(
        functools.partial(_attn_kernel, ctx_block=SEQ // tq if with_ctx else None, lam_init=lam_init),
        grid=(BATCH, nh, nq),
        in_specs=[pl.BlockSpec((1, tq, LANES), lambda b, h, i: (b, i, qb + h)), full(kb), full(vb),
                  tab_full, tab_full, tab_q, tab_q, vec, vec,
                  pl.BlockSpec((4, DIFF_HEAD_DIM), lambda b, h, i: (0, 0)), vec],
        out_specs=pl.BlockSpec((1, tq, LANES), lambda b, h, i: (b, i, h)),
        out_shape=jax.ShapeDtypeStruct((BATCH, n_rows, BRANCH_WIDTH), BF16),
        scratch_shapes=[pltpu.VMEM((TOK, LANES), BF16)],
        compiler_params=_params("parallel", "parallel", "arbitrary"),
        name="diff_attn",
    )(p, p, p, cos, sin, cos, sin, tile2(q_gain), tile2(k_gain), lam_vecs, out_gain.reshape(1, LANES))


LRU_SLAB = 256
LRU_ROWS = 768
LRU_SCAN_BLOCK = 8


def _lru_kernel(x_ref, y_ref, cw_ref, cb_ref, wg_ref, bg_ref, lam_ref, o_ref, xc_scr, af_scr, bf_scr, ab_scr, bb_scr):
    w = LRU_SLAB
    blk = LRU_SCAN_BLOCK
    for c0 in range(0, w, LANES):
        cols = slice(c0, c0 + LANES)
        xc_scr[:, cols] = _conv(x_ref[0, :, cols].astype(F32), cw_ref[:, cols]) + cb_ref[:, cols]
    sp = _softplus(-lam_ref[0])
    sub = lax.broadcasted_iota(jnp.int32, (LRU_ROWS // blk, blk, w), 1)

    def gates(i, _):
        r0 = pl.multiple_of(i * LRU_ROWS, LRU_ROWS)
        xc = xc_scr[pl.ds(r0, LRU_ROWS), :]
        pre = _dot(xc.astype(BF16), wg_ref[0]) + bg_ref[0]
        for d, (a_scr, b_scr) in enumerate(((af_scr, bf_scr), (ab_scr, bb_scr))):
            r = _sigmoid(pre[:, (2 * d) * w:(2 * d + 1) * w])
            gi = _sigmoid(pre[:, (2 * d + 1) * w:(2 * d + 2) * w])
            log_a = -LRU_C * r * sp[d:d + 1]
            a = jnp.exp(log_a)
            b = jnp.sqrt(1.0 - a * a) * gi * xc
            a = a.reshape(LRU_ROWS // blk, blk, w)
            b = b.reshape(LRU_ROWS // blk, blk, w)
            shift = 1
            while shift < blk:
                if d == 0:
                    ok, roll_by = sub >= shift, shift
                else:
                    ok, roll_by = sub < blk - shift, blk - shift
                b = jnp.where(ok, a * pltpu.roll(b, roll_by, 1) + b, b)
                a = jnp.where(ok, a * pltpu.roll(a, roll_by, 1), a)
                shift *= 2
            a_scr[pl.ds(r0, LRU_ROWS), :] = a.reshape(LRU_ROWS, w)
            b_scr[pl.ds(r0, LRU_ROWS), :] = b.reshape(LRU_ROWS, w)
        return 0

    lax.fori_loop(0, TOK // LRU_ROWS, gates, 0)

    n_blk, n_lat_blk, n_ctx_blk = TOK // blk, SEQ // blk, CTX_LEN // blk

    def step(s, carry):
        h_f, h_b = carry
        rows_f = pl.ds(pl.multiple_of(jnp.where(s < n_ctx_blk, n_lat_blk + s, s - n_ctx_blk) * blk, blk), blk)
        rows_b = pl.ds(pl.multiple_of((n_blk - 1 - s) * blk, blk), blk)
        hf = af_scr[rows_f, :] * h_f + bf_scr[rows_f, :]
        bf_scr[rows_f, :] = hf
        hb = ab_scr[rows_b, :] * h_b + bb_scr[rows_b, :]
        bb_scr[rows_b, :] = hb
        return hf[blk - 1:blk, :], hb[0:1, :]

    zero = jnp.zeros((1, w), F32)
    lax.fori_loop(0, n_blk, step, (zero, zero), unroll=4)
    h = bf_scr[...] + bb_scr[...]
    o_ref[0] = (h * jax.nn.gelu(y_ref[0].astype(F32))).astype(BF16)


def _lru_gate_weights(w_gate, b_gate):
    n_slab = BRANCH_WIDTH // LRU_SLAB
    per = LRU_SLAB // LRU_BLOCK_DIM
    wg = w_gate.reshape(2, 2, n_slab, per, LRU_BLOCK_DIM, LRU_BLOCK_DIM)
    eye = jnp.eye(per, dtype=w_gate.dtype)
    dense = jnp.einsum('dgsnjk,nm->snjdgmk', wg, eye)
    dense = dense.reshape(n_slab, LRU_SLAB, 4 * LRU_SLAB)
    bg = b_gate.reshape(2, 2, n_slab, LRU_SLAB).transpose(2, 0, 1, 3).reshape(n_slab, 1, 4 * LRU_SLAB)
    return dense.astype(BF16), bg


def _lru(p, conv_w, conv_b, w_gate, b_gate, lam):
    n_slab = BRANCH_WIDTH // LRU_SLAB
    wg, bg = _lru_gate_weights(w_gate, b_gate)
    lam_s = lam.reshape(2, n_slab, LRU_SLAB).transpose(1, 0, 2)
    xb, yb = COL_LX // LRU_SLAB, COL_LY // LRU_SLAB
    return pl.pallas_call(
        _lru_kernel,
        grid=(BATCH, n_slab),
        in_specs=[pl.BlockSpec((1, TOK, LRU_SLAB), lambda b, s: (b, 0, xb + s)),
                  pl.BlockSpec((1, TOK, LRU_SLAB), lambda b, s: (b, 0, yb + s)),
                  pl.BlockSpec((CONV_WIDTH, LRU_SLAB), lambda b, s: (0, s)),
                  pl.BlockSpec((1, LRU_SLAB), lambda b, s: (0, s)),
                  pl.BlockSpec((1, LRU_SLAB, 4 * LRU_SLAB), lambda b, s: (s, 0, 0)),
                  pl.BlockSpec((1, 1, 4 * LRU_SLAB), lambda b, s: (s, 0, 0)),
                  pl.BlockSpec((1, 2, LRU_SLAB), lambda b, s: (s, 0, 0))],
        out_specs=pl.BlockSpec((1, TOK, LRU_SLAB), lambda b, s: (b, 0, s)),
        out_shape=jax.ShapeDtypeStruct((BATCH, TOK, BRANCH_WIDTH), BF16),
        scratch_shapes=[pltpu.VMEM((TOK, LRU_SLAB), F32)] * 5,
        compiler_params=_params("parallel", "parallel"),
        name="lru",
    )(p, p, conv_w, conv_b.reshape(1, BRANCH_WIDTH), wg, bg, lam_s)


def _route(logits):
    lane = lax.broadcasted_iota(jnp.int32, logits.shape, 1)
    lane_f = lane.astype(F32)
    far = float(LANES)
    lg = jnp.where(lane < N_EXPERTS, logits, -jnp.inf)
    ex = jnp.exp(lg - jnp.max(lg, axis=-1, keepdims=True))
    probs = ex / jnp.sum(ex, axis=-1, keepdims=True)
    per_group = N_EXPERTS // N_GROUPS
    grp = lane // per_group

    def top2(vals):
        m1 = jnp.max(vals, axis=-1, keepdims=True)
        i1 = jnp.min(jnp.where(vals == m1, lane_f, far), axis=-1, keepdims=True)
        rest = jnp.where(lane_f == i1, -2.0, vals)
        m2 = jnp.max(rest, axis=-1, keepdims=True)
        i2 = jnp.min(jnp.where(rest == m2, lane_f, far), axis=-1, keepdims=True)
        return m1, i1, m2, i2

    best = jnp.zeros(logits.shape[:1] + (1,), jnp.int32)
    best_score = None
    for g in range(N_GROUPS):
        m1, _, m2, _ = top2(jnp.where(grp == g, probs, -1.0))
        score = m1 + m2
        if best_score is None:
            best_score = score
        else:
            better = score > best_score
            best = jnp.where(better, g, best)
            best_score = jnp.where(better, score, best_score)
    m1, i1, m2, i2 = top2(jnp.where(grp == best, probs, -1.0))
    den = m1 + m2
    return i1, i2, m1 / den, m2 / den


ROUTE_E, ROUTE_W, ROUTE_RANK = 0, 2, 4


def _pack_bf16_pairs(x):
    n = x.shape[1] // 2
    xb = x.astype(BF16).astype(F32)
    lo = pltpu.bitcast(xb[:, :n], jnp.uint32)
    hi = pltpu.bitcast(xb[:, n:], jnp.uint32)
    return (lo >> 16) | (hi & jnp.uint32(0xFFFF0000))


def _unpack_bf16_pairs(p):
    lo = pltpu.bitcast(p << 16, F32)
    hi = pltpu.bitcast(p & jnp.uint32(0xFFFF0000), F32)
    return jnp.concatenate([lo, hi], axis=1)


def _merge_kernel(ya_ref, yb_ref, yc_ref, gates_ref, x_ref, ml_ref, mc_ref, g2_ref, wbr_ref, wout_ref, wr_ref,
                  br_ref, xo_ref, h2_ref, route_ref, cnt_ref, cnt_scr, tri_scr, *, tm):
    i = pl.program_id(1)

    @pl.when(jnp.logical_and(pl.program_id(0) == 0, i == 0))
    def _():
        cnt_scr[...] = jnp.zeros_like(cnt_scr)
        earlier = (lax.broadcasted_iota(jnp.int32, (tm, tm), 0) > lax.broadcasted_iota(jnp.int32, (tm, tm), 1))
        tri_scr[...] = jnp.where(earlier, 1.0, 0.0).astype(BF16)

    acc = None
    for n, y_ref in enumerate((ya_ref, yb_ref, yc_ref)):
        yn = _dot(y_ref[0], wbr_ref[n])
        gate = _sigmoid(gates_ref[0, :, n * D_MODEL:(n + 1) * D_MODEL].astype(F32))
        acc = gate * yn if acc is None else acc + gate * yn
    out = _dot(acc.astype(BF16), wout_ref[...])
    is_ctx = _ctx_rows(i, tm, D_MODEL)
    xn = x_ref[0] + jnp.where(is_ctx, mc_ref[0, 2:3, :], ml_ref[0, 2:3, :]) * out
    xo_ref[0] = xn
    h2 = _modulated_norm(xn, g2_ref[...], ml_ref, mc_ref, is_ctx, 3)
    h2_ref[0] = _pack_bf16_pairs(h2)
    i1, i2, w1, w2 = _route(_dot(h2.astype(BF16), wr_ref[...]) + br_ref[...])
    lane = lax.broadcasted_iota(jnp.int32, (tm, LANES), 1)
    lane_f = lane.astype(F32)
    chosen = jnp.where(jnp.logical_or(lane_f == i1, lane_f == i2), 1.0, 0.0)
    before = _dot(tri_scr[...], chosen.astype(BF16)) + cnt_scr[...]
    rank1 = jnp.sum(jnp.where(lane_f == i1, before, 0.0), axis=-1, keepdims=True)
    rank2 = jnp.sum(jnp.where(lane_f == i2, before, 0.0), axis=-1, keepdims=True)
    cnt_scr[...] += jnp.sum(chosen, axis=0, keepdims=True)
    cnt_ref[...] = cnt_scr[...]
    record = jnp.zeros((tm, LANES), F32)
    for pos, val in enumerate((i1, i2, w1, w2, rank1, rank2)):
        record = jnp.where(lane == pos, val, record)
    route_ref[0] = record


def _merge(ya, yb, yc, p, xs, mods, gain2, w_branch, w_out, w_router, b_router, rows, tm):
    wr = jnp.zeros((D_MODEL, LANES), BF16).at[:, :N_EXPERTS].set(w_router.astype(BF16))
    br = jnp.zeros((1, LANES), F32).at[0, :N_EXPERTS].set(b_router)
    tile = lambda w: pl.BlockSpec((1, tm, w), lambda b, i: (b, i, 0))
    const = lambda shape: pl.BlockSpec(shape, lambda b, i: (0,) * len(shape))
    return pl.pallas_call(
        functools.partial(_merge_kernel, tm=tm),
        grid=(BATCH, rows // tm),
        in_specs=[tile(BRANCH_WIDTH), tile(BRANCH_WIDTH), tile(BRANCH_WIDTH), tile(3 * D_MODEL), tile(D_MODEL),
                  pl.BlockSpec((1, N_MOD, D_MODEL), lambda b, i: (b, 0, 0)),
                  pl.BlockSpec((1, N_MOD, D_MODEL), lambda b, i: (BATCH, 0, 0)),
                  const((1, D_MODEL)), const((3, BRANCH_WIDTH, D_MODEL)), const((D_MODEL, D_MODEL)),
                  const((D_MODEL, LANES)), const((1, LANES))],
        out_specs=[tile(D_MODEL), tile(D_MODEL // 2), tile(LANES), const((1, LANES))],
        out_shape=[jax.ShapeDtypeStruct((BATCH, rows, D_MODEL), F32),
                   jax.ShapeDtypeStruct((BATCH, rows, D_MODEL // 2), jnp.uint32),
                   jax.ShapeDtypeStruct((BATCH, rows, LANES), F32),
                   jax.ShapeDtypeStruct((1, LANES), F32)],
        scratch_shapes=[pltpu.VMEM((1, LANES), F32), pltpu.VMEM((tm, tm), BF16)],
        compiler_params=_params("arbitrary", "arbitrary"),
        name="merge",
    )(ya, yb, yc, p, xs, mods, mods, gain2.reshape(1, D_MODEL), w_branch, w_out, wr, br)


MOE_TILE = 512
SC_GATHER_ROWS = 64


def _sc_gather(table, idx):
    info = plsc.get_sparse_core_info()
    n_workers = info.num_cores * info.num_subcores
    n_rows, width = idx.shape[0], table.shape[1]
    per_worker = n_rows // n_workers
    assert per_worker * n_workers == n_rows and per_worker % SC_GATHER_ROWS == 0
    mesh = plsc.VectorSubcoreMesh(core_axis_name="c", subcore_axis_name="s")

    @functools.partial(
        pl.kernel, mesh=mesh, out_type=jax.ShapeDtypeStruct((n_rows, width), table.dtype),
        scratch_types=[pltpu.VMEM((SC_GATHER_ROWS,), jnp.int32),
                       pltpu.VMEM((SC_GATHER_ROWS, width), table.dtype),
                       pltpu.SemaphoreType.DMA],
        name="sc_gather")
    def gather(table_hbm, idx_hbm, out_hbm, idx_v, rows_v, sem):
        worker = lax.axis_index("s") * info.num_cores + lax.axis_index("c")
        base = worker * per_worker

        @pl.loop(0, per_worker // SC_GATHER_ROWS)
        def _(j):
            off = base + j * SC_GATHER_ROWS
            pltpu.sync_copy(idx_hbm.at[pl.ds(off, SC_GATHER_ROWS)], idx_v)
            pltpu.async_copy(table_hbm.at[idx_v], rows_v, sem).wait()
            pltpu.sync_copy(rows_v, out_hbm.at[pl.ds(off, SC_GATHER_ROWS)])

    return gather(table, idx)


SC_SCATTER_ROWS = 128


def _sc_scatter(table, dest, n_out):
    info = plsc.get_sparse_core_info()
    n_workers = info.num_cores * info.num_subcores
    n_tok, width = table.shape
    n_assign = dest.shape[0]
    per_worker = n_assign // n_workers
    assert per_worker * n_workers == n_assign and per_worker % SC_SCATTER_ROWS == 0 and n_tok % SC_SCATTER_ROWS == 0
    mesh = plsc.VectorSubcoreMesh(core_axis_name="c", subcore_axis_name="s")

    @functools.partial(
        pl.kernel, mesh=mesh, out_type=jax.ShapeDtypeStruct((n_out, width), table.dtype),
        scratch_types=[pltpu.VMEM((SC_SCATTER_ROWS,), jnp.int32),
                       pltpu.VMEM((SC_SCATTER_ROWS, width), table.dtype),
                       pltpu.SemaphoreType.DMA],
        name="sc_scatter")
    def scatter(table_hbm, dest_hbm, out_hbm, idx_v, rows_v, sem):
        worker = lax.axis_index("s") * info.num_cores + lax.axis_index("c")
        base = worker * per_worker

        @pl.loop(0, per_worker // SC_SCATTER_ROWS)
        def _(j):
            off = base + j * SC_SCATTER_ROWS
            pltpu.sync_copy(dest_hbm.at[pl.ds(off, SC_SCATTER_ROWS)], idx_v)
            pltpu.sync_copy(table_hbm.at[pl.ds(lax.rem(off, n_tok), SC_SCATTER_ROWS)], rows_v)
            pltpu.async_copy(rows_v, out_hbm.at[idx_v], sem).wait()

    return scatter(table, dest)


def _experts_kernel(tile_expert_ref, n_valid_ref, fresh_ref, slot_ref, next_ref, x_ref, wgu_hbm, wd_hbm, o_ref,
                    wgu_scr, wd_scr, wgu_buf, wd_buf, sem, *, layer):
    j = pl.program_id(0)
    valid = j < n_valid_ref[0]

    def fetch(expert, slot):
        return (pltpu.make_async_copy(wgu_hbm.at[layer, expert], wgu_buf.at[slot], sem.at[0, slot]),
                pltpu.make_async_copy(wd_hbm.at[layer, expert], wd_buf.at[slot], sem.at[1, slot]))

    @pl.when(fresh_ref[j] == 1)
    def _():
        slot = slot_ref[j]

        @pl.when(j == 0)
        def _():
            for cp in fetch(tile_expert_ref[0], 0):
                cp.start()

        for cp in fetch(tile_expert_ref[j], slot):
            cp.wait()

        @pl.when(next_ref[j] >= 0)
        def _():
            for cp in fetch(next_ref[j], 1 - slot):
                cp.start()

        wgu_scr[...] = wgu_buf[slot].astype(BF16)
        wd_scr[...] = wd_buf[slot].astype(BF16)

    @pl.when(valid)
    def _():
        x = _unpack_bf16_pairs(x_ref[...]).astype(BF16)
        gu = _dot(x, wgu_scr[...])
        act = (_silu(gu[:, :EXPERT_FF]) * gu[:, EXPERT_FF:]).astype(BF16)
        o_ref[...] = _pack_bf16_pairs(_dot(act, wd_scr[...]))

    @pl.when(jnp.logical_not(valid))
    def _():
        o_ref[...] = jnp.zeros_like(o_ref)


def _experts(x_sorted, tile_expert, n_valid, w_gate_up, w_down, layer):
    n_tiles = x_sorted.shape[0] // MOE_TILE
    half = D_MODEL // 2
    idx = jnp.arange(n_tiles, dtype=jnp.int32)
    valid = idx < n_valid[0]
    changed = jnp.concatenate([jnp.ones((1,), bool), tile_expert[1:] != tile_expert[:-1]])
    fresh = jnp.logical_and(valid, changed)
    slot = ((jnp.cumsum(fresh.astype(jnp.int32)) - 1) % 2).astype(jnp.int32)
    later_fresh = jnp.logical_and(fresh[None, :], idx[None, :] > idx[:, None])
    next_tile = jnp.min(jnp.where(later_fresh, idx[None, :], n_tiles), axis=1)
    next_expert = jnp.where(next_tile < n_tiles, tile_expert[jnp.minimum(next_tile, n_tiles - 1)], -1).astype(jnp.int32)
    tile = lambda j, *_: (j, 0)
    return pl.pallas_call(
        functools.partial(_experts_kernel, layer=layer),
        grid_spec=pltpu.PrefetchScalarGridSpec(
            num_scalar_prefetch=5, grid=(n_tiles,),
            in_specs=[pl.BlockSpec((MOE_TILE, half), tile),
                      pl.BlockSpec(memory_space=pl.ANY), pl.BlockSpec(memory_space=pl.ANY)],
            out_specs=pl.BlockSpec((MOE_TILE, half), tile),
            scratch_shapes=[pltpu.VMEM((D_MODEL, 2 * EXPERT_FF), BF16), pltpu.VMEM((EXPERT_FF, D_MODEL), BF16),
                            pltpu.VMEM((2, D_MODEL, 2 * EXPERT_FF), F32), pltpu.VMEM((2, EXPERT_FF, D_MODEL), F32),
                            pltpu.SemaphoreType.DMA((2, 2))]),
        out_shape=jax.ShapeDtypeStruct((x_sorted.shape[0], half), jnp.uint32),
        compiler_params=_params("arbitrary"),
        name="experts",
    )(tile_expert, n_valid, fresh.astype(jnp.int32), slot, next_expert, x_sorted, w_gate_up, w_down)


def _combine_kernel(y1_ref, y2_ref, route_ref, x_ref, ml_ref, mc_ref, o_ref, *, tm):
    i = pl.program_id(1)
    route = route_ref[0]
    w1 = route[:, ROUTE_W:ROUTE_W + 1]
    w2 = route[:, ROUTE_W + 1:ROUTE_W + 2]
    moe = w1 * _unpack_bf16_pairs(y1_ref[0, 0]) + w2 * _unpack_bf16_pairs(y2_ref[0, 0])
    is_ctx = _ctx_rows(i, tm, D_MODEL)
    o_ref[0] = x_ref[0] + jnp.where(is_ctx, mc_ref[0, 5:6, :], ml_ref[0, 5:6, :]) * moe


def _combine(y_pairs, route, xs, mods, rows, tm):
    half = D_MODEL // 2
    tile = lambda w: pl.BlockSpec((1, tm, w), lambda b, i: (b, i, 0))
    slot = lambda s: pl.BlockSpec((1, 1, tm, half), lambda b, i: (s, b, i, 0))
    return pl.pallas_call(
        functools.partial(_combine_kernel, tm=tm),
        grid=(BATCH, rows // tm),
        in_specs=[slot(0), slot(1), tile(LANES), tile(D_MODEL),
                  pl.BlockSpec((1, N_MOD, D_MODEL), lambda b, i: (b, 0, 0)),
                  pl.BlockSpec((1, N_MOD, D_MODEL), lambda b, i: (BATCH, 0, 0))],
        out_specs=tile(D_MODEL),
        out_shape=jax.ShapeDtypeStruct((BATCH, rows, D_MODEL), F32),
        compiler_params=_params("parallel", "parallel"),
        name="combine",
    )(y_pairs, y_pairs, route, xs, mods, mods)


def _moe(h2, route, counts, xs, mods, w_gate_up, w_down, layer, rows, tm):
    n_tok = BATCH * rows
    half = D_MODEL // 2
    n_sorted = 2 * n_tok + N_EXPERTS * MOE_TILE
    n_tiles = n_sorted // MOE_TILE
    rec = route.reshape(n_tok, LANES)
    expert = rec[:, ROUTE_E:ROUTE_E + 2].astype(jnp.int32)
    rank = rec[:, ROUTE_RANK:ROUTE_RANK + 2].astype(jnp.int32)
    count = counts[0, :N_EXPERTS].astype(jnp.int32)
    padded = (count + MOE_TILE - 1) // MOE_TILE * MOE_TILE
    end = jnp.cumsum(padded)
    start = end - padded
    first = jnp.sum(jnp.where(expert[:, :1] == jnp.arange(N_EXPERTS), start, 0), axis=1)
    second = jnp.sum(jnp.where(expert[:, 1:] == jnp.arange(N_EXPERTS), start, 0), axis=1)
    dest = jnp.concatenate([first + rank[:, 0], second + rank[:, 1]])
    tile_start = jnp.arange(n_tiles, dtype=jnp.int32) * MOE_TILE
    tile_expert = jnp.minimum(jnp.sum(tile_start[:, None] >= end[None, :], axis=1), N_EXPERTS - 1).astype(jnp.int32)
    n_valid = (end[-1:] // MOE_TILE).astype(jnp.int32)
    x_sorted = _sc_scatter(h2.reshape(n_tok, half), dest, n_sorted)
    y_sorted = _experts(x_sorted, tile_expert, n_valid, w_gate_up, w_down, layer)
    y_pairs = _sc_gather(y_sorted, dest).reshape(2, BATCH, rows, half)
    return _combine(y_pairs, route, xs, mods, rows, tm)


def _split_w_in(w_in):
    bw = BRANCH_WIDTH
    sizes = (3 * bw, bw, 2 * GDN_HEADS, 2 * GDN_HEADS, bw, bw, bw, bw, bw, 3 * D_MODEL)
    offs = [0]
    for s in sizes:
        offs.append(offs[-1] + s)
    part = lambda i: w_in[:, :, offs[i]:offs[i + 1]]
    main = jnp.concatenate([part(9), part(0), part(1), part(4), part(5), part(6), part(7), part(8)], axis=2)
    ba = jnp.zeros(w_in.shape[:2] + (LANES,), F32).at[:, :, :4 * GDN_HEADS].set(
        jnp.concatenate([part(2), part(3)], axis=2))
    return main.astype(BF16), ba.astype(BF16)


def kernel(x, c, ctx, c_ctx, w_mod, b_mod, norm1_gain, norm2_gain, w_in, gdn_conv_w, gdn_a_log, gdn_dt_bias, gdn_out_gain, diff_q_gain, diff_k_gain, diff_lambda, diff_out_gain, lru_conv_w, lru_conv_b, lru_w_gate, lru_b_gate, lru_lambda, w_branch, w_out, w_router, b_router, w_gate_up, w_down):
    mods = _mods(c, c_ctx, w_mod, b_mod)
    cos, sin = _rope_tables()
    xs = jnp.concatenate([x, ctx], axis=1)
    w_main, w_ba = _split_w_in(w_in)
    for layer in range(DEPTH):
        last = layer == DEPTH - 1
        lam_init = 0.8 - 0.6 * math.exp(-0.3 * layer)
        m = mods[layer]
        p, gates = _project(xs, m, norm1_gain[layer], w_main, w_ba, gdn_a_log[layer], gdn_dt_bias[layer], layer)
        qkv = _gdn_conv(p, gdn_conv_w[layer])
        ya = _gdn(qkv, p, gates, gdn_out_gain[layer])
        yb = _diff_attn(p, cos, sin, diff_q_gain[layer], diff_k_gain[layer], diff_lambda[layer],
                        diff_out_gain[layer], lam_init, with_ctx=not last)
        yc = _lru(p, lru_conv_w[layer], lru_conv_b[layer], lru_w_gate[layer], lru_b_gate[layer], lru_lambda[layer])
        rows, tm = (SEQ, 512) if last else (TOK, 768)
        xs, h2, route, counts = _merge(ya, yb, yc, p, xs, m, norm2_gain[layer], w_branch[layer].astype(BF16),
                                       w_out[layer].astype(BF16), w_router, b_router, rows, tm)
        xs = _moe(h2, route, counts, xs, m, w_gate_up, w_down, layer, rows, tm)
    return xs
```
